```python
import jax, jax.numpy as jnp
from jax import lax
import numpy as np

D_MODEL = 1024
BATCH = 2
SEQ = 8192
DEPTH = 2

FOX_HEADS = 8
FOX_HEAD_DIM = 64
FOX_W = FOX_HEADS * FOX_HEAD_DIM
FORGET_BIAS_MEAN = 3.0
FORGET_W_SCALE = 0.1
RET_HEADS = 8
RET_QK_DIM = 64
RET_V_DIM = 128
RET_QK_W = RET_HEADS * RET_QK_DIM
RET_V_W = RET_HEADS * RET_V_DIM
RET_CHUNK = 128
MLA_HEADS = 8
MLA_Q_RANK = 384
MLA_KV_RANK = 256
MLA_NOPE_DIM = 64
MLA_ROPE_DIM = 32
MLA_V_DIM = 64
MLA_QK_DIM = MLA_NOPE_DIM + MLA_ROPE_DIM
MLA_V_W = MLA_HEADS * MLA_V_DIM
N_BRANCHES = 3
Q_BLOCK = 128
ROPE_THETA = 10000.0
RMS_EPS = 1e-6
D_FF = 3584
N_EXPERTS = 8
TOP_K = 2
N_DENSE = (DEPTH + 1) // 2
N_MOE = DEPTH // 2
IN_SPLITS = (FOX_W, FOX_W, FOX_W, FOX_HEADS,
             RET_QK_W, RET_QK_W, RET_V_W, RET_V_W,
             MLA_Q_RANK, MLA_KV_RANK, MLA_ROPE_DIM,
             N_BRANCHES * D_MODEL)
D_IN = 3 * FOX_W + FOX_HEADS + 2 * RET_QK_W + 2 * RET_V_W + MLA_Q_RANK + MLA_KV_RANK + MLA_ROPE_DIM + N_BRANCHES * D_MODEL

kernel_name = "hybrid_fox_retnet_mla_moe_block"


def rmsnorm(x, w):
    xf = x.astype(jnp.float32)
    y = xf * lax.rsqrt(jnp.mean(xf * xf, axis=-1, keepdims=True) + RMS_EPS)
    return (y * w.astype(jnp.float32)).astype(x.dtype)


def rope(x):
    S, d = x.shape[1], x.shape[-1]
    pos = jnp.arange(S, dtype=jnp.float32)
    inv_freq = ROPE_THETA ** (-jnp.arange(0, d, 2, dtype=jnp.float32) / d)
    ang = pos[:, None] * inv_freq[None, :]
    cos = jnp.cos(ang)[None, :, None, :]
    sin = jnp.sin(ang)[None, :, None, :]
    xf = x.astype(jnp.float32)
    x1, x2 = xf[..., : d // 2], xf[..., d // 2:]
    out = jnp.concatenate([x1 * cos - x2 * sin, x2 * cos + x1 * sin], axis=-1)
    return out.astype(x.dtype)


def blocked_causal_attention(q, k, v, decay_cum=None):
    B, S, H, dk = q.shape
    dv = v.shape[-1]
    nb = S // Q_BLOCK
    scale = dk ** -0.5
    kpos = jnp.arange(S)
    qb = q.reshape(B, nb, Q_BLOCK, H, dk).transpose(1, 0, 2, 3, 4)
    idx = jnp.arange(nb)
    if decay_cum is not None:
        d_keys = decay_cum.transpose(0, 2, 1)
        d_q = decay_cum.reshape(B, nb, Q_BLOCK, H).transpose(1, 0, 2, 3)
        xs = (idx, qb, d_q)
    else:
        xs = (idx, qb)

    def one_block(args):
        i, qi = args[0], args[1]
        s = jnp.einsum('bqhd,bkhd->bhqk', qi, k).astype(jnp.float32) * scale
        if decay_cum is not None:
            dqi = args[2].transpose(0, 2, 1)
            s = s + (dqi[:, :, :, None] - d_keys[:, :, None, :])
        qpos = i * Q_BLOCK + jnp.arange(Q_BLOCK)
        mask = kpos[None, :] <= qpos[:, None]
        s = jnp.where(mask[None, None], s, -jnp.inf)
        p = jax.nn.softmax(s, axis=-1)
        return jnp.einsum('bhqk,bkhd->bqhd', p.astype(v.dtype), v)

    out = lax.map(one_block, xs)
    return out.transpose(1, 0, 2, 3, 4).reshape(B, S, H, dv)


def retention_chunkwise(q, k, v, gammas):
    B, S, H, dk = q.shape
    dv = v.shape[-1]
    C = RET_CHUNK
    nc = S // C
    qf = q.astype(jnp.float32)
    kf = k.astype(jnp.float32) * (dk ** -0.5)
    vf = v.astype(jnp.float32)
    qc = qf.reshape(B, nc, C, H, dk).transpose(1, 0, 3, 2, 4)
    kc = kf.reshape(B, nc, C, H, dk).transpose(1, 0, 3, 2, 4)
    vc = vf.reshape(B, nc, C, H, dv).transpose(1, 0, 3, 2, 4)
    log_g = jnp.log(gammas)
    j = jnp.arange(C, dtype=jnp.float32)
    diff = j[:, None] - j[None, :]
    d_in = jnp.where(diff[None] >= 0,
                     jnp.exp(jnp.maximum(diff, 0.0)[None] * log_g[:, None, None]), 0.0)
    xi = jnp.exp((j[None, :] + 1.0) * log_g[:, None])
    zeta = jnp.exp((C - 1.0 - j[None, :]) * log_g[:, None])
    g_chunk = jnp.exp(C * log_g)

    def step(R, inp):
        qi, ki, vi = inp
        inner = jnp.einsum('bhnd,bhmd->bhnm', qi, ki) * d_in[None]
        o = jnp.einsum('bhnm,bhme->bhne', inner, vi) \
            + jnp.einsum('bhnd,bhde->bhne', qi, R) * xi[None, :, :, None]
        R = g_chunk[None, :, None, None] * R \
            + jnp.einsum('bhmd,bhme->bhde', ki * zeta[None, :, :, None], vi)
        return R, o

    R0 = jnp.zeros((B, H, dk, dv), jnp.float32)
    _, o = lax.scan(step, R0, (qc, kc, vc))
    return o.transpose(1, 0, 3, 2, 4).reshape(B, S, H, dv)


def hybrid_mixer(h, w_in, fox_f_bias, ret_gn_w, mla_q_norm_w, mla_kv_norm_w,
                 mla_w_uq, mla_w_uk, mla_w_uv, w_br_fox, w_br_ret, w_br_mla, w_out):
    B, S, _ = h.shape
    proj = h @ w_in
    split_at = [int(i) for i in np.cumsum(IN_SPLITS)[:-1]]
    (fq, fk, fv, ff, rq, rk, rv, rg, cq, ckv, kr, gl) = jnp.split(proj, split_at, axis=-1)

    logf = jax.nn.log_sigmoid(ff.astype(jnp.float32) + fox_f_bias.astype(jnp.float32))
    cum = jnp.cumsum(logf, axis=1)
    o_fox = blocked_causal_attention(fq.reshape(B, S, FOX_HEADS, FOX_HEAD_DIM),
                                     fk.reshape(B, S, FOX_HEADS, FOX_HEAD_DIM),
                                     fv.reshape(B, S, FOX_HEADS, FOX_HEAD_DIM),
                                     cum).reshape(B, S, FOX_W)

    r_q = rope(rq.reshape(B, S, RET_HEADS, RET_QK_DIM))
    r_k = rope(rk.reshape(B, S, RET_HEADS, RET_QK_DIM))
    r_v = rv.reshape(B, S, RET_HEADS, RET_V_DIM)
    gammas = 1.0 - 2.0 ** (-5.0 - jnp.arange(RET_HEADS, dtype=jnp.float32))
    o = retention_chunkwise(r_q, r_k, r_v, gammas)
    mu = jnp.mean(o, axis=-1, keepdims=True)
    var = jnp.mean(jnp.square(o - mu), axis=-1, keepdims=True)
    o = (o - mu) * lax.rsqrt(var + RMS_EPS) * ret_gn_w.astype(jnp.float32).reshape(RET_HEADS, RET_V_DIM)
    o_ret = jax.nn.silu(rg) * o.reshape(B, S, RET_V_W).astype(h.dtype)

    c_q = rmsnorm(cq, mla_q_norm_w)
    m_q = (c_q @ mla_w_uq).reshape(B, S, MLA_HEADS, MLA_QK_DIM)
    m_q = jnp.concatenate([m_q[..., :MLA_NOPE_DIM], rope(m_q[..., MLA_NOPE_DIM:])], axis=-1)
    c_kv = rmsnorm(ckv, mla_kv_norm_w)
    k_nope = (c_kv @ mla_w_uk).reshape(B, S, MLA_HEADS, MLA_NOPE_DIM)
    m_v = (c_kv @ mla_w_uv).reshape(B, S, MLA_HEADS, MLA_V_DIM)
    k_rope = rope(kr.reshape(B, S, 1, MLA_ROPE_DIM))
    m_k = jnp.concatenate([k_nope, jnp.broadcast_to(k_rope, (B, S, MLA_HEADS, MLA_ROPE_DIM))], axis=-1)
    o_mla = blocked_causal_attention(m_q, m_k, m_v).reshape(B, S, MLA_V_W)

    gates = jax.nn.sigmoid(gl.reshape(B, S, N_BRANCHES, D_MODEL))
    merged = (gates[:, :, 0] * (o_fox @ w_br_fox)
              + gates[:, :, 1] * (o_ret @ w_br_ret)
              + gates[:, :, 2] * (o_mla @ w_br_mla))
    return merged @ w_out


def swiglu(h, w_gate, w_up, w_down):
    return (jax.nn.silu(h @ w_gate) * (h @ w_up)) @ w_down


def moe_swiglu(h, w_router, w_gate, w_up, w_down):
    logits = (h @ w_router).astype(jnp.float32)
    top_v, top_i = lax.top_k(logits, TOP_K)
    top_w = jax.nn.softmax(top_v, axis=-1)
    combine = jnp.sum(top_w[..., None] * jax.nn.one_hot(top_i, N_EXPERTS, dtype=jnp.float32), axis=-2)
    combine = combine.astype(h.dtype)
    out = jnp.zeros_like(h)
    for e in range(N_EXPERTS):
        out = out + combine[..., e:e + 1] * swiglu(h, w_gate[e], w_up[e], w_down[e])
    return out


def setup_inputs(seed: int = 0) -> dict:
    key = jax.random.key(seed)
    ks = jax.random.split(key, 32)
    f32 = jnp.float32

    def nrm(k, shape, scale):
        return jax.random.normal(k, shape, f32) * scale

    def gain(k, shape):
        return 1.0 + 0.05 * jax.random.normal(k, shape, f32)

    col_scale = jnp.ones((D_IN,), f32).at[3 * FOX_W:3 * FOX_W + FOX_HEADS].set(FORGET_W_SCALE)
    return {
        "x": jax.random.normal(ks[0], (BATCH, SEQ, D_MODEL), f32),
        "norm_mix_w": gain(ks[1], (DEPTH, D_MODEL)),
        "w_in": nrm(ks[2], (DEPTH, D_MODEL, D_IN), D_MODEL ** -0.5) * col_scale,
        "fox_f_bias": FORGET_BIAS_MEAN + 0.5 * jax.random.normal(ks[3], (DEPTH, FOX_HEADS), f32),
        "ret_gn_w": gain(ks[4], (DEPTH, RET_V_W)),
        "mla_q_norm_w": gain(ks[5], (DEPTH, MLA_Q_RANK)),
        "mla_kv_norm_w": gain(ks[6], (DEPTH, MLA_KV_RANK)),
        "mla_w_uq": nrm(ks[7], (DEPTH, MLA_Q_RANK, MLA_HEADS * MLA_QK_DIM), MLA_Q_RANK ** -0.5),
        "mla_w_uk": nrm(ks[8], (DEPTH, MLA_KV_RANK, MLA_HEADS * MLA_NOPE_DIM), MLA_KV_RANK ** -0.5),
        "mla_w_uv": nrm(ks[9], (DEPTH, MLA_KV_RANK, MLA_V_W), MLA_KV_RANK ** -0.5),
        "w_br_fox": nrm(ks[10], (DEPTH, FOX_W, D_MODEL), FOX_W ** -0.5),
        "w_br_ret": nrm(ks[11], (DEPTH, RET_V_W, D_MODEL), RET_V_W ** -0.5),
        "w_br_mla": nrm(ks[12], (DEPTH, MLA_V_W, D_MODEL), MLA_V_W ** -0.5),
        "w_out": nrm(ks[13], (DEPTH, D_MODEL, D_MODEL), D_MODEL ** -0.5),
        "norm_ffn_w": gain(ks[14], (DEPTH, D_MODEL)),
        "dense_w_gate": nrm(ks[15], (N_DENSE, D_MODEL, D_FF), D_MODEL ** -0.5),
        "dense_w_up": nrm(ks[16], (N_DENSE, D_MODEL, D_FF), D_MODEL ** -0.5),
        "dense_w_down": nrm(ks[17], (N_DENSE, D_FF, D_MODEL), D_FF ** -0.5),
        "moe_w_router": nrm(ks[18], (N_MOE, D_MODEL, N_EXPERTS), D_MODEL ** -0.5),
        "moe_w_gate": nrm(ks[19], (N_MOE, N_EXPERTS, D_MODEL, D_FF), D_MODEL ** -0.5),
        "moe_w_up": nrm(ks[20], (N_MOE, N_EXPERTS, D_MODEL, D_FF), D_MODEL ** -0.5),
        "moe_w_down": nrm(ks[21], (N_MOE, N_EXPERTS, D_FF, D_MODEL), D_FF ** -0.5),
        "final_norm_w": gain(ks[22], (D_MODEL,)),
    }


def reference(x, norm_mix_w, w_in, fox_f_bias, ret_gn_w, mla_q_norm_w, mla_kv_norm_w,
              mla_w_uq, mla_w_uk, mla_w_uv, w_br_fox, w_br_ret, w_br_mla, w_out,
              norm_ffn_w, dense_w_gate, dense_w_up, dense_w_down,
              moe_w_router, moe_w_gate, moe_w_up, moe_w_down, final_norm_w):
    for l in range(DEPTH):
        h = rmsnorm(x, norm_mix_w[l])
        x = x + hybrid_mixer(h, w_in[l], fox_f_bias[l], ret_gn_w[l], mla_q_norm_w[l],
                             mla_kv_norm_w[l], mla_w_uq[l], mla_w_uk[l], mla_w_uv[l],
                             w_br_fox[l], w_br_ret[l], w_br_mla[l], w_out[l])
        h = rmsnorm(x, norm_ffn_w[l])
        if l % 2 == 0:
            i = l // 2
            x = x + swiglu(h, dense_w_gate[i], dense_w_up[i], dense_w_down[i])
        else:
            i = l // 2
            x = x + moe_swiglu(h, moe_w_router[i], moe_w_gate[i], moe_w_up[i], moe_w_down[i])
    return rmsnorm(x, final_norm_w)
```

```python
import functools

import numpy as np
import jax
import jax.numpy as jnp
from jax import lax
from jax.experimental import pallas as pl
from jax.experimental.pallas import tpu as pltpu

F32 = jnp.float32
BF16 = jnp.bfloat16

D_MODEL = 1024
DEPTH = 2
FOX_HEADS = 8
FOX_HEAD_DIM = 64
FOX_W = FOX_HEADS * FOX_HEAD_DIM
RET_HEADS = 8
RET_QK_DIM = 64
RET_V_DIM = 128
RET_QK_W = RET_HEADS * RET_QK_DIM
RET_V_W = RET_HEADS * RET_V_DIM
MLA_HEADS = 8
MLA_Q_RANK = 384
MLA_KV_RANK = 256
MLA_NOPE_DIM = 64
MLA_ROPE_DIM = 32
MLA_V_DIM = 64
MLA_QK_DIM = MLA_NOPE_DIM + MLA_ROPE_DIM
MLA_V_W = MLA_HEADS * MLA_V_DIM
N_BRANCHES = 3
ROPE_THETA = 10000.0
RMS_EPS = 1e-6
D_FF = 3584
N_EXPERTS = 8
TOP_K = 2
IN_SPLITS = (FOX_W, FOX_W, FOX_W, FOX_HEADS,
             RET_QK_W, RET_QK_W, RET_V_W, RET_V_W,
             MLA_Q_RANK, MLA_KV_RANK, MLA_ROPE_DIM,
             N_BRANCHES * D_MODEL)

LANE = 128
NEG = -1e30
VMEM_LIMIT = 56 * 1024 * 1024

U_FQ, U_FK, U_FV = 0, 4, 8
U_RQ, U_RQR, U_RK, U_RKR = 12, 16, 20, 24
U_CKV, U_KR, U_KRR = 28, 30, 31
U_RV, U_RG, U_GL, U_CQ = 32, 40, 48, 72
N_PROJ = 75 * LANE
FOX_EXT_STRIDE = 8


def _cparams(sem, vmem=VMEM_LIMIT):
    return pltpu.CompilerParams(dimension_semantics=sem, vmem_limit_bytes=vmem)


def _sigmoid(x):
    return 1.0 / (1.0 + jnp.exp(-x))


def _lane_mask(shape, ranges):
    lane = lax.broadcasted_iota(jnp.int32, shape, len(shape) - 1)
    m = None
    for a, b in ranges:
        r = (lane >= a) & (lane < b)
        m = r if m is None else (m | r)
    return m


def _inproj_kernel(x_ref, nw_ref, w_ref, wff_ref, out_ref, ff_ref, h_scr):
    @pl.when(pl.program_id(1) == 0)
    def _():
        x = x_ref[...]
        ms = jnp.mean(x * x, axis=-1, keepdims=True)
        h = (x * lax.rsqrt(ms + RMS_EPS) * nw_ref[...]).astype(BF16)
        h_scr[...] = h
        ff_ref[...] = jnp.dot(h, wff_ref[...], preferred_element_type=F32)

    out_ref[...] = jnp.dot(h_scr[...], w_ref[...], preferred_element_type=F32).astype(out_ref.dtype)


def _inproj(x2, nw, w_all, w_ff, tm, tn):
    T = x2.shape[0]
    return pl.pallas_call(
        _inproj_kernel,
        grid=(T // tm, N_PROJ // tn),
        in_specs=[
            pl.BlockSpec((tm, D_MODEL), lambda i, j: (i, 0)),
            pl.BlockSpec((1, D_MODEL), lambda i, j: (0, 0)),
            pl.BlockSpec((D_MODEL, tn), lambda i, j: (0, j)),
            pl.BlockSpec((D_MODEL, LANE), lambda i, j: (0, 0)),
        ],
        out_specs=[
            pl.BlockSpec((tm, tn), lambda i, j: (i, j)),
            pl.BlockSpec((tm, LANE), lambda i, j: (i, 0)),
        ],
        out_shape=[
            jax.ShapeDtypeStruct((T, N_PROJ), BF16),
            jax.ShapeDtypeStruct((T, LANE), F32),
        ],
        scratch_shapes=[pltpu.VMEM((tm, D_MODEL), BF16)],
        compiler_params=_cparams(("parallel", "arbitrary")),
        name="inproj",
    )(x2, nw, w_all, w_ff)


def _rot_cols(w, heads, d):
    k = w.shape[0]
    w4 = w.reshape(k, heads, 2, d // 2)
    return jnp.concatenate([-w4[:, :, 1], w4[:, :, 0]], axis=-1).reshape(k, heads * d)


def _build_inproj_weights(w_in):
    split_at = [int(i) for i in np.cumsum(IN_SPLITS)[:-1]]
    fq, fk, fv, ff, rq, rk, rv, rg, cq, ckv, kr, gl = jnp.split(w_in, split_at, axis=-1)
    z64 = jnp.zeros((D_MODEL, 64), F32)
    z32 = jnp.zeros((D_MODEL, 32), F32)
    kr128 = jnp.concatenate([z64, kr, z32], axis=-1)
    krr128 = jnp.concatenate([z64, _rot_cols(kr, 1, MLA_ROPE_DIM), z32], axis=-1)
    w_all = jnp.concatenate(
        [fq, fk, fv, rq, _rot_cols(rq, RET_HEADS, RET_QK_DIM), rk, _rot_cols(rk, RET_HEADS, RET_QK_DIM),
         ckv, kr128, krr128, rv, rg, gl, cq], axis=-1).astype(BF16)
    w_ff = jnp.pad(ff, ((0, 0), (0, LANE - FOX_HEADS))).astype(BF16)
    return w_all, w_ff


def _split3(x):
    hi = x.astype(BF16)
    r1 = x - hi.astype(F32)
    mid = r1.astype(BF16)
    lo = (r1 - mid.astype(F32)).astype(BF16)
    return hi, mid, lo


def _fox_prep_kernel(fq_ref, fk_ref, ff_ref, bias_ref, tri_ref, pq_ref, pk_ref, cq_ref, ck_ref,
                     qx_ref, kx_ref, carry_scr):
    @pl.when(pl.program_id(1) == 0)
    def _():
        carry_scr[...] = jnp.zeros_like(carry_scr)

    ts = ff_ref.shape[0]
    z = ff_ref[...] + bias_ref[...]
    logf = jnp.minimum(z, 0.0) - jnp.log(1.0 + jnp.exp(-jnp.abs(z)))
    tri = tri_ref[...]
    cum = carry_scr[0:1, :]
    for part in _split3(logf):
        cum = cum + jnp.dot(tri, part, preferred_element_type=F32)
    carry_scr[0:1, :] = cum[ts - 1:ts, :]

    eq = cq_ref[...]
    ek = ck_ref[...]
    for j, part in enumerate(_split3(cum)):
        eq = eq + jnp.dot(part, pq_ref[j * LANE:(j + 1) * LANE, :], preferred_element_type=F32)
        ek = ek + jnp.dot(part, pk_ref[j * LANE:(j + 1) * LANE, :], preferred_element_type=F32)
    scale = FOX_HEAD_DIM ** -0.5
    for p in range(FOX_HEADS // 2):
        src = slice(p * LANE, (p + 1) * LANE)
        qx_ref[:, 2 * p * LANE:(2 * p + 1) * LANE] = (fq_ref[:, src].astype(F32) * scale).astype(BF16)
        qx_ref[:, (2 * p + 1) * LANE:(2 * p + 2) * LANE] = eq[:, src].astype(BF16)
        kx_ref[:, 2 * p * LANE:(2 * p + 1) * LANE] = fk_ref[:, src]
        kx_ref[:, (2 * p + 1) * LANE:(2 * p + 2) * LANE] = ek[:, src].astype(BF16)


def _fox_prep_consts(ts):
    tri = np.tril(np.ones((ts, ts), np.float32))
    pq = np.zeros((3 * LANE, FOX_W), np.float32)
    pk = np.zeros((3 * LANE, FOX_W), np.float32)
    cq = np.zeros((1, FOX_W), np.float32)
    ck = np.zeros((1, FOX_W), np.float32)
    for h in range(FOX_HEADS):
        base = LANE * (h // 2) + FOX_EXT_STRIDE * (h % 2)
        for j in range(3):
            pq[j * LANE + h, base + j] = 1.0
            ck[0, base + j] = 1.0
            cq[0, base + 3 + j] = 1.0
            pk[j * LANE + h, base + 3 + j] = -1.0
    return (jnp.asarray(tri, BF16), jnp.asarray(pq, BF16), jnp.asarray(pk, BF16),
            jnp.asarray(cq), jnp.asarray(ck))


def _fox_prep(proj, ff32, bias, B, S, ts):
    T = B * S
    ns = S // ts
    tri, pq, pk, cq, ck = _fox_prep_consts(ts)
    bias128 = jnp.pad(bias.astype(F32), (0, LANE - FOX_HEADS)).reshape(1, LANE)
    const = lambda shape: pl.BlockSpec(shape, lambda b, s: (0, 0))
    return pl.pallas_call(
        _fox_prep_kernel,
        grid=(B, ns),
        in_specs=[
            pl.BlockSpec((ts, FOX_W), lambda b, s: (b * ns + s, U_FQ // 4)),
            pl.BlockSpec((ts, FOX_W), lambda b, s: (b * ns + s, U_FK // 4)),
            pl.BlockSpec((ts, LANE), lambda b, s: (b * ns + s, 0)),
            const((1, LANE)), const((ts, ts)), const((3 * LANE, FOX_W)), const((3 * LANE, FOX_W)),
            const((1, FOX_W)), const((1, FOX_W)),
        ],
        out_specs=[
            pl.BlockSpec((ts, 2 * FOX_W), lambda b, s: (b * ns + s, 0)),
            pl.BlockSpec((ts, 2 * FOX_W), lambda b, s: (b * ns + s, 0)),
        ],
        out_shape=[jax.ShapeDtypeStruct((T, 2 * FOX_W), BF16)] * 2,
        scratch_shapes=[pltpu.VMEM((8, LANE), F32)],
        compiler_params=_cparams(("parallel", "arbitrary")),
        name="fox_prep",
    )(proj, proj, ff32, bias128, tri, pq, pk, cq, ck)


def _attn_kernel(q_ref, k_ref, v_ref, o_ref, qh_scr, m_scr, l_scr, acc_scr, *, t, ranges):
    qi = pl.program_id(2)
    q2 = q_ref[...]
    for h in range(2):
        qh_scr[h] = jnp.where(_lane_mask(q2.shape, ranges[h]), q2, jnp.zeros_like(q2))
    m_scr[...] = jnp.full(m_scr.shape, NEG, F32)
    l_scr[...] = jnp.zeros_like(l_scr)
    acc_scr[...] = jnp.zeros_like(acc_scr)

    def step(ki, masked):
        off = pl.multiple_of(ki * t, t)
        k = k_ref[pl.ds(off, t), :]
        v = v_ref[pl.ds(off, t), :]
        for h in range(2):
            s = lax.dot_general(qh_scr[h], k, (((1,), (1,)), ((), ())), preferred_element_type=F32)
            if masked:
                row = lax.broadcasted_iota(jnp.int32, s.shape, 0)
                col = lax.broadcasted_iota(jnp.int32, s.shape, 1)
                s = jnp.where(col <= row, s, NEG)
            m_prev = m_scr[h]
            m_new = jnp.maximum(m_prev, jnp.max(s, axis=-1, keepdims=True))
            alpha = jnp.exp(m_prev - m_new)
            p = jnp.exp(s - m_new)
            l_scr[h] = alpha * l_scr[h] + jnp.sum(p, axis=-1, keepdims=True)
            acc_scr[h] = alpha * acc_scr[h] + jnp.dot(p.astype(BF16), v, preferred_element_type=F32)
            m_scr[h] = m_new

    def body(ki, c):
        step(ki, False)
        return c

    lax.fori_loop(0, qi, body, 0)
    step(qi, True)
    o0 = acc_scr[0] / l_scr[0]
    o1 = acc_scr[1] / l_scr[1]
    o_ref[...] = jnp.where(_lane_mask(o0.shape, ((0, LANE // 2),)), o0, o1).astype(o_ref.dtype)


def _attention(qx, kx, v_arr, v_unit0, B, S, t, ranges, name):
    T = B * S
    nq = S // t
    W = 2 * LANE
    pairs = qx.shape[1] // W
    kern = functools.partial(_attn_kernel, t=t, ranges=ranges)
    return pl.pallas_call(
        kern,
        grid=(B, pairs, nq),
        in_specs=[
            pl.BlockSpec((t, W), lambda b, p, i: (b * nq + i, p)),
            pl.BlockSpec((S, W), lambda b, p, i: (b, p)),
            pl.BlockSpec((S, LANE), lambda b, p, i: (b, v_unit0 + p)),
        ],
        out_specs=pl.BlockSpec((t, LANE), lambda b, p, i: (b * nq + i, p)),
        out_shape=jax.ShapeDtypeStruct((T, pairs * LANE), BF16),
        scratch_shapes=[
            pltpu.VMEM((2, t, W), BF16),
            pltpu.VMEM((2, t, 1), F32),
            pltpu.VMEM((2, t, 1), F32),
            pltpu.VMEM((2, t, LANE), F32),
        ],
        compiler_params=_cparams(("parallel", "parallel", "arbitrary")),
        name=name,
    )(qx, kx, v_arr)


_E0 = LANE
FOX_RANGES = (((0, 64), (_E0, _E0 + FOX_EXT_STRIDE)),
              ((64, 128), (_E0 + FOX_EXT_STRIDE, _E0 + 2 * FOX_EXT_STRIDE)))
MLA_RANGES = (((0, LANE),), ((LANE, 2 * LANE),))


def _mla_prep_kernel(cq_ref, ckv_ref, kr_ref, krr_ref, qnw_ref, kvnw_ref, wq_ref, wqr_ref, wk_ref, wv_ref,
                     cq_tab, sq_tab, ck_tab, sk_tab, q_ref, k_ref, v_ref):
    def norm(x_ref, w_ref):
        x = x_ref[...].astype(F32)
        ms = jnp.mean(x * x, axis=-1, keepdims=True)
        return (x * lax.rsqrt(ms + RMS_EPS) * w_ref[...]).astype(BF16)

    c_q = norm(cq_ref, qnw_ref)
    c_kv = norm(ckv_ref, kvnw_ref)
    tile8 = lambda a: jnp.concatenate([a] * MLA_HEADS, axis=-1)
    q = (jnp.dot(c_q, wq_ref[...], preferred_element_type=F32) * tile8(cq_tab[...])
         + jnp.dot(c_q, wqr_ref[...], preferred_element_type=F32) * tile8(sq_tab[...]))
    q_ref[...] = q.astype(BF16)
    k_rope = kr_ref[...].astype(F32) * ck_tab[...] + krr_ref[...].astype(F32) * sk_tab[...]
    k = jnp.dot(c_kv, wk_ref[...], preferred_element_type=F32) + tile8(k_rope)
    k_ref[...] = k.astype(BF16)
    v_ref[...] = jnp.dot(c_kv, wv_ref[...], preferred_element_type=F32).astype(BF16)


def _mla_weights(w_uq, w_uk, w_uv):
    q3 = w_uq.reshape(MLA_Q_RANK, MLA_HEADS, MLA_QK_DIM)
    nope, ropep = q3[..., :MLA_NOPE_DIM], q3[..., MLA_NOPE_DIM:]
    half = MLA_ROPE_DIM // 2
    rot = jnp.concatenate([-ropep[..., half:], ropep[..., :half]], axis=-1)
    z32 = jnp.zeros((MLA_Q_RANK, MLA_HEADS, LANE - MLA_QK_DIM), F32)
    z64 = jnp.zeros((MLA_Q_RANK, MLA_HEADS, MLA_NOPE_DIM), F32)
    wq = jnp.concatenate([nope, ropep, z32], axis=-1).reshape(MLA_Q_RANK, MLA_HEADS * LANE)
    wqr = jnp.concatenate([z64, rot, z32], axis=-1).reshape(MLA_Q_RANK, MLA_HEADS * LANE)
    k3 = w_uk.reshape(MLA_KV_RANK, MLA_HEADS, MLA_NOPE_DIM)
    wk = jnp.concatenate([k3, jnp.zeros_like(k3)], axis=-1).reshape(MLA_KV_RANK, MLA_HEADS * LANE)
    return wq.astype(BF16), wqr.astype(BF16), wk.astype(BF16), w_uv.astype(BF16)


def _mla_tables(S):
    pos = jnp.arange(S, dtype=F32)
    inv = ROPE_THETA ** (-jnp.arange(0, MLA_ROPE_DIM, 2, dtype=F32) / MLA_ROPE_DIM)
    ang = pos[:, None] * inv[None, :]
    cos, sin = jnp.cos(ang), jnp.sin(ang)
    ones = jnp.ones((S, MLA_NOPE_DIM), F32)
    z64 = jnp.zeros((S, MLA_NOPE_DIM), F32)
    z32 = jnp.zeros((S, LANE - MLA_QK_DIM), F32)
    ctab = jnp.concatenate([ones, cos, cos, z32], axis=-1)
    stab = jnp.concatenate([z64, sin, sin, z32], axis=-1)
    scale = MLA_QK_DIM ** -0.5
    return ctab * scale, stab * scale, ctab, stab


def _mla_prep(proj, qnw, kvnw, wq, wqr, wk, wv, tabs, B, S, tm):
    T = B * S
    ns = S // tm
    HW = MLA_HEADS * LANE
    const = lambda shape: pl.BlockSpec(shape, lambda i: (0, 0))
    tab = pl.BlockSpec((tm, LANE), lambda i: (i % ns, 0))
    return pl.pallas_call(
        _mla_prep_kernel,
        grid=(T // tm,),
        in_specs=[
            pl.BlockSpec((tm, MLA_Q_RANK), lambda i: (i, U_CQ // 3)),
            pl.BlockSpec((tm, MLA_KV_RANK), lambda i: (i, U_CKV // 2)),
            pl.BlockSpec((tm, LANE), lambda i: (i, U_KR)),
            pl.BlockSpec((tm, LANE), lambda i: (i, U_KRR)),
            const((1, MLA_Q_RANK)), const((1, MLA_KV_RANK)),
            const((MLA_Q_RANK, HW)), const((MLA_Q_RANK, HW)), const((MLA_KV_RANK, HW)),
            const((MLA_KV_RANK, MLA_V_W)),
            tab, tab, tab, tab,
        ],
        out_specs=[
            pl.BlockSpec((tm, HW), lambda i: (i, 0)),
            pl.BlockSpec((tm, HW), lambda i: (i, 0)),
            pl.BlockSpec((tm, MLA_V_W), lambda i: (i, 0)),
        ],
        out_shape=[
            jax.ShapeDtypeStruct((T, HW), BF16),
            jax.ShapeDtypeStruct((T, HW), BF16),
            jax.ShapeDtypeStruct((T, MLA_V_W), BF16),
        ],
        compiler_params=_cparams(("parallel",)),
        name="mla_prep",
    )(proj, proj, proj, proj, qnw, kvnw, wq, wqr, wk, wv, *tabs)


def _ret_kernel(rq_ref, rqr_ref, rk_ref, rkr_ref, rv_ref, rg_ref, cos_ref, sin_ref,
                din_ref, xi_ref, zeta_ref, gch_ref, gnw_ref, o_ref, r_scr):
    @pl.when(pl.program_id(1) == 0)
    def _():
        r_scr[...] = jnp.zeros_like(r_scr)

    tile4 = lambda a: jnp.concatenate([a] * (RET_QK_W // LANE), axis=-1)
    cos4 = tile4(cos_ref[...])
    sin4 = tile4(sin_ref[...])
    q = rq_ref[...].astype(F32) * cos4 + rqr_ref[...].astype(F32) * sin4
    k = (rk_ref[...].astype(F32) * cos4 + rkr_ref[...].astype(F32) * sin4) * (RET_QK_DIM ** -0.5)
    for p in range(RET_HEADS // 2):
        q2 = q[:, p * LANE:(p + 1) * LANE]
        k2 = k[:, p * LANE:(p + 1) * LANE]
        k2b = k2.astype(BF16)
        kz2 = (k2 * zeta_ref[p]).astype(BF16)
        r2 = r_scr[p]
        r2b = r2.astype(BF16)
        new_r = gch_ref[p] * r2
        for j in range(2):
            h = 2 * p + j
            half = ((j * RET_QK_DIM, (j + 1) * RET_QK_DIM),)
            qm = jnp.where(_lane_mask(q2.shape, half), q2, 0.0).astype(BF16)
            inner = lax.dot_general(qm, k2b, (((1,), (1,)), ((), ())), preferred_element_type=F32)
            inner = inner * din_ref[h]
            vh = rv_ref[:, h * LANE:(h + 1) * LANE]
            o = (jnp.dot(inner.astype(BF16), vh, preferred_element_type=F32)
                 + jnp.dot(qm, r2b, preferred_element_type=F32) * xi_ref[h])
            upd = lax.dot_general(kz2, vh, (((0,), (0,)), ((), ())), preferred_element_type=F32)
            rows = lax.broadcasted_iota(jnp.int32, upd.shape, 0)
            new_r = new_r + jnp.where((rows >= half[0][0]) & (rows < half[0][1]), upd, 0.0)
            mu = jnp.mean(o, axis=-1, keepdims=True)
            d = o - mu
            var = jnp.mean(d * d, axis=-1, keepdims=True)
            on = d * lax.rsqrt(var + RMS_EPS) * gnw_ref[:, h * LANE:(h + 1) * LANE]
            g = rg_ref[:, h * LANE:(h + 1) * LANE].astype(F32)
            o_ref[:, h * LANE:(h + 1) * LANE] = (g * _sigmoid(g) * on).astype(o_ref.dtype)
        r_scr[p] = new_r


def _ret_tables(S, C):
    pos = jnp.arange(S, dtype=F32)
    inv = ROPE_THETA ** (-jnp.arange(0, RET_QK_DIM, 2, dtype=F32) / RET_QK_DIM)
    ang = pos[:, None] * inv[None, :]
    cos = jnp.concatenate([jnp.cos(ang)] * 4, axis=-1)
    sin = jnp.concatenate([jnp.sin(ang)] * 4, axis=-1)
    gammas = 1.0 - 2.0 ** (-5.0 - jnp.arange(RET_HEADS, dtype=F32))
    log_g = jnp.log(gammas)
    j = jnp.arange(C, dtype=F32)
    diff = j[:, None] - j[None, :]
    din = jnp.where(diff[None] >= 0, jnp.exp(jnp.maximum(diff, 0.0)[None] * log_g[:, None, None]), 0.0)
    xi = jnp.exp((j[None, :] + 1.0) * log_g[:, None])
    zeta = jnp.exp((C - 1.0 - j[None, :]) * log_g[:, None])
    gch = jnp.exp(C * log_g)
    xi_t = jnp.broadcast_to(xi[:, :, None], (RET_HEADS, C, LANE))
    zeta_p = jnp.repeat(zeta.reshape(RET_HEADS // 2, 2, C).transpose(0, 2, 1), RET_QK_DIM, axis=-1)
    gch_p = jnp.broadcast_to(jnp.repeat(gch.reshape(RET_HEADS // 2, 2), RET_QK_DIM, axis=-1)[:, :, None],
                             (RET_HEADS // 2, LANE, LANE))
    return cos, sin, din, xi_t, zeta_p, gch_p


def _retention(proj, gnw, tabs, B, S, C):
    T = B * S
    nc = S // C
    cos, sin, din, xi, zeta, gch = tabs
    blk = lambda w, unit: pl.BlockSpec((C, w), lambda b, c: (b * nc + c, unit * LANE // w))
    full = lambda a: pl.BlockSpec(a.shape, lambda b, c: (0,) * a.ndim)
    return pl.pallas_call(
        _ret_kernel,
        grid=(B, nc),
        in_specs=[
            blk(RET_QK_W, U_RQ), blk(RET_QK_W, U_RQR), blk(RET_QK_W, U_RK), blk(RET_QK_W, U_RKR),
            blk(RET_V_W, U_RV), blk(RET_V_W, U_RG),
            pl.BlockSpec((C, LANE), lambda b, c: (c, 0)),
            pl.BlockSpec((C, LANE), lambda b, c: (c, 0)),
            full(din), full(xi), full(zeta), full(gch), full(gnw),
        ],
        out_specs=pl.BlockSpec((C, RET_V_W), lambda b, c: (b * nc + c, 0)),
        out_shape=jax.ShapeDtypeStruct((T, RET_V_W), BF16),
        scratch_shapes=[pltpu.VMEM((RET_HEADS // 2, LANE, LANE), F32)],
        compiler_params=_cparams(("parallel", "arbitrary")),
        name="retention",
    )(proj, proj, proj, proj, proj, proj, cos, sin, din, xi, zeta, gch, gnw)


def _merge_kernel(of_ref, or_ref, om_ref, g0_ref, g1_ref, g2_ref, x_ref,
                  wf_ref, wr_ref, wm_ref, wo_ref, nw_ref, xo_ref, h_ref):
    merged = (_sigmoid(g0_ref[...].astype(F32)) * jnp.dot(of_ref[...], wf_ref[...], preferred_element_type=F32)
              + _sigmoid(g1_ref[...].astype(F32)) * jnp.dot(or_ref[...], wr_ref[...], preferred_element_type=F32)
              + _sigmoid(g2_ref[...].astype(F32)) * jnp.dot(om_ref[...], wm_ref[...], preferred_element_type=F32))
    x = x_ref[...] + jnp.dot(merged.astype(BF16), wo_ref[...], preferred_element_type=F32)
    xo_ref[...] = x
    ms = jnp.mean(x * x, axis=-1, keepdims=True)
    h_ref[...] = (x * lax.rsqrt(ms + RMS_EPS) * nw_ref[...]).astype(h_ref.dtype)


def _merge(o_fox, o_ret, o_mla, proj, x2, wf, wr, wm, wo, nw, tm):
    T = x2.shape[0]
    row = lambda w, cb=0: pl.BlockSpec((tm, w), lambda i: (i, cb))
    const = lambda a: pl.BlockSpec(a.shape, lambda i: (0, 0))
    g_unit = U_GL // 8
    return pl.pallas_call(
        _merge_kernel,
        grid=(T // tm,),
        in_specs=[
            row(FOX_W), row(RET_V_W), row(MLA_V_W),
            row(D_MODEL, g_unit), row(D_MODEL, g_unit + 1), row(D_MODEL, g_unit + 2),
            row(D_MODEL),
            const(wf), const(wr), const(wm), const(wo), const(nw),
        ],
        out_specs=[row(D_MODEL), row(D_MODEL)],
        out_shape=[jax.ShapeDtypeStruct((T, D_MODEL), F32), jax.ShapeDtypeStruct((T, D_MODEL), BF16)],
        compiler_params=_cparams(("parallel",)),
        name="merge",
    )(o_fox, o_ret, o_mla, proj, proj, proj, x2, wf, wr, wm, wo, nw)


def _swiglu_tile(h, wg, wu, wd):
    a = jnp.dot(h, wg, preferred_element_type=F32)
    b = jnp.dot(h, wu, preferred_element_type=F32)
    return jnp.dot((a * _sigmoid(a) * b).astype(BF16), wd, preferred_element_type=F32)


def _dense_ffn_kernel(h_ref, wg_ref, wu_ref, wd_ref, x_ref, o_ref, acc_scr):
    f = pl.program_id(1)

    @pl.when(f == 0)
    def _():
        acc_scr[...] = jnp.zeros_like(acc_scr)

    acc_scr[...] += _swiglu_tile(h_ref[...], wg_ref[...], wu_ref[...], wd_ref[...])

    @pl.when(f == pl.num_programs(1) - 1)
    def _():
        o_ref[...] = x_ref[...] + acc_scr[...]


def _dense_ffn(h, x2, wg, wu, wd, tm, tf):
    T = x2.shape[0]
    return pl.pallas_call(
        _dense_ffn_kernel,
        grid=(T // tm, D_FF // tf),
        in_specs=[
            pl.BlockSpec((tm, D_MODEL), lambda i, f: (i, 0)),
            pl.BlockSpec((D_MODEL, tf), lambda i, f: (0, f)),
            pl.BlockSpec((D_MODEL, tf), lambda i, f: (0, f)),
            pl.BlockSpec((tf, D_MODEL), lambda i, f: (f, 0)),
            pl.BlockSpec((tm, D_MODEL), lambda i, f: (i, 0)),
        ],
        out_specs=pl.BlockSpec((tm, D_MODEL), lambda i, f: (i, 0)),
        out_shape=jax.ShapeDtypeStruct((T, D_MODEL), F32),
        scratch_shapes=[pltpu.VMEM((tm, D_MODEL), F32)],
        compiler_params=_cparams(("parallel", "arbitrary")),
        name="dense_ffn",
    )(h, wg, wu, wd, x2)


M_E0, M_E1, M_P0, M_P1, M_W0, M_W1 = 0, 1, 2, 3, 4, 5


def _split2(x):
    hi = x.astype(BF16)
    lo = (x - hi.astype(F32)).astype(BF16)
    return hi, lo


def _router_kernel(x_ref, nw_ref, whi_ref, wlo_ref, tri_ref, h_ref, meta_ref, cnt_ref, carry_scr):
    @pl.when(pl.program_id(0) == 0)
    def _():
        carry_scr[...] = jnp.zeros_like(carry_scr)

    x = x_ref[...]
    ms = jnp.mean(x * x, axis=-1, keepdims=True)
    h = x * lax.rsqrt(ms + RMS_EPS) * nw_ref[...]
    h_ref[...] = h
    hhi, hlo = _split2(h)
    logits = (jnp.dot(hhi, whi_ref[...], preferred_element_type=F32)
              + jnp.dot(hhi, wlo_ref[...], preferred_element_type=F32)
              + jnp.dot(hlo, whi_ref[...], preferred_element_type=F32))
    lane = lax.broadcasted_iota(jnp.int32, logits.shape, 1)
    lanef = lane.astype(F32)
    logits = jnp.where(lane < N_EXPERTS, logits, NEG)
    m0 = jnp.max(logits, axis=-1, keepdims=True)
    e0 = jnp.min(jnp.where(logits == m0, lanef, float(LANE)), axis=-1, keepdims=True)
    oh0 = lanef == e0
    rest = jnp.where(oh0, NEG, logits)
    m1 = jnp.max(rest, axis=-1, keepdims=True)
    e1 = jnp.min(jnp.where(rest == m1, lanef, float(LANE)), axis=-1, keepdims=True)
    oh1 = lanef == e1
    z = jnp.exp(m1 - m0)
    w0 = 1.0 / (1.0 + z)
    w1 = z / (1.0 + z)
    both = jnp.where(oh0 | oh1, 1.0, 0.0)
    before = carry_scr[0:1, :] + jnp.dot(tri_ref[...], both.astype(BF16), preferred_element_type=F32)
    p0 = jnp.sum(jnp.where(oh0, before, 0.0), axis=-1, keepdims=True)
    p1 = jnp.sum(jnp.where(oh1, before, 0.0), axis=-1, keepdims=True)
    total = carry_scr[0:1, :] + jnp.sum(both, axis=0, keepdims=True)
    carry_scr[0:1, :] = total
    cnt_ref[...] = jnp.broadcast_to(total, cnt_ref.shape)
    meta = jnp.zeros(logits.shape, F32)
    for idx, val in ((M_E0, e0), (M_E1, e1), (M_P0, p0), (M_P1, p1), (M_W0, w0), (M_W1, w1)):
        meta = jnp.where(lane == idx, val, meta)
    meta_ref[...] = meta


def _router(x2, nw, w_router, tr):
    T = x2.shape[0]
    wr = jnp.pad(w_router.astype(F32), ((0, 0), (0, LANE - N_EXPERTS)))
    whi = wr.astype(BF16)
    wlo = (wr - whi.astype(F32)).astype(BF16)
    tri = jnp.asarray(np.tril(np.ones((tr, tr), np.float32), -1), BF16)
    const = lambda a: pl.BlockSpec(a.shape, lambda i: (0, 0))
    return pl.pallas_call(
        _router_kernel,
        grid=(T // tr,),
        in_specs=[pl.BlockSpec((tr, D_MODEL), lambda i: (i, 0)), const(nw), const(whi), const(wlo), const(tri)],
        out_specs=[
            pl.BlockSpec((tr, D_MODEL), lambda i: (i, 0)),
            pl.BlockSpec((tr, LANE), lambda i: (i, 0)),
            pl.BlockSpec((8, LANE), lambda i: (0, 0)),
        ],
        out_shape=[
            jax.ShapeDtypeStruct((T, D_MODEL), F32),
            jax.ShapeDtypeStruct((T, LANE), F32),
            jax.ShapeDtypeStruct((8, LANE), F32),
        ],
        scratch_shapes=[pltpu.VMEM((8, LANE), F32)],
        compiler_params=_cparams(("arbitrary",)),
        name="router",
    )(x2, nw, whi, wlo, tri)


def _row_copy(src, dst, s_row, d_row, sem):
    return pltpu.make_async_copy(src.at[pl.ds(s_row, 1)], dst.at[pl.ds(d_row, 1)], sem)


def _scatter_kernel(dest_ref, h_hbm, xs_in, xs_hbm, sem, *, tt):
    del xs_in
    base = pl.program_id(0) * tt

    def issue(t, c):
        _row_copy(h_hbm, xs_hbm, base + t, dest_ref[0, 0, 2 * t], sem).start()
        _row_copy(h_hbm, xs_hbm, base + t, dest_ref[0, 0, 2 * t + 1], sem).start()
        return c

    lax.fori_loop(0, tt, issue, 0)

    def drain(t, c):
        _row_copy(h_hbm, xs_hbm, 0, 0, sem).wait()
        return c

    lax.fori_loop(0, 2 * tt, drain, 0)


def _scatter_rows(dest, h, n_rows, tt):
    T = h.shape[0]
    xs0 = jnp.zeros((n_rows, D_MODEL), h.dtype)
    dest3 = dest.reshape(T // tt, 1, 2 * tt)
    return pl.pallas_call(
        functools.partial(_scatter_kernel, tt=tt),
        grid=(T // tt,),
        in_specs=[
            pl.BlockSpec((1, 1, 2 * tt), lambda i: (i, 0, 0), memory_space=pltpu.SMEM),
            pl.BlockSpec(memory_space=pl.ANY),
            pl.BlockSpec(memory_space=pl.ANY),
        ],
        out_specs=pl.BlockSpec(memory_space=pl.ANY),
        out_shape=jax.ShapeDtypeStruct((n_rows, D_MODEL), h.dtype),
        scratch_shapes=[pltpu.SemaphoreType.DMA(())],
        input_output_aliases={2: 0},
        compiler_params=_cparams(("arbitrary",)),
        name="moe_scatter",
    )(dest3, h, xs0)


def _moe_ffn_kernel(te_ref, na_ref, xs_ref, wg_ref, wu_ref, wd_ref, ys_ref, acc_scr):
    i = pl.program_id(0)
    f = pl.program_id(1)

    @pl.when(i < na_ref[0])
    def _():
        @pl.when(f == 0)
        def _():
            acc_scr[...] = jnp.zeros_like(acc_scr)

        acc_scr[...] += _swiglu_tile(xs_ref[...].astype(BF16), wg_ref[0], wu_ref[0], wd_ref[0])

        @pl.when(f == pl.num_programs(1) - 1)
        def _():
            ys_ref[...] = acc_scr[...]

    @pl.when((i >= na_ref[0]) & (f == 0))
    def _():
        ys_ref[...] = jnp.zeros_like(ys_ref)


def _moe_ffn(tile_expert, n_active, xs, wg, wu, wd, tm, tf):
    n_rows = xs.shape[0]
    n_tiles = n_rows // tm
    nf = D_FF // tf

    def row_map(i, f, te, na):
        return (jnp.minimum(i, na[0] - 1), 0)

    def fcol(i, f, na):
        return jnp.where(i < na[0], f, nf - 1)

    grid_spec = pltpu.PrefetchScalarGridSpec(
        num_scalar_prefetch=2,
        grid=(n_tiles, nf),
        in_specs=[
            pl.BlockSpec((tm, D_MODEL), row_map),
            pl.BlockSpec((1, D_MODEL, tf), lambda i, f, te, na: (te[i], 0, fcol(i, f, na))),
            pl.BlockSpec((1, D_MODEL, tf), lambda i, f, te, na: (te[i], 0, fcol(i, f, na))),
            pl.BlockSpec((1, tf, D_MODEL), lambda i, f, te, na: (te[i], fcol(i, f, na), 0)),
        ],
        out_specs=pl.BlockSpec((tm, D_MODEL), lambda i, f, te, na: (i, 0)),
        scratch_shapes=[pltpu.VMEM((tm, D_MODEL), F32)],
    )
    return pl.pallas_call(
        _moe_ffn_kernel,
        grid_spec=grid_spec,
        out_shape=jax.ShapeDtypeStruct((n_rows, D_MODEL), F32),
        compiler_params=_cparams(("arbitrary", "arbitrary")),
        name="moe_ffn",
    )(tile_expert, n_active, xs, wg, wu, wd)


def _combine_kernel(dest_ref, x_ref, meta_ref, fw_ref, ys_hbm, o_ref, g_scr, sem, *, tc):
    def issue(t, c):
        pltpu.make_async_copy(ys_hbm.at[pl.ds(dest_ref[0, 0, 2 * t], 1)], g_scr.at[0, pl.ds(t, 1)], sem).start()
        pltpu.make_async_copy(ys_hbm.at[pl.ds(dest_ref[0, 0, 2 * t + 1], 1)], g_scr.at[1, pl.ds(t, 1)], sem).start()
        return c

    lax.fori_loop(0, tc, issue, 0)

    def drain(t, c):
        pltpu.make_async_copy(ys_hbm.at[pl.ds(0, 1)], g_scr.at[0, pl.ds(0, 1)], sem).wait()
        return c

    lax.fori_loop(0, 2 * tc, drain, 0)
    meta = meta_ref[...]
    w0 = meta[:, M_W0:M_W0 + 1]
    w1 = meta[:, M_W1:M_W1 + 1]
    x = x_ref[...] + w0 * g_scr[0] + w1 * g_scr[1]
    ms = jnp.mean(x * x, axis=-1, keepdims=True)
    o_ref[...] = x * lax.rsqrt(ms + RMS_EPS) * fw_ref[...]


def _combine(dest, x2, meta, fw, ys, tc):
    T = x2.shape[0]
    dest3 = dest.reshape(T // tc, 1, 2 * tc)
    return pl.pallas_call(
        functools.partial(_combine_kernel, tc=tc),
        grid=(T // tc,),
        in_specs=[
            pl.BlockSpec((1, 1, 2 * tc), lambda i: (i, 0, 0), memory_space=pltpu.SMEM),
            pl.BlockSpec((tc, D_MODEL), lambda i: (i, 0)),
            pl.BlockSpec((tc, LANE), lambda i: (i, 0)),
            pl.BlockSpec((1, D_MODEL), lambda i: (0, 0)),
            pl.BlockSpec(memory_space=pl.ANY),
        ],
        out_specs=pl.BlockSpec((tc, D_MODEL), lambda i: (i, 0)),
        out_shape=jax.ShapeDtypeStruct((T, D_MODEL), F32),
        scratch_shapes=[pltpu.VMEM((2, tc, D_MODEL), F32), pltpu.SemaphoreType.DMA(())],
        compiler_params=_cparams(("arbitrary",)),
        name="moe_combine",
    )(dest3, x2, meta, fw, ys)


def _moe_plan(meta, counts, tm, n_tiles):
    cnt = counts[0, :N_EXPERTS].astype(jnp.int32)
    padded = ((cnt + tm - 1) // tm) * tm
    ends = jnp.cumsum(padded)
    offs = ends - padded
    e = meta[:, M_E0:M_E1 + 1].astype(jnp.int32)
    pos = meta[:, M_P0:M_P1 + 1].astype(jnp.int32)
    dest = offs[e] + pos
    starts = jnp.arange(n_tiles, dtype=jnp.int32) * tm
    n_active = (ends[-1] // tm).astype(jnp.int32)
    te = jnp.sum(starts[:, None] >= ends[None, :], axis=-1).astype(jnp.int32)
    last_e = jnp.sum(jnp.maximum(ends[-1] - tm, 0) >= ends).astype(jnp.int32)
    te = jnp.where(starts < ends[-1], te, last_e)
    return dest, te, n_active.reshape(1)


def _moe_layer(x_mid, nw, w_router, wg, wu, wd, final_w, tiles):
    T = x_mid.shape[0]
    tm = tiles["moe_tm"]
    n_tiles = (TOP_K * T) // tm + N_EXPERTS
    h, meta, counts = _router(x_mid, nw, w_router, tiles["router_t"])
    dest, te, n_active = _moe_plan(meta, counts, tm, n_tiles)
    xs = _scatter_rows(dest, h, n_tiles * tm, tiles["scatter_t"])
    ys = _moe_ffn(te, n_active, xs, wg, wu, wd, tm, tiles["moe_tf"])
    return _combine(dest, x_mid, meta, final_w, ys, tiles["combine_t"])


def _tiles(S, T):
    return dict(
        inproj_tm=min(1024, T), inproj_tn=640,
        fox_ts=min(256, S), attn_t=min(256, S), mla_tm=min(512, T), ret_c=min(128, S),
        merge_tm=min(512, T), ffn_tm=min(1024, T), ffn_tf=512,
        router_t=min(256, T), scatter_t=min(256, T), combine_t=min(256, T),
        moe_tm=min(1024, T), moe_tf=512,
    )


def kernel(x, norm_mix_w, w_in, fox_f_bias, ret_gn_w, mla_q_norm_w, mla_kv_norm_w, mla_w_uq, mla_w_uk, mla_w_uv, w_br_fox, w_br_ret, w_br_mla, w_out, norm_ffn_w, dense_w_gate, dense_w_up, dense_w_down, moe_w_router, moe_w_gate, moe_w_up, moe_w_down, final_norm_w):
    B, S, D = x.shape
    assert D == D_MODEL and w_in.shape[0] == DEPTH == 2
    T = B * S
    tl = _tiles(S, T)
    row = lambda a: a.reshape(1, -1).astype(F32)
    ret_tabs = _ret_tables(S, tl["ret_c"])
    mla_tabs = _mla_tables(S)
    x2 = x.reshape(T, D)
    for l in range(DEPTH):
        w_all, w_ff = _build_inproj_weights(w_in[l])
        proj, ff32 = _inproj(x2, row(norm_mix_w[l]), w_all, w_ff, tl["inproj_tm"], tl["inproj_tn"])
        qx, kx = _fox_prep(proj, ff32, fox_f_bias[l], B, S, tl["fox_ts"])
        o_fox = _attention(qx, kx, proj, U_FV, B, S, tl["attn_t"], FOX_RANGES, "fox_attn")
        o_ret = _retention(proj, row(ret_gn_w[l]), ret_tabs, B, S, tl["ret_c"])
        wq, wqr, wk, wv = _mla_weights(mla_w_uq[l], mla_w_uk[l], mla_w_uv[l])
        mq, mk, mv = _mla_prep(proj, row(mla_q_norm_w[l]), row(mla_kv_norm_w[l]), wq, wqr, wk, wv,
                               mla_tabs, B, S, tl["mla_tm"])
        o_mla = _attention(mq, mk, mv, 0, B, S, tl["attn_t"], MLA_RANGES, "mla_attn")
        x_mid, h2 = _merge(o_fox, o_ret, o_mla, proj, x2,
                           w_br_fox[l].astype(BF16), w_br_ret[l].astype(BF16), w_br_mla[l].astype(BF16),
                           w_out[l].astype(BF16), row(norm_ffn_w[l]), tl["merge_tm"])
        if l % 2 == 0:
            i = l // 2
            x2 = _dense_ffn(h2, x_mid, dense_w_gate[i].astype(BF16), dense_w_up[i].astype(BF16),
                            dense_w_down[i].astype(BF16), tl["ffn_tm"], tl["ffn_tf"])
        else:
            i = l // 2
            x2 = _moe_layer(x_mid, row(norm_ffn_w[l]), moe_w_router[i], moe_w_gate[i].astype(BF16),
                            moe_w_up[i].astype(BF16), moe_w_down[i].astype(BF16), row(final_norm_w), tl)
    return x2.reshape(B, S, D)
```

```python
import functools

import numpy as np
import jax
import jax.numpy as jnp
from jax import lax
from jax.experimental import pallas as pl
from jax.experimental.pallas import tpu as pltpu

F32 = jnp.float32
BF16 = jnp.bfloat16

D_MODEL = 1024
DEPTH = 2
FOX_HEADS = 8
FOX_HEAD_DIM = 64
FOX_W = FOX_HEADS * FOX_HEAD_DIM
RET_HEADS = 8
RET_QK_DIM = 64
RET_V_DIM = 128
RET_QK_W = RET_HEADS * RET_QK_DIM
RET_V_W = RET_HEADS * RET_V_DIM
MLA_HEADS = 8
MLA_Q_RANK = 384
MLA_KV_RANK = 256
MLA_NOPE_DIM = 64
MLA_ROPE_DIM = 32
MLA_V_DIM = 64
MLA_QK_DIM = MLA_NOPE_DIM + MLA_ROPE_DIM
MLA_V_W = MLA_HEADS * MLA_V_DIM
N_BRANCHES = 3
ROPE_THETA = 10000.0
RMS_EPS = 1e-6
D_FF = 3584
N_EXPERTS = 8
TOP_K = 2
IN_SPLITS = (FOX_W, FOX_W, FOX_W, FOX_HEADS,
             RET_QK_W, RET_QK_W, RET_V_W, RET_V_W,
             MLA_Q_RANK, MLA_KV_RANK, MLA_ROPE_DIM,
             N_BRANCHES * D_MODEL)

LANE = 128
NEG = -1e30
VMEM_LIMIT = 56 * 1024 * 1024

U_FQ, U_FK, U_FV = 0, 4, 8
U_RQ, U_RQR, U_RK, U_RKR = 12, 16, 20, 24
U_CKV, U_KR, U_KRR = 28, 30, 31
U_RV, U_RG, U_GL, U_CQ = 32, 40, 48, 72
N_PROJ = 75 * LANE
FOX_EXT_STRIDE = 8
PAIRS = 4
QK_W = 2 * LANE
LOG2E = 1.4426950408889634


def _cparams(sem, vmem=VMEM_LIMIT):
    return pltpu.CompilerParams(dimension_semantics=sem, vmem_limit_bytes=vmem)


def _sigmoid(x):
    return 1.0 / (1.0 + jnp.exp(-x))


def _lane_mask(shape, ranges):
    lane = lax.broadcasted_iota(jnp.int32, shape, len(shape) - 1)
    m = None
    for a, b in ranges:
        r = (lane >= a) & (lane < b)
        m = r if m is None else (m | r)
    return m


def _inproj_kernel(x_ref, nw_ref, w_ref, wff_ref, out_ref, ff_ref, h_scr):
    @pl.when(pl.program_id(1) == 0)
    def _():
        x = x_ref[...]
        ms = jnp.mean(x * x, axis=-1, keepdims=True)
        h = (x * lax.rsqrt(ms + RMS_EPS) * nw_ref[...]).astype(BF16)
        h_scr[...] = h
        ff_ref[...] = jnp.dot(h, wff_ref[...], preferred_element_type=F32)

    out_ref[...] = jnp.dot(h_scr[...], w_ref[...], preferred_element_type=F32).astype(out_ref.dtype)


def _inproj(x2, nw, w_all, w_ff, tm, tn):
    T = x2.shape[0]
    return pl.pallas_call(
        _inproj_kernel,
        grid=(T // tm, N_PROJ // tn),
        in_specs=[
            pl.BlockSpec((tm, D_MODEL), lambda i, j: (i, 0)),
            pl.BlockSpec((1, D_MODEL), lambda i, j: (0, 0)),
            pl.BlockSpec((D_MODEL, tn), lambda i, j: (0, j)),
            pl.BlockSpec((D_MODEL, LANE), lambda i, j: (0, 0)),
        ],
        out_specs=[
            pl.BlockSpec((tm, tn), lambda i, j: (i, j)),
            pl.BlockSpec((tm, LANE), lambda i, j: (i, 0)),
        ],
        out_shape=[
            jax.ShapeDtypeStruct((T, N_PROJ), BF16),
            jax.ShapeDtypeStruct((T, LANE), F32),
        ],
        scratch_shapes=[pltpu.VMEM((tm, D_MODEL), BF16)],
        compiler_params=_cparams(("parallel", "arbitrary")),
        name="inproj",
    )(x2, nw, w_all, w_ff)


def _rot_cols(w, heads, d):
    k = w.shape[0]
    w4 = w.reshape(k, heads, 2, d // 2)
    return jnp.concatenate([-w4[:, :, 1], w4[:, :, 0]], axis=-1).reshape(k, heads * d)


def _build_inproj_weights(w_in):
    split_at = [int(i) for i in np.cumsum(IN_SPLITS)[:-1]]
    fq, fk, fv, ff, rq, rk, rv, rg, cq, ckv, kr, gl = jnp.split(w_in, split_at, axis=-1)
    z64 = jnp.zeros((D_MODEL, 64), F32)
    z32 = jnp.zeros((D_MODEL, 32), F32)
    kr128 = jnp.concatenate([z64, kr, z32], axis=-1)
    krr128 = jnp.concatenate([z64, _rot_cols(kr, 1, MLA_ROPE_DIM), z32], axis=-1)
    w_all = jnp.concatenate(
        [fq, fk, fv, rq, _rot_cols(rq, RET_HEADS, RET_QK_DIM), rk, _rot_cols(rk, RET_HEADS, RET_QK_DIM),
         ckv, kr128, krr128, rv, rg, gl, cq], axis=-1).astype(BF16)
    w_ff = jnp.pad(ff, ((0, 0), (0, LANE - FOX_HEADS))).astype(BF16)
    return w_all, w_ff


def _split3(x):
    hi = x.astype(BF16)
    r1 = x - hi.astype(F32)
    mid = r1.astype(BF16)
    lo = (r1 - mid.astype(F32)).astype(BF16)
    return hi, mid, lo


def _store_t(dst_ref, idx, x):
    dst_ref[idx] = x.T.astype(dst_ref.dtype)


def _fox_prep_kernel(fq_ref, fk_ref, fv_ref, ff_ref, bias_ref, tri_ref, pq_ref, pk_ref, cq_ref, ck_ref,
                     qt_ref, kx_ref, vt_ref, carry_scr):
    @pl.when(pl.program_id(1) == 0)
    def _():
        carry_scr[...] = jnp.zeros_like(carry_scr)

    ts = ff_ref.shape[0]
    z = ff_ref[...] + bias_ref[...]
    logf = jnp.minimum(z, 0.0) - jnp.log(1.0 + jnp.exp(-jnp.abs(z)))
    tri = tri_ref[...]
    cum = carry_scr[0:1, :]
    for part in _split3(logf):
        cum = cum + jnp.dot(tri, part, preferred_element_type=F32)
    carry_scr[0:1, :] = cum[ts - 1:ts, :]

    eq = cq_ref[...]
    ek = ck_ref[...]
    for j, part in enumerate(_split3(cum * LOG2E)):
        eq = eq + jnp.dot(part, pq_ref[j * LANE:(j + 1) * LANE, :], preferred_element_type=F32)
        ek = ek + jnp.dot(part, pk_ref[j * LANE:(j + 1) * LANE, :], preferred_element_type=F32)
    scale = FOX_HEAD_DIM ** -0.5 * LOG2E
    for p in range(FOX_HEADS // 2):
        src = slice(p * LANE, (p + 1) * LANE)
        _store_t(qt_ref, (0, slice(2 * p * LANE, (2 * p + 1) * LANE)), fq_ref[:, src].astype(F32) * scale)
        _store_t(qt_ref, (0, slice((2 * p + 1) * LANE, (2 * p + 2) * LANE)), eq[:, src])
        kx_ref[:, 2 * p * LANE:(2 * p + 1) * LANE] = fk_ref[:, src]
        kx_ref[:, (2 * p + 1) * LANE:(2 * p + 2) * LANE] = ek[:, src].astype(BF16)
        _store_t(vt_ref, (0, 0, src), fv_ref[:, src].astype(F32))


def _fox_prep_consts(ts):
    tri = np.tril(np.ones((ts, ts), np.float32))
    pq = np.zeros((3 * LANE, FOX_W), np.float32)
    pk = np.zeros((3 * LANE, FOX_W), np.float32)
    cq = np.zeros((1, FOX_W), np.float32)
    ck = np.zeros((1, FOX_W), np.float32)
    for h in range(FOX_HEADS):
        base = LANE * (h // 2) + FOX_EXT_STRIDE * (h % 2)
        for j in range(3):
            pq[j * LANE + h, base + j] = 1.0
            ck[0, base + j] = 1.0
            cq[0, base + 3 + j] = 1.0
            pk[j * LANE + h, base + 3 + j] = -1.0
    return (jnp.asarray(tri, BF16), jnp.asarray(pq, BF16), jnp.asarray(pk, BF16),
            jnp.asarray(cq), jnp.asarray(ck))


def _attn_operand_specs(B, S, t):
    T = B * S
    ns = S // t
    specs = [
        pl.BlockSpec((1, PAIRS * QK_W, t), lambda b, s: (b, 0, s)),
        pl.BlockSpec((t, PAIRS * QK_W), lambda b, s: (b * ns + s, 0)),
        pl.BlockSpec((1, 1, PAIRS * LANE, t), lambda b, s: (b, s, 0, 0)),
    ]
    shapes = [
        jax.ShapeDtypeStruct((B, PAIRS * QK_W, S), BF16),
        jax.ShapeDtypeStruct((T, PAIRS * QK_W), BF16),
        jax.ShapeDtypeStruct((B, ns, PAIRS * LANE, t), BF16),
    ]
    return specs, shapes


def _fox_prep(proj, ff32, bias, B, S, ts):
    ns = S // ts
    tri, pq, pk, cq, ck = _fox_prep_consts(ts)
    bias128 = jnp.pad(bias.astype(F32), (0, LANE - FOX_HEADS)).reshape(1, LANE)
    const = lambda shape: pl.BlockSpec(shape, lambda b, s: (0, 0))
    out_specs, out_shape = _attn_operand_specs(B, S, ts)
    return pl.pallas_call(
        _fox_prep_kernel,
        grid=(B, ns),
        in_specs=[
            pl.BlockSpec((ts, FOX_W), lambda b, s: (b * ns + s, U_FQ // 4)),
            pl.BlockSpec((ts, FOX_W), lambda b, s: (b * ns + s, U_FK // 4)),
            pl.BlockSpec((ts, FOX_W), lambda b, s: (b * ns + s, U_FV // 4)),
            pl.BlockSpec((ts, LANE), lambda b, s: (b * ns + s, 0)),
            const((1, LANE)), const((ts, ts)), const((3 * LANE, FOX_W)), const((3 * LANE, FOX_W)),
            const((1, FOX_W)), const((1, FOX_W)),
        ],
        out_specs=out_specs,
        out_shape=out_shape,
        scratch_shapes=[pltpu.VMEM((8, LANE), F32)],
        compiler_params=_cparams(("parallel", "arbitrary")),
        name="fox_prep",
    )(proj, proj, proj, ff32, bias128, tri, pq, pk, cq, ck)


def _row_mask(shape, ranges):
    row = lax.broadcasted_iota(jnp.int32, shape, 0)
    m = None
    for a, b in ranges:
        r = (row >= a) & (row < b)
        m = r if m is None else (m | r)
    return m


def _col_reduce(x, op, reduce_fn, parts=8):
    c = x.shape[0] // parts
    pieces = [x[i * c:(i + 1) * c] for i in range(parts)]
    while len(pieces) > 1:
        pieces = [op(pieces[i], pieces[i + 1]) for i in range(0, len(pieces), 2)]
    return reduce_fn(pieces[0], axis=0, keepdims=True)


def _attn_kernel(qt_ref, k_ref, vt_ref, o_ref, qh_scr, sa_scr, sb_scr, acc_scr, *, tk, ranges):
    qi = pl.program_id(2)
    qt = qt_ref[0]
    tq = qt.shape[-1]
    for h in range(2):
        qh_scr[h] = jnp.where(_row_mask(qt.shape, ranges[h]), qt, jnp.zeros_like(qt))
    acc_scr[...] = jnp.zeros_like(acc_scr)
    hd = LANE // 2

    def scores(kt, s_scr):
        off = pl.multiple_of(kt * tk, tk)
        k = k_ref[pl.ds(off, tk), :]
        for h in range(2):
            s_scr[h] = jnp.dot(k, qh_scr[h], preferred_element_type=F32)

    def softmax_pv(kt, s_scr, carry, diag):
        out = []
        for h in range(2):
            m_prev, l_prev = carry[h]
            s = s_scr[h]
            if diag is not None:
                key = lax.broadcasted_iota(jnp.int32, s.shape, 0) + diag * tk
                qry = lax.broadcasted_iota(jnp.int32, s.shape, 1)
                s = jnp.where(key <= qry, s, NEG)
            m_new = jnp.maximum(m_prev, _col_reduce(s, jnp.maximum, jnp.max))
            alpha = jnp.exp2(m_prev - m_new)
            p = jnp.exp2(s - m_new)
            l_new = alpha * l_prev + _col_reduce(p, jnp.add, jnp.sum)
            vt = vt_ref[0, kt, h * hd:(h + 1) * hd, :]
            acc_scr[h] = alpha * acc_scr[h] + jnp.dot(vt, p.astype(BF16), preferred_element_type=F32)
            out.append((m_new, l_new))
        return tuple(out)

    def body(j, carry):
        kt = 2 * j
        scores(kt + 1, sb_scr)
        carry = softmax_pv(kt, sa_scr, carry, None)
        scores(kt + 2, sa_scr)
        return softmax_pv(kt + 1, sb_scr, carry, None)

    scores(0, sa_scr)
    init = tuple((jnp.full((1, tq), NEG, F32), jnp.zeros((1, tq), F32)) for _ in range(2))
    carry = lax.fori_loop(0, qi, body, init)
    scores(2 * qi + 1, sb_scr)
    carry = softmax_pv(2 * qi, sa_scr, carry, 0)
    carry = softmax_pv(2 * qi + 1, sb_scr, carry, 1)
    ot = jnp.concatenate([acc_scr[h] / carry[h][1] for h in range(2)], axis=0)
    o_ref[...] = ot.T.astype(o_ref.dtype)


def _attention(qt, kx, vt, B, S, tk, ranges, name):
    T = B * S
    tq = 2 * tk
    nq = S // tq
    kern = functools.partial(_attn_kernel, tk=tk, ranges=ranges)
    return pl.pallas_call(
        kern,
        grid=(B, PAIRS, nq),
        in_specs=[
            pl.BlockSpec((1, QK_W, tq), lambda b, p, i: (b, p, i)),
            pl.BlockSpec((S, QK_W), lambda b, p, i: (b, p)),
            pl.BlockSpec((1, S // tk, LANE, tk), lambda b, p, i: (b, 0, p, 0)),
        ],
        out_specs=pl.BlockSpec((tq, LANE), lambda b, p, i: (b * nq + i, p)),
        out_shape=jax.ShapeDtypeStruct((T, PAIRS * LANE), BF16),
        scratch_shapes=[
            pltpu.VMEM((2, QK_W, tq), BF16),
            pltpu.VMEM((2, tk, tq), F32),
            pltpu.VMEM((2, tk, tq), F32),
            pltpu.VMEM((2, LANE // 2, tq), F32),
        ],
        compiler_params=_cparams(("parallel", "parallel", "arbitrary")),
        name=name,
    )(qt, kx, vt)


_E0 = LANE
FOX_RANGES = (((0, 64), (_E0, _E0 + FOX_EXT_STRIDE)),
              ((64, 128), (_E0 + FOX_EXT_STRIDE, _E0 + 2 * FOX_EXT_STRIDE)))
MLA_RANGES = (((0, LANE),), ((LANE, 2 * LANE),))


def _mla_prep_kernel(cq_ref, ckv_ref, kr_ref, krr_ref, qnw_ref, kvnw_ref, wq_ref, wqr_ref, wk_ref, wv_ref,
                     cq_tab, sq_tab, ck_tab, sk_tab, qt_ref, k_ref, vt_ref):
    def norm(x_ref, w_ref):
        x = x_ref[...].astype(F32)
        ms = jnp.mean(x * x, axis=-1, keepdims=True)
        return (x * lax.rsqrt(ms + RMS_EPS) * w_ref[...]).astype(BF16)

    c_q = norm(cq_ref, qnw_ref)
    c_kv = norm(ckv_ref, kvnw_ref)
    tile8 = lambda a: jnp.concatenate([a] * MLA_HEADS, axis=-1)
    q = (jnp.dot(c_q, wq_ref[...], preferred_element_type=F32) * tile8(cq_tab[...])
         + jnp.dot(c_q, wqr_ref[...], preferred_element_type=F32) * tile8(sq_tab[...]))
    for c in range(MLA_HEADS):
        _store_t(qt_ref, (0, slice(c * LANE, (c + 1) * LANE)), q[:, c * LANE:(c + 1) * LANE])
    k_rope = kr_ref[...].astype(F32) * ck_tab[...] + krr_ref[...].astype(F32) * sk_tab[...]
    k = jnp.dot(c_kv, wk_ref[...], preferred_element_type=F32) + tile8(k_rope)
    k_ref[...] = k.astype(BF16)
    v = jnp.dot(c_kv, wv_ref[...], preferred_element_type=F32)
    for p in range(PAIRS):
        _store_t(vt_ref, (0, 0, slice(p * LANE, (p + 1) * LANE)), v[:, p * LANE:(p + 1) * LANE])


def _mla_weights(w_uq, w_uk, w_uv):
    q3 = w_uq.reshape(MLA_Q_RANK, MLA_HEADS, MLA_QK_DIM)
    nope, ropep = q3[..., :MLA_NOPE_DIM], q3[..., MLA_NOPE_DIM:]
    half = MLA_ROPE_DIM // 2
    rot = jnp.concatenate([-ropep[..., half:], ropep[..., :half]], axis=-1)
    z32 = jnp.zeros((MLA_Q_RANK, MLA_HEADS, LANE - MLA_QK_DIM), F32)
    z64 = jnp.zeros((MLA_Q_RANK, MLA_HEADS, MLA_NOPE_DIM), F32)
    wq = jnp.concatenate([nope, ropep, z32], axis=-1).reshape(MLA_Q_RANK, MLA_HEADS * LANE)
    wqr = jnp.concatenate([z64, rot, z32], axis=-1).reshape(MLA_Q_RANK, MLA_HEADS * LANE)
    k3 = w_uk.reshape(MLA_KV_RANK, MLA_HEADS, MLA_NOPE_DIM)
    wk = jnp.concatenate([k3, jnp.zeros_like(k3)], axis=-1).reshape(MLA_KV_RANK, MLA_HEADS * LANE)
    return wq.astype(BF16), wqr.astype(BF16), wk.astype(BF16), w_uv.astype(BF16)


def _mla_tables(S):
    pos = jnp.arange(S, dtype=F32)
    inv = ROPE_THETA ** (-jnp.arange(0, MLA_ROPE_DIM, 2, dtype=F32) / MLA_ROPE_DIM)
    ang = pos[:, None] * inv[None, :]
    cos, sin = jnp.cos(ang), jnp.sin(ang)
    ones = jnp.ones((S, MLA_NOPE_DIM), F32)
    z64 = jnp.zeros((S, MLA_NOPE_DIM), F32)
    z32 = jnp.zeros((S, LANE - MLA_QK_DIM), F32)
    ctab = jnp.concatenate([ones, cos, cos, z32], axis=-1)
    stab = jnp.concatenate([z64, sin, sin, z32], axis=-1)
    scale = MLA_QK_DIM ** -0.5 * LOG2E
    return ctab * scale, stab * scale, ctab, stab


def _mla_prep(proj, qnw, kvnw, wq, wqr, wk, wv, tabs, B, S, tm):
    ns = S // tm
    HW = MLA_HEADS * LANE
    const = lambda shape: pl.BlockSpec(shape, lambda b, s: (0, 0))
    tab = pl.BlockSpec((tm, LANE), lambda b, s: (s, 0))
    out_specs, out_shape = _attn_operand_specs(B, S, tm)
    return pl.pallas_call(
        _mla_prep_kernel,
        grid=(B, ns),
        in_specs=[
            pl.BlockSpec((tm, MLA_Q_RANK), lambda b, s: (b * ns + s, U_CQ // 3)),
            pl.BlockSpec((tm, MLA_KV_RANK), lambda b, s: (b * ns + s, U_CKV // 2)),
            pl.BlockSpec((tm, LANE), lambda b, s: (b * ns + s, U_KR)),
            pl.BlockSpec((tm, LANE), lambda b, s: (b * ns + s, U_KRR)),
            const((1, MLA_Q_RANK)), const((1, MLA_KV_RANK)),
            const((MLA_Q_RANK, HW)), const((MLA_Q_RANK, HW)), const((MLA_KV_RANK, HW)),
            const((MLA_KV_RANK, MLA_V_W)),
            tab, tab, tab, tab,
        ],
        out_specs=out_specs,
        out_shape=out_shape,
        compiler_params=_cparams(("parallel", "parallel")),
        name="mla_prep",
    )(proj, proj, proj, proj, qnw, kvnw, wq, wqr, wk, wv, *tabs)


def _ret_kernel(rq_ref, rqr_ref, rk_ref, rkr_ref, rv_ref, rg_ref, cos_ref, sin_ref,
                din_ref, xi_ref, zeta_ref, gch_ref, gnw_ref, o_ref, r_scr):
    @pl.when(pl.program_id(1) == 0)
    def _():
        r_scr[...] = jnp.zeros_like(r_scr)

    tile4 = lambda a: jnp.concatenate([a] * (RET_QK_W // LANE), axis=-1)
    cos4 = tile4(cos_ref[...])
    sin4 = tile4(sin_ref[...])
    q = rq_ref[...].astype(F32) * cos4 + rqr_ref[...].astype(F32) * sin4
    k = (rk_ref[...].astype(F32) * cos4 + rkr_ref[...].astype(F32) * sin4) * (RET_QK_DIM ** -0.5)
    for p in range(RET_HEADS // 2):
        q2 = q[:, p * LANE:(p + 1) * LANE]
        k2 = k[:, p * LANE:(p + 1) * LANE]
        k2b = k2.astype(BF16)
        kz2 = (k2 * zeta_ref[p]).astype(BF16)
        r2 = r_scr[p]
        r2b = r2.astype(BF16)
        new_r = gch_ref[p] * r2
        for j in range(2):
            h = 2 * p + j
            half = ((j * RET_QK_DIM, (j + 1) * RET_QK_DIM),)
            qm = jnp.where(_lane_mask(q2.shape, half), q2, 0.0).astype(BF16)
            inner = lax.dot_general(qm, k2b, (((1,), (1,)), ((), ())), preferred_element_type=F32)
            inner = inner * din_ref[h]
            vh = rv_ref[:, h * LANE:(h + 1) * LANE]
            o = (jnp.dot(inner.astype(BF16), vh, preferred_element_type=F32)
                 + jnp.dot(qm, r2b, preferred_element_type=F32) * xi_ref[h])
            upd = lax.dot_general(kz2, vh, (((0,), (0,)), ((), ())), preferred_element_type=F32)
            rows = lax.broadcasted_iota(jnp.int32, upd.shape, 0)
            new_r = new_r + jnp.where((rows >= half[0][0]) & (rows < half[0][1]), upd, 0.0)
            mu = jnp.mean(o, axis=-1, keepdims=True)
            d = o - mu
            var = jnp.mean(d * d, axis=-1, keepdims=True)
            on = d * lax.rsqrt(var + RMS_EPS) * gnw_ref[:, h * LANE:(h + 1) * LANE]
            g = rg_ref[:, h * LANE:(h + 1) * LANE].astype(F32)
            o_ref[:, h * LANE:(h + 1) * LANE] = (g * _sigmoid(g) * on).astype(o_ref.dtype)
        r_scr[p] = new_r


def _ret_tables(S, C):
    pos = jnp.arange(S, dtype=F32)
    inv = ROPE_THETA ** (-jnp.arange(0, RET_QK_DIM, 2, dtype=F32) / RET_QK_DIM)
    ang = pos[:, None] * inv[None, :]
    cos = jnp.concatenate([jnp.cos(ang)] * 4, axis=-1)
    sin = jnp.concatenate([jnp.sin(ang)] * 4, axis=-1)
    gammas = 1.0 - 2.0 ** (-5.0 - jnp.arange(RET_HEADS, dtype=F32))
    log_g = jnp.log(gammas)
    j = jnp.arange(C, dtype=F32)
    diff = j[:, None] - j[None, :]
    din = jnp.where(diff[None] >= 0, jnp.exp(jnp.maximum(diff, 0.0)[None] * log_g[:, None, None]), 0.0)
    xi = jnp.exp((j[None, :] + 1.0) * log_g[:, None])
    zeta = jnp.exp((C - 1.0 - j[None, :]) * log_g[:, None])
    gch = jnp.exp(C * log_g)
    xi_t = jnp.broadcast_to(xi[:, :, None], (RET_HEADS, C, LANE))
    zeta_p = jnp.repeat(zeta.reshape(RET_HEADS // 2, 2, C).transpose(0, 2, 1), RET_QK_DIM, axis=-1)
    gch_p = jnp.broadcast_to(jnp.repeat(gch.reshape(RET_HEADS // 2, 2), RET_QK_DIM, axis=-1)[:, :, None],
                             (RET_HEADS // 2, LANE, LANE))
    return cos, sin, din, xi_t, zeta_p, gch_p


def _retention(proj, gnw, tabs, B, S, C):
    T = B * S
    nc = S // C
    cos, sin, din, xi, zeta, gch = tabs
    blk = lambda w, unit: pl.BlockSpec((C, w), lambda b, c: (b * nc + c, unit * LANE // w))
    full = lambda a: pl.BlockSpec(a.shape, lambda b, c: (0,) * a.ndim)
    return pl.pallas_call(
        _ret_kernel,
        grid=(B, nc),
        in_specs=[
            blk(RET_QK_W, U_RQ), blk(RET_QK_W, U_RQR), blk(RET_QK_W, U_RK), blk(RET_QK_W, U_RKR),
            blk(RET_V_W, U_RV), blk(RET_V_W, U_RG),
            pl.BlockSpec((C, LANE), lambda b, c: (c, 0)),
            pl.BlockSpec((C, LANE), lambda b, c: (c, 0)),
            full(din), full(xi), full(zeta), full(gch), full(gnw),
        ],
        out_specs=pl.BlockSpec((C, RET_V_W), lambda b, c: (b * nc + c, 0)),
        out_shape=jax.ShapeDtypeStruct((T, RET_V_W), BF16),
        scratch_shapes=[pltpu.VMEM((RET_HEADS // 2, LANE, LANE), F32)],
        compiler_params=_cparams(("parallel", "arbitrary")),
        name="retention",
    )(proj, proj, proj, proj, proj, proj, cos, sin, din, xi, zeta, gch, gnw)


def _merge_kernel(of_ref, or_ref, om_ref, g0_ref, g1_ref, g2_ref, x_ref,
                  wf_ref, wr_ref, wm_ref, wo_ref, nw_ref, xo_ref, h_ref):
    merged = (_sigmoid(g0_ref[...].astype(F32)) * jnp.dot(of_ref[...], wf_ref[...], preferred_element_type=F32)
              + _sigmoid(g1_ref[...].astype(F32)) * jnp.dot(or_ref[...], wr_ref[...], preferred_element_type=F32)
              + _sigmoid(g2_ref[...].astype(F32)) * jnp.dot(om_ref[...], wm_ref[...], preferred_element_type=F32))
    x = x_ref[...] + jnp.dot(merged.astype(BF16), wo_ref[...], preferred_element_type=F32)
    xo_ref[...] = x
    ms = jnp.mean(x * x, axis=-1, keepdims=True)
    h_ref[...] = (x * lax.rsqrt(ms + RMS_EPS) * nw_ref[...]).astype(h_ref.dtype)


def _merge(o_fox, o_ret, o_mla, proj, x2, wf, wr, wm, wo, nw, tm):
    T = x2.shape[0]
    row = lambda w, cb=0: pl.BlockSpec((tm, w), lambda i: (i, cb))
    const = lambda a: pl.BlockSpec(a.shape, lambda i: (0, 0))
    g_unit = U_GL // 8
    return pl.pallas_call(
        _merge_kernel,
        grid=(T // tm,),
        in_specs=[
            row(FOX_W), row(RET_V_W), row(MLA_V_W),
            row(D_MODEL, g_unit), row(D_MODEL, g_unit + 1), row(D_MODEL, g_unit + 2),
            row(D_MODEL),
            const(wf), const(wr), const(wm), const(wo), const(nw),
        ],
        out_specs=[row(D_MODEL), row(D_MODEL)],
        out_shape=[jax.ShapeDtypeStruct((T, D_MODEL), F32), jax.ShapeDtypeStruct((T, D_MODEL), BF16)],
        compiler_params=_cparams(("parallel",)),
        name="merge",
    )(o_fox, o_ret, o_mla, proj, proj, proj, x2, wf, wr, wm, wo, nw)


def _swiglu_tile(h, wg, wu, wd):
    a = jnp.dot(h, wg, preferred_element_type=F32)
    b = jnp.dot(h, wu, preferred_element_type=F32)
    return jnp.dot((a * _sigmoid(a) * b).astype(BF16), wd, preferred_element_type=F32)


def _dense_ffn_kernel(h_ref, wg_ref, wu_ref, wd_ref, x_ref, o_ref, acc_scr):
    f = pl.program_id(1)

    @pl.when(f == 0)
    def _():
        acc_scr[...] = jnp.zeros_like(acc_scr)

    acc_scr[...] += _swiglu_tile(h_ref[...], wg_ref[...], wu_ref[...], wd_ref[...])

    @pl.when(f == pl.num_programs(1) - 1)
    def _():
        o_ref[...] = x_ref[...] + acc_scr[...]


def _dense_ffn(h, x2, wg, wu, wd, tm, tf):
    T = x2.shape[0]
    return pl.pallas_call(
        _dense_ffn_kernel,
        grid=(T // tm, D_FF // tf),
        in_specs=[
            pl.BlockSpec((tm, D_MODEL), lambda i, f: (i, 0)),
            pl.BlockSpec((D_MODEL, tf), lambda i, f: (0, f)),
            pl.BlockSpec((D_MODEL, tf), lambda i, f: (0, f)),
            pl.BlockSpec((tf, D_MODEL), lambda i, f: (f, 0)),
            pl.BlockSpec((tm, D_MODEL), lambda i, f: (i, 0)),
        ],
        out_specs=pl.BlockSpec((tm, D_MODEL), lambda i, f: (i, 0)),
        out_shape=jax.ShapeDtypeStruct((T, D_MODEL), F32),
        scratch_shapes=[pltpu.VMEM((tm, D_MODEL), F32)],
        compiler_params=_cparams(("parallel", "arbitrary")),
        name="dense_ffn",
    )(h, wg, wu, wd, x2)


M_E0, M_E1, M_P0, M_P1, M_W0, M_W1 = 0, 1, 2, 3, 4, 5


def _split2(x):
    hi = x.astype(BF16)
    lo = (x - hi.astype(F32)).astype(BF16)
    return hi, lo


def _router_kernel(x_ref, nw_ref, whi_ref, wlo_ref, tri_ref, h_ref, meta_ref, cnt_ref, carry_scr):
    @pl.when(pl.program_id(0) == 0)
    def _():
        carry_scr[...] = jnp.zeros_like(carry_scr)

    x = x_ref[...]
    ms = jnp.mean(x * x, axis=-1, keepdims=True)
    h = x * lax.rsqrt(ms + RMS_EPS) * nw_ref[...]
    h_ref[...] = h
    hhi, hlo = _split2(h)
    logits = (jnp.dot(hhi, whi_ref[...], preferred_element_type=F32)
              + jnp.dot(hhi, wlo_ref[...], preferred_element_type=F32)
              + jnp.dot(hlo, whi_ref[...], preferred_element_type=F32))
    lane = lax.broadcasted_iota(jnp.int32, logits.shape, 1)
    lanef = lane.astype(F32)
    logits = jnp.where(lane < N_EXPERTS, logits, NEG)
    m0 = jnp.max(logits, axis=-1, keepdims=True)
    e0 = jnp.min(jnp.where(logits == m0, lanef, float(LANE)), axis=-1, keepdims=True)
    oh0 = lanef == e0
    rest = jnp.where(oh0, NEG, logits)
    m1 = jnp.max(rest, axis=-1, keepdims=True)
    e1 = jnp.min(jnp.where(rest == m1, lanef, float(LANE)), axis=-1, keepdims=True)
    oh1 = lanef == e1
    z = jnp.exp(m1 - m0)
    w0 = 1.0 / (1.0 + z)
    w1 = z / (1.0 + z)
    both = jnp.where(oh0 | oh1, 1.0, 0.0)
    before = carry_scr[0:1, :] + jnp.dot(tri_ref[...], both.astype(BF16), preferred_element_type=F32)
    p0 = jnp.sum(jnp.where(oh0, before, 0.0), axis=-1, keepdims=True)
    p1 = jnp.sum(jnp.where(oh1, before, 0.0), axis=-1, keepdims=True)
    total = carry_scr[0:1, :] + jnp.sum(both, axis=0, keepdims=True)
    carry_scr[0:1, :] = total
    cnt_ref[...] = jnp.broadcast_to(total, cnt_ref.shape)
    meta = jnp.zeros(logits.shape, F32)
    for idx, val in ((M_E0, e0), (M_E1, e1), (M_P0, p0), (M_P1, p1), (M_W0, w0), (M_W1, w1)):
        meta = jnp.where(lane == idx, val, meta)
    meta_ref[...] = meta


def _router(x2, nw, w_router, tr):
    T = x2.shape[0]
    wr = jnp.pad(w_router.astype(F32), ((0, 0), (0, LANE - N_EXPERTS)))
    whi = wr.astype(BF16)
    wlo = (wr - whi.astype(F32)).astype(BF16)
    tri = jnp.asarray(np.tril(np.ones((tr, tr), np.float32), -1), BF16)
    const = lambda a: pl.BlockSpec(a.shape, lambda i: (0, 0))
    return pl.pallas_call(
        _router_kernel,
        grid=(T // tr,),
        in_specs=[pl.BlockSpec((tr, D_MODEL), lambda i: (i, 0)), const(nw), const(whi), const(wlo), const(tri)],
        out_specs=[
            pl.BlockSpec((tr, D_MODEL), lambda i: (i, 0)),
            pl.BlockSpec((tr, LANE), lambda i: (i, 0)),
            pl.BlockSpec((8, LANE), lambda i: (0, 0)),
        ],
        out_shape=[
            jax.ShapeDtypeStruct((T, D_MODEL), F32),
            jax.ShapeDtypeStruct((T, LANE), F32),
            jax.ShapeDtypeStruct((8, LANE), F32),
        ],
        scratch_shapes=[pltpu.VMEM((8, LANE), F32)],
        compiler_params=_cparams(("arbitrary",)),
        name="router",
    )(x2, nw, whi, wlo, tri)


def _scatter_kernel(dest_ref, h_ref, xs_in, xs_hbm, sem, *, tt):
    del xs_in

    def row_copy(t, d_row):
        return pltpu.make_async_copy(h_ref.at[pl.ds(t, 1)], xs_hbm.at[pl.ds(d_row, 1)], sem)

    def issue(t, c):
        row_copy(t, dest_ref[0, 0, 2 * t]).start(priority=0)
        row_copy(t, dest_ref[0, 0, 2 * t + 1]).start(priority=1)
        return c

    lax.fori_loop(0, tt, issue, 0)

    def drain(t, c):
        row_copy(0, 0).wait()
        return c

    lax.fori_loop(0, 2 * tt, drain, 0)


def _scatter_rows(dest, h, n_rows, tt):
    T = h.shape[0]
    xs0 = jnp.zeros((n_rows, D_MODEL), h.dtype)
    dest3 = dest.reshape(T // tt, 1, 2 * tt)
    return pl.pallas_call(
        functools.partial(_scatter_kernel, tt=tt),
        grid=(T // tt,),
        in_specs=[
            pl.BlockSpec((1, 1, 2 * tt), lambda i: (i, 0, 0), memory_space=pltpu.SMEM),
            pl.BlockSpec((tt, D_MODEL), lambda i: (i, 0)),
            pl.BlockSpec(memory_space=pl.ANY),
        ],
        out_specs=pl.BlockSpec(memory_space=pl.ANY),
        out_shape=jax.ShapeDtypeStruct((n_rows, D_MODEL), h.dtype),
        scratch_shapes=[pltpu.SemaphoreType.DMA(())],
        input_output_aliases={2: 0},
        compiler_params=_cparams(("arbitrary",)),
        name="moe_scatter",
    )(dest3, h, xs0)


def _moe_ffn_kernel(te_ref, na_ref, xs_ref, wg_ref, wu_ref, wd_ref, ys_ref, acc_scr):
    i = pl.program_id(0)
    f = pl.program_id(1)

    @pl.when(i < na_ref[0])
    def _():
        @pl.when(f == 0)
        def _():
            acc_scr[...] = jnp.zeros_like(acc_scr)

        acc_scr[...] += _swiglu_tile(xs_ref[...].astype(BF16), wg_ref[0], wu_ref[0], wd_ref[0])

        @pl.when(f == pl.num_programs(1) - 1)
        def _():
            ys_ref[...] = acc_scr[...]

    @pl.when((i >= na_ref[0]) & (f == 0))
    def _():
        ys_ref[...] = jnp.zeros_like(ys_ref)


def _moe_ffn(tile_expert, n_active, xs, wg, wu, wd, tm, tf):
    n_rows = xs.shape[0]
    n_tiles = n_rows // tm
    nf = D_FF // tf

    def row_map(i, f, te, na):
        return (jnp.minimum(i, na[0] - 1), 0)

    def fcol(i, f, na):
        return jnp.where(i < na[0], f, nf - 1)

    grid_spec = pltpu.PrefetchScalarGridSpec(
        num_scalar_prefetch=2,
        grid=(n_tiles, nf),
        in_specs=[
            pl.BlockSpec((tm, D_MODEL), row_map),
            pl.BlockSpec((1, D_MODEL, tf), lambda i, f, te, na: (te[i], 0, fcol(i, f, na))),
            pl.BlockSpec((1, D_MODEL, tf), lambda i, f, te, na: (te[i], 0, fcol(i, f, na))),
            pl.BlockSpec((1, tf, D_MODEL), lambda i, f, te, na: (te[i], fcol(i, f, na), 0)),
        ],
        out_specs=pl.BlockSpec((tm, D_MODEL), lambda i, f, te, na: (i, 0)),
        scratch_shapes=[pltpu.VMEM((tm, D_MODEL), F32)],
    )
    return pl.pallas_call(
        _moe_ffn_kernel,
        grid_spec=grid_spec,
        out_shape=jax.ShapeDtypeStruct((n_rows, D_MODEL), F32),
        compiler_params=_cparams(("arbitrary", "arbitrary")),
        name="moe_ffn",
    )(tile_expert, n_active, xs, wg, wu, wd)


def _combine_kernel(dest_ref, x_ref, meta_ref, fw_ref, ys_hbm, o_ref, g_scr, sem, *, tc):
    def issue(t, c):
        pltpu.make_async_copy(ys_hbm.at[pl.ds(dest_ref[0, 0, 2 * t], 1)], g_scr.at[0, pl.ds(t, 1)], sem).start()
        pltpu.make_async_copy(ys_hbm.at[pl.ds(dest_ref[0, 0, 2 * t + 1], 1)], g_scr.at[1, pl.ds(t, 1)], sem).start()
        return c

    lax.fori_loop(0, tc, issue, 0)

    def drain(t, c):
        pltpu.make_async_copy(ys_hbm.at[pl.ds(0, 1)], g_scr.at[0, pl.ds(0, 1)], sem).wait()
        return c

    lax.fori_loop(0, 2 * tc, drain, 0)
    meta = meta_ref[...]
    w0 = meta[:, M_W0:M_W0 + 1]
    w1 = meta[:, M_W1:M_W1 + 1]
    x = x_ref[...] + w0 * g_scr[0] + w1 * g_scr[1]
    ms = jnp.mean(x * x, axis=-1, keepdims=True)
    o_ref[...] = x * lax.rsqrt(ms + RMS_EPS) * fw_ref[...]


def _combine(dest, x2, meta, fw, ys, tc):
    T = x2.shape[0]
    dest3 = dest.reshape(T // tc, 1, 2 * tc)
    return pl.pallas_call(
        functools.partial(_combine_kernel, tc=tc),
        grid=(T // tc,),
        in_specs=[
            pl.BlockSpec((1, 1, 2 * tc), lambda i: (i, 0, 0), memory_space=pltpu.SMEM),
            pl.BlockSpec((tc, D_MODEL), lambda i: (i, 0)),
            pl.BlockSpec((tc, LANE), lambda i: (i, 0)),
            pl.BlockSpec((1, D_MODEL), lambda i: (0, 0)),
            pl.BlockSpec(memory_space=pl.ANY),
        ],
        out_specs=pl.BlockSpec((tc, D_MODEL), lambda i: (i, 0)),
        out_shape=jax.ShapeDtypeStruct((T, D_MODEL), F32),
        scratch_shapes=[pltpu.VMEM((2, tc, D_MODEL), F32), pltpu.SemaphoreType.DMA(())],
        compiler_params=_cparams(("arbitrary",)),
        name="moe_combine",
    )(dest3, x2, meta, fw, ys)


def _moe_plan(meta, counts, tm, n_tiles):
    cnt = counts[0, :N_EXPERTS].astype(jnp.int32)
    padded = ((cnt + tm - 1) // tm) * tm
    ends = jnp.cumsum(padded)
    offs = ends - padded
    e = meta[:, M_E0:M_E1 + 1].astype(jnp.int32)
    pos = meta[:, M_P0:M_P1 + 1].astype(jnp.int32)
    dest = offs[e] + pos
    starts = jnp.arange(n_tiles, dtype=jnp.int32) * tm
    n_active = (ends[-1] // tm).astype(jnp.int32)
    te = jnp.sum(starts[:, None] >= ends[None, :], axis=-1).astype(jnp.int32)
    last_e = jnp.sum(jnp.maximum(ends[-1] - tm, 0) >= ends).astype(jnp.int32)
    te = jnp.where(starts < ends[-1], te, last_e)
    return dest, te, n_active.reshape(1)


def _moe_layer(x_mid, nw, w_router, wg, wu, wd, final_w, tiles):
    T = x_mid.shape[0]
    tm = tiles["moe_tm"]
    n_tiles = (TOP_K * T) // tm + N_EXPERTS
    h, meta, counts = _router(x_mid, nw, w_router, tiles["router_t"])
    dest, te, n_active = _moe_plan(meta, counts, tm, n_tiles)
    xs = _scatter_rows(dest, h, n_tiles * tm, tiles["scatter_t"])
    ys = _moe_ffn(te, n_active, xs, wg, wu, wd, tm, tiles["moe_tf"])
    return _combine(dest, x_mid, meta, final_w, ys, tiles["combine_t"])


def _tiles(S, T):
    return dict(
        inproj_tm=min(1024, T), inproj_tn=640,
        attn_t=min(512, S // 2), ret_c=min(128, S),
        merge_tm=min(512, T), ffn_tm=min(1024, T), ffn_tf=512,
        router_t=min(256, T), scatter_t=min(256, T), combine_t=min(256, T),
        moe_tm=min(1024, T), moe_tf=512,
    )


def kernel(x, norm_mix_w, w_in, fox_f_bias, ret_gn_w, mla_q_norm_w, mla_kv_norm_w, mla_w_uq, mla_w_uk, mla_w_uv, w_br_fox, w_br_ret, w_br_mla, w_out, norm_ffn_w, dense_w_gate, dense_w_up, dense_w_down, moe_w_router, moe_w_gate, moe_w_up, moe_w_down, final_norm_w):
    B, S, D = x.shape
    assert D == D_MODEL and w_in.shape[0] == DEPTH == 2
    T = B * S
    tl = _tiles(S, T)
    row = lambda a: a.reshape(1, -1).astype(F32)
    ret_tabs = _ret_tables(S, tl["ret_c"])
    mla_tabs = _mla_tables(S)
    x2 = x.reshape(T, D)
    for l in range(DEPTH):
        w_all, w_ff = _build_inproj_weights(w_in[l])
        proj, ff32 = _inproj(x2, row(norm_mix_w[l]), w_all, w_ff, tl["inproj_tm"], tl["inproj_tn"])
        fqt, fkx, fvt = _fox_prep(proj, ff32, fox_f_bias[l], B, S, tl["attn_t"])
        o_fox = _attention(fqt, fkx, fvt, B, S, tl["attn_t"], FOX_RANGES, "fox_attn")
        o_ret = _retention(proj, row(ret_gn_w[l]), ret_tabs, B, S, tl["ret_c"])
        wq, wqr, wk, wv = _mla_weights(mla_w_uq[l], mla_w_uk[l], mla_w_uv[l])
        mqt, mkx, mvt = _mla_prep(proj, row(mla_q_norm_w[l]), row(mla_kv_norm_w[l]), wq, wqr, wk, wv,
                                  mla_tabs, B, S, tl["attn_t"])
        o_mla = _attention(mqt, mkx, mvt, B, S, tl["attn_t"], MLA_RANGES, "mla_attn")
        x_mid, h2 = _merge(o_fox, o_ret, o_mla, proj, x2,
                           w_br_fox[l].astype(BF16), w_br_ret[l].astype(BF16), w_br_mla[l].astype(BF16),
                           w_out[l].astype(BF16), row(norm_ffn_w[l]), tl["merge_tm"])
        if l % 2 == 0:
            i = l // 2
            x2 = _dense_ffn(h2, x_mid, dense_w_gate[i].astype(BF16), dense_w_up[i].astype(BF16),
                            dense_w_down[i].astype(BF16), tl["ffn_tm"], tl["ffn_tf"])
        else:
            i = l // 2
            x2 = _moe_layer(x_mid, row(norm_ffn_w[l]), moe_w_router[i], moe_w_gate[i].astype(BF16),
                            moe_w_up[i].astype(BF16), moe_w_down[i].astype(BF16), row(final_norm_w), tl)
    return x2.reshape(B, S, D)
```

```python
import functools

import numpy as np
import jax
import jax.numpy as jnp
from jax import lax
from jax.experimental import pallas as pl
from jax.experimental.pallas import tpu as pltpu

F32 = jnp.float32
BF16 = jnp.bfloat16

D_MODEL = 1024
DEPTH = 2
FOX_HEADS = 8
FOX_HEAD_DIM = 64
FOX_W = FOX_HEADS * FOX_HEAD_DIM
RET_HEADS = 8
RET_QK_DIM = 64
RET_V_DIM = 128
RET_QK_W = RET_HEADS * RET_QK_DIM
RET_V_W = RET_HEADS * RET_V_DIM
MLA_HEADS = 8
MLA_Q_RANK = 384
MLA_KV_RANK = 256
MLA_NOPE_DIM = 64
MLA_ROPE_DIM = 32
MLA_V_DIM = 64
MLA_QK_DIM = MLA_NOPE_DIM + MLA_ROPE_DIM
MLA_V_W = MLA_HEADS * MLA_V_DIM
N_BRANCHES = 3
ROPE_THETA = 10000.0
RMS_EPS = 1e-6
D_FF = 3584
N_EXPERTS = 8
TOP_K = 2
IN_SPLITS = (FOX_W, FOX_W, FOX_W, FOX_HEADS,
             RET_QK_W, RET_QK_W, RET_V_W, RET_V_W,
             MLA_Q_RANK, MLA_KV_RANK, MLA_ROPE_DIM,
             N_BRANCHES * D_MODEL)

LANE = 128
NEG = -1e30
VMEM_LIMIT = 56 * 1024 * 1024

U_FQ, U_FK, U_FV = 0, 4, 8
U_RQ, U_RQR, U_RK, U_RKR = 12, 16, 20, 24
U_CKV, U_KR, U_KRR = 28, 30, 31
U_RV, U_RG, U_GL, U_CQ = 32, 40, 48, 72
N_PROJ = 75 * LANE
FOX_EXT_STRIDE = 8
PAIRS = 4
QK_W = 2 * LANE
VT_ROWS = 80
LOG2E = 1.4426950408889634


def _cparams(sem, vmem=VMEM_LIMIT):
    return pltpu.CompilerParams(dimension_semantics=sem, vmem_limit_bytes=vmem)


def _sigmoid(x):
    return 1.0 / (1.0 + jnp.exp(-x))


def _lane_mask(shape, ranges):
    lane = lax.broadcasted_iota(jnp.int32, shape, len(shape) - 1)
    m = None
    for a, b in ranges:
        r = (lane >= a) & (lane < b)
        m = r if m is None else (m | r)
    return m


def _inproj_kernel(x_ref, nw_ref, w_ref, wff_ref, out_ref, ff_ref, h_scr):
    @pl.when(pl.program_id(1) == 0)
    def _():
        x = x_ref[...]
        ms = jnp.mean(x * x, axis=-1, keepdims=True)
        h = (x * lax.rsqrt(ms + RMS_EPS) * nw_ref[...]).astype(BF16)
        h_scr[...] = h
        ff_ref[...] = jnp.dot(h, wff_ref[...], preferred_element_type=F32)

    out_ref[...] = jnp.dot(h_scr[...], w_ref[...], preferred_element_type=F32).astype(out_ref.dtype)


def _inproj(x2, nw, w_all, w_ff, tm, tn):
    T = x2.shape[0]
    return pl.pallas_call(
        _inproj_kernel,
        grid=(T // tm, N_PROJ // tn),
        in_specs=[
            pl.BlockSpec((tm, D_MODEL), lambda i, j: (i, 0)),
            pl.BlockSpec((1, D_MODEL), lambda i, j: (0, 0)),
            pl.BlockSpec((D_MODEL, tn), lambda i, j: (0, j)),
            pl.BlockSpec((D_MODEL, LANE), lambda i, j: (0, 0)),
        ],
        out_specs=[
            pl.BlockSpec((tm, tn), lambda i, j: (i, j)),
            pl.BlockSpec((tm, LANE), lambda i, j: (i, 0)),
        ],
        out_shape=[
            jax.ShapeDtypeStruct((T, N_PROJ), BF16),
            jax.ShapeDtypeStruct((T, LANE), F32),
        ],
        scratch_shapes=[pltpu.VMEM((tm, D_MODEL), BF16)],
        compiler_params=_cparams(("parallel", "arbitrary")),
        name="inproj",
    )(x2, nw, w_all, w_ff)


def _rot_cols(w, heads, d):
    k = w.shape[0]
    w4 = w.reshape(k, heads, 2, d // 2)
    return jnp.concatenate([-w4[:, :, 1], w4[:, :, 0]], axis=-1).reshape(k, heads * d)


def _build_inproj_weights(w_in):
    split_at = [int(i) for i in np.cumsum(IN_SPLITS)[:-1]]
    fq, fk, fv, ff, rq, rk, rv, rg, cq, ckv, kr, gl = jnp.split(w_in, split_at, axis=-1)
    z64 = jnp.zeros((D_MODEL, 64), F32)
    z32 = jnp.zeros((D_MODEL, 32), F32)
    kr128 = jnp.concatenate([z64, kr, z32], axis=-1)
    krr128 = jnp.concatenate([z64, _rot_cols(kr, 1, MLA_ROPE_DIM), z32], axis=-1)
    w_all = jnp.concatenate(
        [fq, fk, fv, rq, _rot_cols(rq, RET_HEADS, RET_QK_DIM), rk, _rot_cols(rk, RET_HEADS, RET_QK_DIM),
         ckv, kr128, krr128, rv, rg, gl, cq], axis=-1).astype(BF16)
    w_ff = jnp.pad(ff, ((0, 0), (0, LANE - FOX_HEADS))).astype(BF16)
    return w_all, w_ff


def _split3(x):
    hi = x.astype(BF16)
    r1 = x - hi.astype(F32)
    mid = r1.astype(BF16)
    lo = (r1 - mid.astype(F32)).astype(BF16)
    return hi, mid, lo


def _store_t(dst_ref, idx, x):
    dst_ref[idx] = x.T.astype(dst_ref.dtype)


def _store_vt_pair(vt_ref, p, v2):
    hd = LANE // 2
    vt = v2.T.astype(vt_ref.dtype)
    pad_rows = lax.broadcasted_iota(jnp.int32, (VT_ROWS - hd, vt.shape[1]), 0)
    ones_then_zeros = jnp.where(pad_rows == 0, 1.0, 0.0).astype(vt_ref.dtype)
    for j in range(2):
        base = (2 * p + j) * VT_ROWS
        vt_ref[0, 0, base:base + hd, :] = vt[j * hd:(j + 1) * hd]
        vt_ref[0, 0, base + hd:base + VT_ROWS, :] = ones_then_zeros


def _fox_prep_kernel(fq_ref, fk_ref, fv_ref, ff_ref, bias_ref, tri_ref, pq_ref, pk_ref, cq_ref, ck_ref,
                     qt_ref, kx_ref, vt_ref, carry_scr):
    @pl.when(pl.program_id(1) == 0)
    def _():
        carry_scr[...] = jnp.zeros_like(carry_scr)

    ts = ff_ref.shape[0]
    z = ff_ref[...] + bias_ref[...]
    logf = jnp.minimum(z, 0.0) - jnp.log(1.0 + jnp.exp(-jnp.abs(z)))
    tri = tri_ref[...]
    cum = carry_scr[0:1, :]
    for part in _split3(logf):
        cum = cum + jnp.dot(tri, part, preferred_element_type=F32)
    carry_scr[0:1, :] = cum[ts - 1:ts, :]

    eq = cq_ref[...]
    ek = ck_ref[...]
    for j, part in enumerate(_split3(cum * LOG2E)):
        eq = eq + jnp.dot(part, pq_ref[j * LANE:(j + 1) * LANE, :], preferred_element_type=F32)
        ek = ek + jnp.dot(part, pk_ref[j * LANE:(j + 1) * LANE, :], preferred_element_type=F32)
    scale = FOX_HEAD_DIM ** -0.5 * LOG2E
    for p in range(FOX_HEADS // 2):
        src = slice(p * LANE, (p + 1) * LANE)
        _store_t(qt_ref, (0, slice(2 * p * LANE, (2 * p + 1) * LANE)), fq_ref[:, src].astype(F32) * scale)
        _store_t(qt_ref, (0, slice((2 * p + 1) * LANE, (2 * p + 2) * LANE)), eq[:, src])
        kx_ref[:, 2 * p * LANE:(2 * p + 1) * LANE] = fk_ref[:, src]
        kx_ref[:, (2 * p + 1) * LANE:(2 * p + 2) * LANE] = ek[:, src].astype(BF16)
        _store_vt_pair(vt_ref, p, fv_ref[:, src].astype(F32))


def _fox_prep_consts(ts):
    tri = np.tril(np.ones((ts, ts), np.float32))
    pq = np.zeros((3 * LANE, FOX_W), np.float32)
    pk = np.zeros((3 * LANE, FOX_W), np.float32)
    cq = np.zeros((1, FOX_W), np.float32)
    ck = np.zeros((1, FOX_W), np.float32)
    for h in range(FOX_HEADS):
        base = LANE * (h // 2) + FOX_EXT_STRIDE * (h % 2)
        for j in range(3):
            pq[j * LANE + h, base + j] = 1.0
            ck[0, base + j] = 1.0
            cq[0, base + 3 + j] = 1.0
            pk[j * LANE + h, base + 3 + j] = -1.0
    return (jnp.asarray(tri, BF16), jnp.asarray(pq, BF16), jnp.asarray(pk, BF16),
            jnp.asarray(cq), jnp.asarray(ck))


def _attn_operand_specs(B, S, t):
    T = B * S
    ns = S // t
    specs = [
        pl.BlockSpec((1, PAIRS * QK_W, t), lambda b, s: (b, 0, s)),
        pl.BlockSpec((t, PAIRS * QK_W), lambda b, s: (b * ns + s, 0)),
        pl.BlockSpec((1, 1, 2 * PAIRS * VT_ROWS, t), lambda b, s: (b, s, 0, 0)),
    ]
    shapes = [
        jax.ShapeDtypeStruct((B, PAIRS * QK_W, S), BF16),
        jax.ShapeDtypeStruct((T, PAIRS * QK_W), BF16),
        jax.ShapeDtypeStruct((B, ns, 2 * PAIRS * VT_ROWS, t), BF16),
    ]
    return specs, shapes


def _fox_prep(proj, ff32, bias, B, S, ts):
    ns = S // ts
    tri, pq, pk, cq, ck = _fox_prep_consts(ts)
    bias128 = jnp.pad(bias.astype(F32), (0, LANE - FOX_HEADS)).reshape(1, LANE)
    const = lambda shape: pl.BlockSpec(shape, lambda b, s: (0, 0))
    out_specs, out_shape = _attn_operand_specs(B, S, ts)
    return pl.pallas_call(
        _fox_prep_kernel,
        grid=(B, ns),
        in_specs=[
            pl.BlockSpec((ts, FOX_W), lambda b, s: (b * ns + s, U_FQ // 4)),
            pl.BlockSpec((ts, FOX_W), lambda b, s: (b * ns + s, U_FK // 4)),
            pl.BlockSpec((ts, FOX_W), lambda b, s: (b * ns + s, U_FV // 4)),
            pl.BlockSpec((ts, LANE), lambda b, s: (b * ns + s, 0)),
            const((1, LANE)), const((ts, ts)), const((3 * LANE, FOX_W)), const((3 * LANE, FOX_W)),
            const((1, FOX_W)), const((1, FOX_W)),
        ],
        out_specs=out_specs,
        out_shape=out_shape,
        scratch_shapes=[pltpu.VMEM((8, LANE), F32)],
        compiler_params=_cparams(("parallel", "arbitrary")),
        name="fox_prep",
    )(proj, proj, proj, ff32, bias128, tri, pq, pk, cq, ck)


def _row_mask(shape, ranges):
    row = lax.broadcasted_iota(jnp.int32, shape, 0)
    m = None
    for a, b in ranges:
        r = (row >= a) & (row < b)
        m = r if m is None else (m | r)
    return m


def _col_reduce(x, op, reduce_fn, parts=8):
    c = x.shape[0] // parts
    pieces = [x[i * c:(i + 1) * c] for i in range(parts)]
    while len(pieces) > 1:
        pieces = [op(pieces[i], pieces[i + 1]) for i in range(0, len(pieces), 2)]
    return reduce_fn(pieces[0], axis=0, keepdims=True)


def _attn_kernel(qt_ref, k_ref, vt_ref, o_ref, qh_scr, sa_scr, sb_scr, ma_scr, mb_scr, acc_scr, *, tk, ranges):
    qi = pl.program_id(2)
    qt = qt_ref[0]
    tq = qt.shape[-1]
    for h in range(2):
        qh_scr[h] = jnp.where(_row_mask(qt.shape, ranges[h]), qt, jnp.zeros_like(qt))
    acc_scr[...] = jnp.zeros_like(acc_scr)

    def key_tile(kt):
        return k_ref[pl.ds(pl.multiple_of(kt * tk, tk), tk), :]

    def scores(kt, s_scr, m_scr):
        k = key_tile(kt)
        for h in range(2):
            s = jnp.dot(k, qh_scr[h], preferred_element_type=F32)
            s_scr[h] = s
            m_scr[h] = _col_reduce(s, jnp.maximum, jnp.max)

    def update(kt, h, s, m_tile, m_prev, lo):
        m_new = jnp.maximum(m_prev, m_tile)
        alpha = jnp.exp2(m_prev - m_new)
        p = jnp.exp2((s - m_new).astype(BF16))
        vt = vt_ref[0, kt, h * VT_ROWS:(h + 1) * VT_ROWS, :]
        acc_scr[h, :, lo:] = alpha * acc_scr[h, :, lo:] + jnp.dot(vt, p, preferred_element_type=F32)
        return m_new

    def softmax_pv(kt, s_scr, m_scr, carry):
        return tuple(update(kt, h, s_scr[h], m_scr[h], carry[h], 0) for h in range(2))

    def softmax_pv_diag(kt, s_scr, carry, diag):
        lo = diag * tk
        out = []
        for h in range(2):
            s = s_scr[h, :, lo:]
            key = lax.broadcasted_iota(jnp.int32, s.shape, 0)
            qry = lax.broadcasted_iota(jnp.int32, s.shape, 1) + (lo - diag * tk)
            s = jnp.where(key <= qry, s, NEG)
            m_tile = _col_reduce(s, jnp.maximum, jnp.max)
            m_new = update(kt, h, s, m_tile, carry[h][:, lo:], lo)
            out.append(m_new if lo == 0 else jnp.concatenate([carry[h][:, :lo], m_new], axis=-1))
        return tuple(out)

    def body(j, carry):
        kt = 2 * j
        scores(kt + 1, sb_scr, mb_scr)
        carry = softmax_pv(kt, sa_scr, ma_scr, carry)
        scores(kt + 2, sa_scr, ma_scr)
        return softmax_pv(kt + 1, sb_scr, mb_scr, carry)

    init = tuple(jnp.full((1, tq), NEG, F32) for _ in range(2))
    scores(0, sa_scr, ma_scr)
    carry = lax.fori_loop(0, qi, body, init)
    k_last = key_tile(2 * qi + 1)
    for h in range(2):
        sb_scr[h, :, tk:] = jnp.dot(k_last, qh_scr[h, :, tk:], preferred_element_type=F32)
    carry = softmax_pv_diag(2 * qi, sa_scr, carry, 0)
    carry = softmax_pv_diag(2 * qi + 1, sb_scr, carry, 1)
    hd = LANE // 2
    ot = jnp.concatenate([acc_scr[h, :hd] / acc_scr[h, hd:hd + 1] for h in range(2)], axis=0)
    o_ref[...] = ot.T.astype(o_ref.dtype)


def _attention(qt, kx, vt, B, S, tk, ranges, name):
    T = B * S
    tq = 2 * tk
    nq = S // tq
    kern = functools.partial(_attn_kernel, tk=tk, ranges=ranges)
    return pl.pallas_call(
        kern,
        grid=(B, PAIRS, nq),
        in_specs=[
            pl.BlockSpec((1, QK_W, tq), lambda b, p, i: (b, p, i)),
            pl.BlockSpec((S, QK_W), lambda b, p, i: (b, p)),
            pl.BlockSpec((1, S // tk, 2 * VT_ROWS, tk), lambda b, p, i: (b, 0, p, 0)),
        ],
        out_specs=pl.BlockSpec((tq, LANE), lambda b, p, i: (b * nq + i, p)),
        out_shape=jax.ShapeDtypeStruct((T, PAIRS * LANE), BF16),
        scratch_shapes=[
            pltpu.VMEM((2, QK_W, tq), BF16),
            pltpu.VMEM((2, tk, tq), F32),
            pltpu.VMEM((2, tk, tq), F32),
            pltpu.VMEM((2, 1, tq), F32),
            pltpu.VMEM((2, 1, tq), F32),
            pltpu.VMEM((2, VT_ROWS, tq), F32),
        ],
        compiler_params=_cparams(("parallel", "parallel", "arbitrary")),
        name=name,
    )(qt, kx, vt)


_E0 = LANE
FOX_RANGES = (((0, 64), (_E0, _E0 + FOX_EXT_STRIDE)),
              ((64, 128), (_E0 + FOX_EXT_STRIDE, _E0 + 2 * FOX_EXT_STRIDE)))
MLA_RANGES = (((0, LANE),), ((LANE, 2 * LANE),))


def _mla_prep_kernel(cq_ref, ckv_ref, kr_ref, krr_ref, qnw_ref, kvnw_ref, wq_ref, wqr_ref, wk_ref, wv_ref,
                     cq_tab, sq_tab, ck_tab, sk_tab, qt_ref, k_ref, vt_ref):
    def norm(x_ref, w_ref):
        x = x_ref[...].astype(F32)
        ms = jnp.mean(x * x, axis=-1, keepdims=True)
        return (x * lax.rsqrt(ms + RMS_EPS) * w_ref[...]).astype(BF16)

    c_q = norm(cq_ref, qnw_ref)
    c_kv = norm(ckv_ref, kvnw_ref)
    tile8 = lambda a: jnp.concatenate([a] * MLA_HEADS, axis=-1)
    q = (jnp.dot(c_q, wq_ref[...], preferred_element_type=F32) * tile8(cq_tab[...])
         + jnp.dot(c_q, wqr_ref[...], preferred_element_type=F32) * tile8(sq_tab[...]))
    for c in range(MLA_HEADS):
        _store_t(qt_ref, (0, slice(c * LANE, (c + 1) * LANE)), q[:, c * LANE:(c + 1) * LANE])
    k_rope = kr_ref[...].astype(F32) * ck_tab[...] + krr_ref[...].astype(F32) * sk_tab[...]
    k = jnp.dot(c_kv, wk_ref[...], preferred_element_type=F32) + tile8(k_rope)
    k_ref[...] = k.astype(BF16)
    v = jnp.dot(c_kv, wv_ref[...], preferred_element_type=F32)
    for p in range(PAIRS):
        _store_vt_pair(vt_ref, p, v[:, p * LANE:(p + 1) * LANE])


def _mla_weights(w_uq, w_uk, w_uv):
    q3 = w_uq.reshape(MLA_Q_RANK, MLA_HEADS, MLA_QK_DIM)
    nope, ropep = q3[..., :MLA_NOPE_DIM], q3[..., MLA_NOPE_DIM:]
    half = MLA_ROPE_DIM // 2
    rot = jnp.concatenate([-ropep[..., half:], ropep[..., :half]], axis=-1)
    z32 = jnp.zeros((MLA_Q_RANK, MLA_HEADS, LANE - MLA_QK_DIM), F32)
    z64 = jnp.zeros((MLA_Q_RANK, MLA_HEADS, MLA_NOPE_DIM), F32)
    wq = jnp.concatenate([nope, ropep, z32], axis=-1).reshape(MLA_Q_RANK, MLA_HEADS * LANE)
    wqr = jnp.concatenate([z64, rot, z32], axis=-1).reshape(MLA_Q_RANK, MLA_HEADS * LANE)
    k3 = w_uk.reshape(MLA_KV_RANK, MLA_HEADS, MLA_NOPE_DIM)
    wk = jnp.concatenate([k3, jnp.zeros_like(k3)], axis=-1).reshape(MLA_KV_RANK, MLA_HEADS * LANE)
    return wq.astype(BF16), wqr.astype(BF16), wk.astype(BF16), w_uv.astype(BF16)


def _mla_tables(S):
    pos = jnp.arange(S, dtype=F32)
    inv = ROPE_THETA ** (-jnp.arange(0, MLA_ROPE_DIM, 2, dtype=F32) / MLA_ROPE_DIM)
    ang = pos[:, None] * inv[None, :]
    cos, sin = jnp.cos(ang), jnp.sin(ang)
    ones = jnp.ones((S, MLA_NOPE_DIM), F32)
    z64 = jnp.zeros((S, MLA_NOPE_DIM), F32)
    z32 = jnp.zeros((S, LANE - MLA_QK_DIM), F32)
    ctab = jnp.concatenate([ones, cos, cos, z32], axis=-1)
    stab = jnp.concatenate([z64, sin, sin, z32], axis=-1)
    scale = MLA_QK_DIM ** -0.5 * LOG2E
    return ctab * scale, stab * scale, ctab, stab


def _mla_prep(proj, qnw, kvnw, wq, wqr, wk, wv, tabs, B, S, tm):
    ns = S // tm
    HW = MLA_HEADS * LANE
    const = lambda shape: pl.BlockSpec(shape, lambda b, s: (0, 0))
    tab = pl.BlockSpec((tm, LANE), lambda b, s: (s, 0))
    out_specs, out_shape = _attn_operand_specs(B, S, tm)
    return pl.pallas_call(
        _mla_prep_kernel,
        grid=(B, ns),
        in_specs=[
            pl.BlockSpec((tm, MLA_Q_RANK), lambda b, s: (b * ns + s, U_CQ // 3)),
            pl.BlockSpec((tm, MLA_KV_RANK), lambda b, s: (b * ns + s, U_CKV // 2)),
            pl.BlockSpec((tm, LANE), lambda b, s: (b * ns + s, U_KR)),
            pl.BlockSpec((tm, LANE), lambda b, s: (b * ns + s, U_KRR)),
            const((1, MLA_Q_RANK)), const((1, MLA_KV_RANK)),
            const((MLA_Q_RANK, HW)), const((MLA_Q_RANK, HW)), const((MLA_KV_RANK, HW)),
            const((MLA_KV_RANK, MLA_V_W)),
            tab, tab, tab, tab,
        ],
        out_specs=out_specs,
        out_shape=out_shape,
        compiler_params=_cparams(("parallel", "parallel")),
        name="mla_prep",
    )(proj, proj, proj, proj, qnw, kvnw, wq, wqr, wk, wv, *tabs)


def _ret_kernel(rq_ref, rqr_ref, rk_ref, rkr_ref, rv_ref, rg_ref, cos_ref, sin_ref,
                din_ref, xi_ref, zeta_ref, gch_ref, gnw_ref, o_ref, r_scr):
    @pl.when(pl.program_id(1) == 0)
    def _():
        r_scr[...] = jnp.zeros_like(r_scr)

    tile4 = lambda a: jnp.concatenate([a] * (RET_QK_W // LANE), axis=-1)
    cos4 = tile4(cos_ref[...])
    sin4 = tile4(sin_ref[...])
    q = rq_ref[...].astype(F32) * cos4 + rqr_ref[...].astype(F32) * sin4
    k = (rk_ref[...].astype(F32) * cos4 + rkr_ref[...].astype(F32) * sin4) * (RET_QK_DIM ** -0.5)
    for p in range(RET_HEADS // 2):
        q2 = q[:, p * LANE:(p + 1) * LANE]
        k2 = k[:, p * LANE:(p + 1) * LANE]
        k2b = k2.astype(BF16)
        kz2 = (k2 * zeta_ref[p]).astype(BF16)
        r2 = r_scr[p]
        r2b = r2.astype(BF16)
        new_r = gch_ref[p] * r2
        for j in range(2):
            h = 2 * p + j
            half = ((j * RET_QK_DIM, (j + 1) * RET_QK_DIM),)
            qm = jnp.where(_lane_mask(q2.shape, half), q2, 0.0).astype(BF16)
            inner = lax.dot_general(qm, k2b, (((1,), (1,)), ((), ())), preferred_element_type=F32)
            inner = inner * din_ref[h]
            vh = rv_ref[:, h * LANE:(h + 1) * LANE]
            o = (jnp.dot(inner.astype(BF16), vh, preferred_element_type=F32)
                 + jnp.dot(qm, r2b, preferred_element_type=F32) * xi_ref[h])
            upd = lax.dot_general(kz2, vh, (((0,), (0,)), ((), ())), preferred_element_type=F32)
            rows = lax.broadcasted_iota(jnp.int32, upd.shape, 0)
            new_r = new_r + jnp.where((rows >= half[0][0]) & (rows < half[0][1]), upd, 0.0)
            mu = jnp.mean(o, axis=-1, keepdims=True)
            d = o - mu
            var = jnp.mean(d * d, axis=-1, keepdims=True)
            on = d * lax.rsqrt(var + RMS_EPS) * gnw_ref[:, h * LANE:(h + 1) * LANE]
            g = rg_ref[:, h * LANE:(h + 1) * LANE].astype(F32)
            o_ref[:, h * LANE:(h + 1) * LANE] = (g * _sigmoid(g) * on).astype(o_ref.dtype)
        r_scr[p] = new_r


def _ret_tables(S, C):
    pos = jnp.arange(S, dtype=F32)
    inv = ROPE_THETA ** (-jnp.arange(0, RET_QK_DIM, 2, dtype=F32) / RET_QK_DIM)
    ang = pos[:, None] * inv[None, :]
    cos = jnp.concatenate([jnp.cos(ang)] * 4, axis=-1)
    sin = jnp.concatenate([jnp.sin(ang)] * 4, axis=-1)
    gammas = 1.0 - 2.0 ** (-5.0 - jnp.arange(RET_HEADS, dtype=F32))
    log_g = jnp.log(gammas)
    j = jnp.arange(C, dtype=F32)
    diff = j[:, None] - j[None, :]
    din = jnp.where(diff[None] >= 0, jnp.exp(jnp.maximum(diff, 0.0)[None] * log_g[:, None, None]), 0.0)
    xi = jnp.exp((j[None, :] + 1.0) * log_g[:, None])
    zeta = jnp.exp((C - 1.0 - j[None, :]) * log_g[:, None])
    gch = jnp.exp(C * log_g)
    xi_t = jnp.broadcast_to(xi[:, :, None], (RET_HEADS, C, LANE))
    zeta_p = jnp.repeat(zeta.reshape(RET_HEADS // 2, 2, C).transpose(0, 2, 1), RET_QK_DIM, axis=-1)
    gch_p = jnp.broadcast_to(jnp.repeat(gch.reshape(RET_HEADS // 2, 2), RET_QK_DIM, axis=-1)[:, :, None],
                             (RET_HEADS // 2, LANE, LANE))
    return cos, sin, din, xi_t, zeta_p, gch_p


def _retention(proj, gnw, tabs, B, S, C):
    T = B * S
    nc = S // C
    cos, sin, din, xi, zeta, gch = tabs
    blk = lambda w, unit: pl.BlockSpec((C, w), lambda b, c: (b * nc + c, unit * LANE // w))
    full = lambda a: pl.BlockSpec(a.shape, lambda b, c: (0,) * a.ndim)
    return pl.pallas_call(
        _ret_kernel,
        grid=(B, nc),
        in_specs=[
            blk(RET_QK_W, U_RQ), blk(RET_QK_W, U_RQR), blk(RET_QK_W, U_RK), blk(RET_QK_W, U_RKR),
            blk(RET_V_W, U_RV), blk(RET_V_W, U_RG),
            pl.BlockSpec((C, LANE), lambda b, c: (c, 0)),
            pl.BlockSpec((C, LANE), lambda b, c: (c, 0)),
            full(din), full(xi), full(zeta), full(gch), full(gnw),
        ],
        out_specs=pl.BlockSpec((C, RET_V_W), lambda b, c: (b * nc + c, 0)),
        out_shape=jax.ShapeDtypeStruct((T, RET_V_W), BF16),
        scratch_shapes=[pltpu.VMEM((RET_HEADS // 2, LANE, LANE), F32)],
        compiler_params=_cparams(("parallel", "arbitrary")),
        name="retention",
    )(proj, proj, proj, proj, proj, proj, cos, sin, din, xi, zeta, gch, gnw)


def _merge_kernel(of_ref, or_ref, om_ref, g0_ref, g1_ref, g2_ref, x_ref,
                  wf_ref, wr_ref, wm_ref, wo_ref, nw_ref, xo_ref, h_ref):
    merged = (_sigmoid(g0_ref[...].astype(F32)) * jnp.dot(of_ref[...], wf_ref[...], preferred_element_type=F32)
              + _sigmoid(g1_ref[...].astype(F32)) * jnp.dot(or_ref[...], wr_ref[...], preferred_element_type=F32)
              + _sigmoid(g2_ref[...].astype(F32)) * jnp.dot(om_ref[...], wm_ref[...], preferred_element_type=F32))
    x = x_ref[...] + jnp.dot(merged.astype(BF16), wo_ref[...], preferred_element_type=F32)
    xo_ref[...] = x
    ms = jnp.mean(x * x, axis=-1, keepdims=True)
    h_ref[...] = (x * lax.rsqrt(ms + RMS_EPS) * nw_ref[...]).astype(h_ref.dtype)


def _merge(o_fox, o_ret, o_mla, proj, x2, wf, wr, wm, wo, nw, tm):
    T = x2.shape[0]
    row = lambda w, cb=0: pl.BlockSpec((tm, w), lambda i: (i, cb))
    const = lambda a: pl.BlockSpec(a.shape, lambda i: (0, 0))
    g_unit = U_GL // 8
    return pl.pallas_call(
        _merge_kernel,
        grid=(T // tm,),
        in_specs=[
            row(FOX_W), row(RET_V_W), row(MLA_V_W),
            row(D_MODEL, g_unit), row(D_MODEL, g_unit + 1), row(D_MODEL, g_unit + 2),
            row(D_MODEL),
            const(wf), const(wr), const(wm), const(wo), const(nw),
        ],
        out_specs=[row(D_MODEL), row(D_MODEL)],
        out_shape=[jax.ShapeDtypeStruct((T, D_MODEL), F32), jax.ShapeDtypeStruct((T, D_MODEL), BF16)],
        compiler_params=_cparams(("parallel",)),
        name="merge",
    )(o_fox, o_ret, o_mla, proj, proj, proj, x2, wf, wr, wm, wo, nw)


def _swiglu_tile(h, wg, wu, wd):
    a = jnp.dot(h, wg, preferred_element_type=F32)
    b = jnp.dot(h, wu, preferred_element_type=F32)
    return jnp.dot((a * _sigmoid(a) * b).astype(BF16), wd, preferred_element_type=F32)


def _dense_ffn_kernel(h_ref, wg_ref, wu_ref, wd_ref, x_ref, o_ref, acc_scr):
    f = pl.program_id(1)

    @pl.when(f == 0)
    def _():
        acc_scr[...] = jnp.zeros_like(acc_scr)

    acc_scr[...] += _swiglu_tile(h_ref[...], wg_ref[...], wu_ref[...], wd_ref[...])

    @pl.when(f == pl.num_programs(1) - 1)
    def _():
        o_ref[...] = x_ref[...] + acc_scr[...]


def _dense_ffn(h, x2, wg, wu, wd, tm, tf):
    T = x2.shape[0]
    return pl.pallas_call(
        _dense_ffn_kernel,
        grid=(T // tm, D_FF // tf),
        in_specs=[
            pl.BlockSpec((tm, D_MODEL), lambda i, f: (i, 0)),
            pl.BlockSpec((D_MODEL, tf), lambda i, f: (0, f)),
            pl.BlockSpec((D_MODEL, tf), lambda i, f: (0, f)),
            pl.BlockSpec((tf, D_MODEL), lambda i, f: (f, 0)),
            pl.BlockSpec((tm, D_MODEL), lambda i, f: (i, 0)),
        ],
        out_specs=pl.BlockSpec((tm, D_MODEL), lambda i, f: (i, 0)),
        out_shape=jax.ShapeDtypeStruct((T, D_MODEL), F32),
        scratch_shapes=[pltpu.VMEM((tm, D_MODEL), F32)],
        compiler_params=_cparams(("parallel", "arbitrary")),
        name="dense_ffn",
    )(h, wg, wu, wd, x2)


DMA_ISSUE_UNROLL = 8

M_E0, M_E1, M_P0, M_P1, M_W0, M_W1 = 0, 1, 2, 3, 4, 5


def _split2(x):
    hi = x.astype(BF16)
    lo = (x - hi.astype(F32)).astype(BF16)
    return hi, lo


def _router_kernel(x_ref, nw_ref, whi_ref, wlo_ref, tri_ref, h_ref, meta_ref, cnt_ref, carry_scr):
    @pl.when(pl.program_id(0) == 0)
    def _():
        carry_scr[...] = jnp.zeros_like(carry_scr)

    x = x_ref[...]
    ms = jnp.mean(x * x, axis=-1, keepdims=True)
    h = x * lax.rsqrt(ms + RMS_EPS) * nw_ref[...]
    h_ref[...] = h
    hhi, hlo = _split2(h)
    logits = (jnp.dot(hhi, whi_ref[...], preferred_element_type=F32)
              + jnp.dot(hhi, wlo_ref[...], preferred_element_type=F32)
              + jnp.dot(hlo, whi_ref[...], preferred_element_type=F32))
    lane = lax.broadcasted_iota(jnp.int32, logits.shape, 1)
    lanef = lane.astype(F32)
    logits = jnp.where(lane < N_EXPERTS, logits, NEG)
    m0 = jnp.max(logits, axis=-1, keepdims=True)
    e0 = jnp.min(jnp.where(logits == m0, lanef, float(LANE)), axis=-1, keepdims=True)
    oh0 = lanef == e0
    rest = jnp.where(oh0, NEG, logits)
    m1 = jnp.max(rest, axis=-1, keepdims=True)
    e1 = jnp.min(jnp.where(rest == m1, lanef, float(LANE)), axis=-1, keepdims=True)
    oh1 = lanef == e1
    z = jnp.exp(m1 - m0)
    w0 = 1.0 / (1.0 + z)
    w1 = z / (1.0 + z)
    both = jnp.where(oh0 | oh1, 1.0, 0.0)
    before = carry_scr[0:1, :] + jnp.dot(tri_ref[...], both.astype(BF16), preferred_element_type=F32)
    p0 = jnp.sum(jnp.where(oh0, before, 0.0), axis=-1, keepdims=True)
    p1 = jnp.sum(jnp.where(oh1, before, 0.0), axis=-1, keepdims=True)
    total = carry_scr[0:1, :] + jnp.sum(both, axis=0, keepdims=True)
    carry_scr[0:1, :] = total
    cnt_ref[...] = jnp.broadcast_to(total, cnt_ref.shape)
    meta = jnp.zeros(logits.shape, F32)
    for idx, val in ((M_E0, e0), (M_E1, e1), (M_P0, p0), (M_P1, p1), (M_W0, w0), (M_W1, w1)):
        meta = jnp.where(lane == idx, val, meta)
    meta_ref[...] = meta


def _router(x2, nw, w_router, tr):
    T = x2.shape[0]
    wr = jnp.pad(w_router.astype(F32), ((0, 0), (0, LANE - N_EXPERTS)))
    whi = wr.astype(BF16)
    wlo = (wr - whi.astype(F32)).astype(BF16)
    tri = jnp.asarray(np.tril(np.ones((tr, tr), np.float32), -1), BF16)
    const = lambda a: pl.BlockSpec(a.shape, lambda i: (0, 0))
    return pl.pallas_call(
        _router_kernel,
        grid=(T // tr,),
        in_specs=[pl.BlockSpec((tr, D_MODEL), lambda i: (i, 0)), const(nw), const(whi), const(wlo), const(tri)],
        out_specs=[
            pl.BlockSpec((tr, D_MODEL), lambda i: (i, 0)),
            pl.BlockSpec((tr, LANE), lambda i: (i, 0)),
            pl.BlockSpec((8, LANE), lambda i: (0, 0)),
        ],
        out_shape=[
            jax.ShapeDtypeStruct((T, D_MODEL), F32),
            jax.ShapeDtypeStruct((T, LANE), F32),
            jax.ShapeDtypeStruct((8, LANE), F32),
        ],
        scratch_shapes=[pltpu.VMEM((8, LANE), F32)],
        compiler_params=_cparams(("arbitrary",)),
        name="router",
    )(x2, nw, whi, wlo, tri)


def _scatter_kernel(dest_ref, h_ref, xs_in, xs_hbm, sem, *, tt):
    del xs_in

    def row_copy(t, d_row):
        return pltpu.make_async_copy(h_ref.at[pl.ds(t, 1)], xs_hbm.at[pl.ds(d_row, 1)], sem)

    def issue(t, c):
        row_copy(t, dest_ref[0, 0, 2 * t]).start(priority=0)
        row_copy(t, dest_ref[0, 0, 2 * t + 1]).start(priority=1)
        return c

    lax.fori_loop(0, tt, issue, 0, unroll=DMA_ISSUE_UNROLL)
    for _ in range(TOP_K):
        pltpu.make_async_copy(h_ref, xs_hbm.at[pl.ds(0, tt)], sem).wait()


def _scatter_rows(dest, h, n_rows, tt):
    T = h.shape[0]
    xs0 = jnp.zeros((n_rows, D_MODEL), h.dtype)
    dest3 = dest.reshape(T // tt, 1, 2 * tt)
    return pl.pallas_call(
        functools.partial(_scatter_kernel, tt=tt),
        grid=(T // tt,),
        in_specs=[
            pl.BlockSpec((1, 1, 2 * tt), lambda i: (i, 0, 0), memory_space=pltpu.SMEM),
            pl.BlockSpec((tt, D_MODEL), lambda i: (i, 0)),
            pl.BlockSpec(memory_space=pl.ANY),
        ],
        out_specs=pl.BlockSpec(memory_space=pl.ANY),
        out_shape=jax.ShapeDtypeStruct((n_rows, D_MODEL), h.dtype),
        scratch_shapes=[pltpu.SemaphoreType.DMA(())],
        input_output_aliases={2: 0},
        compiler_params=_cparams(("arbitrary",)),
        name="moe_scatter",
    )(dest3, h, xs0)


def _moe_ffn_kernel(te_ref, na_ref, xs_ref, wg_ref, wu_ref, wd_ref, ys_ref, acc_scr):
    i = pl.program_id(0)
    f = pl.program_id(1)

    @pl.when(i < na_ref[0])
    def _():
        @pl.when(f == 0)
        def _():
            acc_scr[...] = jnp.zeros_like(acc_scr)

        acc_scr[...] += _swiglu_tile(xs_ref[...].astype(BF16), wg_ref[0].astype(BF16),
                                     wu_ref[0].astype(BF16), wd_ref[0].astype(BF16))

        @pl.when(f == pl.num_programs(1) - 1)
        def _():
            ys_ref[...] = acc_scr[...]

    @pl.when((i >= na_ref[0]) & (f == 0))
    def _():
        ys_ref[...] = jnp.zeros_like(ys_ref)


def _moe_ffn(tile_expert, n_active, xs, wg, wu, wd, tm, tf):
    n_rows = xs.shape[0]
    n_tiles = n_rows // tm
    nf = D_FF // tf

    def row_map(i, f, te, na):
        return (jnp.minimum(i, na[0] - 1), 0)

    def fcol(i, f, na):
        return jnp.where(i < na[0], f, nf - 1)

    grid_spec = pltpu.PrefetchScalarGridSpec(
        num_scalar_prefetch=2,
        grid=(n_tiles, nf),
        in_specs=[
            pl.BlockSpec((tm, D_MODEL), row_map),
            pl.BlockSpec((1, D_MODEL, tf), lambda i, f, te, na: (te[i], 0, fcol(i, f, na))),
            pl.BlockSpec((1, D_MODEL, tf), lambda i, f, te, na: (te[i], 0, fcol(i, f, na))),
            pl.BlockSpec((1, tf, D_MODEL), lambda i, f, te, na: (te[i], fcol(i, f, na), 0)),
        ],
        out_specs=pl.BlockSpec((tm, D_MODEL), lambda i, f, te, na: (i, 0)),
        scratch_shapes=[pltpu.VMEM((tm, D_MODEL), F32)],
    )
    return pl.pallas_call(
        _moe_ffn_kernel,
        grid_spec=grid_spec,
        out_shape=jax.ShapeDtypeStruct((n_rows, D_MODEL), F32),
        compiler_params=_cparams(("arbitrary", "arbitrary")),
        name="moe_ffn",
    )(tile_expert, n_active, xs, wg, wu, wd)


def _combine_kernel(dest_ref, x_ref, meta_ref, fw_ref, ys_hbm, o_ref, g_scr, sem, *, tc):
    def issue(t, c):
        for j in range(TOP_K):
            pltpu.make_async_copy(ys_hbm.at[pl.ds(dest_ref[0, 0, TOP_K * t + j], 1)],
                                  g_scr.at[j, pl.ds(t, 1)], sem).start(priority=j)
        return c

    lax.fori_loop(0, tc, issue, 0, unroll=DMA_ISSUE_UNROLL)
    for j in range(TOP_K):
        pltpu.make_async_copy(ys_hbm.at[pl.ds(0, tc)], g_scr.at[j], sem).wait()
    meta = meta_ref[...]
    w0 = meta[:, M_W0:M_W0 + 1]
    w1 = meta[:, M_W1:M_W1 + 1]
    x = x_ref[...] + w0 * g_scr[0] + w1 * g_scr[1]
    ms = jnp.mean(x * x, axis=-1, keepdims=True)
    o_ref[...] = x * lax.rsqrt(ms + RMS_EPS) * fw_ref[...]


def _combine(dest, x2, meta, fw, ys, tc):
    T = x2.shape[0]
    dest3 = dest.reshape(T // tc, 1, 2 * tc)
    return pl.pallas_call(
        functools.partial(_combine_kernel, tc=tc),
        grid=(T // tc,),
        in_specs=[
            pl.BlockSpec((1, 1, 2 * tc), lambda i: (i, 0, 0), memory_space=pltpu.SMEM),
            pl.BlockSpec((tc, D_MODEL), lambda i: (i, 0)),
            pl.BlockSpec((tc, LANE), lambda i: (i, 0)),
            pl.BlockSpec((1, D_MODEL), lambda i: (0, 0)),
            pl.BlockSpec(memory_space=pl.ANY),
        ],
        out_specs=pl.BlockSpec((tc, D_MODEL), lambda i: (i, 0)),
        out_shape=jax.ShapeDtypeStruct((T, D_MODEL), F32),
        scratch_shapes=[pltpu.VMEM((2, tc, D_MODEL), F32), pltpu.SemaphoreType.DMA(())],
        compiler_params=_cparams(("arbitrary",)),
        name="moe_combine",
    )(dest3, x2, meta, fw, ys)


def _moe_plan(meta, counts, tm, n_tiles):
    cnt = counts[0, :N_EXPERTS].astype(jnp.int32)
    padded = ((cnt + tm - 1) // tm) * tm
    ends = jnp.cumsum(padded)
    offs = ends - padded
    e = meta[:, M_E0:M_E1 + 1].astype(jnp.int32)
    pos = meta[:, M_P0:M_P1 + 1].astype(jnp.int32)
    dest = offs[e] + pos
    starts = jnp.arange(n_tiles, dtype=jnp.int32) * tm
    n_active = (ends[-1] // tm).astype(jnp.int32)
    te = jnp.sum(starts[:, None] >= ends[None, :], axis=-1).astype(jnp.int32)
    last_e = jnp.sum(jnp.maximum(ends[-1] - tm, 0) >= ends).astype(jnp.int32)
    te = jnp.where(starts < ends[-1], te, last_e)
    return dest, te, n_active.reshape(1)


def _moe_layer(x_mid, nw, w_router, wg, wu, wd, final_w, tiles):
    T = x_mid.shape[0]
    tm = tiles["moe_tm"]
    n_tiles = (TOP_K * T) // tm + N_EXPERTS
    h, meta, counts = _router(x_mid, nw, w_router, tiles["router_t"])
    dest, te, n_active = _moe_plan(meta, counts, tm, n_tiles)
    xs = _scatter_rows(dest, h, n_tiles * tm, tiles["scatter_t"])
    ys = _moe_ffn(te, n_active, xs, wg, wu, wd, tm, tiles["moe_tf"])
    return _combine(dest, x_mid, meta, final_w, ys, tiles["combine_t"])


def _tiles(S, T):
    return dict(
        inproj_tm=min(1024, T), inproj_tn=1920,
        attn_t=min(512, S // 2), ret_c=min(128, S),
        merge_tm=min(512, T), ffn_tm=min(1024, T), ffn_tf=512,
        router_t=min(256, T), scatter_t=min(256, T), combine_t=min(256, T),
        moe_tm=min(1024, T), moe_tf=512,
    )


def kernel(x, norm_mix_w, w_in, fox_f_bias, ret_gn_w, mla_q_norm_w, mla_kv_norm_w, mla_w_uq, mla_w_uk, mla_w_uv, w_br_fox, w_br_ret, w_br_mla, w_out, norm_ffn_w, dense_w_gate, dense_w_up, dense_w_down, moe_w_router, moe_w_gate, moe_w_up, moe_w_down, final_norm_w):
    B, S, D = x.shape
    assert D == D_MODEL and w_in.shape[0] == DEPTH == 2
    T = B * S
    tl = _tiles(S, T)
    row = lambda a: a.reshape(1, -1).astype(F32)
    ret_tabs = _ret_tables(S, tl["ret_c"])
    mla_tabs = _mla_tables(S)
    x2 = x.reshape(T, D)
    for l in range(DEPTH):
        w_all, w_ff = _build_inproj_weights(w_in[l])
        proj, ff32 = _inproj(x2, row(norm_mix_w[l]), w_all, w_ff, tl["inproj_tm"], tl["inproj_tn"])
        fqt, fkx, fvt = _fox_prep(proj, ff32, fox_f_bias[l], B, S, tl["attn_t"])
        o_fox = _attention(fqt, fkx, fvt, B, S, tl["attn_t"], FOX_RANGES, "fox_attn")
        o_ret = _retention(proj, row(ret_gn_w[l]), ret_tabs, B, S, tl["ret_c"])
        wq, wqr, wk, wv = _mla_weights(mla_w_uq[l], mla_w_uk[l], mla_w_uv[l])
        mqt, mkx, mvt = _mla_prep(proj, row(mla_q_norm_w[l]), row(mla_kv_norm_w[l]), wq, wqr, wk, wv,
                                  mla_tabs, B, S, tl["attn_t"])
        o_mla = _attention(mqt, mkx, mvt, B, S, tl["attn_t"], MLA_RANGES, "mla_attn")
        x_mid, h2 = _merge(o_fox, o_ret, o_mla, proj, x2,
                           w_br_fox[l].astype(BF16), w_br_ret[l].astype(BF16), w_br_mla[l].astype(BF16),
                           w_out[l].astype(BF16), row(norm_ffn_w[l]), tl["merge_tm"])
        if l % 2 == 0:
            i = l // 2
            x2 = _dense_ffn(h2, x_mid, dense_w_gate[i].astype(BF16), dense_w_up[i].astype(BF16),
                            dense_w_down[i].astype(BF16), tl["ffn_tm"], tl["ffn_tf"])
        else:
            i = l // 2
            x2 = _moe_layer(x_mid, row(norm_ffn_w[l]), moe_w_router[i], moe_w_gate[i],
                            moe_w_up[i], moe_w_down[i], row(final_norm_w), tl)
    return x2.reshape(B, S, D)
```

```python
import functools

import numpy as np
import jax
import jax.numpy as jnp
from jax import lax
from jax.experimental import pallas as pl
from jax.experimental.pallas import tpu as pltpu

F32 = jnp.float32
BF16 = jnp.bfloat16

D_MODEL = 1024
DEPTH = 2
FOX_HEADS = 8
FOX_HEAD_DIM = 64
FOX_W = FOX_HEADS * FOX_HEAD_DIM
RET_HEADS = 8
RET_QK_DIM = 64
RET_V_DIM = 128
RET_QK_W = RET_HEADS * RET_QK_DIM
RET_V_W = RET_HEADS * RET_V_DIM
MLA_HEADS = 8
MLA_Q_RANK = 384
MLA_KV_RANK = 256
MLA_NOPE_DIM = 64
MLA_ROPE_DIM = 32
MLA_V_DIM = 64
MLA_QK_DIM = MLA_NOPE_DIM + MLA_ROPE_DIM
MLA_V_W = MLA_HEADS * MLA_V_DIM
N_BRANCHES = 3
ROPE_THETA = 10000.0
RMS_EPS = 1e-6
D_FF = 3584
N_EXPERTS = 8
TOP_K = 2
IN_SPLITS = (FOX_W, FOX_W, FOX_W, FOX_HEADS,
             RET_QK_W, RET_QK_W, RET_V_W, RET_V_W,
             MLA_Q_RANK, MLA_KV_RANK, MLA_ROPE_DIM,
             N_BRANCHES * D_MODEL)

LANE = 128
NEG = -1e30
VMEM_LIMIT = 56 * 1024 * 1024

U_FQ, U_FK, U_FV = 0, 4, 8
U_RQ, U_RQR, U_RK, U_RKR = 12, 16, 20, 24
U_CKV, U_KR, U_KRR = 28, 30, 31
U_RV, U_RG, U_GL, U_CQ = 32, 40, 48, 72
N_PROJ = 75 * LANE
FOX_EXT_STRIDE = 8
PAIRS = 4
QK_W = 2 * LANE
ATTN_STRIP = 256
VT_ROWS = 80
LOG2E = 1.4426950408889634


def _cparams(sem, vmem=VMEM_LIMIT, flags=None):
    return pltpu.CompilerParams(dimension_semantics=sem, vmem_limit_bytes=vmem, flags=flags)


def _sigmoid(x):
    return 1.0 / (1.0 + jnp.exp(-x))


def _lane_mask(shape, ranges):
    lane = lax.broadcasted_iota(jnp.int32, shape, len(shape) - 1)
    m = None
    for a, b in ranges:
        r = (lane >= a) & (lane < b)
        m = r if m is None else (m | r)
    return m


def _inproj_kernel(x_ref, nw_ref, w_ref, wff_ref, out_ref, ff_ref, h_scr):
    @pl.when(pl.program_id(1) == 0)
    def _():
        x = x_ref[...]
        ms = jnp.mean(x * x, axis=-1, keepdims=True)
        h = (x * lax.rsqrt(ms + RMS_EPS) * nw_ref[...]).astype(BF16)
        h_scr[...] = h
        ff_ref[...] = jnp.dot(h, wff_ref[...], preferred_element_type=F32)

    out_ref[...] = jnp.dot(h_scr[...], w_ref[...], preferred_element_type=F32).astype(out_ref.dtype)


def _inproj(x2, nw, w_all, w_ff, tm, tn):
    T = x2.shape[0]
    return pl.pallas_call(
        _inproj_kernel,
        grid=(T // tm, N_PROJ // tn),
        in_specs=[
            pl.BlockSpec((tm, D_MODEL), lambda i, j: (i, 0)),
            pl.BlockSpec((1, D_MODEL), lambda i, j: (0, 0)),
            pl.BlockSpec((D_MODEL, tn), lambda i, j: (0, j)),
            pl.BlockSpec((D_MODEL, LANE), lambda i, j: (0, 0)),
        ],
        out_specs=[
            pl.BlockSpec((tm, tn), lambda i, j: (i, j)),
            pl.BlockSpec((tm, LANE), lambda i, j: (i, 0)),
        ],
        out_shape=[
            jax.ShapeDtypeStruct((T, N_PROJ), BF16),
            jax.ShapeDtypeStruct((T, LANE), F32),
        ],
        scratch_shapes=[pltpu.VMEM((tm, D_MODEL), BF16)],
        compiler_params=_cparams(("parallel", "arbitrary")),
        name="inproj",
    )(x2, nw, w_all, w_ff)


def _rot_cols(w, heads, d):
    k = w.shape[0]
    w4 = w.reshape(k, heads, 2, d // 2)
    return jnp.concatenate([-w4[:, :, 1], w4[:, :, 0]], axis=-1).reshape(k, heads * d)


def _build_inproj_weights(w_in):
    split_at = [int(i) for i in np.cumsum(IN_SPLITS)[:-1]]
    fq, fk, fv, ff, rq, rk, rv, rg, cq, ckv, kr, gl = jnp.split(w_in, split_at, axis=-1)
    z64 = jnp.zeros((D_MODEL, 64), F32)
    z32 = jnp.zeros((D_MODEL, 32), F32)
    kr128 = jnp.concatenate([z64, kr, z32], axis=-1)
    krr128 = jnp.concatenate([z64, _rot_cols(kr, 1, MLA_ROPE_DIM), z32], axis=-1)
    w_all = jnp.concatenate(
        [fq, fk, fv, rq, _rot_cols(rq, RET_HEADS, RET_QK_DIM), rk, _rot_cols(rk, RET_HEADS, RET_QK_DIM),
         ckv, kr128, krr128, rv, rg, gl, cq], axis=-1).astype(BF16)
    w_ff = jnp.pad(ff, ((0, 0), (0, LANE - FOX_HEADS))).astype(BF16)
    return w_all, w_ff


def _split3(x):
    hi = x.astype(BF16)
    r1 = x - hi.astype(F32)
    mid = r1.astype(BF16)
    lo = (r1 - mid.astype(F32)).astype(BF16)
    return hi, mid, lo


def _store_t(dst_ref, idx, x):
    dst_ref[idx] = x.T.astype(dst_ref.dtype)


def _store_vt_pair(vt_ref, p, v2):
    hd = LANE // 2
    vt = v2.T.astype(vt_ref.dtype)
    pad_rows = lax.broadcasted_iota(jnp.int32, (VT_ROWS - hd, vt.shape[1]), 0)
    ones_then_zeros = jnp.where(pad_rows == 0, 1.0, 0.0).astype(vt_ref.dtype)
    for j in range(2):
        base = (2 * p + j) * VT_ROWS
        vt_ref[0, 0, base:base + hd, :] = vt[j * hd:(j + 1) * hd]
        vt_ref[0, 0, base + hd:base + VT_ROWS, :] = ones_then_zeros


def _fox_prep_kernel(fq_ref, fk_ref, fv_ref, ff_ref, bias_ref, tri_ref, pq_ref, pk_ref, cq_ref, ck_ref,
                     qt_ref, kx_ref, vt_ref, carry_scr):
    @pl.when(pl.program_id(1) == 0)
    def _():
        carry_scr[...] = jnp.zeros_like(carry_scr)

    ts = ff_ref.shape[0]
    z = ff_ref[...] + bias_ref[...]
    logf = jnp.minimum(z, 0.0) - jnp.log(1.0 + jnp.exp(-jnp.abs(z)))
    tri = tri_ref[...]
    cum = carry_scr[0:1, :]
    for part in _split3(logf):
        cum = cum + jnp.dot(tri, part, preferred_element_type=F32)
    carry_scr[0:1, :] = cum[ts - 1:ts, :]

    eq = cq_ref[...]
    ek = ck_ref[...]
    for j, part in enumerate(_split3(cum * LOG2E)):
        eq = eq + jnp.dot(part, pq_ref[j * LANE:(j + 1) * LANE, :], preferred_element_type=F32)
        ek = ek + jnp.dot(part, pk_ref[j * LANE:(j + 1) * LANE, :], preferred_element_type=F32)
    scale = FOX_HEAD_DIM ** -0.5 * LOG2E
    for p in range(FOX_HEADS // 2):
        src = slice(p * LANE, (p + 1) * LANE)
        _store_t(qt_ref, (0, slice(2 * p * LANE, (2 * p + 1) * LANE)), fq_ref[:, src].astype(F32) * scale)
        _store_t(qt_ref, (0, slice((2 * p + 1) * LANE, (2 * p + 2) * LANE)), eq[:, src])
        kx_ref[:, 2 * p * LANE:(2 * p + 1) * LANE] = fk_ref[:, src]
        kx_ref[:, (2 * p + 1) * LANE:(2 * p + 2) * LANE] = ek[:, src].astype(BF16)
        _store_vt_pair(vt_ref, p, fv_ref[:, src].astype(F32))


def _fox_prep_consts(ts):
    tri = np.tril(np.ones((ts, ts), np.float32))
    pq = np.zeros((3 * LANE, FOX_W), np.float32)
    pk = np.zeros((3 * LANE, FOX_W), np.float32)
    cq = np.zeros((1, FOX_W), np.float32)
    ck = np.zeros((1, FOX_W), np.float32)
    for h in range(FOX_HEADS):
        base = LANE * (h // 2) + FOX_EXT_STRIDE * (h % 2)
        for j in range(3):
            pq[j * LANE + h, base + j] = 1.0
            ck[0, base + j] = 1.0
            cq[0, base + 3 + j] = 1.0
            pk[j * LANE + h, base + 3 + j] = -1.0
    return (jnp.asarray(tri, BF16), jnp.asarray(pq, BF16), jnp.asarray(pk, BF16),
            jnp.asarray(cq), jnp.asarray(ck))


def _attn_operand_specs(B, S, t):
    T = B * S
    ns = S // t
    specs = [
        pl.BlockSpec((1, PAIRS * QK_W, t), lambda b, s: (b, 0, s)),
        pl.BlockSpec((t, PAIRS * QK_W), lambda b, s: (b * ns + s, 0)),
        pl.BlockSpec((1, 1, 2 * PAIRS * VT_ROWS, t), lambda b, s: (b, s, 0, 0)),
    ]
    shapes = [
        jax.ShapeDtypeStruct((B, PAIRS * QK_W, S), BF16),
        jax.ShapeDtypeStruct((T, PAIRS * QK_W), BF16),
        jax.ShapeDtypeStruct((B, ns, 2 * PAIRS * VT_ROWS, t), BF16),
    ]
    return specs, shapes


def _fox_prep(proj, ff32, bias, B, S, ts):
    ns = S // ts
    tri, pq, pk, cq, ck = _fox_prep_consts(ts)
    bias128 = jnp.pad(bias.astype(F32), (0, LANE - FOX_HEADS)).reshape(1, LANE)
    const = lambda shape: pl.BlockSpec(shape, lambda b, s: (0, 0))
    out_specs, out_shape = _attn_operand_specs(B, S, ts)
    return pl.pallas_call(
        _fox_prep_kernel,
        grid=(B, ns),
        in_specs=[
            pl.BlockSpec((ts, FOX_W), lambda b, s: (b * ns + s, U_FQ // 4)),
            pl.BlockSpec((ts, FOX_W), lambda b, s: (b * ns + s, U_FK // 4)),
            pl.BlockSpec((ts, FOX_W), lambda b, s: (b * ns + s, U_FV // 4)),
            pl.BlockSpec((ts, LANE), lambda b, s: (b * ns + s, 0)),
            const((1, LANE)), const((ts, ts)), const((3 * LANE, FOX_W)), const((3 * LANE, FOX_W)),
            const((1, FOX_W)), const((1, FOX_W)),
        ],
        out_specs=out_specs,
        out_shape=out_shape,
        scratch_shapes=[pltpu.VMEM((8, LANE), F32)],
        compiler_params=_cparams(("parallel", "arbitrary")),
        name="fox_prep",
    )(proj, proj, proj, ff32, bias128, tri, pq, pk, cq, ck)


def _row_mask(shape, ranges):
    row = lax.broadcasted_iota(jnp.int32, shape, 0)
    m = None
    for a, b in ranges:
        r = (row >= a) & (row < b)
        m = r if m is None else (m | r)
    return m


def _col_reduce(x, op, reduce_fn, parts=2):
    c = x.shape[0] // parts
    pieces = [x[i * c:(i + 1) * c] for i in range(parts)]
    while len(pieces) > 1:
        pieces = [op(pieces[i], pieces[i + 1]) for i in range(0, len(pieces), 2)]
    return reduce_fn(pieces[0], axis=0, keepdims=True)


def _attn_kernel(qt_ref, k_ref, vt_ref, o_ref, qh_scr, sa_scr, sb_scr, ma_scr, mb_scr, acc_scr, *, tk, ranges):
    qi = pl.program_id(2)
    qt = qt_ref[0]
    tq = qt.shape[-1]
    for h in range(2):
        qh_scr[h] = jnp.where(_row_mask(qt.shape, ranges[h]), qt, jnp.zeros_like(qt))
    acc_scr[...] = jnp.zeros_like(acc_scr)

    def key_tile(kt):
        return k_ref[pl.ds(pl.multiple_of(kt * tk, tk), tk), :]

    strips = [(c, c + ATTN_STRIP) for c in range(0, tq, ATTN_STRIP)]

    def scores(kt, s_scr, m_scr):
        k = key_tile(kt)
        for h in range(2):
            for a, b in strips:
                s = jnp.dot(k, qh_scr[h, :, a:b], preferred_element_type=F32)
                s_scr[h, :, a:b] = s
                m_scr[h, :, a:b] = _col_reduce(s, jnp.maximum, jnp.max)

    def update(kt, h, s, m_tile, m_prev, lo):
        m_new = jnp.maximum(m_prev, m_tile)
        alpha = jnp.exp2(m_prev - m_new)
        p = jnp.exp2((s - m_new).astype(BF16))
        vt = vt_ref[0, kt, h * VT_ROWS:(h + 1) * VT_ROWS, :]
        hi = lo + s.shape[-1]
        acc_scr[h, :, lo:hi] = alpha * acc_scr[h, :, lo:hi] + jnp.dot(vt, p, preferred_element_type=F32)
        return m_new

    def softmax_pv(kt, s_scr, m_scr, carry):
        out = []
        for h in range(2):
            ms = [update(kt, h, s_scr[h, :, a:b], m_scr[h, :, a:b], carry[h][:, a:b], a) for a, b in strips]
            out.append(jnp.concatenate(ms, axis=-1))
        return tuple(out)

    def softmax_pv_diag(kt, s_scr, carry, diag):
        lo = diag * tk
        out = []
        for h in range(2):
            s = s_scr[h, :, lo:]
            key = lax.broadcasted_iota(jnp.int32, s.shape, 0)
            qry = lax.broadcasted_iota(jnp.int32, s.shape, 1) + (lo - diag * tk)
            s = jnp.where(key <= qry, s, NEG)
            m_tile = _col_reduce(s, jnp.maximum, jnp.max)
            m_new = update(kt, h, s, m_tile, carry[h][:, lo:], lo)
            out.append(m_new if lo == 0 else jnp.concatenate([carry[h][:, :lo], m_new], axis=-1))
        return tuple(out)

    def body(j, carry):
        kt = 2 * j
        scores(kt + 1, sb_scr, mb_scr)
        carry = softmax_pv(kt, sa_scr, ma_scr, carry)
        scores(kt + 2, sa_scr, ma_scr)
        return softmax_pv(kt + 1, sb_scr, mb_scr, carry)

    init = tuple(jnp.full((1, tq), NEG, F32) for _ in range(2))
    scores(0, sa_scr, ma_scr)
    carry = lax.fori_loop(0, qi, body, init)
    k_last = key_tile(2 * qi + 1)
    for h in range(2):
        sb_scr[h, :, tk:] = jnp.dot(k_last, qh_scr[h, :, tk:], preferred_element_type=F32)
    carry = softmax_pv_diag(2 * qi, sa_scr, carry, 0)
    carry = softmax_pv_diag(2 * qi + 1, sb_scr, carry, 1)
    hd = LANE // 2
    ot = jnp.concatenate([acc_scr[h, :hd] / acc_scr[h, hd:hd + 1] for h in range(2)], axis=0)
    o_ref[...] = ot.T.astype(o_ref.dtype)


def _attention(qt, kx, vt, B, S, tk, ranges, name):
    T = B * S
    tq = 2 * tk
    nq = S // tq
    kern = functools.partial(_attn_kernel, tk=tk, ranges=ranges)
    return pl.pallas_call(
        kern,
        grid=(B, PAIRS, nq),
        in_specs=[
            pl.BlockSpec((1, QK_W, tq), lambda b, p, i: (b, p, i)),
            pl.BlockSpec((S, QK_W), lambda b, p, i: (b, p)),
            pl.BlockSpec((1, S // tk, 2 * VT_ROWS, tk), lambda b, p, i: (b, 0, p, 0)),
        ],
        out_specs=pl.BlockSpec((tq, LANE), lambda b, p, i: (b * nq + i, p)),
        out_shape=jax.ShapeDtypeStruct((T, PAIRS * LANE), BF16),
        scratch_shapes=[
            pltpu.VMEM((2, QK_W, tq), BF16),
            pltpu.VMEM((2, tk, tq), F32),
            pltpu.VMEM((2, tk, tq), F32),
            pltpu.VMEM((2, 1, tq), F32),
            pltpu.VMEM((2, 1, tq), F32),
            pltpu.VMEM((2, VT_ROWS, tq), F32),
        ],
        compiler_params=_cparams(("parallel", "parallel", "arbitrary")),
        name=name,
    )(qt, kx, vt)


_E0 = LANE
FOX_RANGES = (((0, 64), (_E0, _E0 + FOX_EXT_STRIDE)),
              ((64, 128), (_E0 + FOX_EXT_STRIDE, _E0 + 2 * FOX_EXT_STRIDE)))
MLA_RANGES = (((0, LANE),), ((LANE, 2 * LANE),))


def _mla_prep_kernel(cq_ref, ckv_ref, kr_ref, krr_ref, qnw_ref, kvnw_ref, wq_ref, wqr_ref, wk_ref, wv_ref,
                     cq_tab, sq_tab, ck_tab, sk_tab, qt_ref, k_ref, vt_ref):
    def norm(x_ref, w_ref):
        x = x_ref[...].astype(F32)
        ms = jnp.mean(x * x, axis=-1, keepdims=True)
        return (x * lax.rsqrt(ms + RMS_EPS) * w_ref[...]).astype(BF16)

    c_q = norm(cq_ref, qnw_ref)
    c_kv = norm(ckv_ref, kvnw_ref)
    tile8 = lambda a: jnp.concatenate([a] * MLA_HEADS, axis=-1)
    q = (jnp.dot(c_q, wq_ref[...], preferred_element_type=F32) * tile8(cq_tab[...])
         + jnp.dot(c_q, wqr_ref[...], preferred_element_type=F32) * tile8(sq_tab[...]))
    for c in range(MLA_HEADS):
        _store_t(qt_ref, (0, slice(c * LANE, (c + 1) * LANE)), q[:, c * LANE:(c + 1) * LANE])
    k_rope = kr_ref[...].astype(F32) * ck_tab[...] + krr_ref[...].astype(F32) * sk_tab[...]
    k = jnp.dot(c_kv, wk_ref[...], preferred_element_type=F32) + tile8(k_rope)
    k_ref[...] = k.astype(BF16)
    v = jnp.dot(c_kv, wv_ref[...], preferred_element_type=F32)
    for p in range(PAIRS):
        _store_vt_pair(vt_ref, p, v[:, p * LANE:(p + 1) * LANE])


def _mla_weights(w_uq, w_uk, w_uv):
    q3 = w_uq.reshape(MLA_Q_RANK, MLA_HEADS, MLA_QK_DIM)
    nope, ropep = q3[..., :MLA_NOPE_DIM], q3[..., MLA_NOPE_DIM:]
    half = MLA_ROPE_DIM // 2
    rot = jnp.concatenate([-ropep[..., half:], ropep[..., :half]], axis=-1)
    z32 = jnp.zeros((MLA_Q_RANK, MLA_HEADS, LANE - MLA_QK_DIM), F32)
    z64 = jnp.zeros((MLA_Q_RANK, MLA_HEADS, MLA_NOPE_DIM), F32)
    wq = jnp.concatenate([nope, ropep, z32], axis=-1).reshape(MLA_Q_RANK, MLA_HEADS * LANE)
    wqr = jnp.concatenate([z64, rot, z32], axis=-1).reshape(MLA_Q_RANK, MLA_HEADS * LANE)
    k3 = w_uk.reshape(MLA_KV_RANK, MLA_HEADS, MLA_NOPE_DIM)
    wk = jnp.concatenate([k3, jnp.zeros_like(k3)], axis=-1).reshape(MLA_KV_RANK, MLA_HEADS * LANE)
    return wq.astype(BF16), wqr.astype(BF16), wk.astype(BF16), w_uv.astype(BF16)


def _mla_tables(S):
    pos = jnp.arange(S, dtype=F32)
    inv = ROPE_THETA ** (-jnp.arange(0, MLA_ROPE_DIM, 2, dtype=F32) / MLA_ROPE_DIM)
    ang = pos[:, None] * inv[None, :]
    cos, sin = jnp.cos(ang), jnp.sin(ang)
    ones = jnp.ones((S, MLA_NOPE_DIM), F32)
    z64 = jnp.zeros((S, MLA_NOPE_DIM), F32)
    z32 = jnp.zeros((S, LANE - MLA_QK_DIM), F32)
    ctab = jnp.concatenate([ones, cos, cos, z32], axis=-1)
    stab = jnp.concatenate([z64, sin, sin, z32], axis=-1)
    scale = MLA_QK_DIM ** -0.5 * LOG2E
    return ctab * scale, stab * scale, ctab, stab


def _mla_prep(proj, qnw, kvnw, wq, wqr, wk, wv, tabs, B, S, tm):
    ns = S // tm
    HW = MLA_HEADS * LANE
    const = lambda shape: pl.BlockSpec(shape, lambda b, s: (0, 0))
    tab = pl.BlockSpec((tm, LANE), lambda b, s: (s, 0))
    out_specs, out_shape = _attn_operand_specs(B, S, tm)
    return pl.pallas_call(
        _mla_prep_kernel,
        grid=(B, ns),
        in_specs=[
            pl.BlockSpec((tm, MLA_Q_RANK), lambda b, s: (b * ns + s, U_CQ // 3)),
            pl.BlockSpec((tm, MLA_KV_RANK), lambda b, s: (b * ns + s, U_CKV // 2)),
            pl.BlockSpec((tm, LANE), lambda b, s: (b * ns + s, U_KR)),
            pl.BlockSpec((tm, LANE), lambda b, s: (b * ns + s, U_KRR)),
            const((1, MLA_Q_RANK)), const((1, MLA_KV_RANK)),
            const((MLA_Q_RANK, HW)), const((MLA_Q_RANK, HW)), const((MLA_KV_RANK, HW)),
            const((MLA_KV_RANK, MLA_V_W)),
            tab, tab, tab, tab,
        ],
        out_specs=out_specs,
        out_shape=out_shape,
        compiler_params=_cparams(("parallel", "parallel")),
        name="mla_prep",
    )(proj, proj, proj, proj, qnw, kvnw, wq, wqr, wk, wv, *tabs)


def _ret_kernel(rq_ref, rqr_ref, rk_ref, rkr_ref, rv_ref, rg_ref, cos_ref, sin_ref,
                din_ref, xi_ref, zeta_ref, gch_ref, gnw_ref, o_ref, r_scr):
    @pl.when(pl.program_id(0) == 0)
    def _():
        r_scr[...] = jnp.zeros_like(r_scr)

    tile4 = lambda a: jnp.concatenate([a] * (RET_QK_W // LANE), axis=-1)
    cos4 = tile4(cos_ref[...])
    sin4 = tile4(sin_ref[...])
    for bi in range(rq_ref.shape[0]):
        q = rq_ref[bi].astype(F32) * cos4 + rqr_ref[bi].astype(F32) * sin4
        k = (rk_ref[bi].astype(F32) * cos4 + rkr_ref[bi].astype(F32) * sin4) * (RET_QK_DIM ** -0.5)
        for p in range(RET_HEADS // 2):
            q2 = q[:, p * LANE:(p + 1) * LANE]
            k2 = k[:, p * LANE:(p + 1) * LANE]
            k2b = k2.astype(BF16)
            kz2 = (k2 * zeta_ref[p]).astype(BF16)
            r2 = r_scr[bi, p]
            r2b = r2.astype(BF16)
            new_r = gch_ref[p] * r2
            for j in range(2):
                h = 2 * p + j
                half = ((j * RET_QK_DIM, (j + 1) * RET_QK_DIM),)
                qm = jnp.where(_lane_mask(q2.shape, half), q2, 0.0).astype(BF16)
                inner = lax.dot_general(qm, k2b, (((1,), (1,)), ((), ())), preferred_element_type=F32)
                inner = inner * din_ref[h]
                vh = rv_ref[bi, :, h * LANE:(h + 1) * LANE]
                o = (jnp.dot(inner.astype(BF16), vh, preferred_element_type=F32)
                     + jnp.dot(qm, r2b, preferred_element_type=F32) * xi_ref[h])
                upd = lax.dot_general(kz2, vh, (((0,), (0,)), ((), ())), preferred_element_type=F32)
                rows = lax.broadcasted_iota(jnp.int32, upd.shape, 0)
                new_r = new_r + jnp.where((rows >= half[0][0]) & (rows < half[0][1]), upd, 0.0)
                mu = jnp.mean(o, axis=-1, keepdims=True)
                d = o - mu
                var = jnp.mean(d * d, axis=-1, keepdims=True)
                on = d * lax.rsqrt(var + RMS_EPS) * gnw_ref[:, h * LANE:(h + 1) * LANE]
                g = rg_ref[bi, :, h * LANE:(h + 1) * LANE].astype(F32)
                o_ref[bi, :, h * LANE:(h + 1) * LANE] = (g * _sigmoid(g) * on).astype(o_ref.dtype)
            r_scr[bi, p] = new_r


def _ret_tables(S, C):
    pos = jnp.arange(S, dtype=F32)
    inv = ROPE_THETA ** (-jnp.arange(0, RET_QK_DIM, 2, dtype=F32) / RET_QK_DIM)
    ang = pos[:, None] * inv[None, :]
    cos = jnp.concatenate([jnp.cos(ang)] * 4, axis=-1)
    sin = jnp.concatenate([jnp.sin(ang)] * 4, axis=-1)
    gammas = 1.0 - 2.0 ** (-5.0 - jnp.arange(RET_HEADS, dtype=F32))
    log_g = jnp.log(gammas)
    j = jnp.arange(C, dtype=F32)
    diff = j[:, None] - j[None, :]
    din = jnp.where(diff[None] >= 0, jnp.exp(jnp.maximum(diff, 0.0)[None] * log_g[:, None, None]), 0.0)
    xi = jnp.exp((j[None, :] + 1.0) * log_g[:, None])
    zeta = jnp.exp((C - 1.0 - j[None, :]) * log_g[:, None])
    gch = jnp.exp(C * log_g)
    xi_t = jnp.broadcast_to(xi[:, :, None], (RET_HEADS, C, LANE))
    zeta_p = jnp.repeat(zeta.reshape(RET_HEADS // 2, 2, C).transpose(0, 2, 1), RET_QK_DIM, axis=-1)
    gch_p = jnp.broadcast_to(jnp.repeat(gch.reshape(RET_HEADS // 2, 2), RET_QK_DIM, axis=-1)[:, :, None],
                             (RET_HEADS // 2, LANE, LANE))
    return cos, sin, din, xi_t, zeta_p, gch_p


def _retention(proj, gnw, tabs, B, S, C):
    nc = S // C
    cos, sin, din, xi, zeta, gch = tabs
    proj3 = proj.reshape(B, S, N_PROJ)
    blk = lambda w, unit: pl.BlockSpec((B, C, w), lambda c: (0, c, unit * LANE // w))
    full = lambda a: pl.BlockSpec(a.shape, lambda c: (0,) * a.ndim)
    out = pl.pallas_call(
        _ret_kernel,
        grid=(nc,),
        in_specs=[
            blk(RET_QK_W, U_RQ), blk(RET_QK_W, U_RQR), blk(RET_QK_W, U_RK), blk(RET_QK_W, U_RKR),
            blk(RET_V_W, U_RV), blk(RET_V_W, U_RG),
            pl.BlockSpec((C, LANE), lambda c: (c, 0)),
            pl.BlockSpec((C, LANE), lambda c: (c, 0)),
            full(din), full(xi), full(zeta), full(gch), full(gnw),
        ],
        out_specs=pl.BlockSpec((B, C, RET_V_W), lambda c: (0, c, 0)),
        out_shape=jax.ShapeDtypeStruct((B, S, RET_V_W), BF16),
        scratch_shapes=[pltpu.VMEM((B, RET_HEADS // 2, LANE, LANE), F32)],
        compiler_params=_cparams(("arbitrary",)),
        name="retention",
    )(proj3, proj3, proj3, proj3, proj3, proj3, cos, sin, din, xi, zeta, gch, gnw)
    return out.reshape(B * S, RET_V_W)


def _merge_kernel(of_ref, or_ref, om_ref, g0_ref, g1_ref, g2_ref, x_ref,
                  wf_ref, wr_ref, wm_ref, wo_ref, nw_ref, xo_ref, h_ref):
    merged = (_sigmoid(g0_ref[...].astype(F32)) * jnp.dot(of_ref[...], wf_ref[...], preferred_element_type=F32)
              + _sigmoid(g1_ref[...].astype(F32)) * jnp.dot(or_ref[...], wr_ref[...], preferred_element_type=F32)
              + _sigmoid(g2_ref[...].astype(F32)) * jnp.dot(om_ref[...], wm_ref[...], preferred_element_type=F32))
    x = x_ref[...] + jnp.dot(merged.astype(BF16), wo_ref[...], preferred_element_type=F32)
    xo_ref[...] = x
    ms = jnp.mean(x * x, axis=-1, keepdims=True)
    h_ref[...] = (x * lax.rsqrt(ms + RMS_EPS) * nw_ref[...]).astype(h_ref.dtype)


def _merge(o_fox, o_ret, o_mla, proj, x2, wf, wr, wm, wo, nw, tm):
    T = x2.shape[0]
    row = lambda w, cb=0: pl.BlockSpec((tm, w), lambda i: (i, cb))
    const = lambda a: pl.BlockSpec(a.shape, lambda i: (0, 0))
    g_unit = U_GL // 8
    return pl.pallas_call(
        _merge_kernel,
        grid=(T // tm,),
        in_specs=[
            row(FOX_W), row(RET_V_W), row(MLA_V_W),
            row(D_MODEL, g_unit), row(D_MODEL, g_unit + 1), row(D_MODEL, g_unit + 2),
            row(D_MODEL),
            const(wf), const(wr), const(wm), const(wo), const(nw),
        ],
        out_specs=[row(D_MODEL), row(D_MODEL)],
        out_shape=[jax.ShapeDtypeStruct((T, D_MODEL), F32), jax.ShapeDtypeStruct((T, D_MODEL), BF16)],
        compiler_params=_cparams(("parallel",)),
        name="merge",
    )(o_fox, o_ret, o_mla, proj, proj, proj, x2, wf, wr, wm, wo, nw)


def _swiglu_tile(h, wg, wu, wd):
    a = jnp.dot(h, wg, preferred_element_type=F32)
    b = jnp.dot(h, wu, preferred_element_type=F32)
    return jnp.dot((a * _sigmoid(a) * b).astype(BF16), wd, preferred_element_type=F32)


def _dense_ffn_kernel(h_ref, wg_ref, wu_ref, wd_ref, x_ref, o_ref, acc_scr):
    f = pl.program_id(1)

    @pl.when(f == 0)
    def _():
        acc_scr[...] = jnp.zeros_like(acc_scr)

    acc_scr[...] += _swiglu_tile(h_ref[...], wg_ref[...], wu_ref[...], wd_ref[...])

    @pl.when(f == pl.num_programs(1) - 1)
    def _():
        o_ref[...] = x_ref[...] + acc_scr[...]


def _dense_ffn(h, x2, wg, wu, wd, tm, tf):
    T = x2.shape[0]
    return pl.pallas_call(
        _dense_ffn_kernel,
        grid=(T // tm, D_FF // tf),
        in_specs=[
            pl.BlockSpec((tm, D_MODEL), lambda i, f: (i, 0)),
            pl.BlockSpec((D_MODEL, tf), lambda i, f: (0, f)),
            pl.BlockSpec((D_MODEL, tf), lambda i, f: (0, f)),
            pl.BlockSpec((tf, D_MODEL), lambda i, f: (f, 0)),
            pl.BlockSpec((tm, D_MODEL), lambda i, f: (i, 0)),
        ],
        out_specs=pl.BlockSpec((tm, D_MODEL), lambda i, f: (i, 0)),
        out_shape=jax.ShapeDtypeStruct((T, D_MODEL), F32),
        scratch_shapes=[pltpu.VMEM((tm, D_MODEL), F32)],
        compiler_params=_cparams(("parallel", "arbitrary")),
        name="dense_ffn",
    )(h, wg, wu, wd, x2)


DMA_ISSUE_UNROLL = 8

M_E0, M_E1, M_P0, M_P1, M_W0, M_W1 = 0, 1, 2, 3, 4, 5


def _split2(x):
    hi = x.astype(BF16)
    lo = (x - hi.astype(F32)).astype(BF16)
    return hi, lo


def _router_kernel(x_ref, nw_ref, whi_ref, wlo_ref, tri_ref, h_ref, meta_ref, cnt_ref, carry_scr):
    @pl.when(pl.program_id(0) == 0)
    def _():
        carry_scr[...] = jnp.zeros_like(carry_scr)

    x = x_ref[...]
    ms = jnp.mean(x * x, axis=-1, keepdims=True)
    h = x * lax.rsqrt(ms + RMS_EPS) * nw_ref[...]
    h_ref[...] = h
    hhi, hlo = _split2(h)
    logits = (jnp.dot(hhi, whi_ref[...], preferred_element_type=F32)
              + jnp.dot(hhi, wlo_ref[...], preferred_element_type=F32)
              + jnp.dot(hlo, whi_ref[...], preferred_element_type=F32))
    lane = lax.broadcasted_iota(jnp.int32, logits.shape, 1)
    lanef = lane.astype(F32)
    logits = jnp.where(lane < N_EXPERTS, logits, NEG)
    m0 = jnp.max(logits, axis=-1, keepdims=True)
    e0 = jnp.min(jnp.where(logits == m0, lanef, float(LANE)), axis=-1, keepdims=True)
    oh0 = lanef == e0
    rest = jnp.where(oh0, NEG, logits)
    m1 = jnp.max(rest, axis=-1, keepdims=True)
    e1 = jnp.min(jnp.where(rest == m1, lanef, float(LANE)), axis=-1, keepdims=True)
    oh1 = lanef == e1
    z = jnp.exp(m1 - m0)
    w0 = 1.0 / (1.0 + z)
    w1 = z / (1.0 + z)
    both = jnp.where(oh0 | oh1, 1.0, 0.0)
    before = carry_scr[0:1, :] + jnp.dot(tri_ref[...], both.astype(BF16), preferred_element_type=F32)
    p0 = jnp.sum(jnp.where(oh0, before, 0.0), axis=-1, keepdims=True)
    p1 = jnp.sum(jnp.where(oh1, before, 0.0), axis=-1, keepdims=True)
    total = carry_scr[0:1, :] + jnp.sum(both, axis=0, keepdims=True)
    carry_scr[0:1, :] = total
    cnt_ref[...] = jnp.broadcast_to(total, cnt_ref.shape)
    meta = jnp.zeros(logits.shape, F32)
    for idx, val in ((M_E0, e0), (M_E1, e1), (M_P0, p0), (M_P1, p1), (M_W0, w0), (M_W1, w1)):
        meta = jnp.where(lane == idx, val, meta)
    meta_ref[...] = meta


def _router(x2, nw, w_router, tr):
    T = x2.shape[0]
    wr = jnp.pad(w_router.astype(F32), ((0, 0), (0, LANE - N_EXPERTS)))
    whi = wr.astype(BF16)
    wlo = (wr - whi.astype(F32)).astype(BF16)
    tri = jnp.asarray(np.tril(np.ones((tr, tr), np.float32), -1), BF16)
    const = lambda a: pl.BlockSpec(a.shape, lambda i: (0, 0))
    return pl.pallas_call(
        _router_kernel,
        grid=(T // tr,),
        in_specs=[pl.BlockSpec((tr, D_MODEL), lambda i: (i, 0)), const(nw), const(whi), const(wlo), const(tri)],
        out_specs=[
            pl.BlockSpec((tr, D_MODEL), lambda i: (i, 0)),
            pl.BlockSpec((tr, LANE), lambda i: (i, 0)),
            pl.BlockSpec((8, LANE), lambda i: (0, 0)),
        ],
        out_shape=[
            jax.ShapeDtypeStruct((T, D_MODEL), F32),
            jax.ShapeDtypeStruct((T, LANE), F32),
            jax.ShapeDtypeStruct((8, LANE), F32),
        ],
        scratch_shapes=[pltpu.VMEM((8, LANE), F32)],
        compiler_params=_cparams(("arbitrary",)),
        name="router",
    )(x2, nw, whi, wlo, tri)


def _scatter_kernel(dest_ref, h_ref, xs_in, xs_hbm, sem, *, tt):
    del xs_in

    def row_copy(t, d_row):
        return pltpu.make_async_copy(h_ref.at[pl.ds(t, 1)], xs_hbm.at[pl.ds(d_row, 1)], sem)

    def issue(t, c):
        row_copy(t, dest_ref[0, 0, 2 * t]).start(priority=0)
        row_copy(t, dest_ref[0, 0, 2 * t + 1]).start(priority=1)
        return c

    lax.fori_loop(0, tt, issue, 0, unroll=DMA_ISSUE_UNROLL)
    for _ in range(TOP_K):
        pltpu.make_async_copy(h_ref, xs_hbm.at[pl.ds(0, tt)], sem).wait()


def _scatter_rows(dest, h, n_rows, tt):
    T = h.shape[0]
    xs0 = jnp.zeros((n_rows, D_MODEL), h.dtype)
    dest3 = dest.reshape(T // tt, 1, 2 * tt)
    return pl.pallas_call(
        functools.partial(_scatter_kernel, tt=tt),
        grid=(T // tt,),
        in_specs=[
            pl.BlockSpec((1, 1, 2 * tt), lambda i: (i, 0, 0), memory_space=pltpu.SMEM),
            pl.BlockSpec((tt, D_MODEL), lambda i: (i, 0)),
            pl.BlockSpec(memory_space=pl.ANY),
        ],
        out_specs=pl.BlockSpec(memory_space=pl.ANY),
        out_shape=jax.ShapeDtypeStruct((n_rows, D_MODEL), h.dtype),
        scratch_shapes=[pltpu.SemaphoreType.DMA(())],
        input_output_aliases={2: 0},
        compiler_params=_cparams(("arbitrary",)),
        name="moe_scatter",
    )(dest3, h, xs0)


def _moe_ffn_kernel(te_ref, na_ref, xs_ref, wg_ref, wu_ref, wd_ref, ys_ref, acc_scr):
    i = pl.program_id(0)
    f = pl.program_id(1)

    @pl.when(i < na_ref[0])
    def _():
        @pl.when(f == 0)
        def _():
            acc_scr[...] = jnp.zeros_like(acc_scr)

        acc_scr[...] += _swiglu_tile(xs_ref[...].astype(BF16), wg_ref[0].astype(BF16),
                                     wu_ref[0].astype(BF16), wd_ref[0].astype(BF16))

        @pl.when(f == pl.num_programs(1) - 1)
        def _():
            ys_ref[...] = acc_scr[...]

    @pl.when((i >= na_ref[0]) & (f == 0))
    def _():
        ys_ref[...] = jnp.zeros_like(ys_ref)


def _moe_ffn(tile_expert, n_active, xs, wg, wu, wd, tm, tf):
    n_rows = xs.shape[0]
    n_tiles = n_rows // tm
    nf = D_FF // tf

    def row_map(i, f, te, na):
        return (jnp.minimum(i, na[0] - 1), 0)

    def fcol(i, f, na):
        return jnp.where(i < na[0], f, nf - 1)

    grid_spec = pltpu.PrefetchScalarGridSpec(
        num_scalar_prefetch=2,
        grid=(n_tiles, nf),
        in_specs=[
            pl.BlockSpec((tm, D_MODEL), row_map),
            pl.BlockSpec((1, D_MODEL, tf), lambda i, f, te, na: (te[i], 0, fcol(i, f, na))),
            pl.BlockSpec((1, D_MODEL, tf), lambda i, f, te, na: (te[i], 0, fcol(i, f, na))),
            pl.BlockSpec((1, tf, D_MODEL), lambda i, f, te, na: (te[i], fcol(i, f, na), 0)),
        ],
        out_specs=pl.BlockSpec((tm, D_MODEL), lambda i, f, te, na: (i, 0)),
        scratch_shapes=[pltpu.VMEM((tm, D_MODEL), F32)],
    )
    return pl.pallas_call(
        _moe_ffn_kernel,
        grid_spec=grid_spec,
        out_shape=jax.ShapeDtypeStruct((n_rows, D_MODEL), F32),
        compiler_params=_cparams(("arbitrary", "arbitrary")),
        name="moe_ffn",
    )(tile_expert, n_active, xs, wg, wu, wd)


def _combine_kernel(dest_ref, x_ref, meta_ref, fw_ref, ys_hbm, o_ref, g_scr, sem, *, tc):
    def issue(t, c):
        for j in range(TOP_K):
            pltpu.make_async_copy(ys_hbm.at[pl.ds(dest_ref[0, 0, TOP_K * t + j], 1)],
                                  g_scr.at[j, pl.ds(t, 1)], sem).start(priority=j)
        return c

    lax.fori_loop(0, tc, issue, 0, unroll=DMA_ISSUE_UNROLL)
    for j in range(TOP_K):
        pltpu.make_async_copy(ys_hbm.at[pl.ds(0, tc)], g_scr.at[j], sem).wait()
    meta = meta_ref[...]
    w0 = meta[:, M_W0:M_W0 + 1]
    w1 = meta[:, M_W1:M_W1 + 1]
    x = x_ref[...] + w0 * g_scr[0] + w1 * g_scr[1]
    ms = jnp.mean(x * x, axis=-1, keepdims=True)
    o_ref[...] = x * lax.rsqrt(ms + RMS_EPS) * fw_ref[...]


def _combine(dest, x2, meta, fw, ys, tc):
    T = x2.shape[0]
    dest3 = dest.reshape(T // tc, 1, 2 * tc)
    return pl.pallas_call(
        functools.partial(_combine_kernel, tc=tc),
        grid=(T // tc,),
        in_specs=[
            pl.BlockSpec((1, 1, 2 * tc), lambda i: (i, 0, 0), memory_space=pltpu.SMEM),
            pl.BlockSpec((tc, D_MODEL), lambda i: (i, 0)),
            pl.BlockSpec((tc, LANE), lambda i: (i, 0)),
            pl.BlockSpec((1, D_MODEL), lambda i: (0, 0)),
            pl.BlockSpec(memory_space=pl.ANY),
        ],
        out_specs=pl.BlockSpec((tc, D_MODEL), lambda i: (i, 0)),
        out_shape=jax.ShapeDtypeStruct((T, D_MODEL), F32),
        scratch_shapes=[pltpu.VMEM((2, tc, D_MODEL), F32), pltpu.SemaphoreType.DMA(())],
        compiler_params=_cparams(("arbitrary",)),
        name="moe_combine",
    )(dest3, x2, meta, fw, ys)


def _moe_plan(meta, counts, tm, n_tiles):
    cnt = counts[0, :N_EXPERTS].astype(jnp.int32)
    padded = ((cnt + tm - 1) // tm) * tm
    ends = jnp.cumsum(padded)
    offs = ends - padded
    e = meta[:, M_E0:M_E1 + 1].astype(jnp.int32)
    pos = meta[:, M_P0:M_P1 + 1].astype(jnp.int32)
    dest = offs[e] + pos
    starts = jnp.arange(n_tiles, dtype=jnp.int32) * tm
    n_active = (ends[-1] // tm).astype(jnp.int32)
    te = jnp.sum(starts[:, None] >= ends[None, :], axis=-1).astype(jnp.int32)
    last_e = jnp.sum(jnp.maximum(ends[-1] - tm, 0) >= ends).astype(jnp.int32)
    te = jnp.where(starts < ends[-1], te, last_e)
    return dest, te, n_active.reshape(1)


def _moe_layer(x_mid, nw, w_router, wg, wu, wd, final_w, tiles):
    T = x_mid.shape[0]
    tm = tiles["moe_tm"]
    n_tiles = (TOP_K * T) // tm + N_EXPERTS
    h, meta, counts = _router(x_mid, nw, w_router, tiles["router_t"])
    dest, te, n_active = _moe_plan(meta, counts, tm, n_tiles)
    xs = _scatter_rows(dest, h, n_tiles * tm, tiles["scatter_t"])
    ys = _moe_ffn(te, n_active, xs, wg, wu, wd, tm, tiles["moe_tf"])
    return _combine(dest, x_mid, meta, final_w, ys, tiles["combine_t"])


def _tiles(S, T):
    return dict(
        inproj_tm=min(1024, T), inproj_tn=1920,
        attn_t=min(512, S // 2), ret_c=min(256, S),
        merge_tm=min(512, T), ffn_tm=min(1024, T), ffn_tf=512,
        router_t=min(256, T), scatter_t=min(256, T), combine_t=min(256, T),
        moe_tm=min(1024, T), moe_tf=512,
    )


def kernel(x, norm_mix_w, w_in, fox_f_bias, ret_gn_w, mla_q_norm_w, mla_kv_norm_w, mla_w_uq, mla_w_uk, mla_w_uv, w_br_fox, w_br_ret, w_br_mla, w_out, norm_ffn_w, dense_w_gate, dense_w_up, dense_w_down, moe_w_router, moe_w_gate, moe_w_up, moe_w_down, final_norm_w):
    B, S, D = x.shape
    assert D == D_MODEL and w_in.shape[0] == DEPTH == 2
    T = B * S
    tl = _tiles(S, T)
    row = lambda a: a.reshape(1, -1).astype(F32)
    ret_tabs = _ret_tables(S, tl["ret_c"])
    mla_tabs = _mla_tables(S)
    x2 = x.reshape(T, D)
    for l in range(DEPTH):
        w_all, w_ff = _build_inproj_weights(w_in[l])
        proj, ff32 = _inproj(x2, row(norm_mix_w[l]), w_all, w_ff, tl["inproj_tm"], tl["inproj_tn"])
        fqt, fkx, fvt = _fox_prep(proj, ff32, fox_f_bias[l], B, S, tl["attn_t"])
        o_fox = _attention(fqt, fkx, fvt, B, S, tl["attn_t"], FOX_RANGES, "fox_attn")
        o_ret = _retention(proj, row(ret_gn_w[l]), ret_tabs, B, S, tl["ret_c"])
        wq, wqr, wk, wv = _mla_weights(mla_w_uq[l], mla_w_uk[l], mla_w_uv[l])
        mqt, mkx, mvt = _mla_prep(proj, row(mla_q_norm_w[l]), row(mla_kv_norm_w[l]), wq, wqr, wk, wv,
                                  mla_tabs, B, S, tl["attn_t"])
        o_mla = _attention(mqt, mkx, mvt, B, S, tl["attn_t"], MLA_RANGES, "mla_attn")
        x_mid, h2 = _merge(o_fox, o_ret, o_mla, proj, x2,
                           w_br_fox[l].astype(BF16), w_br_ret[l].astype(BF16), w_br_mla[l].astype(BF16),
                           w_out[l].astype(BF16), row(norm_ffn_w[l]), tl["merge_tm"])
        if l % 2 == 0:
            i = l // 2
            x2 = _dense_ffn(h2, x_mid, dense_w_gate[i].astype(BF16), dense_w_up[i].astype(BF16),
                            dense_w_down[i].astype(BF16), tl["ffn_tm"], tl["ffn_tf"])
        else:
            i = l // 2
            x2 = _moe_layer(x_mid, row(norm_ffn_w[l]), moe_w_router[i], moe_w_gate[i],
                            moe_w_up[i], moe_w_down[i], row(final_norm_w), tl)
    return x2.reshape(B, S, D)
```

```python
import functools

import numpy as np
import jax
import jax.numpy as jnp
from jax import lax
from jax.experimental import pallas as pl
from jax.experimental.pallas import tpu as pltpu

F32 = jnp.float32
BF16 = jnp.bfloat16

D_MODEL = 1024
DEPTH = 2
FOX_HEADS = 8
FOX_HEAD_DIM = 64
FOX_W = FOX_HEADS * FOX_HEAD_DIM
RET_HEADS = 8
RET_QK_DIM = 64
RET_V_DIM = 128
RET_QK_W = RET_HEADS * RET_QK_DIM
RET_V_W = RET_HEADS * RET_V_DIM
MLA_HEADS = 8
MLA_Q_RANK = 384
MLA_KV_RANK = 256
MLA_NOPE_DIM = 64
MLA_ROPE_DIM = 32
MLA_V_DIM = 64
MLA_QK_DIM = MLA_NOPE_DIM + MLA_ROPE_DIM
MLA_V_W = MLA_HEADS * MLA_V_DIM
N_BRANCHES = 3
ROPE_THETA = 10000.0
RMS_EPS = 1e-6
D_FF = 3584
N_EXPERTS = 8
TOP_K = 2
IN_SPLITS = (FOX_W, FOX_W, FOX_W, FOX_HEADS,
             RET_QK_W, RET_QK_W, RET_V_W, RET_V_W,
             MLA_Q_RANK, MLA_KV_RANK, MLA_ROPE_DIM,
             N_BRANCHES * D_MODEL)

LANE = 128
NEG = -1e30
VMEM_LIMIT = 56 * 1024 * 1024

U_FQ, U_FK, U_FV = 0, 4, 8
U_RQ, U_RQR, U_RK, U_RKR = 12, 16, 20, 24
U_CKV, U_KR, U_KRR = 28, 30, 31
U_RV, U_RG, U_GL, U_CQ = 32, 40, 48, 72
N_PROJ = 75 * LANE
FOX_EXT_STRIDE = 8
PAIRS = 4
QK_W = 2 * LANE
ATTN_KEY_CHUNK = 256
ATTN_STRIP = 256
VT_ROWS = 80
LOG2E = 1.4426950408889634


def _cparams(sem, vmem=VMEM_LIMIT, flags=None):
    return pltpu.CompilerParams(dimension_semantics=sem, vmem_limit_bytes=vmem, flags=flags)


def _sigmoid(x):
    return 1.0 / (1.0 + jnp.exp(-x))


def _lane_mask(shape, ranges):
    lane = lax.broadcasted_iota(jnp.int32, shape, len(shape) - 1)
    m = None
    for a, b in ranges:
        r = (lane >= a) & (lane < b)
        m = r if m is None else (m | r)
    return m


def _inproj_kernel(x_ref, nw_ref, w_ref, wff_ref, out_ref, ff_ref, h_scr):
    @pl.when(pl.program_id(1) == 0)
    def _():
        x = x_ref[...]
        ms = jnp.mean(x * x, axis=-1, keepdims=True)
        h = (x * lax.rsqrt(ms + RMS_EPS) * nw_ref[...]).astype(BF16)
        h_scr[...] = h
        ff_ref[...] = jnp.dot(h, wff_ref[...], preferred_element_type=F32)

    out_ref[...] = jnp.dot(h_scr[...], w_ref[...], preferred_element_type=F32).astype(out_ref.dtype)


def _inproj(x2, nw, w_all, w_ff, tm, tn):
    T = x2.shape[0]
    return pl.pallas_call(
        _inproj_kernel,
        grid=(T // tm, N_PROJ // tn),
        in_specs=[
            pl.BlockSpec((tm, D_MODEL), lambda i, j: (i, 0)),
            pl.BlockSpec((1, D_MODEL), lambda i, j: (0, 0)),
            pl.BlockSpec((D_MODEL, tn), lambda i, j: (0, j)),
            pl.BlockSpec((D_MODEL, LANE), lambda i, j: (0, 0)),
        ],
        out_specs=[
            pl.BlockSpec((tm, tn), lambda i, j: (i, j)),
            pl.BlockSpec((tm, LANE), lambda i, j: (i, 0)),
        ],
        out_shape=[
            jax.ShapeDtypeStruct((T, N_PROJ), BF16),
            jax.ShapeDtypeStruct((T, LANE), F32),
        ],
        scratch_shapes=[pltpu.VMEM((tm, D_MODEL), BF16)],
        compiler_params=_cparams(("parallel", "arbitrary")),
        name="inproj",
    )(x2, nw, w_all, w_ff)


def _rot_cols(w, heads, d):
    k = w.shape[0]
    w4 = w.reshape(k, heads, 2, d // 2)
    return jnp.concatenate([-w4[:, :, 1], w4[:, :, 0]], axis=-1).reshape(k, heads * d)


def _build_inproj_weights(w_in):
    split_at = [int(i) for i in np.cumsum(IN_SPLITS)[:-1]]
    fq, fk, fv, ff, rq, rk, rv, rg, cq, ckv, kr, gl = jnp.split(w_in, split_at, axis=-1)
    z64 = jnp.zeros((D_MODEL, 64), F32)
    z32 = jnp.zeros((D_MODEL, 32), F32)
    kr128 = jnp.concatenate([z64, kr, z32], axis=-1)
    krr128 = jnp.concatenate([z64, _rot_cols(kr, 1, MLA_ROPE_DIM), z32], axis=-1)
    w_all = jnp.concatenate(
        [fq, fk, fv, rq, _rot_cols(rq, RET_HEADS, RET_QK_DIM), rk, _rot_cols(rk, RET_HEADS, RET_QK_DIM),
         ckv, kr128, krr128, rv, rg, gl, cq], axis=-1).astype(BF16)
    w_ff = jnp.pad(ff, ((0, 0), (0, LANE - FOX_HEADS))).astype(BF16)
    return w_all, w_ff


def _split3(x):
    hi = x.astype(BF16)
    r1 = x - hi.astype(F32)
    mid = r1.astype(BF16)
    lo = (r1 - mid.astype(F32)).astype(BF16)
    return hi, mid, lo


def _store_t(dst_ref, idx, x):
    dst_ref[idx] = x.T.astype(dst_ref.dtype)


def _store_vt_pair(vt_ref, p, v2):
    hd = LANE // 2
    vt = v2.T.astype(vt_ref.dtype)
    pad_rows = lax.broadcasted_iota(jnp.int32, (VT_ROWS - hd, vt.shape[1]), 0)
    ones_then_zeros = jnp.where(pad_rows == 0, 1.0, 0.0).astype(vt_ref.dtype)
    for j in range(2):
        base = (2 * p + j) * VT_ROWS
        vt_ref[0, 0, base:base + hd, :] = vt[j * hd:(j + 1) * hd]
        vt_ref[0, 0, base + hd:base + VT_ROWS, :] = ones_then_zeros


def _fox_prep_kernel(fq_ref, fk_ref, fv_ref, ff_ref, bias_ref, tri_ref, pq_ref, pk_ref, cq_ref, ck_ref,
                     qt_ref, kx_ref, vt_ref, carry_scr):
    @pl.when(pl.program_id(1) == 0)
    def _():
        carry_scr[...] = jnp.zeros_like(carry_scr)

    ts = ff_ref.shape[0]
    z = ff_ref[...] + bias_ref[...]
    logf = jnp.minimum(z, 0.0) - jnp.log(1.0 + jnp.exp(-jnp.abs(z)))
    tri = tri_ref[...]
    cum = carry_scr[0:1, :]
    for part in _split3(logf):
        cum = cum + jnp.dot(tri, part, preferred_element_type=F32)
    carry_scr[0:1, :] = cum[ts - 1:ts, :]

    eq = cq_ref[...]
    ek = ck_ref[...]
    for j, part in enumerate(_split3(cum * LOG2E)):
        eq = eq + jnp.dot(part, pq_ref[j * LANE:(j + 1) * LANE, :], preferred_element_type=F32)
        ek = ek + jnp.dot(part, pk_ref[j * LANE:(j + 1) * LANE, :], preferred_element_type=F32)
    scale = FOX_HEAD_DIM ** -0.5 * LOG2E
    for p in range(FOX_HEADS // 2):
        src = slice(p * LANE, (p + 1) * LANE)
        _store_t(qt_ref, (0, slice(2 * p * LANE, (2 * p + 1) * LANE)), fq_ref[:, src].astype(F32) * scale)
        _store_t(qt_ref, (0, slice((2 * p + 1) * LANE, (2 * p + 2) * LANE)), eq[:, src])
        kx_ref[:, 2 * p * LANE:(2 * p + 1) * LANE] = fk_ref[:, src]
        kx_ref[:, (2 * p + 1) * LANE:(2 * p + 2) * LANE] = ek[:, src].astype(BF16)
        _store_vt_pair(vt_ref, p, fv_ref[:, src].astype(F32))


def _fox_prep_consts(ts):
    tri = np.tril(np.ones((ts, ts), np.float32))
    pq = np.zeros((3 * LANE, FOX_W), np.float32)
    pk = np.zeros((3 * LANE, FOX_W), np.float32)
    cq = np.zeros((1, FOX_W), np.float32)
    ck = np.zeros((1, FOX_W), np.float32)
    for h in range(FOX_HEADS):
        base = LANE * (h // 2) + FOX_EXT_STRIDE * (h % 2)
        for j in range(3):
            pq[j * LANE + h, base + j] = 1.0
            ck[0, base + j] = 1.0
            cq[0, base + 3 + j] = 1.0
            pk[j * LANE + h, base + 3 + j] = -1.0
    return (jnp.asarray(tri, BF16), jnp.asarray(pq, BF16), jnp.asarray(pk, BF16),
            jnp.asarray(cq), jnp.asarray(ck))


def _attn_operand_specs(B, S, t):
    T = B * S
    ns = S // t
    specs = [
        pl.BlockSpec((1, PAIRS * QK_W, t), lambda b, s: (b, 0, s)),
        pl.BlockSpec((t, PAIRS * QK_W), lambda b, s: (b * ns + s, 0)),
        pl.BlockSpec((1, 1, 2 * PAIRS * VT_ROWS, t), lambda b, s: (b, s, 0, 0)),
    ]
    shapes = [
        jax.ShapeDtypeStruct((B, PAIRS * QK_W, S), BF16),
        jax.ShapeDtypeStruct((T, PAIRS * QK_W), BF16),
        jax.ShapeDtypeStruct((B, ns, 2 * PAIRS * VT_ROWS, t), BF16),
    ]
    return specs, shapes


def _fox_prep(proj, ff32, bias, B, S, ts):
    ns = S // ts
    tri, pq, pk, cq, ck = _fox_prep_consts(ts)
    bias128 = jnp.pad(bias.astype(F32), (0, LANE - FOX_HEADS)).reshape(1, LANE)
    const = lambda shape: pl.BlockSpec(shape, lambda b, s: (0, 0))
    out_specs, out_shape = _attn_operand_specs(B, S, ts)
    return pl.pallas_call(
        _fox_prep_kernel,
        grid=(B, ns),
        in_specs=[
            pl.BlockSpec((ts, FOX_W), lambda b, s: (b * ns + s, U_FQ // 4)),
            pl.BlockSpec((ts, FOX_W), lambda b, s: (b * ns + s, U_FK // 4)),
            pl.BlockSpec((ts, FOX_W), lambda b, s: (b * ns + s, U_FV // 4)),
            pl.BlockSpec((ts, LANE), lambda b, s: (b * ns + s, 0)),
            const((1, LANE)), const((ts, ts)), const((3 * LANE, FOX_W)), const((3 * LANE, FOX_W)),
            const((1, FOX_W)), const((1, FOX_W)),
        ],
        out_specs=out_specs,
        out_shape=out_shape,
        scratch_shapes=[pltpu.VMEM((8, LANE), F32)],
        compiler_params=_cparams(("parallel", "arbitrary")),
        name="fox_prep",
    )(proj, proj, proj, ff32, bias128, tri, pq, pk, cq, ck)


def _row_mask(shape, ranges):
    row = lax.broadcasted_iota(jnp.int32, shape, 0)
    m = None
    for a, b in ranges:
        r = (row >= a) & (row < b)
        m = r if m is None else (m | r)
    return m


def _col_reduce(x, op, reduce_fn, parts=2):
    c = x.shape[0] // parts
    pieces = [x[i * c:(i + 1) * c] for i in range(parts)]
    while len(pieces) > 1:
        pieces = [op(pieces[i], pieces[i + 1]) for i in range(0, len(pieces), 2)]
    return reduce_fn(pieces[0], axis=0, keepdims=True)


def _attn_kernel(qt_ref, k_ref, vt_ref, o_ref, qh_scr, sa_scr, sb_scr, ma_scr, mb_scr, acc_scr, *, tk, ranges):
    qi = pl.program_id(2)
    qt = qt_ref[0]
    tq = qt.shape[-1]
    for h in range(2):
        qh_scr[h] = jnp.where(_row_mask(qt.shape, ranges[h]), qt, jnp.zeros_like(qt))
    acc_scr[...] = jnp.zeros_like(acc_scr)

    def key_tile(kt):
        return k_ref[pl.ds(pl.multiple_of(kt * tk, tk), tk), :]

    strips = [(c, c + ATTN_STRIP) for c in range(0, tq, ATTN_STRIP)]

    def scores(kt, s_scr, m_scr):
        k = key_tile(kt)
        for h in range(2):
            for a, b in strips:
                s = jnp.dot(k, qh_scr[h, :, a:b], preferred_element_type=F32)
                s_scr[h, :, a:b] = s
                m_scr[h, :, a:b] = _col_reduce(s, jnp.maximum, jnp.max)

    def update(kt, h, s, m_tile, m_prev, lo):
        m_new = jnp.maximum(m_prev, m_tile)
        alpha = jnp.exp2(m_prev - m_new)
        hi = lo + s.shape[-1]
        pv = None
        for r in range(0, tk, ATTN_KEY_CHUNK):
            p = jnp.exp2((s[r:r + ATTN_KEY_CHUNK] - m_new).astype(BF16))
            vt = vt_ref[0, kt, h * VT_ROWS:(h + 1) * VT_ROWS, r:r + ATTN_KEY_CHUNK]
            d = jnp.dot(vt, p, preferred_element_type=F32)
            pv = d if pv is None else pv + d
        acc_scr[h, :, lo:hi] = alpha * acc_scr[h, :, lo:hi] + pv
        return m_new

    def softmax_pv(kt, s_scr, m_scr, carry):
        out = []
        for h in range(2):
            ms = [update(kt, h, s_scr[h, :, a:b], m_scr[h, :, a:b], carry[h][:, a:b], a) for a, b in strips]
            out.append(jnp.concatenate(ms, axis=-1))
        return tuple(out)

    def diag_update(kt, h, s_scr, m_prev, a, b, lo):
        s = s_scr[h, :, a:b]
        if a < lo + tk:
            key = lax.broadcasted_iota(jnp.int32, s.shape, 0)
            qry = lax.broadcasted_iota(jnp.int32, s.shape, 1) + (a - lo)
            s = jnp.where(key <= qry, s, NEG)
        return update(kt, h, s, _col_reduce(s, jnp.maximum, jnp.max), m_prev, a)

    def diagonal_tiles(carry):
        k_last = key_tile(2 * qi + 1)
        out = []
        for h in range(2):
            ms = []
            for a, b in strips:
                if a >= tk:
                    sb_scr[h, :, a:b] = jnp.dot(k_last, qh_scr[h, :, a:b], preferred_element_type=F32)
                ms.append(diag_update(2 * qi, h, sa_scr, carry[h][:, a:b], a, b, 0))
            out.append(ms)
        for h in range(2):
            for i, (a, b) in enumerate(strips):
                if a >= tk:
                    out[h][i] = diag_update(2 * qi + 1, h, sb_scr, out[h][i], a, b, tk)

    def fused_step(kt, cur_s, cur_m, nxt_s, nxt_m, carry):
        k = key_tile(kt + 1)
        ms = [[], []]
        for a, b in strips:
            for h in range(2):
                s = jnp.dot(k, qh_scr[h, :, a:b], preferred_element_type=F32)
                nxt_s[h, :, a:b] = s
                nxt_m[h, :, a:b] = _col_reduce(s, jnp.maximum, jnp.max)
                ms[h].append(update(kt, h, cur_s[h, :, a:b], cur_m[h, :, a:b], carry[h][:, a:b], a))
        return tuple(jnp.concatenate(m, axis=-1) for m in ms)

    def body(j, carry):
        kt = 2 * j
        carry = fused_step(kt, sa_scr, ma_scr, sb_scr, mb_scr, carry)
        return fused_step(kt + 1, sb_scr, mb_scr, sa_scr, ma_scr, carry)

    init = tuple(jnp.full((1, tq), NEG, F32) for _ in range(2))
    scores(0, sa_scr, ma_scr)
    carry = lax.fori_loop(0, qi, body, init)
    diagonal_tiles(carry)
    hd = LANE // 2
    ot = jnp.concatenate([acc_scr[h, :hd] / acc_scr[h, hd:hd + 1] for h in range(2)], axis=0)
    o_ref[...] = ot.T.astype(o_ref.dtype)


def _attention(qt, kx, vt, B, S, tk, ranges, name):
    T = B * S
    tq = 2 * tk
    nq = S // tq
    kern = functools.partial(_attn_kernel, tk=tk, ranges=ranges)
    return pl.pallas_call(
        kern,
        grid=(B, PAIRS, nq),
        in_specs=[
            pl.BlockSpec((1, QK_W, tq), lambda b, p, i: (b, p, i)),
            pl.BlockSpec((S, QK_W), lambda b, p, i: (b, p)),
            pl.BlockSpec((1, S // tk, 2 * VT_ROWS, tk), lambda b, p, i: (b, 0, p, 0)),
        ],
        out_specs=pl.BlockSpec((tq, LANE), lambda b, p, i: (b * nq + i, p)),
        out_shape=jax.ShapeDtypeStruct((T, PAIRS * LANE), BF16),
        scratch_shapes=[
            pltpu.VMEM((2, QK_W, tq), BF16),
            pltpu.VMEM((2, tk, tq), F32),
            pltpu.VMEM((2, tk, tq), F32),
            pltpu.VMEM((2, 1, tq), F32),
            pltpu.VMEM((2, 1, tq), F32),
            pltpu.VMEM((2, VT_ROWS, tq), F32),
        ],
        compiler_params=_cparams(("parallel", "parallel", "arbitrary")),
        name=name,
    )(qt, kx, vt)


_E0 = LANE
FOX_RANGES = (((0, 64), (_E0, _E0 + FOX_EXT_STRIDE)),
              ((64, 128), (_E0 + FOX_EXT_STRIDE, _E0 + 2 * FOX_EXT_STRIDE)))
MLA_RANGES = (((0, LANE),), ((LANE, 2 * LANE),))


def _mla_prep_kernel(cq_ref, ckv_ref, kr_ref, krr_ref, qnw_ref, kvnw_ref, wq_ref, wqr_ref, wk_ref, wv_ref,
                     cq_tab, sq_tab, ck_tab, sk_tab, qt_ref, k_ref, vt_ref):
    def norm(x_ref, w_ref):
        x = x_ref[...].astype(F32)
        ms = jnp.mean(x * x, axis=-1, keepdims=True)
        return (x * lax.rsqrt(ms + RMS_EPS) * w_ref[...]).astype(BF16)

    c_q = norm(cq_ref, qnw_ref)
    c_kv = norm(ckv_ref, kvnw_ref)
    tile8 = lambda a: jnp.concatenate([a] * MLA_HEADS, axis=-1)
    q = (jnp.dot(c_q, wq_ref[...], preferred_element_type=F32) * tile8(cq_tab[...])
         + jnp.dot(c_q, wqr_ref[...], preferred_element_type=F32) * tile8(sq_tab[...]))
    for c in range(MLA_HEADS):
        _store_t(qt_ref, (0, slice(c * LANE, (c + 1) * LANE)), q[:, c * LANE:(c + 1) * LANE])
    k_rope = kr_ref[...].astype(F32) * ck_tab[...] + krr_ref[...].astype(F32) * sk_tab[...]
    k = jnp.dot(c_kv, wk_ref[...], preferred_element_type=F32) + tile8(k_rope)
    k_ref[...] = k.astype(BF16)
    v = jnp.dot(c_kv, wv_ref[...], preferred_element_type=F32)
    for p in range(PAIRS):
        _store_vt_pair(vt_ref, p, v[:, p * LANE:(p + 1) * LANE])


def _mla_weights(w_uq, w_uk, w_uv):
    q3 = w_uq.reshape(MLA_Q_RANK, MLA_HEADS, MLA_QK_DIM)
    nope, ropep = q3[..., :MLA_NOPE_DIM], q3[..., MLA_NOPE_DIM:]
    half = MLA_ROPE_DIM // 2
    rot = jnp.concatenate([-ropep[..., half:], ropep[..., :half]], axis=-1)
    z32 = jnp.zeros((MLA_Q_RANK, MLA_HEADS, LANE - MLA_QK_DIM), F32)
    z64 = jnp.zeros((MLA_Q_RANK, MLA_HEADS, MLA_NOPE_DIM), F32)
    wq = jnp.concatenate([nope, ropep, z32], axis=-1).reshape(MLA_Q_RANK, MLA_HEADS * LANE)
    wqr = jnp.concatenate([z64, rot, z32], axis=-1).reshape(MLA_Q_RANK, MLA_HEADS * LANE)
    k3 = w_uk.reshape(MLA_KV_RANK, MLA_HEADS, MLA_NOPE_DIM)
    wk = jnp.concatenate([k3, jnp.zeros_like(k3)], axis=-1).reshape(MLA_KV_RANK, MLA_HEADS * LANE)
    return wq.astype(BF16), wqr.astype(BF16), wk.astype(BF16), w_uv.astype(BF16)


def _mla_tables(S):
    pos = jnp.arange(S, dtype=F32)
    inv = ROPE_THETA ** (-jnp.arange(0, MLA_ROPE_DIM, 2, dtype=F32) / MLA_ROPE_DIM)
    ang = pos[:, None] * inv[None, :]
    cos, sin = jnp.cos(ang), jnp.sin(ang)
    ones = jnp.ones((S, MLA_NOPE_DIM), F32)
    z64 = jnp.zeros((S, MLA_NOPE_DIM), F32)
    z32 = jnp.zeros((S, LANE - MLA_QK_DIM), F32)
    ctab = jnp.concatenate([ones, cos, cos, z32], axis=-1)
    stab = jnp.concatenate([z64, sin, sin, z32], axis=-1)
    scale = MLA_QK_DIM ** -0.5 * LOG2E
    return ctab * scale, stab * scale, ctab, stab


def _mla_prep(proj, qnw, kvnw, wq, wqr, wk, wv, tabs, B, S, tm):
    ns = S // tm
    HW = MLA_HEADS * LANE
    const = lambda shape: pl.BlockSpec(shape, lambda b, s: (0, 0))
    tab = pl.BlockSpec((tm, LANE), lambda b, s: (s, 0))
    out_specs, out_shape = _attn_operand_specs(B, S, tm)
    return pl.pallas_call(
        _mla_prep_kernel,
        grid=(B, ns),
        in_specs=[
            pl.BlockSpec((tm, MLA_Q_RANK), lambda b, s: (b * ns + s, U_CQ // 3)),
            pl.BlockSpec((tm, MLA_KV_RANK), lambda b, s: (b * ns + s, U_CKV // 2)),
            pl.BlockSpec((tm, LANE), lambda b, s: (b * ns + s, U_KR)),
            pl.BlockSpec((tm, LANE), lambda b, s: (b * ns + s, U_KRR)),
            const((1, MLA_Q_RANK)), const((1, MLA_KV_RANK)),
            const((MLA_Q_RANK, HW)), const((MLA_Q_RANK, HW)), const((MLA_KV_RANK, HW)),
            const((MLA_KV_RANK, MLA_V_W)),
            tab, tab, tab, tab,
        ],
        out_specs=out_specs,
        out_shape=out_shape,
        compiler_params=_cparams(("parallel", "parallel")),
        name="mla_prep",
    )(proj, proj, proj, proj, qnw, kvnw, wq, wqr, wk, wv, *tabs)


def _ret_kernel(rq_ref, rqr_ref, rk_ref, rkr_ref, rv_ref, rg_ref, cos_ref, sin_ref,
                din_ref, xi_ref, zeta_ref, gch_ref, gnw_ref, o_ref, r_scr):
    @pl.when(pl.program_id(0) == 0)
    def _():
        r_scr[...] = jnp.zeros_like(r_scr)

    tile4 = lambda a: jnp.concatenate([a] * (RET_QK_W // LANE), axis=-1)
    cos4 = tile4(cos_ref[...])
    sin4 = tile4(sin_ref[...])
    for bi in range(rq_ref.shape[0]):
        q = rq_ref[bi].astype(F32) * cos4 + rqr_ref[bi].astype(F32) * sin4
        k = (rk_ref[bi].astype(F32) * cos4 + rkr_ref[bi].astype(F32) * sin4) * (RET_QK_DIM ** -0.5)
        for p in range(RET_HEADS // 2):
            q2 = q[:, p * LANE:(p + 1) * LANE]
            k2 = k[:, p * LANE:(p + 1) * LANE]
            k2b = k2.astype(BF16)
            kz2 = (k2 * zeta_ref[p]).astype(BF16)
            r2 = r_scr[bi, p]
            r2b = r2.astype(BF16)
            new_r = gch_ref[p] * r2
            for j in range(2):
                h = 2 * p + j
                half = ((j * RET_QK_DIM, (j + 1) * RET_QK_DIM),)
                qm = jnp.where(_lane_mask(q2.shape, half), q2, 0.0).astype(BF16)
                inner = lax.dot_general(qm, k2b, (((1,), (1,)), ((), ())), preferred_element_type=F32)
                inner = inner * din_ref[h]
                vh = rv_ref[bi, :, h * LANE:(h + 1) * LANE]
                o = (jnp.dot(inner.astype(BF16), vh, preferred_element_type=F32)
                     + jnp.dot(qm, r2b, preferred_element_type=F32) * xi_ref[h])
                upd = lax.dot_general(kz2, vh, (((0,), (0,)), ((), ())), preferred_element_type=F32)
                rows = lax.broadcasted_iota(jnp.int32, upd.shape, 0)
                new_r = new_r + jnp.where((rows >= half[0][0]) & (rows < half[0][1]), upd, 0.0)
                mu = jnp.mean(o, axis=-1, keepdims=True)
                d = o - mu
                var = jnp.mean(d * d, axis=-1, keepdims=True)
                on = d * lax.rsqrt(var + RMS_EPS) * gnw_ref[:, h * LANE:(h + 1) * LANE]
                g = rg_ref[bi, :, h * LANE:(h + 1) * LANE].astype(F32)
                o_ref[bi, :, h * LANE:(h + 1) * LANE] = (g * _sigmoid(g) * on).astype(o_ref.dtype)
            r_scr[bi, p] = new_r


def _ret_tables(S, C):
    pos = jnp.arange(S, dtype=F32)
    inv = ROPE_THETA ** (-jnp.arange(0, RET_QK_DIM, 2, dtype=F32) / RET_QK_DIM)
    ang = pos[:, None] * inv[None, :]
    cos = jnp.concatenate([jnp.cos(ang)] * 4, axis=-1)
    sin = jnp.concatenate([jnp.sin(ang)] * 4, axis=-1)
    gammas = 1.0 - 2.0 ** (-5.0 - jnp.arange(RET_HEADS, dtype=F32))
    log_g = jnp.log(gammas)
    j = jnp.arange(C, dtype=F32)
    diff = j[:, None] - j[None, :]
    din = jnp.where(diff[None] >= 0, jnp.exp(jnp.maximum(diff, 0.0)[None] * log_g[:, None, None]), 0.0)
    xi = jnp.exp((j[None, :] + 1.0) * log_g[:, None])
    zeta = jnp.exp((C - 1.0 - j[None, :]) * log_g[:, None])
    gch = jnp.exp(C * log_g)
    xi_t = jnp.broadcast_to(xi[:, :, None], (RET_HEADS, C, LANE))
    zeta_p = jnp.repeat(zeta.reshape(RET_HEADS // 2, 2, C).transpose(0, 2, 1), RET_QK_DIM, axis=-1)
    gch_p = jnp.broadcast_to(jnp.repeat(gch.reshape(RET_HEADS // 2, 2), RET_QK_DIM, axis=-1)[:, :, None],
                             (RET_HEADS // 2, LANE, LANE))
    return cos, sin, din, xi_t, zeta_p, gch_p


def _retention(proj, gnw, tabs, B, S, C):
    nc = S // C
    cos, sin, din, xi, zeta, gch = tabs
    proj3 = proj.reshape(B, S, N_PROJ)
    blk = lambda w, unit: pl.BlockSpec((B, C, w), lambda c: (0, c, unit * LANE // w))
    full = lambda a: pl.BlockSpec(a.shape, lambda c: (0,) * a.ndim)
    out = pl.pallas_call(
        _ret_kernel,
        grid=(nc,),
        in_specs=[
            blk(RET_QK_W, U_RQ), blk(RET_QK_W, U_RQR), blk(RET_QK_W, U_RK), blk(RET_QK_W, U_RKR),
            blk(RET_V_W, U_RV), blk(RET_V_W, U_RG),
            pl.BlockSpec((C, LANE), lambda c: (c, 0)),
            pl.BlockSpec((C, LANE), lambda c: (c, 0)),
            full(din), full(xi), full(zeta), full(gch), full(gnw),
        ],
        out_specs=pl.BlockSpec((B, C, RET_V_W), lambda c: (0, c, 0)),
        out_shape=jax.ShapeDtypeStruct((B, S, RET_V_W), BF16),
        scratch_shapes=[pltpu.VMEM((B, RET_HEADS // 2, LANE, LANE), F32)],
        compiler_params=_cparams(("arbitrary",)),
        name="retention",
    )(proj3, proj3, proj3, proj3, proj3, proj3, cos, sin, din, xi, zeta, gch, gnw)
    return out.reshape(B * S, RET_V_W)


def _merge_kernel(of_ref, or_ref, om_ref, g0_ref, g1_ref, g2_ref, x_ref,
                  wf_ref, wr_ref, wm_ref, wo_ref, nw_ref, xo_ref, h_ref):
    merged = (_sigmoid(g0_ref[...].astype(F32)) * jnp.dot(of_ref[...], wf_ref[...], preferred_element_type=F32)
              + _sigmoid(g1_ref[...].astype(F32)) * jnp.dot(or_ref[...], wr_ref[...], preferred_element_type=F32)
              + _sigmoid(g2_ref[...].astype(F32)) * jnp.dot(om_ref[...], wm_ref[...], preferred_element_type=F32))
    x = x_ref[...] + jnp.dot(merged.astype(BF16), wo_ref[...], preferred_element_type=F32)
    xo_ref[...] = x
    ms = jnp.mean(x * x, axis=-1, keepdims=True)
    h_ref[...] = (x * lax.rsqrt(ms + RMS_EPS) * nw_ref[...]).astype(h_ref.dtype)


def _merge(o_fox, o_ret, o_mla, proj, x2, wf, wr, wm, wo, nw, tm):
    T = x2.shape[0]
    row = lambda w, cb=0: pl.BlockSpec((tm, w), lambda i: (i, cb))
    const = lambda a: pl.BlockSpec(a.shape, lambda i: (0, 0))
    g_unit = U_GL // 8
    return pl.pallas_call(
        _merge_kernel,
        grid=(T // tm,),
        in_specs=[
            row(FOX_W), row(RET_V_W), row(MLA_V_W),
            row(D_MODEL, g_unit), row(D_MODEL, g_unit + 1), row(D_MODEL, g_unit + 2),
            row(D_MODEL),
            const(wf), const(wr), const(wm), const(wo), const(nw),
        ],
        out_specs=[row(D_MODEL), row(D_MODEL)],
        out_shape=[jax.ShapeDtypeStruct((T, D_MODEL), F32), jax.ShapeDtypeStruct((T, D_MODEL), BF16)],
        compiler_params=_cparams(("parallel",)),
        name="merge",
    )(o_fox, o_ret, o_mla, proj, proj, proj, x2, wf, wr, wm, wo, nw)


def _swiglu_tile(h, wg, wu, wd):
    a = jnp.dot(h, wg, preferred_element_type=F32)
    b = jnp.dot(h, wu, preferred_element_type=F32)
    return jnp.dot((a * _sigmoid(a) * b).astype(BF16), wd, preferred_element_type=F32)


def _dense_ffn_kernel(h_ref, wg_ref, wu_ref, wd_ref, x_ref, o_ref, acc_scr):
    f = pl.program_id(1)

    @pl.when(f == 0)
    def _():
        acc_scr[...] = jnp.zeros_like(acc_scr)

    acc_scr[...] += _swiglu_tile(h_ref[...], wg_ref[...], wu_ref[...], wd_ref[...])

    @pl.when(f == pl.num_programs(1) - 1)
    def _():
        o_ref[...] = x_ref[...] + acc_scr[...]


def _dense_ffn(h, x2, wg, wu, wd, tm, tf):
    T = x2.shape[0]
    return pl.pallas_call(
        _dense_ffn_kernel,
        grid=(T // tm, D_FF // tf),
        in_specs=[
            pl.BlockSpec((tm, D_MODEL), lambda i, f: (i, 0)),
            pl.BlockSpec((D_MODEL, tf), lambda i, f: (0, f)),
            pl.BlockSpec((D_MODEL, tf), lambda i, f: (0, f)),
            pl.BlockSpec((tf, D_MODEL), lambda i, f: (f, 0)),
            pl.BlockSpec((tm, D_MODEL), lambda i, f: (i, 0)),
        ],
        out_specs=pl.BlockSpec((tm, D_MODEL), lambda i, f: (i, 0)),
        out_shape=jax.ShapeDtypeStruct((T, D_MODEL), F32),
        scratch_shapes=[pltpu.VMEM((tm, D_MODEL), F32)],
        compiler_params=_cparams(("parallel", "arbitrary")),
        name="dense_ffn",
    )(h, wg, wu, wd, x2)


DMA_ISSUE_UNROLL = 8

M_E0, M_E1, M_P0, M_P1, M_W0, M_W1 = 0, 1, 2, 3, 4, 5


def _split2(x):
    hi = x.astype(BF16)
    lo = (x - hi.astype(F32)).astype(BF16)
    return hi, lo


def _router_kernel(x_ref, nw_ref, whi_ref, wlo_ref, tri_ref, h_ref, meta_ref, cnt_ref, carry_scr):
    @pl.when(pl.program_id(0) == 0)
    def _():
        carry_scr[...] = jnp.zeros_like(carry_scr)

    x = x_ref[...]
    ms = jnp.mean(x * x, axis=-1, keepdims=True)
    h = x * lax.rsqrt(ms + RMS_EPS) * nw_ref[...]
    h_ref[...] = h
    hhi, hlo = _split2(h)
    logits = (jnp.dot(hhi, whi_ref[...], preferred_element_type=F32)
              + jnp.dot(hhi, wlo_ref[...], preferred_element_type=F32)
              + jnp.dot(hlo, whi_ref[...], preferred_element_type=F32))
    lane = lax.broadcasted_iota(jnp.int32, logits.shape, 1)
    lanef = lane.astype(F32)
    logits = jnp.where(lane < N_EXPERTS, logits, NEG)
    m0 = jnp.max(logits, axis=-1, keepdims=True)
    e0 = jnp.min(jnp.where(logits == m0, lanef, float(LANE)), axis=-1, keepdims=True)
    oh0 = lanef == e0
    rest = jnp.where(oh0, NEG, logits)
    m1 = jnp.max(rest, axis=-1, keepdims=True)
    e1 = jnp.min(jnp.where(rest == m1, lanef, float(LANE)), axis=-1, keepdims=True)
    oh1 = lanef == e1
    z = jnp.exp(m1 - m0)
    w0 = 1.0 / (1.0 + z)
    w1 = z / (1.0 + z)
    both = jnp.where(oh0 | oh1, 1.0, 0.0)
    before = carry_scr[0:1, :] + jnp.dot(tri_ref[...], both.astype(BF16), preferred_element_type=F32)
    p0 = jnp.sum(jnp.where(oh0, before, 0.0), axis=-1, keepdims=True)
    p1 = jnp.sum(jnp.where(oh1, before, 0.0), axis=-1, keepdims=True)
    total = carry_scr[0:1, :] + jnp.sum(both, axis=0, keepdims=True)
    carry_scr[0:1, :] = total
    cnt_ref[...] = jnp.broadcast_to(total, cnt_ref.shape)
    meta = jnp.zeros(logits.shape, F32)
    for idx, val in ((M_E0, e0), (M_E1, e1), (M_P0, p0), (M_P1, p1), (M_W0, w0), (M_W1, w1)):
        meta = jnp.where(lane == idx, val, meta)
    meta_ref[...] = meta


def _router(x2, nw, w_router, tr):
    T = x2.shape[0]
    wr = jnp.pad(w_router.astype(F32), ((0, 0), (0, LANE - N_EXPERTS)))
    whi = wr.astype(BF16)
    wlo = (wr - whi.astype(F32)).astype(BF16)
    tri = jnp.asarray(np.tril(np.ones((tr, tr), np.float32), -1), BF16)
    const = lambda a: pl.BlockSpec(a.shape, lambda i: (0, 0))
    return pl.pallas_call(
        _router_kernel,
        grid=(T // tr,),
        in_specs=[pl.BlockSpec((tr, D_MODEL), lambda i: (i, 0)), const(nw), const(whi), const(wlo), const(tri)],
        out_specs=[
            pl.BlockSpec((tr, D_MODEL), lambda i: (i, 0)),
            pl.BlockSpec((tr, LANE), lambda i: (i, 0)),
            pl.BlockSpec((8, LANE), lambda i: (0, 0)),
        ],
        out_shape=[
            jax.ShapeDtypeStruct((T, D_MODEL), F32),
            jax.ShapeDtypeStruct((T, LANE), F32),
            jax.ShapeDtypeStruct((8, LANE), F32),
        ],
        scratch_shapes=[pltpu.VMEM((8, LANE), F32)],
        compiler_params=_cparams(("arbitrary",)),
        name="router",
    )(x2, nw, whi, wlo, tri)


def _scatter_kernel(dest_ref, h_ref, xs_in, xs_hbm, sem, *, tt):
    del xs_in

    def row_copy(t, d_row):
        return pltpu.make_async_copy(h_ref.at[pl.ds(t, 1)], xs_hbm.at[pl.ds(d_row, 1)], sem)

    def issue(t, c):
        row_copy(t, dest_ref[0, 0, 2 * t]).start(priority=0)
        row_copy(t, dest_ref[0, 0, 2 * t + 1]).start(priority=1)
        return c

    lax.fori_loop(0, tt, issue, 0, unroll=DMA_ISSUE_UNROLL)
    for _ in range(TOP_K):
        pltpu.make_async_copy(h_ref, xs_hbm.at[pl.ds(0, tt)], sem).wait()


def _scatter_rows(dest, h, n_rows, tt):
    T = h.shape[0]
    xs0 = jnp.zeros((n_rows, D_MODEL), h.dtype)
    dest3 = dest.reshape(T // tt, 1, 2 * tt)
    return pl.pallas_call(
        functools.partial(_scatter_kernel, tt=tt),
        grid=(T // tt,),
        in_specs=[
            pl.BlockSpec((1, 1, 2 * tt), lambda i: (i, 0, 0), memory_space=pltpu.SMEM),
            pl.BlockSpec((tt, D_MODEL), lambda i: (i, 0)),
            pl.BlockSpec(memory_space=pl.ANY),
        ],
        out_specs=pl.BlockSpec(memory_space=pl.ANY),
        out_shape=jax.ShapeDtypeStruct((n_rows, D_MODEL), h.dtype),
        scratch_shapes=[pltpu.SemaphoreType.DMA(())],
        input_output_aliases={2: 0},
        compiler_params=_cparams(("arbitrary",)),
        name="moe_scatter",
    )(dest3, h, xs0)


def _moe_ffn_kernel(te_ref, na_ref, xs_ref, wg_ref, wu_ref, wd_ref, ys_ref, acc_scr):
    i = pl.program_id(0)
    f = pl.program_id(1)

    @pl.when(i < na_ref[0])
    def _():
        @pl.when(f == 0)
        def _():
            acc_scr[...] = jnp.zeros_like(acc_scr)

        acc_scr[...] += _swiglu_tile(xs_ref[...].astype(BF16), wg_ref[0].astype(BF16),
                                     wu_ref[0].astype(BF16), wd_ref[0].astype(BF16))

        @pl.when(f == pl.num_programs(1) - 1)
        def _():
            ys_ref[...] = acc_scr[...]

    @pl.when((i >= na_ref[0]) & (f == 0))
    def _():
        ys_ref[...] = jnp.zeros_like(ys_ref)


def _moe_ffn(tile_expert, n_active, xs, wg, wu, wd, tm, tf):
    n_rows = xs.shape[0]
    n_tiles = n_rows // tm
    nf = D_FF // tf

    def row_map(i, f, te, na):
        return (jnp.minimum(i, na[0] - 1), 0)

    def fcol(i, f, na):
        return jnp.where(i < na[0], f, nf - 1)

    grid_spec = pltpu.PrefetchScalarGridSpec(
        num_scalar_prefetch=2,
        grid=(n_tiles, nf),
        in_specs=[
            pl.BlockSpec((tm, D_MODEL), row_map),
            pl.BlockSpec((1, D_MODEL, tf), lambda i, f, te, na: (te[i], 0, fcol(i, f, na))),
            pl.BlockSpec((1, D_MODEL, tf), lambda i, f, te, na: (te[i], 0, fcol(i, f, na))),
            pl.BlockSpec((1, tf, D_MODEL), lambda i, f, te, na: (te[i], fcol(i, f, na), 0)),
        ],
        out_specs=pl.BlockSpec((tm, D_MODEL), lambda i, f, te, na: (i, 0)),
        scratch_shapes=[pltpu.VMEM((tm, D_MODEL), F32)],
    )
    return pl.pallas_call(
        _moe_ffn_kernel,
        grid_spec=grid_spec,
        out_shape=jax.ShapeDtypeStruct((n_rows, D_MODEL), F32),
        compiler_params=_cparams(("arbitrary", "arbitrary")),
        name="moe_ffn",
    )(tile_expert, n_active, xs, wg, wu, wd)


def _combine_kernel(dest_ref, x_ref, meta_ref, fw_ref, ys_hbm, o_ref, g_scr, sem, *, tc):
    def issue(t, c):
        for j in range(TOP_K):
            pltpu.make_async_copy(ys_hbm.at[pl.ds(dest_ref[0, 0, TOP_K * t + j], 1)],
                                  g_scr.at[j, pl.ds(t, 1)], sem).start(priority=j)
        return c

    lax.fori_loop(0, tc, issue, 0, unroll=DMA_ISSUE_UNROLL)
    for j in range(TOP_K):
        pltpu.make_async_copy(ys_hbm.at[pl.ds(0, tc)], g_scr.at[j], sem).wait()
    meta = meta_ref[...]
    w0 = meta[:, M_W0:M_W0 + 1]
    w1 = meta[:, M_W1:M_W1 + 1]
    x = x_ref[...] + w0 * g_scr[0] + w1 * g_scr[1]
    ms = jnp.mean(x * x, axis=-1, keepdims=True)
    o_ref[...] = x * lax.rsqrt(ms + RMS_EPS) * fw_ref[...]


def _combine(dest, x2, meta, fw, ys, tc):
    T = x2.shape[0]
    dest3 = dest.reshape(T // tc, 1, 2 * tc)
    return pl.pallas_call(
        functools.partial(_combine_kernel, tc=tc),
        grid=(T // tc,),
        in_specs=[
            pl.BlockSpec((1, 1, 2 * tc), lambda i: (i, 0, 0), memory_space=pltpu.SMEM),
            pl.BlockSpec((tc, D_MODEL), lambda i: (i, 0)),
            pl.BlockSpec((tc, LANE), lambda i: (i, 0)),
            pl.BlockSpec((1, D_MODEL), lambda i: (0, 0)),
            pl.BlockSpec(memory_space=pl.ANY),
        ],
        out_specs=pl.BlockSpec((tc, D_MODEL), lambda i: (i, 0)),
        out_shape=jax.ShapeDtypeStruct((T, D_MODEL), F32),
        scratch_shapes=[pltpu.VMEM((2, tc, D_MODEL), F32), pltpu.SemaphoreType.DMA(())],
        compiler_params=_cparams(("arbitrary",)),
        name="moe_combine",
    )(dest3, x2, meta, fw, ys)


def _moe_plan(meta, counts, tm, n_tiles):
    cnt = counts[0, :N_EXPERTS].astype(jnp.int32)
    padded = ((cnt + tm - 1) // tm) * tm
    ends = jnp.cumsum(padded)
    offs = ends - padded
    e = meta[:, M_E0:M_E1 + 1].astype(jnp.int32)
    pos = meta[:, M_P0:M_P1 + 1].astype(jnp.int32)
    dest = offs[e] + pos
    starts = jnp.arange(n_tiles, dtype=jnp.int32) * tm
    n_active = (ends[-1] // tm).astype(jnp.int32)
    te = jnp.sum(starts[:, None] >= ends[None, :], axis=-1).astype(jnp.int32)
    last_e = jnp.sum(jnp.maximum(ends[-1] - tm, 0) >= ends).astype(jnp.int32)
    te = jnp.where(starts < ends[-1], te, last_e)
    return dest, te, n_active.reshape(1)


def _moe_layer(x_mid, nw, w_router, wg, wu, wd, final_w, tiles):
    T = x_mid.shape[0]
    tm = tiles["moe_tm"]
    n_tiles = (TOP_K * T) // tm + N_EXPERTS
    h, meta, counts = _router(x_mid, nw, w_router, tiles["router_t"])
    dest, te, n_active = _moe_plan(meta, counts, tm, n_tiles)
    xs = _scatter_rows(dest, h, n_tiles * tm, tiles["scatter_t"])
    ys = _moe_ffn(te, n_active, xs, wg, wu, wd, tm, tiles["moe_tf"])
    return _combine(dest, x_mid, meta, final_w, ys, tiles["combine_t"])


def _tiles(S, T):
    return dict(
        inproj_tm=min(1024, T), inproj_tn=1920,
        attn_t=min(512, S // 2), ret_c=min(256, S),
        merge_tm=min(512, T), ffn_tm=min(1024, T), ffn_tf=512,
        router_t=min(256, T), scatter_t=min(256, T), combine_t=min(256, T),
        moe_tm=min(1024, T), moe_tf=512,
    )


def kernel(x, norm_mix_w, w_in, fox_f_bias, ret_gn_w, mla_q_norm_w, mla_kv_norm_w, mla_w_uq, mla_w_uk, mla_w_uv, w_br_fox, w_br_ret, w_br_mla, w_out, norm_ffn_w, dense_w_gate, dense_w_up, dense_w_down, moe_w_router, moe_w_gate, moe_w_up, moe_w_down, final_norm_w):
    B, S, D = x.shape
    assert D == D_MODEL and w_in.shape[0] == DEPTH == 2
    T = B * S
    tl = _tiles(S, T)
    row = lambda a: a.reshape(1, -1).astype(F32)
    ret_tabs = _ret_tables(S, tl["ret_c"])
    mla_tabs = _mla_tables(S)
    x2 = x.reshape(T, D)
    for l in range(DEPTH):
        w_all, w_ff = _build_inproj_weights(w_in[l])
        proj, ff32 = _inproj(x2, row(norm_mix_w[l]), w_all, w_ff, tl["inproj_tm"], tl["inproj_tn"])
        fqt, fkx, fvt = _fox_prep(proj, ff32, fox_f_bias[l], B, S, tl["attn_t"])
        o_fox = _attention(fqt, fkx, fvt, B, S, tl["attn_t"], FOX_RANGES, "fox_attn")
        o_ret = _retention(proj, row(ret_gn_w[l]), ret_tabs, B, S, tl["ret_c"])
        wq, wqr, wk, wv = _mla_weights(mla_w_uq[l], mla_w_uk[l], mla_w_uv[l])
        mqt, mkx, mvt = _mla_prep(proj, row(mla_q_norm_w[l]), row(mla_kv_norm_w[l]), wq, wqr, wk, wv,
                                  mla_tabs, B, S, tl["attn_t"])
        o_mla = _attention(mqt, mkx, mvt, B, S, tl["attn_t"], MLA_RANGES, "mla_attn")
        x_mid, h2 = _merge(o_fox, o_ret, o_mla, proj, x2,
                           w_br_fox[l].astype(BF16), w_br_ret[l].astype(BF16), w_br_mla[l].astype(BF16),
                           w_out[l].astype(BF16), row(norm_ffn_w[l]), tl["merge_tm"])
        if l % 2 == 0:
            i = l // 2
            x2 = _dense_ffn(h2, x_mid, dense_w_gate[i].astype(BF16), dense_w_up[i].astype(BF16),
                            dense_w_down[i].astype(BF16), tl["ffn_tm"], tl["ffn_tf"])
        else:
            i = l // 2
            x2 = _moe_layer(x_mid, row(norm_ffn_w[l]), moe_w_router[i], moe_w_gate[i],
                            moe_w_up[i], moe_w_down[i], row(final_norm_w), tl)
    return x2.reshape(B, S, D)
```

```python
import functools

import numpy as np
import jax
import jax.numpy as jnp
from jax import lax
from jax.experimental import pallas as pl
from jax.experimental.pallas import tpu as pltpu

F32 = jnp.float32
BF16 = jnp.bfloat16

D_MODEL = 1024
DEPTH = 2
FOX_HEADS = 8
FOX_HEAD_DIM = 64
FOX_W = FOX_HEADS * FOX_HEAD_DIM
RET_HEADS = 8
RET_QK_DIM = 64
RET_V_DIM = 128
RET_QK_W = RET_HEADS * RET_QK_DIM
RET_V_W = RET_HEADS * RET_V_DIM
MLA_HEADS = 8
MLA_Q_RANK = 384
MLA_KV_RANK = 256
MLA_NOPE_DIM = 64
MLA_ROPE_DIM = 32
MLA_V_DIM = 64
MLA_QK_DIM = MLA_NOPE_DIM + MLA_ROPE_DIM
MLA_V_W = MLA_HEADS * MLA_V_DIM
N_BRANCHES = 3
ROPE_THETA = 10000.0
RMS_EPS = 1e-6
D_FF = 3584
N_EXPERTS = 8
TOP_K = 2
IN_SPLITS = (FOX_W, FOX_W, FOX_W, FOX_HEADS,
             RET_QK_W, RET_QK_W, RET_V_W, RET_V_W,
             MLA_Q_RANK, MLA_KV_RANK, MLA_ROPE_DIM,
             N_BRANCHES * D_MODEL)

LANE = 128
NEG = -1e30
VMEM_LIMIT = 56 * 1024 * 1024

U_FQ, U_FK, U_FV = 0, 4, 8
U_RQ, U_RQR, U_RK, U_RKR = 12, 16, 20, 24
U_CKV, U_KR, U_KRR = 28, 30, 31
U_RV, U_RG, U_GL, U_CQ = 32, 40, 48, 72
N_PROJ = 75 * LANE
FOX_EXT_STRIDE = 8
PAIRS = 4
QK_W = 2 * LANE
ATTN_KEY_CHUNK = 256
ATTN_STRIP = 256
VT_ROWS = 80
LOG2E = 1.4426950408889634


def _cparams(sem, vmem=VMEM_LIMIT, flags=None):
    return pltpu.CompilerParams(dimension_semantics=sem, vmem_limit_bytes=vmem, flags=flags)


def _sigmoid(x):
    return 1.0 / (1.0 + jnp.exp(-x))


def _lane_mask(shape, ranges):
    lane = lax.broadcasted_iota(jnp.int32, shape, len(shape) - 1)
    m = None
    for a, b in ranges:
        r = (lane >= a) & (lane < b)
        m = r if m is None else (m | r)
    return m


def _inproj_kernel(x_ref, nw_ref, w_ref, wff_ref, out_ref, ff_ref, h_scr):
    @pl.when(pl.program_id(1) == 0)
    def _():
        x = x_ref[...]
        ms = jnp.mean(x * x, axis=-1, keepdims=True)
        h = (x * lax.rsqrt(ms + RMS_EPS) * nw_ref[...]).astype(BF16)
        h_scr[...] = h
        ff_ref[...] = jnp.dot(h, wff_ref[...], preferred_element_type=F32)

    out_ref[...] = jnp.dot(h_scr[...], w_ref[...], preferred_element_type=F32).astype(out_ref.dtype)


def _inproj(x2, nw, w_all, w_ff, tm, tn):
    T = x2.shape[0]
    return pl.pallas_call(
        _inproj_kernel,
        grid=(T // tm, N_PROJ // tn),
        in_specs=[
            pl.BlockSpec((tm, D_MODEL), lambda i, j: (i, 0)),
            pl.BlockSpec((1, D_MODEL), lambda i, j: (0, 0)),
            pl.BlockSpec((D_MODEL, tn), lambda i, j: (0, j)),
            pl.BlockSpec((D_MODEL, LANE), lambda i, j: (0, 0)),
        ],
        out_specs=[
            pl.BlockSpec((tm, tn), lambda i, j: (i, j)),
            pl.BlockSpec((tm, LANE), lambda i, j: (i, 0)),
        ],
        out_shape=[
            jax.ShapeDtypeStruct((T, N_PROJ), BF16),
            jax.ShapeDtypeStruct((T, LANE), F32),
        ],
        scratch_shapes=[pltpu.VMEM((tm, D_MODEL), BF16)],
        compiler_params=_cparams(("parallel", "arbitrary")),
        name="inproj",
    )(x2, nw, w_all, w_ff)


def _rot_cols(w, heads, d):
    k = w.shape[0]
    w4 = w.reshape(k, heads, 2, d // 2)
    return jnp.concatenate([-w4[:, :, 1], w4[:, :, 0]], axis=-1).reshape(k, heads * d)


def _build_inproj_weights(w_in):
    split_at = [int(i) for i in np.cumsum(IN_SPLITS)[:-1]]
    fq, fk, fv, ff, rq, rk, rv, rg, cq, ckv, kr, gl = jnp.split(w_in.astype(BF16), split_at, axis=-1)
    z64 = jnp.zeros((D_MODEL, 64), BF16)
    z32 = jnp.zeros((D_MODEL, 32), BF16)
    kr128 = jnp.concatenate([z64, kr, z32], axis=-1)
    krr128 = jnp.concatenate([z64, _rot_cols(kr, 1, MLA_ROPE_DIM), z32], axis=-1)
    w_all = jnp.concatenate(
        [fq, fk, fv, rq, _rot_cols(rq, RET_HEADS, RET_QK_DIM), rk, _rot_cols(rk, RET_HEADS, RET_QK_DIM),
         ckv, kr128, krr128, rv, rg, gl, cq], axis=-1)
    w_ff = jnp.pad(ff, ((0, 0), (0, LANE - FOX_HEADS)))
    return w_all, w_ff


def _split3(x):
    hi = x.astype(BF16)
    r1 = x - hi.astype(F32)
    mid = r1.astype(BF16)
    lo = (r1 - mid.astype(F32)).astype(BF16)
    return hi, mid, lo


def _store_t(dst_ref, idx, x):
    dst_ref[idx] = x.T.astype(dst_ref.dtype)


def _store_vt_pair(vt_ref, p, v2):
    hd = LANE // 2
    vt = v2.T.astype(vt_ref.dtype)
    pad_rows = lax.broadcasted_iota(jnp.int32, (VT_ROWS - hd, vt.shape[1]), 0)
    ones_then_zeros = jnp.where(pad_rows == 0, 1.0, 0.0).astype(vt_ref.dtype)
    for j in range(2):
        base = (2 * p + j) * VT_ROWS
        vt_ref[0, 0, base:base + hd, :] = vt[j * hd:(j + 1) * hd]
        vt_ref[0, 0, base + hd:base + VT_ROWS, :] = ones_then_zeros


def _fox_prep_kernel(fq_ref, fk_ref, fv_ref, ff_ref, bias_ref, tri_ref, pq_ref, pk_ref, cq_ref, ck_ref,
                     qt_ref, kx_ref, vt_ref, carry_scr):
    @pl.when(pl.program_id(1) == 0)
    def _():
        carry_scr[...] = jnp.zeros_like(carry_scr)

    ts = ff_ref.shape[0]
    z = ff_ref[...] + bias_ref[...]
    logf = jnp.minimum(z, 0.0) - jnp.log(1.0 + jnp.exp(-jnp.abs(z)))
    tri = tri_ref[...]
    cum = carry_scr[0:1, :]
    for part in _split3(logf):
        cum = cum + jnp.dot(tri, part, preferred_element_type=F32)
    carry_scr[0:1, :] = cum[ts - 1:ts, :]

    eq = cq_ref[...]
    ek = ck_ref[...]
    for j, part in enumerate(_split3(cum * LOG2E)):
        eq = eq + jnp.dot(part, pq_ref[j * LANE:(j + 1) * LANE, :], preferred_element_type=F32)
        ek = ek + jnp.dot(part, pk_ref[j * LANE:(j + 1) * LANE, :], preferred_element_type=F32)
    scale = FOX_HEAD_DIM ** -0.5 * LOG2E
    for p in range(FOX_HEADS // 2):
        src = slice(p * LANE, (p + 1) * LANE)
        _store_t(qt_ref, (0, slice(2 * p * LANE, (2 * p + 1) * LANE)), fq_ref[:, src].astype(F32) * scale)
        _store_t(qt_ref, (0, slice((2 * p + 1) * LANE, (2 * p + 2) * LANE)), eq[:, src])
        kx_ref[:, 2 * p * LANE:(2 * p + 1) * LANE] = fk_ref[:, src]
        kx_ref[:, (2 * p + 1) * LANE:(2 * p + 2) * LANE] = ek[:, src].astype(BF16)
        _store_vt_pair(vt_ref, p, fv_ref[:, src].astype(F32))


def _fox_prep_consts(ts):
    tri = np.tril(np.ones((ts, ts), np.float32))
    pq = np.zeros((3 * LANE, FOX_W), np.float32)
    pk = np.zeros((3 * LANE, FOX_W), np.float32)
    cq = np.zeros((1, FOX_W), np.float32)
    ck = np.zeros((1, FOX_W), np.float32)
    for h in range(FOX_HEADS):
        base = LANE * (h // 2) + FOX_EXT_STRIDE * (h % 2)
        for j in range(3):
            pq[j * LANE + h, base + j] = 1.0
            ck[0, base + j] = 1.0
            cq[0, base + 3 + j] = 1.0
            pk[j * LANE + h, base + 3 + j] = -1.0
    return (jnp.asarray(tri, BF16), jnp.asarray(pq, BF16), jnp.asarray(pk, BF16),
            jnp.asarray(cq), jnp.asarray(ck))


def _attn_operand_specs(B, S, t):
    T = B * S
    ns = S // t
    specs = [
        pl.BlockSpec((1, PAIRS * QK_W, t), lambda b, s: (b, 0, s)),
        pl.BlockSpec((t, PAIRS * QK_W), lambda b, s: (b * ns + s, 0)),
        pl.BlockSpec((1, 1, 2 * PAIRS * VT_ROWS, t), lambda b, s: (b, s, 0, 0)),
    ]
    shapes = [
        jax.ShapeDtypeStruct((B, PAIRS * QK_W, S), BF16),
        jax.ShapeDtypeStruct((T, PAIRS * QK_W), BF16),
        jax.ShapeDtypeStruct((B, ns, 2 * PAIRS * VT_ROWS, t), BF16),
    ]
    return specs, shapes


def _fox_prep(proj, ff32, bias, B, S, ts):
    ns = S // ts
    tri, pq, pk, cq, ck = _fox_prep_consts(ts)
    bias128 = jnp.pad(bias.astype(F32), (0, LANE - FOX_HEADS)).reshape(1, LANE)
    const = lambda shape: pl.BlockSpec(shape, lambda b, s: (0, 0))
    out_specs, out_shape = _attn_operand_specs(B, S, ts)
    return pl.pallas_call(
        _fox_prep_kernel,
        grid=(B, ns),
        in_specs=[
            pl.BlockSpec((ts, FOX_W), lambda b, s: (b * ns + s, U_FQ // 4)),
            pl.BlockSpec((ts, FOX_W), lambda b, s: (b * ns + s, U_FK // 4)),
            pl.BlockSpec((ts, FOX_W), lambda b, s: (b * ns + s, U_FV // 4)),
            pl.BlockSpec((ts, LANE), lambda b, s: (b * ns + s, 0)),
            const((1, LANE)), const((ts, ts)), const((3 * LANE, FOX_W)), const((3 * LANE, FOX_W)),
            const((1, FOX_W)), const((1, FOX_W)),
        ],
        out_specs=out_specs,
        out_shape=out_shape,
        scratch_shapes=[pltpu.VMEM((8, LANE), F32)],
        compiler_params=_cparams(("parallel", "arbitrary")),
        name="fox_prep",
    )(proj, proj, proj, ff32, bias128, tri, pq, pk, cq, ck)


def _row_mask(shape, ranges):
    row = lax.broadcasted_iota(jnp.int32, shape, 0)
    m = None
    for a, b in ranges:
        r = (row >= a) & (row < b)
        m = r if m is None else (m | r)
    return m


def _col_reduce(x, op, reduce_fn, parts=2):
    c = x.shape[0] // parts
    pieces = [x[i * c:(i + 1) * c] for i in range(parts)]
    while len(pieces) > 1:
        pieces = [op(pieces[i], pieces[i + 1]) for i in range(0, len(pieces), 2)]
    return reduce_fn(pieces[0], axis=0, keepdims=True)


def _attn_kernel(qt_ref, qn_ref, k_ref, vt_ref, o_ref, qh_scr, qn_scr, sa_scr, sb_scr, ma_scr, mb_scr, acc_scr,
                 *, tk, ranges):
    qi = pl.program_id(2)
    qt = qt_ref[0]
    tq = qt.shape[-1]
    for h in range(2):
        qh_scr[h] = jnp.where(_row_mask(qt.shape, ranges[h]), qt, jnp.zeros_like(qt))
    acc_scr[...] = jnp.zeros_like(acc_scr)

    def key_tile(kt):
        return k_ref[pl.ds(pl.multiple_of(kt * tk, tk), tk), :]

    strips = [(c, c + ATTN_STRIP) for c in range(0, tq, ATTN_STRIP)]

    def scores(kt, s_scr, m_scr):
        k = key_tile(kt)
        for h in range(2):
            for a, b in strips:
                s = jnp.dot(k, qh_scr[h, :, a:b], preferred_element_type=F32)
                s_scr[h, :, a:b] = s
                m_scr[h, :, a:b] = _col_reduce(s, jnp.maximum, jnp.max)

    def update(kt, h, s, m_tile, m_prev, lo):
        m_new = jnp.maximum(m_prev, m_tile)
        alpha = jnp.exp2(m_prev - m_new)
        hi = lo + s.shape[-1]
        pv = None
        for r in range(0, tk, ATTN_KEY_CHUNK):
            p = jnp.exp2((s[r:r + ATTN_KEY_CHUNK] - m_new).astype(BF16))
            vt = vt_ref[0, kt, h * VT_ROWS:(h + 1) * VT_ROWS, r:r + ATTN_KEY_CHUNK]
            d = jnp.dot(vt, p, preferred_element_type=F32)
            pv = d if pv is None else pv + d
        acc_scr[h, :, lo:hi] = alpha * acc_scr[h, :, lo:hi] + pv
        return m_new

    def diag_update(kt, h, s_scr, m_prev, a, b, lo):
        s = s_scr[h, :, a:b]
        if a < lo + tk:
            key = lax.broadcasted_iota(jnp.int32, s.shape, 0)
            qry = lax.broadcasted_iota(jnp.int32, s.shape, 1) + (a - lo)
            s = jnp.where(key <= qry, s, NEG)
        return update(kt, h, s, _col_reduce(s, jnp.maximum, jnp.max), m_prev, a)

    def diagonal_tiles(carry, lookahead):
        k_last = key_tile(2 * qi + 1)
        if lookahead:
            qn = qn_ref[0]
            k0 = key_tile(0)
            for h in range(2):
                qn_scr[h] = jnp.where(_row_mask(qn.shape, ranges[h]), qn, jnp.zeros_like(qn))
        out = []
        for h in range(2):
            ms = []
            for a, b in strips:
                if a >= tk:
                    sb_scr[h, :, a:b] = jnp.dot(k_last, qh_scr[h, :, a:b], preferred_element_type=F32)
                ms.append(diag_update(2 * qi, h, sa_scr, carry[h][:, a:b], a, b, 0))
                if lookahead:
                    s = jnp.dot(k0, qn_scr[h, :, a:b], preferred_element_type=F32)
                    sa_scr[h, :, a:b] = s
                    ma_scr[h, :, a:b] = _col_reduce(s, jnp.maximum, jnp.max)
            out.append(ms)
        for h in range(2):
            for i, (a, b) in enumerate(strips):
                if a >= tk:
                    diag_update(2 * qi + 1, h, sb_scr, out[h][i], a, b, tk)
        hd = LANE // 2
        ot = jnp.concatenate([acc_scr[h, :hd] / acc_scr[h, hd:hd + 1] for h in range(2)], axis=0)
        o_ref[...] = ot.T.astype(o_ref.dtype)

    def fused_step(kt, cur_s, cur_m, nxt_s, nxt_m, carry):
        k = key_tile(kt + 1)
        ms = [[], []]
        for a, b in strips:
            for h in range(2):
                s = jnp.dot(k, qh_scr[h, :, a:b], preferred_element_type=F32)
                nxt_s[h, :, a:b] = s
                nxt_m[h, :, a:b] = _col_reduce(s, jnp.maximum, jnp.max)
                ms[h].append(update(kt, h, cur_s[h, :, a:b], cur_m[h, :, a:b], carry[h][:, a:b], a))
        return tuple(jnp.concatenate(m, axis=-1) for m in ms)

    def body(j, carry):
        kt = 2 * j
        carry = fused_step(kt, sa_scr, ma_scr, sb_scr, mb_scr, carry)
        return fused_step(kt + 1, sb_scr, mb_scr, sa_scr, ma_scr, carry)

    init = tuple(jnp.full((1, tq), NEG, F32) for _ in range(2))

    @pl.when(qi == 0)
    def _():
        scores(0, sa_scr, ma_scr)

    carry = lax.fori_loop(0, qi, body, init)
    last = pl.num_programs(2) - 1

    @pl.when(qi < last)
    def _():
        diagonal_tiles(carry, True)

    @pl.when(qi == last)
    def _():
        diagonal_tiles(carry, False)


def _attention(qt, kx, vt, B, S, tk, ranges, name):
    T = B * S
    tq = 2 * tk
    nq = S // tq
    kern = functools.partial(_attn_kernel, tk=tk, ranges=ranges)
    return pl.pallas_call(
        kern,
        grid=(B, PAIRS, nq),
        in_specs=[
            pl.BlockSpec((1, QK_W, tq), lambda b, p, i: (b, p, i)),
            pl.BlockSpec((1, QK_W, tq), lambda b, p, i: (b, p, jnp.minimum(i + 1, nq - 1))),
            pl.BlockSpec((S, QK_W), lambda b, p, i: (b, p)),
            pl.BlockSpec((1, S // tk, 2 * VT_ROWS, tk), lambda b, p, i: (b, 0, p, 0)),
        ],
        out_specs=pl.BlockSpec((tq, LANE), lambda b, p, i: (b * nq + i, p)),
        out_shape=jax.ShapeDtypeStruct((T, PAIRS * LANE), BF16),
        scratch_shapes=[
            pltpu.VMEM((2, QK_W, tq), BF16),
            pltpu.VMEM((2, QK_W, tq), BF16),
            pltpu.VMEM((2, tk, tq), F32),
            pltpu.VMEM((2, tk, tq), F32),
            pltpu.VMEM((2, 1, tq), F32),
            pltpu.VMEM((2, 1, tq), F32),
            pltpu.VMEM((2, VT_ROWS, tq), F32),
        ],
        compiler_params=_cparams(("parallel", "parallel", "arbitrary")),
        name=name,
    )(qt, qt, kx, vt)


_E0 = LANE
FOX_RANGES = (((0, 64), (_E0, _E0 + FOX_EXT_STRIDE)),
              ((64, 128), (_E0 + FOX_EXT_STRIDE, _E0 + 2 * FOX_EXT_STRIDE)))
MLA_RANGES = (((0, LANE),), ((LANE, 2 * LANE),))


def _mla_prep_kernel(cq_ref, ckv_ref, kr_ref, krr_ref, qnw_ref, kvnw_ref, wq_ref, wqr_ref, wk_ref, wv_ref,
                     cq_tab, sq_tab, ck_tab, sk_tab, qt_ref, k_ref, vt_ref):
    def norm(x_ref, w_ref):
        x = x_ref[...].astype(F32)
        ms = jnp.mean(x * x, axis=-1, keepdims=True)
        return (x * lax.rsqrt(ms + RMS_EPS) * w_ref[...]).astype(BF16)

    c_q = norm(cq_ref, qnw_ref)
    c_kv = norm(ckv_ref, kvnw_ref)
    tile8 = lambda a: jnp.concatenate([a] * MLA_HEADS, axis=-1)
    q = (jnp.dot(c_q, wq_ref[...], preferred_element_type=F32) * tile8(cq_tab[...])
         + jnp.dot(c_q, wqr_ref[...], preferred_element_type=F32) * tile8(sq_tab[...]))
    for c in range(MLA_HEADS):
        _store_t(qt_ref, (0, slice(c * LANE, (c + 1) * LANE)), q[:, c * LANE:(c + 1) * LANE])
    k_rope = kr_ref[...].astype(F32) * ck_tab[...] + krr_ref[...].astype(F32) * sk_tab[...]
    k = jnp.dot(c_kv, wk_ref[...], preferred_element_type=F32) + tile8(k_rope)
    k_ref[...] = k.astype(BF16)
    v = jnp.dot(c_kv, wv_ref[...], preferred_element_type=F32)
    for p in range(PAIRS):
        _store_vt_pair(vt_ref, p, v[:, p * LANE:(p + 1) * LANE])


def _mla_weights(w_uq, w_uk, w_uv):
    q3 = w_uq.reshape(MLA_Q_RANK, MLA_HEADS, MLA_QK_DIM)
    nope, ropep = q3[..., :MLA_NOPE_DIM], q3[..., MLA_NOPE_DIM:]
    half = MLA_ROPE_DIM // 2
    rot = jnp.concatenate([-ropep[..., half:], ropep[..., :half]], axis=-1)
    z32 = jnp.zeros((MLA_Q_RANK, MLA_HEADS, LANE - MLA_QK_DIM), F32)
    z64 = jnp.zeros((MLA_Q_RANK, MLA_HEADS, MLA_NOPE_DIM), F32)
    wq = jnp.concatenate([nope, ropep, z32], axis=-1).reshape(MLA_Q_RANK, MLA_HEADS * LANE)
    wqr = jnp.concatenate([z64, rot, z32], axis=-1).reshape(MLA_Q_RANK, MLA_HEADS * LANE)
    k3 = w_uk.reshape(MLA_KV_RANK, MLA_HEADS, MLA_NOPE_DIM)
    wk = jnp.concatenate([k3, jnp.zeros_like(k3)], axis=-1).reshape(MLA_KV_RANK, MLA_HEADS * LANE)
    return wq.astype(BF16), wqr.astype(BF16), wk.astype(BF16), w_uv.astype(BF16)


def _mla_tables(S):
    pos = jnp.arange(S, dtype=F32)
    inv = ROPE_THETA ** (-jnp.arange(0, MLA_ROPE_DIM, 2, dtype=F32) / MLA_ROPE_DIM)
    ang = pos[:, None] * inv[None, :]
    cos, sin = jnp.cos(ang), jnp.sin(ang)
    ones = jnp.ones((S, MLA_NOPE_DIM), F32)
    z64 = jnp.zeros((S, MLA_NOPE_DIM), F32)
    z32 = jnp.zeros((S, LANE - MLA_QK_DIM), F32)
    ctab = jnp.concatenate([ones, cos, cos, z32], axis=-1)
    stab = jnp.concatenate([z64, sin, sin, z32], axis=-1)
    scale = MLA_QK_DIM ** -0.5 * LOG2E
    return ctab * scale, stab * scale, ctab, stab


def _mla_prep(proj, qnw, kvnw, wq, wqr, wk, wv, tabs, B, S, tm):
    ns = S // tm
    HW = MLA_HEADS * LANE
    const = lambda shape: pl.BlockSpec(shape, lambda b, s: (0, 0))
    tab = pl.BlockSpec((tm, LANE), lambda b, s: (s, 0))
    out_specs, out_shape = _attn_operand_specs(B, S, tm)
    return pl.pallas_call(
        _mla_prep_kernel,
        grid=(B, ns),
        in_specs=[
            pl.BlockSpec((tm, MLA_Q_RANK), lambda b, s: (b * ns + s, U_CQ // 3)),
            pl.BlockSpec((tm, MLA_KV_RANK), lambda b, s: (b * ns + s, U_CKV // 2)),
            pl.BlockSpec((tm, LANE), lambda b, s: (b * ns + s, U_KR)),
            pl.BlockSpec((tm, LANE), lambda b, s: (b * ns + s, U_KRR)),
            const((1, MLA_Q_RANK)), const((1, MLA_KV_RANK)),
            const((MLA_Q_RANK, HW)), const((MLA_Q_RANK, HW)), const((MLA_KV_RANK, HW)),
            const((MLA_KV_RANK, MLA_V_W)),
            tab, tab, tab, tab,
        ],
        out_specs=out_specs,
        out_shape=out_shape,
        compiler_params=_cparams(("parallel", "parallel")),
        name="mla_prep",
    )(proj, proj, proj, proj, qnw, kvnw, wq, wqr, wk, wv, *tabs)


def _ret_kernel(rq_ref, rqr_ref, rk_ref, rkr_ref, rv_ref, rg_ref, cos_ref, sin_ref,
                din_ref, xi_ref, zeta_ref, gch_ref, gnw_ref, o_ref, r_scr):
    @pl.when(pl.program_id(0) == 0)
    def _():
        r_scr[...] = jnp.zeros_like(r_scr)

    tile4 = lambda a: jnp.concatenate([a] * (RET_QK_W // LANE), axis=-1)
    cos4 = tile4(cos_ref[...])
    sin4 = tile4(sin_ref[...])
    for bi in range(rq_ref.shape[0]):
        q = rq_ref[bi].astype(F32) * cos4 + rqr_ref[bi].astype(F32) * sin4
        k = (rk_ref[bi].astype(F32) * cos4 + rkr_ref[bi].astype(F32) * sin4) * (RET_QK_DIM ** -0.5)
        for p in range(RET_HEADS // 2):
            q2 = q[:, p * LANE:(p + 1) * LANE]
            k2 = k[:, p * LANE:(p + 1) * LANE]
            k2b = k2.astype(BF16)
            kz2 = (k2 * zeta_ref[p]).astype(BF16)
            r2 = r_scr[bi, p]
            r2b = r2.astype(BF16)
            new_r = gch_ref[p] * r2
            for j in range(2):
                h = 2 * p + j
                half = ((j * RET_QK_DIM, (j + 1) * RET_QK_DIM),)
                qm = jnp.where(_lane_mask(q2.shape, half), q2, 0.0).astype(BF16)
                inner = lax.dot_general(qm, k2b, (((1,), (1,)), ((), ())), preferred_element_type=F32)
                inner = inner * din_ref[h]
                vh = rv_ref[bi, :, h * LANE:(h + 1) * LANE]
                o = (jnp.dot(inner.astype(BF16), vh, preferred_element_type=F32)
                     + jnp.dot(qm, r2b, preferred_element_type=F32) * xi_ref[h])
                upd = lax.dot_general(kz2, vh, (((0,), (0,)), ((), ())), preferred_element_type=F32)
                rows = lax.broadcasted_iota(jnp.int32, upd.shape, 0)
                new_r = new_r + jnp.where((rows >= half[0][0]) & (rows < half[0][1]), upd, 0.0)
                mu = jnp.mean(o, axis=-1, keepdims=True)
                d = o - mu
                var = jnp.mean(d * d, axis=-1, keepdims=True)
                on = d * lax.rsqrt(var + RMS_EPS) * gnw_ref[:, h * LANE:(h + 1) * LANE]
                g = rg_ref[bi, :, h * LANE:(h + 1) * LANE].astype(F32)
                o_ref[bi, :, h * LANE:(h + 1) * LANE] = (g * _sigmoid(g) * on).astype(o_ref.dtype)
            r_scr[bi, p] = new_r


def _ret_tables(S, C):
    pos = jnp.arange(S, dtype=F32)
    inv = ROPE_THETA ** (-jnp.arange(0, RET_QK_DIM, 2, dtype=F32) / RET_QK_DIM)
    ang = pos[:, None] * inv[None, :]
    cos = jnp.concatenate([jnp.cos(ang)] * 4, axis=-1)
    sin = jnp.concatenate([jnp.sin(ang)] * 4, axis=-1)
    gammas = 1.0 - 2.0 ** (-5.0 - jnp.arange(RET_HEADS, dtype=F32))
    log_g = jnp.log(gammas)
    j = jnp.arange(C, dtype=F32)
    diff = j[:, None] - j[None, :]
    din = jnp.where(diff[None] >= 0, jnp.exp(jnp.maximum(diff, 0.0)[None] * log_g[:, None, None]), 0.0)
    xi = jnp.exp((j[None, :] + 1.0) * log_g[:, None])
    zeta = jnp.exp((C - 1.0 - j[None, :]) * log_g[:, None])
    gch = jnp.exp(C * log_g)
    xi_t = jnp.broadcast_to(xi[:, :, None], (RET_HEADS, C, LANE))
    zeta_p = jnp.repeat(zeta.reshape(RET_HEADS // 2, 2, C).transpose(0, 2, 1), RET_QK_DIM, axis=-1)
    gch_p = jnp.broadcast_to(jnp.repeat(gch.reshape(RET_HEADS // 2, 2), RET_QK_DIM, axis=-1)[:, :, None],
                             (RET_HEADS // 2, LANE, LANE))
    return cos, sin, din, xi_t, zeta_p, gch_p


def _retention(proj, gnw, tabs, B, S, C):
    nc = S // C
    cos, sin, din, xi, zeta, gch = tabs
    proj3 = proj.reshape(B, S, N_PROJ)
    blk = lambda w, unit: pl.BlockSpec((B, C, w), lambda c: (0, c, unit * LANE // w))
    full = lambda a: pl.BlockSpec(a.shape, lambda c: (0,) * a.ndim)
    out = pl.pallas_call(
        _ret_kernel,
        grid=(nc,),
        in_specs=[
            blk(RET_QK_W, U_RQ), blk(RET_QK_W, U_RQR), blk(RET_QK_W, U_RK), blk(RET_QK_W, U_RKR),
            blk(RET_V_W, U_RV), blk(RET_V_W, U_RG),
            pl.BlockSpec((C, LANE), lambda c: (c, 0)),
            pl.BlockSpec((C, LANE), lambda c: (c, 0)),
            full(din), full(xi), full(zeta), full(gch), full(gnw),
        ],
        out_specs=pl.BlockSpec((B, C, RET_V_W), lambda c: (0, c, 0)),
        out_shape=jax.ShapeDtypeStruct((B, S, RET_V_W), BF16),
        scratch_shapes=[pltpu.VMEM((B, RET_HEADS // 2, LANE, LANE), F32)],
        compiler_params=_cparams(("arbitrary",)),
        name="retention",
    )(proj3, proj3, proj3, proj3, proj3, proj3, cos, sin, din, xi, zeta, gch, gnw)
    return out.reshape(B * S, RET_V_W)


def _merge_kernel(of_ref, or_ref, om_ref, g0_ref, g1_ref, g2_ref, x_ref,
                  wf_ref, wr_ref, wm_ref, wo_ref, nw_ref, xo_ref, h_ref):
    merged = (_sigmoid(g0_ref[...].astype(F32)) * jnp.dot(of_ref[...], wf_ref[...], preferred_element_type=F32)
              + _sigmoid(g1_ref[...].astype(F32)) * jnp.dot(or_ref[...], wr_ref[...], preferred_element_type=F32)
              + _sigmoid(g2_ref[...].astype(F32)) * jnp.dot(om_ref[...], wm_ref[...], preferred_element_type=F32))
    x = x_ref[...] + jnp.dot(merged.astype(BF16), wo_ref[...], preferred_element_type=F32)
    xo_ref[...] = x
    ms = jnp.mean(x * x, axis=-1, keepdims=True)
    h_ref[...] = (x * lax.rsqrt(ms + RMS_EPS) * nw_ref[...]).astype(h_ref.dtype)


def _merge(o_fox, o_ret, o_mla, proj, x2, wf, wr, wm, wo, nw, tm):
    T = x2.shape[0]
    row = lambda w, cb=0: pl.BlockSpec((tm, w), lambda i: (i, cb))
    const = lambda a: pl.BlockSpec(a.shape, lambda i: (0, 0))
    g_unit = U_GL // 8
    return pl.pallas_call(
        _merge_kernel,
        grid=(T // tm,),
        in_specs=[
            row(FOX_W), row(RET_V_W), row(MLA_V_W),
            row(D_MODEL, g_unit), row(D_MODEL, g_unit + 1), row(D_MODEL, g_unit + 2),
            row(D_MODEL),
            const(wf), const(wr), const(wm), const(wo), const(nw),
        ],
        out_specs=[row(D_MODEL), row(D_MODEL)],
        out_shape=[jax.ShapeDtypeStruct((T, D_MODEL), F32), jax.ShapeDtypeStruct((T, D_MODEL), BF16)],
        compiler_params=_cparams(("parallel",)),
        name="merge",
    )(o_fox, o_ret, o_mla, proj, proj, proj, x2, wf, wr, wm, wo, nw)


def _swiglu_tile(h, wg, wu, wd):
    a = jnp.dot(h, wg, preferred_element_type=F32)
    b = jnp.dot(h, wu, preferred_element_type=F32)
    return jnp.dot((a * _sigmoid(a) * b).astype(BF16), wd, preferred_element_type=F32)


def _dense_ffn_kernel(h_ref, wg_ref, wu_ref, wd_ref, x_ref, o_ref, acc_scr):
    f = pl.program_id(1)

    @pl.when(f == 0)
    def _():
        acc_scr[...] = jnp.zeros_like(acc_scr)

    acc_scr[...] += _swiglu_tile(h_ref[...], wg_ref[...], wu_ref[...], wd_ref[...])

    @pl.when(f == pl.num_programs(1) - 1)
    def _():
        o_ref[...] = x_ref[...] + acc_scr[...]


def _dense_ffn(h, x2, wg, wu, wd, tm, tf):
    T = x2.shape[0]
    return pl.pallas_call(
        _dense_ffn_kernel,
        grid=(T // tm, D_FF // tf),
        in_specs=[
            pl.BlockSpec((tm, D_MODEL), lambda i, f: (i, 0)),
            pl.BlockSpec((D_MODEL, tf), lambda i, f: (0, f)),
            pl.BlockSpec((D_MODEL, tf), lambda i, f: (0, f)),
            pl.BlockSpec((tf, D_MODEL), lambda i, f: (f, 0)),
            pl.BlockSpec((tm, D_MODEL), lambda i, f: (i, 0)),
        ],
        out_specs=pl.BlockSpec((tm, D_MODEL), lambda i, f: (i, 0)),
        out_shape=jax.ShapeDtypeStruct((T, D_MODEL), F32),
        scratch_shapes=[pltpu.VMEM((tm, D_MODEL), F32)],
        compiler_params=_cparams(("parallel", "arbitrary")),
        name="dense_ffn",
    )(h, wg, wu, wd, x2)


M_E0, M_E1, M_P0, M_P1, M_W0, M_W1 = 0, 1, 2, 3, 4, 5


def _split2(x):
    hi = x.astype(BF16)
    lo = (x - hi.astype(F32)).astype(BF16)
    return hi, lo


def _router_kernel(x_ref, nw_ref, whi_ref, wlo_ref, tri_ref, h_ref, meta_ref, cnt_ref, carry_scr):
    @pl.when(pl.program_id(0) == 0)
    def _():
        carry_scr[...] = jnp.zeros_like(carry_scr)

    x = x_ref[...]
    ms = jnp.mean(x * x, axis=-1, keepdims=True)
    h = x * lax.rsqrt(ms + RMS_EPS) * nw_ref[...]
    h_ref[...] = h
    hhi, hlo = _split2(h)
    logits = (jnp.dot(hhi, whi_ref[...], preferred_element_type=F32)
              + jnp.dot(hhi, wlo_ref[...], preferred_element_type=F32)
              + jnp.dot(hlo, whi_ref[...], preferred_element_type=F32))
    lane = lax.broadcasted_iota(jnp.int32, logits.shape, 1)
    lanef = lane.astype(F32)
    logits = jnp.where(lane < N_EXPERTS, logits, NEG)
    m0 = jnp.max(logits, axis=-1, keepdims=True)
    e0 = jnp.min(jnp.where(logits == m0, lanef, float(LANE)), axis=-1, keepdims=True)
    oh0 = lanef == e0
    rest = jnp.where(oh0, NEG, logits)
    m1 = jnp.max(rest, axis=-1, keepdims=True)
    e1 = jnp.min(jnp.where(rest == m1, lanef, float(LANE)), axis=-1, keepdims=True)
    oh1 = lanef == e1
    z = jnp.exp(m1 - m0)
    w0 = 1.0 / (1.0 + z)
    w1 = z / (1.0 + z)
    both = jnp.where(oh0 | oh1, 1.0, 0.0)
    before = carry_scr[0:1, :] + jnp.dot(tri_ref[...], both.astype(BF16), preferred_element_type=F32)
    p0 = jnp.sum(jnp.where(oh0, before, 0.0), axis=-1, keepdims=True)
    p1 = jnp.sum(jnp.where(oh1, before, 0.0), axis=-1, keepdims=True)
    total = carry_scr[0:1, :] + jnp.sum(both, axis=0, keepdims=True)
    carry_scr[0:1, :] = total
    cnt_ref[...] = jnp.broadcast_to(total, cnt_ref.shape)
    meta = jnp.zeros(logits.shape, F32)
    for idx, val in ((M_E0, e0), (M_E1, e1), (M_P0, p0), (M_P1, p1), (M_W0, w0), (M_W1, w1)):
        meta = jnp.where(lane == idx, val, meta)
    meta_ref[...] = meta


def _router(x2, nw, w_router, tr):
    T = x2.shape[0]
    wr = jnp.pad(w_router.astype(F32), ((0, 0), (0, LANE - N_EXPERTS)))
    whi = wr.astype(BF16)
    wlo = (wr - whi.astype(F32)).astype(BF16)
    tri = jnp.asarray(np.tril(np.ones((tr, tr), np.float32), -1), BF16)
    const = lambda a: pl.BlockSpec(a.shape, lambda i: (0, 0))
    return pl.pallas_call(
        _router_kernel,
        grid=(T // tr,),
        in_specs=[pl.BlockSpec((tr, D_MODEL), lambda i: (i, 0)), const(nw), const(whi), const(wlo), const(tri)],
        out_specs=[
            pl.BlockSpec((tr, D_MODEL), lambda i: (i, 0)),
            pl.BlockSpec((tr, LANE), lambda i: (i, 0)),
            pl.BlockSpec((8, LANE), lambda i: (0, 0)),
        ],
        out_shape=[
            jax.ShapeDtypeStruct((T, D_MODEL), F32),
            jax.ShapeDtypeStruct((T, LANE), F32),
            jax.ShapeDtypeStruct((8, LANE), F32),
        ],
        scratch_shapes=[pltpu.VMEM((8, LANE), F32)],
        compiler_params=_cparams(("arbitrary",)),
        name="router",
    )(x2, nw, whi, wlo, tri)


def _scatter_kernel(dest_ref, h_ref, xs_in, xs_hbm, sem, *, tt):
    del xs_in

    def row_copy(t, d_row):
        return pltpu.make_async_copy(h_ref.at[pl.ds(t, 1)], xs_hbm.at[pl.ds(d_row, 1)], sem)

    for t in range(tt):
        for j in range(TOP_K):
            row_copy(t, dest_ref[0, 0, TOP_K * t + j]).start(priority=j)

    for _ in range(TOP_K):
        pltpu.make_async_copy(h_ref, xs_hbm.at[pl.ds(0, tt)], sem).wait()


def _scatter_rows(dest, h, n_rows, tt):
    T = h.shape[0]
    xs0 = jnp.zeros((n_rows, D_MODEL), h.dtype)
    dest3 = dest.reshape(T // tt, 1, 2 * tt)
    return pl.pallas_call(
        functools.partial(_scatter_kernel, tt=tt),
        grid=(T // tt,),
        in_specs=[
            pl.BlockSpec((1, 1, 2 * tt), lambda i: (i, 0, 0), memory_space=pltpu.SMEM),
            pl.BlockSpec((tt, D_MODEL), lambda i: (i, 0)),
            pl.BlockSpec(memory_space=pl.ANY),
        ],
        out_specs=pl.BlockSpec(memory_space=pl.ANY),
        out_shape=jax.ShapeDtypeStruct((n_rows, D_MODEL), h.dtype),
        scratch_shapes=[pltpu.SemaphoreType.DMA(())],
        input_output_aliases={2: 0},
        compiler_params=_cparams(("arbitrary",)),
        name="moe_scatter",
    )(dest3, h, xs0)


def _moe_ffn_kernel(te_ref, na_ref, xs_ref, wg_ref, wu_ref, wd_ref, ys_ref, acc_scr):
    i = pl.program_id(0)
    f = pl.program_id(1)

    @pl.when(i < na_ref[0])
    def _():
        @pl.when(f == 0)
        def _():
            acc_scr[...] = jnp.zeros_like(acc_scr)

        acc_scr[...] += _swiglu_tile(xs_ref[...].astype(BF16), wg_ref[0].astype(BF16),
                                     wu_ref[0].astype(BF16), wd_ref[0].astype(BF16))

        @pl.when(f == pl.num_programs(1) - 1)
        def _():
            ys_ref[...] = acc_scr[...]

    @pl.when((i >= na_ref[0]) & (f == 0))
    def _():
        ys_ref[...] = jnp.zeros_like(ys_ref)


def _moe_ffn(tile_expert, n_active, xs, wg, wu, wd, tm, tf):
    n_rows = xs.shape[0]
    n_tiles = n_rows // tm
    nf = D_FF // tf

    def row_map(i, f, te, na):
        return (jnp.minimum(i, na[0] - 1), 0)

    def fcol(i, f, na):
        return jnp.where(i < na[0], f, nf - 1)

    grid_spec = pltpu.PrefetchScalarGridSpec(
        num_scalar_prefetch=2,
        grid=(n_tiles, nf),
        in_specs=[
            pl.BlockSpec((tm, D_MODEL), row_map),
            pl.BlockSpec((1, D_MODEL, tf), lambda i, f, te, na: (te[i], 0, fcol(i, f, na))),
            pl.BlockSpec((1, D_MODEL, tf), lambda i, f, te, na: (te[i], 0, fcol(i, f, na))),
            pl.BlockSpec((1, tf, D_MODEL), lambda i, f, te, na: (te[i], fcol(i, f, na), 0)),
        ],
        out_specs=pl.BlockSpec((tm, D_MODEL), lambda i, f, te, na: (i, 0)),
        scratch_shapes=[pltpu.VMEM((tm, D_MODEL), F32)],
    )
    return pl.pallas_call(
        _moe_ffn_kernel,
        grid_spec=grid_spec,
        out_shape=jax.ShapeDtypeStruct((n_rows, D_MODEL), F32),
        compiler_params=_cparams(("arbitrary", "arbitrary")),
        name="moe_ffn",
    )(tile_expert, n_active, xs, wg, wu, wd)


def _combine_kernel(dest_ref, dnext_ref, x_ref, meta_ref, fw_ref, ys_hbm, o_ref, g_scr, sems, *, tc):
    i = pl.program_id(0)

    def gather(d_ref, half, s):
        for t in range(tc):
            for j in range(TOP_K):
                row = d_ref[0, 0, TOP_K * (half * tc + t) + j]
                pltpu.make_async_copy(ys_hbm.at[pl.ds(row, 1)], g_scr.at[s, j, pl.ds(t, 1)],
                                      sems.at[s]).start(priority=j)

    def wait(s):
        for j in range(TOP_K):
            pltpu.make_async_copy(ys_hbm.at[pl.ds(0, tc)], g_scr.at[s, j], sems.at[s]).wait()

    def combine(s):
        rows = pl.ds(s * tc, tc)
        meta = meta_ref[rows, :]
        w0 = meta[:, M_W0:M_W0 + 1]
        w1 = meta[:, M_W1:M_W1 + 1]
        x = x_ref[rows, :] + w0 * g_scr[s, 0] + w1 * g_scr[s, 1]
        ms = jnp.mean(x * x, axis=-1, keepdims=True)
        o_ref[rows, :] = x * lax.rsqrt(ms + RMS_EPS) * fw_ref[...]

    @pl.when(i == 0)
    def _():
        gather(dest_ref, 0, 0)

    gather(dest_ref, 1, 1)
    wait(0)
    combine(0)
    gather(dnext_ref, 0, 0)
    wait(1)
    combine(1)

    @pl.when(i == pl.num_programs(0) - 1)
    def _():
        wait(0)


def _combine(dest, x2, meta, fw, ys, tc):
    T = x2.shape[0]
    n = T // (2 * tc)
    dest3 = dest.reshape(n, 1, 2 * TOP_K * tc)
    dspec = lambda imap: pl.BlockSpec((1, 1, 2 * TOP_K * tc), imap, memory_space=pltpu.SMEM)
    return pl.pallas_call(
        functools.partial(_combine_kernel, tc=tc),
        grid=(n,),
        in_specs=[
            dspec(lambda i: (i, 0, 0)),
            dspec(lambda i: (jnp.minimum(i + 1, n - 1), 0, 0)),
            pl.BlockSpec((2 * tc, D_MODEL), lambda i: (i, 0)),
            pl.BlockSpec((2 * tc, LANE), lambda i: (i, 0)),
            pl.BlockSpec((1, D_MODEL), lambda i: (0, 0)),
            pl.BlockSpec(memory_space=pl.ANY),
        ],
        out_specs=pl.BlockSpec((2 * tc, D_MODEL), lambda i: (i, 0)),
        out_shape=jax.ShapeDtypeStruct((T, D_MODEL), F32),
        scratch_shapes=[pltpu.VMEM((2, TOP_K, tc, D_MODEL), F32), pltpu.SemaphoreType.DMA((2,))],
        compiler_params=_cparams(("arbitrary",)),
        name="moe_combine",
    )(dest3, dest3, x2, meta, fw, ys)


def _moe_plan(meta, counts, tm, n_tiles):
    cnt = counts[0, :N_EXPERTS].astype(jnp.int32)
    padded = ((cnt + tm - 1) // tm) * tm
    ends = jnp.cumsum(padded)
    offs = ends - padded
    e = meta[:, M_E0:M_E1 + 1].astype(jnp.int32)
    pos = meta[:, M_P0:M_P1 + 1].astype(jnp.int32)
    dest = offs[e] + pos
    starts = jnp.arange(n_tiles, dtype=jnp.int32) * tm
    n_active = (ends[-1] // tm).astype(jnp.int32)
    te = jnp.sum(starts[:, None] >= ends[None, :], axis=-1).astype(jnp.int32)
    last_e = jnp.sum(jnp.maximum(ends[-1] - tm, 0) >= ends).astype(jnp.int32)
    te = jnp.where(starts < ends[-1], te, last_e)
    return dest, te, n_active.reshape(1)


def _moe_layer(x_mid, nw, w_router, wg, wu, wd, final_w, tiles):
    T = x_mid.shape[0]
    tm = tiles["moe_tm"]
    n_tiles = (TOP_K * T) // tm + N_EXPERTS
    h, meta, counts = _router(x_mid, nw, w_router, tiles["router_t"])
    dest, te, n_active = _moe_plan(meta, counts, tm, n_tiles)
    xs = _scatter_rows(dest, h, n_tiles * tm, tiles["scatter_t"])
    ys = _moe_ffn(te, n_active, xs, wg, wu, wd, tm, tiles["moe_tf"])
    return _combine(dest, x_mid, meta, final_w, ys, tiles["combine_t"])


def _tiles(S, T):
    return dict(
        inproj_tm=min(1024, T), inproj_tn=1920,
        attn_t=min(512, S // 2), ret_c=min(256, S),
        merge_tm=min(512, T), ffn_tm=min(1024, T), ffn_tf=512,
        router_t=min(256, T), scatter_t=min(256, T), combine_t=min(256, T),
        moe_tm=min(1024, T), moe_tf=512,
    )


def kernel(x, norm_mix_w, w_in, fox_f_bias, ret_gn_w, mla_q_norm_w, mla_kv_norm_w, mla_w_uq, mla_w_uk, mla_w_uv, w_br_fox, w_br_ret, w_br_mla, w_out, norm_ffn_w, dense_w_gate, dense_w_up, dense_w_down, moe_w_router, moe_w_gate, moe_w_up, moe_w_down, final_norm_w):
    B, S, D = x.shape
    assert D == D_MODEL and w_in.shape[0] == DEPTH == 2
    T = B * S
    tl = _tiles(S, T)
    row = lambda a: a.reshape(1, -1).astype(F32)
    ret_tabs = _ret_tables(S, tl["ret_c"])
    mla_tabs = _mla_tables(S)
    x2 = x.reshape(T, D)
    for l in range(DEPTH):
        w_all, w_ff = _build_inproj_weights(w_in[l])
        proj, ff32 = _inproj(x2, row(norm_mix_w[l]), w_all, w_ff, tl["inproj_tm"], tl["inproj_tn"])
        fqt, fkx, fvt = _fox_prep(proj, ff32, fox_f_bias[l], B, S, tl["attn_t"])
        o_fox = _attention(fqt, fkx, fvt, B, S, tl["attn_t"], FOX_RANGES, "fox_attn")
        o_ret = _retention(proj, row(ret_gn_w[l]), ret_tabs, B, S, tl["ret_c"])
        wq, wqr, wk, wv = _mla_weights(mla_w_uq[l], mla_w_uk[l], mla_w_uv[l])
        mqt, mkx, mvt = _mla_prep(proj, row(mla_q_norm_w[l]), row(mla_kv_norm_w[l]), wq, wqr, wk, wv,
                                  mla_tabs, B, S, tl["attn_t"])
        o_mla = _attention(mqt, mkx, mvt, B, S, tl["attn_t"], MLA_RANGES, "mla_attn")
        x_mid, h2 = _merge(o_fox, o_ret, o_mla, proj, x2,
                           w_br_fox[l].astype(BF16), w_br_ret[l].astype(BF16), w_br_mla[l].astype(BF16),
                           w_out[l].astype(BF16), row(norm_ffn_w[l]), tl["merge_tm"])
        if l % 2 == 0:
            i = l // 2
            x2 = _dense_ffn(h2, x_mid, dense_w_gate[i].astype(BF16), dense_w_up[i].astype(BF16),
                            dense_w_down[i].astype(BF16), tl["ffn_tm"], tl["ffn_tf"])
        else:
            i = l // 2
            x2 = _moe_layer(x_mid, row(norm_ffn_w[l]), moe_w_router[i], moe_w_gate[i],
                            moe_w_up[i], moe_w_down[i], row(final_norm_w), tl)
    return x2.reshape(B, S, D)
```

```python
import functools

import numpy as np
import jax
import jax.numpy as jnp
from jax import lax
from jax.experimental import pallas as pl
from jax.experimental.pallas import tpu as pltpu

F32 = jnp.float32
BF16 = jnp.bfloat16

D_MODEL = 1024
DEPTH = 2
FOX_HEADS = 8
FOX_HEAD_DIM = 64
FOX_W = FOX_HEADS * FOX_HEAD_DIM
RET_HEADS = 8
RET_QK_DIM = 64
RET_V_DIM = 128
RET_QK_W = RET_HEADS * RET_QK_DIM
RET_V_W = RET_HEADS * RET_V_DIM
MLA_HEADS = 8
MLA_Q_RANK = 384
MLA_KV_RANK = 256
MLA_NOPE_DIM = 64
MLA_ROPE_DIM = 32
MLA_V_DIM = 64
MLA_QK_DIM = MLA_NOPE_DIM + MLA_ROPE_DIM
MLA_V_W = MLA_HEADS * MLA_V_DIM
N_BRANCHES = 3
ROPE_THETA = 10000.0
RMS_EPS = 1e-6
D_FF = 3584
N_EXPERTS = 8
TOP_K = 2
IN_SPLITS = (FOX_W, FOX_W, FOX_W, FOX_HEADS,
             RET_QK_W, RET_QK_W, RET_V_W, RET_V_W,
             MLA_Q_RANK, MLA_KV_RANK, MLA_ROPE_DIM,
             N_BRANCHES * D_MODEL)

LANE = 128
NEG = -1e30
VMEM_LIMIT = 56 * 1024 * 1024

U_FQ, U_FK, U_FV = 0, 4, 8
U_RQ, U_RQR, U_RK, U_RKR = 12, 16, 20, 24
U_CKV, U_KR, U_KRR = 28, 30, 31
U_RV, U_RG, U_GL, U_CQ = 32, 40, 48, 72
N_PROJ = 75 * LANE
FOX_EXT_STRIDE = 8
PAIRS = 4
QK_W = 2 * LANE
ATTN_KEY_CHUNK = 256
ATTN_STRIP = 256
VT_ROWS = 80
LOG2E = 1.4426950408889634


def _cparams(sem, vmem=VMEM_LIMIT, flags=None):
    return pltpu.CompilerParams(dimension_semantics=sem, vmem_limit_bytes=vmem, flags=flags)


def _sigmoid(x):
    return 1.0 / (1.0 + jnp.exp(-x))


def _lane_mask(shape, ranges):
    lane = lax.broadcasted_iota(jnp.int32, shape, len(shape) - 1)
    m = None
    for a, b in ranges:
        r = (lane >= a) & (lane < b)
        m = r if m is None else (m | r)
    return m


def _inproj_kernel(x_ref, nw_ref, w_ref, wff_ref, out_ref, ff_ref, h_scr):
    @pl.when(pl.program_id(1) == 0)
    def _():
        x = x_ref[...]
        ms = jnp.mean(x * x, axis=-1, keepdims=True)
        h = (x * lax.rsqrt(ms + RMS_EPS) * nw_ref[...]).astype(BF16)
        h_scr[...] = h
        ff_ref[...] = jnp.dot(h, wff_ref[...], preferred_element_type=F32)

    out_ref[...] = jnp.dot(h_scr[...], w_ref[...], preferred_element_type=F32).astype(out_ref.dtype)


def _inproj(x2, nw, w_all, w_ff, tm, tn):
    T = x2.shape[0]
    return pl.pallas_call(
        _inproj_kernel,
        grid=(T // tm, N_PROJ // tn),
        in_specs=[
            pl.BlockSpec((tm, D_MODEL), lambda i, j: (i, 0)),
            pl.BlockSpec((1, D_MODEL), lambda i, j: (0, 0)),
            pl.BlockSpec((D_MODEL, tn), lambda i, j: (0, j)),
            pl.BlockSpec((D_MODEL, LANE), lambda i, j: (0, 0)),
        ],
        out_specs=[
            pl.BlockSpec((tm, tn), lambda i, j: (i, j)),
            pl.BlockSpec((tm, LANE), lambda i, j: (i, 0)),
        ],
        out_shape=[
            jax.ShapeDtypeStruct((T, N_PROJ), BF16),
            jax.ShapeDtypeStruct((T, LANE), F32),
        ],
        scratch_shapes=[pltpu.VMEM((tm, D_MODEL), BF16)],
        compiler_params=_cparams(("parallel", "arbitrary")),
        name="inproj",
    )(x2, nw, w_all, w_ff)


def _rot_cols(w, heads, d):
    perm = np.zeros((heads * d, heads * d), np.float32)
    half = d // 2
    for h in range(heads):
        for j in range(half):
            perm[h * d + half + j, h * d + j] = -1.0
            perm[h * d + j, h * d + half + j] = 1.0
    return jnp.dot(w, jnp.asarray(perm, w.dtype), preferred_element_type=F32).astype(w.dtype)


def _build_inproj_weights(w_in):
    split_at = [int(i) for i in np.cumsum(IN_SPLITS)[:-1]]
    fq, fk, fv, ff, rq, rk, rv, rg, cq, ckv, kr, gl = jnp.split(w_in.astype(BF16), split_at, axis=-1)
    z64 = jnp.zeros((D_MODEL, 64), BF16)
    z32 = jnp.zeros((D_MODEL, 32), BF16)
    kr128 = jnp.concatenate([z64, kr, z32], axis=-1)
    krr128 = jnp.concatenate([z64, _rot_cols(kr, 1, MLA_ROPE_DIM), z32], axis=-1)
    w_all = jnp.concatenate(
        [fq, fk, fv, rq, _rot_cols(rq, RET_HEADS, RET_QK_DIM), rk, _rot_cols(rk, RET_HEADS, RET_QK_DIM),
         ckv, kr128, krr128, rv, rg, gl, cq], axis=-1)
    w_ff = jnp.pad(ff, ((0, 0), (0, LANE - FOX_HEADS)))
    return w_all, w_ff


def _split3(x):
    hi = x.astype(BF16)
    r1 = x - hi.astype(F32)
    mid = r1.astype(BF16)
    lo = (r1 - mid.astype(F32)).astype(BF16)
    return hi, mid, lo


def _store_t(dst_ref, idx, x):
    dst_ref[idx] = x.T.astype(dst_ref.dtype)


def _store_vt_pair(vt_ref, p, v2):
    hd = LANE // 2
    vt = v2.T.astype(vt_ref.dtype)
    pad_rows = lax.broadcasted_iota(jnp.int32, (VT_ROWS - hd, vt.shape[1]), 0)
    ones_then_zeros = jnp.where(pad_rows == 0, 1.0, 0.0).astype(vt_ref.dtype)
    for j in range(2):
        base = (2 * p + j) * VT_ROWS
        vt_ref[0, 0, base:base + hd, :] = vt[j * hd:(j + 1) * hd]
        vt_ref[0, 0, base + hd:base + VT_ROWS, :] = ones_then_zeros


def _fox_prep_kernel(fq_ref, fk_ref, fv_ref, ff_ref, bias_ref, tri_ref, pq_ref, pk_ref, cq_ref, ck_ref,
                     qt_ref, kx_ref, vt_ref, carry_scr):
    @pl.when(pl.program_id(1) == 0)
    def _():
        carry_scr[...] = jnp.zeros_like(carry_scr)

    ts = ff_ref.shape[0]
    z = ff_ref[...] + bias_ref[...]
    logf = jnp.minimum(z, 0.0) - jnp.log(1.0 + jnp.exp(-jnp.abs(z)))
    tri = tri_ref[...]
    cum = carry_scr[0:1, :]
    for part in _split3(logf):
        cum = cum + jnp.dot(tri, part, preferred_element_type=F32)
    carry_scr[0:1, :] = cum[ts - 1:ts, :]

    eq = cq_ref[...]
    ek = ck_ref[...]
    for j, part in enumerate(_split3(cum * LOG2E)):
        eq = eq + jnp.dot(part, pq_ref[j * LANE:(j + 1) * LANE, :], preferred_element_type=F32)
        ek = ek + jnp.dot(part, pk_ref[j * LANE:(j + 1) * LANE, :], preferred_element_type=F32)
    scale = FOX_HEAD_DIM ** -0.5 * LOG2E
    for p in range(FOX_HEADS // 2):
        src = slice(p * LANE, (p + 1) * LANE)
        _store_t(qt_ref, (0, slice(2 * p * LANE, (2 * p + 1) * LANE)), fq_ref[:, src].astype(F32) * scale)
        _store_t(qt_ref, (0, slice((2 * p + 1) * LANE, (2 * p + 2) * LANE)), eq[:, src])
        kx_ref[:, 2 * p * LANE:(2 * p + 1) * LANE] = fk_ref[:, src]
        kx_ref[:, (2 * p + 1) * LANE:(2 * p + 2) * LANE] = ek[:, src].astype(BF16)
        _store_vt_pair(vt_ref, p, fv_ref[:, src].astype(F32))


def _fox_prep_consts(ts):
    tri = np.tril(np.ones((ts, ts), np.float32))
    pq = np.zeros((3 * LANE, FOX_W), np.float32)
    pk = np.zeros((3 * LANE, FOX_W), np.float32)
    cq = np.zeros((1, FOX_W), np.float32)
    ck = np.zeros((1, FOX_W), np.float32)
    for h in range(FOX_HEADS):
        base = LANE * (h // 2) + FOX_EXT_STRIDE * (h % 2)
        for j in range(3):
            pq[j * LANE + h, base + j] = 1.0
            ck[0, base + j] = 1.0
            cq[0, base + 3 + j] = 1.0
            pk[j * LANE + h, base + 3 + j] = -1.0
    return (jnp.asarray(tri, BF16), jnp.asarray(pq, BF16), jnp.asarray(pk, BF16),
            jnp.asarray(cq), jnp.asarray(ck))


def _attn_operand_specs(B, S, t):
    T = B * S
    ns = S // t
    specs = [
        pl.BlockSpec((1, PAIRS * QK_W, t), lambda b, s: (b, 0, s)),
        pl.BlockSpec((t, PAIRS * QK_W), lambda b, s: (b * ns + s, 0)),
        pl.BlockSpec((1, 1, 2 * PAIRS * VT_ROWS, t), lambda b, s: (b, s, 0, 0)),
    ]
    shapes = [
        jax.ShapeDtypeStruct((B, PAIRS * QK_W, S), BF16),
        jax.ShapeDtypeStruct((T, PAIRS * QK_W), BF16),
        jax.ShapeDtypeStruct((B, ns, 2 * PAIRS * VT_ROWS, t), BF16),
    ]
    return specs, shapes


def _fox_prep(proj, ff32, bias, B, S, ts):
    ns = S // ts
    tri, pq, pk, cq, ck = _fox_prep_consts(ts)
    bias128 = jnp.pad(bias.astype(F32), (0, LANE - FOX_HEADS)).reshape(1, LANE)
    const = lambda shape: pl.BlockSpec(shape, lambda b, s: (0, 0))
    out_specs, out_shape = _attn_operand_specs(B, S, ts)
    return pl.pallas_call(
        _fox_prep_kernel,
        grid=(B, ns),
        in_specs=[
            pl.BlockSpec((ts, FOX_W), lambda b, s: (b * ns + s, U_FQ // 4)),
            pl.BlockSpec((ts, FOX_W), lambda b, s: (b * ns + s, U_FK // 4)),
            pl.BlockSpec((ts, FOX_W), lambda b, s: (b * ns + s, U_FV // 4)),
            pl.BlockSpec((ts, LANE), lambda b, s: (b * ns + s, 0)),
            const((1, LANE)), const((ts, ts)), const((3 * LANE, FOX_W)), const((3 * LANE, FOX_W)),
            const((1, FOX_W)), const((1, FOX_W)),
        ],
        out_specs=out_specs,
        out_shape=out_shape,
        scratch_shapes=[pltpu.VMEM((8, LANE), F32)],
        compiler_params=_cparams(("parallel", "arbitrary")),
        name="fox_prep",
    )(proj, proj, proj, ff32, bias128, tri, pq, pk, cq, ck)


def _row_mask(shape, ranges):
    row = lax.broadcasted_iota(jnp.int32, shape, 0)
    m = None
    for a, b in ranges:
        r = (row >= a) & (row < b)
        m = r if m is None else (m | r)
    return m


def _col_reduce(x, op, reduce_fn, parts=2):
    c = x.shape[0] // parts
    pieces = [x[i * c:(i + 1) * c] for i in range(parts)]
    while len(pieces) > 1:
        pieces = [op(pieces[i], pieces[i + 1]) for i in range(0, len(pieces), 2)]
    return reduce_fn(pieces[0], axis=0, keepdims=True)


def _attn_kernel(qt_ref, qn_ref, k_ref, vt_ref, o_ref, qh_scr, qn_scr, sa_scr, sb_scr, ma_scr, mb_scr, acc_scr,
                 *, tk, ranges):
    qi = pl.program_id(2)
    qt = qt_ref[0]
    tq = qt.shape[-1]
    for h in range(2):
        qh_scr[h] = jnp.where(_row_mask(qt.shape, ranges[h]), qt, jnp.zeros_like(qt))
    acc_scr[...] = jnp.zeros_like(acc_scr)

    def key_tile(kt):
        return k_ref[pl.ds(pl.multiple_of(kt * tk, tk), tk), :]

    strips = [(c, c + ATTN_STRIP) for c in range(0, tq, ATTN_STRIP)]

    def scores(kt, s_scr, m_scr):
        k = key_tile(kt)
        for h in range(2):
            for a, b in strips:
                s = jnp.dot(k, qh_scr[h, :, a:b], preferred_element_type=F32)
                s_scr[h, :, a:b] = s
                m_scr[h, :, a:b] = _col_reduce(s, jnp.maximum, jnp.max)

    def update(kt, h, s, m_tile, m_prev, lo):
        m_new = jnp.maximum(m_prev, m_tile)
        alpha = jnp.exp2(m_prev - m_new)
        hi = lo + s.shape[-1]
        pv = None
        for r in range(0, tk, ATTN_KEY_CHUNK):
            p = jnp.exp2((s[r:r + ATTN_KEY_CHUNK] - m_new).astype(BF16))
            vt = vt_ref[0, kt, h * VT_ROWS:(h + 1) * VT_ROWS, r:r + ATTN_KEY_CHUNK]
            d = jnp.dot(vt, p, preferred_element_type=F32)
            pv = d if pv is None else pv + d
        acc_scr[h, :, lo:hi] = alpha * acc_scr[h, :, lo:hi] + pv
        return m_new

    def diag_update(kt, h, s_scr, m_prev, a, b, lo):
        s = s_scr[h, :, a:b]
        if a < lo + tk:
            key = lax.broadcasted_iota(jnp.int32, s.shape, 0)
            qry = lax.broadcasted_iota(jnp.int32, s.shape, 1) + (a - lo)
            s = jnp.where(key <= qry, s, NEG)
        return update(kt, h, s, _col_reduce(s, jnp.maximum, jnp.max), m_prev, a)

    def diagonal_tiles(carry, lookahead):
        k_last = key_tile(2 * qi + 1)
        if lookahead:
            qn = qn_ref[0]
            k0 = key_tile(0)
            for h in range(2):
                qn_scr[h] = jnp.where(_row_mask(qn.shape, ranges[h]), qn, jnp.zeros_like(qn))
        out = []
        for h in range(2):
            ms = []
            for a, b in strips:
                if a >= tk:
                    sb_scr[h, :, a:b] = jnp.dot(k_last, qh_scr[h, :, a:b], preferred_element_type=F32)
                ms.append(diag_update(2 * qi, h, sa_scr, carry[h][:, a:b], a, b, 0))
                if lookahead:
                    s = jnp.dot(k0, qn_scr[h, :, a:b], preferred_element_type=F32)
                    sa_scr[h, :, a:b] = s
                    ma_scr[h, :, a:b] = _col_reduce(s, jnp.maximum, jnp.max)
            out.append(ms)
        for h in range(2):
            for i, (a, b) in enumerate(strips):
                if a >= tk:
                    diag_update(2 * qi + 1, h, sb_scr, out[h][i], a, b, tk)
        hd = LANE // 2
        ot = jnp.concatenate([acc_scr[h, :hd] / acc_scr[h, hd:hd + 1] for h in range(2)], axis=0)
        o_ref[...] = ot.T.astype(o_ref.dtype)

    def fused_step(kt, cur_s, cur_m, nxt_s, nxt_m, carry):
        k = key_tile(kt + 1)
        ms = [[], []]
        for a, b in strips:
            for h in range(2):
                s = jnp.dot(k, qh_scr[h, :, a:b], preferred_element_type=F32)
                nxt_s[h, :, a:b] = s
                nxt_m[h, :, a:b] = _col_reduce(s, jnp.maximum, jnp.max)
                ms[h].append(update(kt, h, cur_s[h, :, a:b], cur_m[h, :, a:b], carry[h][:, a:b], a))
        return tuple(jnp.concatenate(m, axis=-1) for m in ms)

    def body(j, carry):
        kt = 2 * j
        carry = fused_step(kt, sa_scr, ma_scr, sb_scr, mb_scr, carry)
        return fused_step(kt + 1, sb_scr, mb_scr, sa_scr, ma_scr, carry)

    init = tuple(jnp.full((1, tq), NEG, F32) for _ in range(2))

    @pl.when(qi == 0)
    def _():
        scores(0, sa_scr, ma_scr)

    carry = lax.fori_loop(0, qi, body, init)
    last = pl.num_programs(2) - 1

    @pl.when(qi < last)
    def _():
        diagonal_tiles(carry, True)

    @pl.when(qi == last)
    def _():
        diagonal_tiles(carry, False)


def _attention(qt, kx, vt, B, S, tk, ranges, name):
    T = B * S
    tq = 2 * tk
    nq = S // tq
    kern = functools.partial(_attn_kernel, tk=tk, ranges=ranges)
    return pl.pallas_call(
        kern,
        grid=(B, PAIRS, nq),
        in_specs=[
            pl.BlockSpec((1, QK_W, tq), lambda b, p, i: (b, p, i)),
            pl.BlockSpec((1, QK_W, tq), lambda b, p, i: (b, p, jnp.minimum(i + 1, nq - 1))),
            pl.BlockSpec((S, QK_W), lambda b, p, i: (b, p)),
            pl.BlockSpec((1, S // tk, 2 * VT_ROWS, tk), lambda b, p, i: (b, 0, p, 0)),
        ],
        out_specs=pl.BlockSpec((tq, LANE), lambda b, p, i: (b * nq + i, p)),
        out_shape=jax.ShapeDtypeStruct((T, PAIRS * LANE), BF16),
        scratch_shapes=[
            pltpu.VMEM((2, QK_W, tq), BF16),
            pltpu.VMEM((2, QK_W, tq), BF16),
            pltpu.VMEM((2, tk, tq), F32),
            pltpu.VMEM((2, tk, tq), F32),
            pltpu.VMEM((2, 1, tq), F32),
            pltpu.VMEM((2, 1, tq), F32),
            pltpu.VMEM((2, VT_ROWS, tq), F32),
        ],
        compiler_params=_cparams(("parallel", "parallel", "arbitrary")),
        name=name,
    )(qt, qt, kx, vt)


_E0 = LANE
FOX_RANGES = (((0, 64), (_E0, _E0 + FOX_EXT_STRIDE)),
              ((64, 128), (_E0 + FOX_EXT_STRIDE, _E0 + 2 * FOX_EXT_STRIDE)))
MLA_RANGES = (((0, LANE),), ((LANE, 2 * LANE),))


def _mla_prep_kernel(cq_ref, ckv_ref, kr_ref, krr_ref, qnw_ref, kvnw_ref, wq_ref, wqr_ref, wk_ref, wv_ref,
                     cq_tab, sq_tab, ck_tab, sk_tab, qt_ref, k_ref, vt_ref):
    def norm(x_ref, w_ref):
        x = x_ref[...].astype(F32)
        ms = jnp.mean(x * x, axis=-1, keepdims=True)
        return (x * lax.rsqrt(ms + RMS_EPS) * w_ref[...]).astype(BF16)

    c_q = norm(cq_ref, qnw_ref)
    c_kv = norm(ckv_ref, kvnw_ref)
    tile8 = lambda a: jnp.concatenate([a] * MLA_HEADS, axis=-1)
    q = (jnp.dot(c_q, wq_ref[...], preferred_element_type=F32) * tile8(cq_tab[...])
         + jnp.dot(c_q, wqr_ref[...], preferred_element_type=F32) * tile8(sq_tab[...]))
    for c in range(MLA_HEADS):
        _store_t(qt_ref, (0, slice(c * LANE, (c + 1) * LANE)), q[:, c * LANE:(c + 1) * LANE])
    k_rope = kr_ref[...].astype(F32) * ck_tab[...] + krr_ref[...].astype(F32) * sk_tab[...]
    k = jnp.dot(c_kv, wk_ref[...], preferred_element_type=F32) + tile8(k_rope)
    k_ref[...] = k.astype(BF16)
    v = jnp.dot(c_kv, wv_ref[...], preferred_element_type=F32)
    for p in range(PAIRS):
        _store_vt_pair(vt_ref, p, v[:, p * LANE:(p + 1) * LANE])


def _mla_weights(w_uq, w_uk, w_uv):
    q3 = w_uq.reshape(MLA_Q_RANK, MLA_HEADS, MLA_QK_DIM)
    nope, ropep = q3[..., :MLA_NOPE_DIM], q3[..., MLA_NOPE_DIM:]
    half = MLA_ROPE_DIM // 2
    rot = jnp.concatenate([-ropep[..., half:], ropep[..., :half]], axis=-1)
    z32 = jnp.zeros((MLA_Q_RANK, MLA_HEADS, LANE - MLA_QK_DIM), F32)
    z64 = jnp.zeros((MLA_Q_RANK, MLA_HEADS, MLA_NOPE_DIM), F32)
    wq = jnp.concatenate([nope, ropep, z32], axis=-1).reshape(MLA_Q_RANK, MLA_HEADS * LANE)
    wqr = jnp.concatenate([z64, rot, z32], axis=-1).reshape(MLA_Q_RANK, MLA_HEADS * LANE)
    k3 = w_uk.reshape(MLA_KV_RANK, MLA_HEADS, MLA_NOPE_DIM)
    wk = jnp.concatenate([k3, jnp.zeros_like(k3)], axis=-1).reshape(MLA_KV_RANK, MLA_HEADS * LANE)
    return wq.astype(BF16), wqr.astype(BF16), wk.astype(BF16), w_uv.astype(BF16)


def _mla_tables(S):
    pos = jnp.arange(S, dtype=F32)
    inv = ROPE_THETA ** (-jnp.arange(0, MLA_ROPE_DIM, 2, dtype=F32) / MLA_ROPE_DIM)
    ang = pos[:, None] * inv[None, :]
    cos, sin = jnp.cos(ang), jnp.sin(ang)
    ones = jnp.ones((S, MLA_NOPE_DIM), F32)
    z64 = jnp.zeros((S, MLA_NOPE_DIM), F32)
    z32 = jnp.zeros((S, LANE - MLA_QK_DIM), F32)
    ctab = jnp.concatenate([ones, cos, cos, z32], axis=-1)
    stab = jnp.concatenate([z64, sin, sin, z32], axis=-1)
    scale = MLA_QK_DIM ** -0.5 * LOG2E
    return ctab * scale, stab * scale, ctab, stab


def _mla_prep(proj, qnw, kvnw, wq, wqr, wk, wv, tabs, B, S, tm):
    ns = S // tm
    HW = MLA_HEADS * LANE
    const = lambda shape: pl.BlockSpec(shape, lambda b, s: (0, 0))
    tab = pl.BlockSpec((tm, LANE), lambda b, s: (s, 0))
    out_specs, out_shape = _attn_operand_specs(B, S, tm)
    return pl.pallas_call(
        _mla_prep_kernel,
        grid=(B, ns),
        in_specs=[
            pl.BlockSpec((tm, MLA_Q_RANK), lambda b, s: (b * ns + s, U_CQ // 3)),
            pl.BlockSpec((tm, MLA_KV_RANK), lambda b, s: (b * ns + s, U_CKV // 2)),
            pl.BlockSpec((tm, LANE), lambda b, s: (b * ns + s, U_KR)),
            pl.BlockSpec((tm, LANE), lambda b, s: (b * ns + s, U_KRR)),
            const((1, MLA_Q_RANK)), const((1, MLA_KV_RANK)),
            const((MLA_Q_RANK, HW)), const((MLA_Q_RANK, HW)), const((MLA_KV_RANK, HW)),
            const((MLA_KV_RANK, MLA_V_W)),
            tab, tab, tab, tab,
        ],
        out_specs=out_specs,
        out_shape=out_shape,
        compiler_params=_cparams(("parallel", "parallel")),
        name="mla_prep",
    )(proj, proj, proj, proj, qnw, kvnw, wq, wqr, wk, wv, *tabs)


def _ret_kernel(rq_ref, rqr_ref, rk_ref, rkr_ref, rv_ref, rg_ref, cos_ref, sin_ref,
                din_ref, xi_ref, zeta_ref, gch_ref, gnw_ref, o_ref, r_scr):
    @pl.when(pl.program_id(0) == 0)
    def _():
        r_scr[...] = jnp.zeros_like(r_scr)

    tile4 = lambda a: jnp.concatenate([a] * (RET_QK_W // LANE), axis=-1)
    cos4 = tile4(cos_ref[...])
    sin4 = tile4(sin_ref[...])
    for bi in range(rq_ref.shape[0]):
        q = rq_ref[bi].astype(F32) * cos4 + rqr_ref[bi].astype(F32) * sin4
        k = (rk_ref[bi].astype(F32) * cos4 + rkr_ref[bi].astype(F32) * sin4) * (RET_QK_DIM ** -0.5)
        for p in range(RET_HEADS // 2):
            q2 = q[:, p * LANE:(p + 1) * LANE]
            k2 = k[:, p * LANE:(p + 1) * LANE]
            k2b = k2.astype(BF16)
            kz2 = (k2 * zeta_ref[p]).astype(BF16)
            r2 = r_scr[bi, p]
            r2b = r2.astype(BF16)
            new_r = gch_ref[p] * r2
            for j in range(2):
                h = 2 * p + j
                half = ((j * RET_QK_DIM, (j + 1) * RET_QK_DIM),)
                qm = jnp.where(_lane_mask(q2.shape, half), q2, 0.0).astype(BF16)
                inner = lax.dot_general(qm, k2b, (((1,), (1,)), ((), ())), preferred_element_type=F32)
                inner = inner * din_ref[h]
                vh = rv_ref[bi, :, h * LANE:(h + 1) * LANE]
                o = (jnp.dot(inner.astype(BF16), vh, preferred_element_type=F32)
                     + jnp.dot(qm, r2b, preferred_element_type=F32) * xi_ref[h])
                upd = lax.dot_general(kz2, vh, (((0,), (0,)), ((), ())), preferred_element_type=F32)
                rows = lax.broadcasted_iota(jnp.int32, upd.shape, 0)
                new_r = new_r + jnp.where((rows >= half[0][0]) & (rows < half[0][1]), upd, 0.0)
                mu = jnp.mean(o, axis=-1, keepdims=True)
                d = o - mu
                var = jnp.mean(d * d, axis=-1, keepdims=True)
                on = d * lax.rsqrt(var + RMS_EPS) * gnw_ref[:, h * LANE:(h + 1) * LANE]
                g = rg_ref[bi, :, h * LANE:(h + 1) * LANE].astype(F32)
                o_ref[bi, :, h * LANE:(h + 1) * LANE] = (g * _sigmoid(g) * on).astype(o_ref.dtype)
            r_scr[bi, p] = new_r


def _ret_tables(S, C):
    pos = jnp.arange(S, dtype=F32)
    inv = ROPE_THETA ** (-jnp.arange(0, RET_QK_DIM, 2, dtype=F32) / RET_QK_DIM)
    ang = pos[:, None] * inv[None, :]
    cos = jnp.concatenate([jnp.cos(ang)] * 4, axis=-1)
    sin = jnp.concatenate([jnp.sin(ang)] * 4, axis=-1)
    gammas = 1.0 - 2.0 ** (-5.0 - jnp.arange(RET_HEADS, dtype=F32))
    log_g = jnp.log(gammas)
    j = jnp.arange(C, dtype=F32)
    diff = j[:, None] - j[None, :]
    din = jnp.where(diff[None] >= 0, jnp.exp(jnp.maximum(diff, 0.0)[None] * log_g[:, None, None]), 0.0)
    xi = jnp.exp((j[None, :] + 1.0) * log_g[:, None])
    zeta = jnp.exp((C - 1.0 - j[None, :]) * log_g[:, None])
    gch = jnp.exp(C * log_g)
    xi_t = jnp.broadcast_to(xi[:, :, None], (RET_HEADS, C, LANE))
    zeta_p = jnp.repeat(zeta.reshape(RET_HEADS // 2, 2, C).transpose(0, 2, 1), RET_QK_DIM, axis=-1)
    gch_p = jnp.broadcast_to(jnp.repeat(gch.reshape(RET_HEADS // 2, 2), RET_QK_DIM, axis=-1)[:, :, None],
                             (RET_HEADS // 2, LANE, LANE))
    return cos, sin, din, xi_t, zeta_p, gch_p


def _retention(proj, gnw, tabs, B, S, C):
    nc = S // C
    cos, sin, din, xi, zeta, gch = tabs
    proj3 = proj.reshape(B, S, N_PROJ)
    blk = lambda w, unit: pl.BlockSpec((B, C, w), lambda c: (0, c, unit * LANE // w))
    full = lambda a: pl.BlockSpec(a.shape, lambda c: (0,) * a.ndim)
    out = pl.pallas_call(
        _ret_kernel,
        grid=(nc,),
        in_specs=[
            blk(RET_QK_W, U_RQ), blk(RET_QK_W, U_RQR), blk(RET_QK_W, U_RK), blk(RET_QK_W, U_RKR),
            blk(RET_V_W, U_RV), blk(RET_V_W, U_RG),
            pl.BlockSpec((C, LANE), lambda c: (c, 0)),
            pl.BlockSpec((C, LANE), lambda c: (c, 0)),
            full(din), full(xi), full(zeta), full(gch), full(gnw),
        ],
        out_specs=pl.BlockSpec((B, C, RET_V_W), lambda c: (0, c, 0)),
        out_shape=jax.ShapeDtypeStruct((B, S, RET_V_W), BF16),
        scratch_shapes=[pltpu.VMEM((B, RET_HEADS // 2, LANE, LANE), F32)],
        compiler_params=_cparams(("arbitrary",)),
        name="retention",
    )(proj3, proj3, proj3, proj3, proj3, proj3, cos, sin, din, xi, zeta, gch, gnw)
    return out.reshape(B * S, RET_V_W)


def _merge_kernel(of_ref, or_ref, om_ref, g0_ref, g1_ref, g2_ref, x_ref,
                  wf_ref, wr_ref, wm_ref, wo_ref, nw_ref, xo_ref, h_ref):
    merged = (_sigmoid(g0_ref[...].astype(F32)) * jnp.dot(of_ref[...], wf_ref[...], preferred_element_type=F32)
              + _sigmoid(g1_ref[...].astype(F32)) * jnp.dot(or_ref[...], wr_ref[...], preferred_element_type=F32)
              + _sigmoid(g2_ref[...].astype(F32)) * jnp.dot(om_ref[...], wm_ref[...], preferred_element_type=F32))
    x = x_ref[...] + jnp.dot(merged.astype(BF16), wo_ref[...], preferred_element_type=F32)
    xo_ref[...] = x
    ms = jnp.mean(x * x, axis=-1, keepdims=True)
    h_ref[...] = (x * lax.rsqrt(ms + RMS_EPS) * nw_ref[...]).astype(h_ref.dtype)


def _merge(o_fox, o_ret, o_mla, proj, x2, wf, wr, wm, wo, nw, tm):
    T = x2.shape[0]
    row = lambda w, cb=0: pl.BlockSpec((tm, w), lambda i: (i, cb))
    const = lambda a: pl.BlockSpec(a.shape, lambda i: (0, 0))
    g_unit = U_GL // 8
    return pl.pallas_call(
        _merge_kernel,
        grid=(T // tm,),
        in_specs=[
            row(FOX_W), row(RET_V_W), row(MLA_V_W),
            row(D_MODEL, g_unit), row(D_MODEL, g_unit + 1), row(D_MODEL, g_unit + 2),
            row(D_MODEL),
            const(wf), const(wr), const(wm), const(wo), const(nw),
        ],
        out_specs=[row(D_MODEL), row(D_MODEL)],
        out_shape=[jax.ShapeDtypeStruct((T, D_MODEL), F32), jax.ShapeDtypeStruct((T, D_MODEL), BF16)],
        compiler_params=_cparams(("parallel",)),
        name="merge",
    )(o_fox, o_ret, o_mla, proj, proj, proj, x2, wf, wr, wm, wo, nw)


def _swiglu_tile(h, wg, wu, wd):
    a = jnp.dot(h, wg, preferred_element_type=F32)
    b = jnp.dot(h, wu, preferred_element_type=F32)
    return jnp.dot((a * _sigmoid(a) * b).astype(BF16), wd, preferred_element_type=F32)


def _dense_ffn_kernel(h_ref, wg_ref, wu_ref, wd_ref, x_ref, o_ref, acc_scr):
    f = pl.program_id(1)

    @pl.when(f == 0)
    def _():
        acc_scr[...] = jnp.zeros_like(acc_scr)

    acc_scr[...] += _swiglu_tile(h_ref[...], wg_ref[...], wu_ref[...], wd_ref[...])

    @pl.when(f == pl.num_programs(1) - 1)
    def _():
        o_ref[...] = x_ref[...] + acc_scr[...]


def _dense_ffn(h, x2, wg, wu, wd, tm, tf):
    T = x2.shape[0]
    return pl.pallas_call(
        _dense_ffn_kernel,
        grid=(T // tm, D_FF // tf),
        in_specs=[
            pl.BlockSpec((tm, D_MODEL), lambda i, f: (i, 0)),
            pl.BlockSpec((D_MODEL, tf), lambda i, f: (0, f)),
            pl.BlockSpec((D_MODEL, tf), lambda i, f: (0, f)),
            pl.BlockSpec((tf, D_MODEL), lambda i, f: (f, 0)),
            pl.BlockSpec((tm, D_MODEL), lambda i, f: (i, 0)),
        ],
        out_specs=pl.BlockSpec((tm, D_MODEL), lambda i, f: (i, 0)),
        out_shape=jax.ShapeDtypeStruct((T, D_MODEL), F32),
        scratch_shapes=[pltpu.VMEM((tm, D_MODEL), F32)],
        compiler_params=_cparams(("parallel", "arbitrary")),
        name="dense_ffn",
    )(h, wg, wu, wd, x2)


M_E0, M_E1, M_P0, M_P1, M_W0, M_W1 = 0, 1, 2, 3, 4, 5


def _split2(x):
    hi = x.astype(BF16)
    lo = (x - hi.astype(F32)).astype(BF16)
    return hi, lo


HALF_D = D_MODEL // 2


def _pack_rows(h):
    bits = pltpu.bitcast(h.astype(BF16).astype(F32), jnp.uint32)
    return (bits[:, :HALF_D] & jnp.uint32(0xFFFF0000)) | (bits[:, HALF_D:] >> 16)


def _unpack_rows(u):
    left = pltpu.bitcast(u & jnp.uint32(0xFFFF0000), F32)
    right = pltpu.bitcast(u << 16, F32)
    return jnp.concatenate([left, right], axis=-1).astype(BF16)


def _router_kernel(x_ref, nw_ref, whi_ref, wlo_ref, tri_ref, h_ref, meta_ref, cnt_ref, carry_scr):
    @pl.when(pl.program_id(0) == 0)
    def _():
        carry_scr[...] = jnp.zeros_like(carry_scr)

    x = x_ref[...]
    ms = jnp.mean(x * x, axis=-1, keepdims=True)
    h = x * lax.rsqrt(ms + RMS_EPS) * nw_ref[...]
    h_ref[...] = _pack_rows(h)
    hhi, hlo = _split2(h)
    logits = (jnp.dot(hhi, whi_ref[...], preferred_element_type=F32)
              + jnp.dot(hhi, wlo_ref[...], preferred_element_type=F32)
              + jnp.dot(hlo, whi_ref[...], preferred_element_type=F32))
    lane = lax.broadcasted_iota(jnp.int32, logits.shape, 1)
    lanef = lane.astype(F32)
    logits = jnp.where(lane < N_EXPERTS, logits, NEG)
    m0 = jnp.max(logits, axis=-1, keepdims=True)
    e0 = jnp.min(jnp.where(logits == m0, lanef, float(LANE)), axis=-1, keepdims=True)
    oh0 = lanef == e0
    rest = jnp.where(oh0, NEG, logits)
    m1 = jnp.max(rest, axis=-1, keepdims=True)
    e1 = jnp.min(jnp.where(rest == m1, lanef, float(LANE)), axis=-1, keepdims=True)
    oh1 = lanef == e1
    z = jnp.exp(m1 - m0)
    w0 = 1.0 / (1.0 + z)
    w1 = z / (1.0 + z)
    both = jnp.where(oh0 | oh1, 1.0, 0.0)
    before = carry_scr[0:1, :] + jnp.dot(tri_ref[...], both.astype(BF16), preferred_element_type=F32)
    p0 = jnp.sum(jnp.where(oh0, before, 0.0), axis=-1, keepdims=True)
    p1 = jnp.sum(jnp.where(oh1, before, 0.0), axis=-1, keepdims=True)
    total = carry_scr[0:1, :] + jnp.sum(both, axis=0, keepdims=True)
    carry_scr[0:1, :] = total
    cnt_ref[...] = jnp.broadcast_to(total, cnt_ref.shape)
    meta = jnp.zeros(logits.shape, F32)
    for idx, val in ((M_E0, e0), (M_E1, e1), (M_P0, p0), (M_P1, p1), (M_W0, w0), (M_W1, w1)):
        meta = jnp.where(lane == idx, val, meta)
    meta_ref[...] = meta


def _router(x2, nw, w_router, tr):
    T = x2.shape[0]
    wr = jnp.pad(w_router.astype(F32), ((0, 0), (0, LANE - N_EXPERTS)))
    whi = wr.astype(BF16)
    wlo = (wr - whi.astype(F32)).astype(BF16)
    tri = jnp.asarray(np.tril(np.ones((tr, tr), np.float32), -1), BF16)
    const = lambda a: pl.BlockSpec(a.shape, lambda i: (0, 0))
    return pl.pallas_call(
        _router_kernel,
        grid=(T // tr,),
        in_specs=[pl.BlockSpec((tr, D_MODEL), lambda i: (i, 0)), const(nw), const(whi), const(wlo), const(tri)],
        out_specs=[
            pl.BlockSpec((tr, HALF_D), lambda i: (i, 0)),
            pl.BlockSpec((tr, LANE), lambda i: (i, 0)),
            pl.BlockSpec((8, LANE), lambda i: (0, 0)),
        ],
        out_shape=[
            jax.ShapeDtypeStruct((T, HALF_D), jnp.uint32),
            jax.ShapeDtypeStruct((T, LANE), F32),
            jax.ShapeDtypeStruct((8, LANE), F32),
        ],
        scratch_shapes=[pltpu.VMEM((8, LANE), F32)],
        compiler_params=_cparams(("arbitrary",)),
        name="router",
    )(x2, nw, whi, wlo, tri)


def _scatter_kernel(dest_ref, h_ref, xs_in, xs_hbm, sem, *, tt):
    del xs_in

    def row_copy(t, d_row):
        return pltpu.make_async_copy(h_ref.at[pl.ds(t, 1)], xs_hbm.at[pl.ds(d_row, 1)], sem)

    for t in range(tt):
        for j in range(TOP_K):
            row_copy(t, dest_ref[0, 0, TOP_K * t + j]).start(priority=j)

    for _ in range(TOP_K):
        pltpu.make_async_copy(h_ref, xs_hbm.at[pl.ds(0, tt)], sem).wait()


def _scatter_rows(dest, h, n_rows, tt):
    T = h.shape[0]
    width = h.shape[1]
    xs0 = jnp.zeros((n_rows, width), h.dtype)
    dest3 = dest.reshape(T // tt, 1, 2 * tt)
    return pl.pallas_call(
        functools.partial(_scatter_kernel, tt=tt),
        grid=(T // tt,),
        in_specs=[
            pl.BlockSpec((1, 1, 2 * tt), lambda i: (i, 0, 0), memory_space=pltpu.SMEM),
            pl.BlockSpec((tt, width), lambda i: (i, 0)),
            pl.BlockSpec(memory_space=pl.ANY),
        ],
        out_specs=pl.BlockSpec(memory_space=pl.ANY),
        out_shape=jax.ShapeDtypeStruct((n_rows, width), h.dtype),
        scratch_shapes=[pltpu.SemaphoreType.DMA(())],
        input_output_aliases={2: 0},
        compiler_params=_cparams(("arbitrary",)),
        name="moe_scatter",
    )(dest3, h, xs0)


def _moe_ffn_kernel(te_ref, na_ref, xs_ref, wg_ref, wu_ref, wd_ref, ys_ref, acc_scr):
    i = pl.program_id(0)
    f = pl.program_id(1)

    @pl.when(i < na_ref[0])
    def _():
        @pl.when(f == 0)
        def _():
            acc_scr[...] = jnp.zeros_like(acc_scr)

        acc_scr[...] += _swiglu_tile(_unpack_rows(xs_ref[...]), wg_ref[0].astype(BF16),
                                     wu_ref[0].astype(BF16), wd_ref[0].astype(BF16))

        @pl.when(f == pl.num_programs(1) - 1)
        def _():
            ys_ref[...] = acc_scr[...]

    @pl.when((i >= na_ref[0]) & (f == 0))
    def _():
        ys_ref[...] = jnp.zeros_like(ys_ref)


def _moe_ffn(tile_expert, n_active, xs, wg, wu, wd, tm, tf):
    n_rows = xs.shape[0]
    n_tiles = n_rows // tm
    nf = D_FF // tf

    def row_map(i, f, te, na):
        return (jnp.minimum(i, na[0] - 1), 0)

    def fcol(i, f, na):
        return jnp.where(i < na[0], f, nf - 1)

    grid_spec = pltpu.PrefetchScalarGridSpec(
        num_scalar_prefetch=2,
        grid=(n_tiles, nf),
        in_specs=[
            pl.BlockSpec((tm, HALF_D), row_map),
            pl.BlockSpec((1, D_MODEL, tf), lambda i, f, te, na: (te[i], 0, fcol(i, f, na))),
            pl.BlockSpec((1, D_MODEL, tf), lambda i, f, te, na: (te[i], 0, fcol(i, f, na))),
            pl.BlockSpec((1, tf, D_MODEL), lambda i, f, te, na: (te[i], fcol(i, f, na), 0)),
        ],
        out_specs=pl.BlockSpec((tm, D_MODEL), lambda i, f, te, na: (i, 0)),
        scratch_shapes=[pltpu.VMEM((tm, D_MODEL), F32)],
    )
    return pl.pallas_call(
        _moe_ffn_kernel,
        grid_spec=grid_spec,
        out_shape=jax.ShapeDtypeStruct((n_rows, D_MODEL), F32),
        compiler_params=_cparams(("arbitrary", "arbitrary")),
        name="moe_ffn",
    )(tile_expert, n_active, xs, wg, wu, wd)


def _combine_kernel(dest_ref, dnext_ref, x_ref, meta_ref, fw_ref, ys_hbm, o_ref, g_scr, sems, *, tc):
    i = pl.program_id(0)

    def gather(d_ref, half, s):
        for t in range(tc):
            for j in range(TOP_K):
                row = d_ref[0, 0, TOP_K * (half * tc + t) + j]
                pltpu.make_async_copy(ys_hbm.at[pl.ds(row, 1)], g_scr.at[s, j, pl.ds(t, 1)],
                                      sems.at[s]).start(priority=j)

    def wait(s):
        for j in range(TOP_K):
            pltpu.make_async_copy(ys_hbm.at[pl.ds(0, tc)], g_scr.at[s, j], sems.at[s]).wait()

    def combine(s):
        rows = pl.ds(s * tc, tc)
        meta = meta_ref[rows, :]
        w0 = meta[:, M_W0:M_W0 + 1]
        w1 = meta[:, M_W1:M_W1 + 1]
        x = x_ref[rows, :] + w0 * g_scr[s, 0] + w1 * g_scr[s, 1]
        ms = jnp.mean(x * x, axis=-1, keepdims=True)
        o_ref[rows, :] = x * lax.rsqrt(ms + RMS_EPS) * fw_ref[...]

    @pl.when(i == 0)
    def _():
        gather(dest_ref, 0, 0)

    gather(dest_ref, 1, 1)
    wait(0)
    combine(0)
    gather(dnext_ref, 0, 0)
    wait(1)
    combine(1)

    @pl.when(i == pl.num_programs(0) - 1)
    def _():
        wait(0)


def _combine(dest, x2, meta, fw, ys, tc):
    T = x2.shape[0]
    n = T // (2 * tc)
    dest3 = dest.reshape(n, 1, 2 * TOP_K * tc)
    dspec = lambda imap: pl.BlockSpec((1, 1, 2 * TOP_K * tc), imap, memory_space=pltpu.SMEM)
    return pl.pallas_call(
        functools.partial(_combine_kernel, tc=tc),
        grid=(n,),
        in_specs=[
            dspec(lambda i: (i, 0, 0)),
            dspec(lambda i: (jnp.minimum(i + 1, n - 1), 0, 0)),
            pl.BlockSpec((2 * tc, D_MODEL), lambda i: (i, 0)),
            pl.BlockSpec((2 * tc, LANE), lambda i: (i, 0)),
            pl.BlockSpec((1, D_MODEL), lambda i: (0, 0)),
            pl.BlockSpec(memory_space=pl.ANY),
        ],
        out_specs=pl.BlockSpec((2 * tc, D_MODEL), lambda i: (i, 0)),
        out_shape=jax.ShapeDtypeStruct((T, D_MODEL), F32),
        scratch_shapes=[pltpu.VMEM((2, TOP_K, tc, D_MODEL), F32), pltpu.SemaphoreType.DMA((2,))],
        compiler_params=_cparams(("arbitrary",)),
        name="moe_combine",
    )(dest3, dest3, x2, meta, fw, ys)


def _moe_plan(meta, counts, tm, n_tiles):
    cnt = counts[0, :N_EXPERTS].astype(jnp.int32)
    padded = ((cnt + tm - 1) // tm) * tm
    ends = jnp.cumsum(padded)
    offs = ends - padded
    e = meta[:, M_E0:M_E1 + 1].astype(jnp.int32)
    pos = meta[:, M_P0:M_P1 + 1].astype(jnp.int32)
    dest = offs[e] + pos
    starts = jnp.arange(n_tiles, dtype=jnp.int32) * tm
    n_active = (ends[-1] // tm).astype(jnp.int32)
    te = jnp.sum(starts[:, None] >= ends[None, :], axis=-1).astype(jnp.int32)
    last_e = jnp.sum(jnp.maximum(ends[-1] - tm, 0) >= ends).astype(jnp.int32)
    te = jnp.where(starts < ends[-1], te, last_e)
    return dest, te, n_active.reshape(1)


def _moe_layer(x_mid, nw, w_router, wg, wu, wd, final_w, tiles):
    T = x_mid.shape[0]
    tm = tiles["moe_tm"]
    n_tiles = (TOP_K * T) // tm + N_EXPERTS
    h, meta, counts = _router(x_mid, nw, w_router, tiles["router_t"])
    dest, te, n_active = _moe_plan(meta, counts, tm, n_tiles)
    xs = _scatter_rows(dest, h, n_tiles * tm, tiles["scatter_t"])
    ys = _moe_ffn(te, n_active, xs, wg, wu, wd, tm, tiles["moe_tf"])
    return _combine(dest, x_mid, meta, final_w, ys, tiles["combine_t"])


def _tiles(S, T):
    return dict(
        inproj_tm=min(1024, T), inproj_tn=1920,
        attn_t=min(512, S // 2), ret_c=min(256, S),
        merge_tm=min(512, T), ffn_tm=min(1024, T), ffn_tf=512,
        router_t=min(256, T), scatter_t=min(256, T), combine_t=min(256, T),
        moe_tm=min(1024, T), moe_tf=512,
    )


def kernel(x, norm_mix_w, w_in, fox_f_bias, ret_gn_w, mla_q_norm_w, mla_kv_norm_w, mla_w_uq, mla_w_uk, mla_w_uv, w_br_fox, w_br_ret, w_br_mla, w_out, norm_ffn_w, dense_w_gate, dense_w_up, dense_w_down, moe_w_router, moe_w_gate, moe_w_up, moe_w_down, final_norm_w):
    B, S, D = x.shape
    assert D == D_MODEL and w_in.shape[0] == DEPTH == 2
    T = B * S
    tl = _tiles(S, T)
    row = lambda a: a.reshape(1, -1).astype(F32)
    ret_tabs = _ret_tables(S, tl["ret_c"])
    mla_tabs = _mla_tables(S)
    x2 = x.reshape(T, D)
    for l in range(DEPTH):
        w_all, w_ff = _build_inproj_weights(w_in[l])
        proj, ff32 = _inproj(x2, row(norm_mix_w[l]), w_all, w_ff, tl["inproj_tm"], tl["inproj_tn"])
        fqt, fkx, fvt = _fox_prep(proj, ff32, fox_f_bias[l], B, S, tl["attn_t"])
        o_fox = _attention(fqt, fkx, fvt, B, S, tl["attn_t"], FOX_RANGES, "fox_attn")
        o_ret = _retention(proj, row(ret_gn_w[l]), ret_tabs, B, S, tl["ret_c"])
        wq, wqr, wk, wv = _mla_weights(mla_w_uq[l], mla_w_uk[l], mla_w_uv[l])
        mqt, mkx, mvt = _mla_prep(proj, row(mla_q_norm_w[l]), row(mla_kv_norm_w[l]), wq, wqr, wk, wv,
                                  mla_tabs, B, S, tl["attn_t"])
        o_mla = _attention(mqt, mkx, mvt, B, S, tl["attn_t"], MLA_RANGES, "mla_attn")
        x_mid, h2 = _merge(o_fox, o_ret, o_mla, proj, x2,
                           w_br_fox[l].astype(BF16), w_br_ret[l].astype(BF16), w_br_mla[l].astype(BF16),
                           w_out[l].astype(BF16), row(norm_ffn_w[l]), tl["merge_tm"])
        if l % 2 == 0:
            i = l // 2
            x2 = _dense_ffn(h2, x_mid, dense_w_gate[i].astype(BF16), dense_w_up[i].astype(BF16),
                            dense_w_down[i].astype(BF16), tl["ffn_tm"], tl["ffn_tf"])
        else:
            i = l // 2
            x2 = _moe_layer(x_mid, row(norm_ffn_w[l]), moe_w_router[i], moe_w_gate[i],
                            moe_w_up[i], moe_w_down[i], row(final_norm_w), tl)
    return x2.reshape(B, S, D)
```

```python
import functools

import numpy as np
import jax
import jax.numpy as jnp
from jax import lax
from jax.experimental import pallas as pl
from jax.experimental.pallas import tpu as pltpu

F32 = jnp.float32
BF16 = jnp.bfloat16

D_MODEL = 1024
DEPTH = 2
FOX_HEADS = 8
FOX_HEAD_DIM = 64
FOX_W = FOX_HEADS * FOX_HEAD_DIM
RET_HEADS = 8
RET_QK_DIM = 64
RET_V_DIM = 128
RET_QK_W = RET_HEADS * RET_QK_DIM
RET_V_W = RET_HEADS * RET_V_DIM
MLA_HEADS = 8
MLA_Q_RANK = 384
MLA_KV_RANK = 256
MLA_NOPE_DIM = 64
MLA_ROPE_DIM = 32
MLA_V_DIM = 64
MLA_QK_DIM = MLA_NOPE_DIM + MLA_ROPE_DIM
MLA_V_W = MLA_HEADS * MLA_V_DIM
N_BRANCHES = 3
ROPE_THETA = 10000.0
RMS_EPS = 1e-6
D_FF = 3584
N_EXPERTS = 8
TOP_K = 2
IN_SPLITS = (FOX_W, FOX_W, FOX_W, FOX_HEADS,
             RET_QK_W, RET_QK_W, RET_V_W, RET_V_W,
             MLA_Q_RANK, MLA_KV_RANK, MLA_ROPE_DIM,
             N_BRANCHES * D_MODEL)

LANE = 128
NEG = -1e30
VMEM_LIMIT = 56 * 1024 * 1024

U_FQ, U_FK, U_FV = 0, 4, 8
U_RQ, U_RQR, U_RK, U_RKR = 12, 16, 20, 24
U_CKV, U_KR, U_KRR = 28, 30, 31
U_RV, U_RG, U_GL, U_CQ = 32, 40, 48, 72
N_PROJ = 75 * LANE
FOX_EXT_STRIDE = 8
PAIRS = 4
QK_W = 2 * LANE
ATTN_KEY_CHUNK = 256
ATTN_STRIP = 256
VT_ROWS = 80
LOG2E = 1.4426950408889634


def _cparams(sem, vmem=VMEM_LIMIT, flags=None):
    return pltpu.CompilerParams(dimension_semantics=sem, vmem_limit_bytes=vmem, flags=flags)


def _sigmoid(x):
    return 1.0 / (1.0 + jnp.exp(-x))


def _lane_mask(shape, ranges):
    lane = lax.broadcasted_iota(jnp.int32, shape, len(shape) - 1)
    m = None
    for a, b in ranges:
        r = (lane >= a) & (lane < b)
        m = r if m is None else (m | r)
    return m


def _inproj_kernel(x_ref, nw_ref, w_ref, wff_ref, out_ref, ff_ref, h_scr):
    @pl.when(pl.program_id(1) == 0)
    def _():
        x = x_ref[...]
        ms = jnp.mean(x * x, axis=-1, keepdims=True)
        h = (x * lax.rsqrt(ms + RMS_EPS) * nw_ref[...]).astype(BF16)
        h_scr[...] = h
        ff_ref[...] = jnp.dot(h, wff_ref[...], preferred_element_type=F32)

    out_ref[...] = jnp.dot(h_scr[...], w_ref[...], preferred_element_type=F32).astype(out_ref.dtype)


def _inproj(x2, nw, w_all, w_ff, tm, tn):
    T = x2.shape[0]
    return pl.pallas_call(
        _inproj_kernel,
        grid=(T // tm, N_PROJ // tn),
        in_specs=[
            pl.BlockSpec((tm, D_MODEL), lambda i, j: (i, 0)),
            pl.BlockSpec((1, D_MODEL), lambda i, j: (0, 0)),
            pl.BlockSpec((D_MODEL, tn), lambda i, j: (0, j)),
            pl.BlockSpec((D_MODEL, LANE), lambda i, j: (0, 0)),
        ],
        out_specs=[
            pl.BlockSpec((tm, tn), lambda i, j: (i, j)),
            pl.BlockSpec((tm, LANE), lambda i, j: (i, 0)),
        ],
        out_shape=[
            jax.ShapeDtypeStruct((T, N_PROJ), BF16),
            jax.ShapeDtypeStruct((T, LANE), F32),
        ],
        scratch_shapes=[pltpu.VMEM((tm, D_MODEL), BF16)],
        compiler_params=_cparams(("parallel", "arbitrary")),
        name="inproj",
    )(x2, nw, w_all, w_ff)


def _rot_half_lanes(x, d):
    lane = lax.broadcasted_iota(jnp.int32, x.shape, 1)
    from_right = pltpu.roll(x, LANE - d // 2, 1)
    from_left = pltpu.roll(x, d // 2, 1)
    return jnp.where(lane % d < d // 2, -from_right, from_left)


def _wprep_kernel(w_ref, out_ref, ff_ref):
    offs = [int(i) for i in np.cumsum((0,) + IN_SPLITS)]
    o_fq, o_fk, o_fv, o_ff, o_rq, o_rk, o_rv, o_rg, o_cq, o_ckv, o_kr, o_gl, _ = offs

    def put(unit, piece):
        out_ref[0, :, unit * LANE:unit * LANE + piece.shape[-1]] = piece.astype(out_ref.dtype)

    put(U_FQ, w_ref[0, :, o_fq:o_ff])
    lane = lax.broadcasted_iota(jnp.int32, (w_ref.shape[1], LANE), 1)
    ff = w_ref[0, :, o_ff:o_ff + LANE]
    ff_ref[0] = jnp.where(lane < FOX_HEADS, ff, 0.0).astype(ff_ref.dtype)
    for unit, unit_rot, start in ((U_RQ, U_RQR, o_rq), (U_RK, U_RKR, o_rk)):
        for c in range(RET_QK_W // LANE):
            slab = w_ref[0, :, start + c * LANE:start + (c + 1) * LANE]
            put(unit + c, slab)
            put(unit_rot + c, _rot_half_lanes(slab, RET_QK_DIM))
    put(U_CKV, w_ref[0, :, o_ckv:o_kr])
    kr_blk = w_ref[0, :, o_kr - MLA_NOPE_DIM:o_kr - MLA_NOPE_DIM + LANE]
    kr128 = jnp.where((lane >= MLA_NOPE_DIM) & (lane < MLA_QK_DIM), kr_blk, 0.0)
    put(U_KR, kr128)
    put(U_KRR, jnp.where((lane >= MLA_NOPE_DIM) & (lane < MLA_QK_DIM), _rot_half_lanes(kr128, MLA_ROPE_DIM), 0.0))
    put(U_RV, w_ref[0, :, o_rv:o_rg])
    put(U_RG, w_ref[0, :, o_rg:o_cq])
    put(U_GL, w_ref[0, :, o_gl:o_gl + N_BRANCHES * D_MODEL])
    put(U_CQ, w_ref[0, :, o_cq:o_ckv])


def _build_inproj_weights(w_in, rows=128):
    depth, _, d_in = w_in.shape
    return pl.pallas_call(
        _wprep_kernel,
        grid=(depth, D_MODEL // rows),
        in_specs=[pl.BlockSpec((1, rows, d_in), lambda l, r: (l, r, 0))],
        out_specs=[
            pl.BlockSpec((1, rows, N_PROJ), lambda l, r: (l, r, 0)),
            pl.BlockSpec((1, rows, LANE), lambda l, r: (l, r, 0)),
        ],
        out_shape=[
            jax.ShapeDtypeStruct((depth, D_MODEL, N_PROJ), BF16),
            jax.ShapeDtypeStruct((depth, D_MODEL, LANE), BF16),
        ],
        compiler_params=_cparams(("parallel", "parallel")),
        name="inproj_weight_prep",
    )(w_in)


def _split3(x):
    hi = x.astype(BF16)
    r1 = x - hi.astype(F32)
    mid = r1.astype(BF16)
    lo = (r1 - mid.astype(F32)).astype(BF16)
    return hi, mid, lo


def _store_t(dst_ref, idx, x):
    dst_ref[idx] = x.T.astype(dst_ref.dtype)


def _store_vt_pair(vt_ref, p, v2):
    hd = LANE // 2
    vt = v2.T.astype(vt_ref.dtype)
    pad_rows = lax.broadcasted_iota(jnp.int32, (VT_ROWS - hd, vt.shape[1]), 0)
    ones_then_zeros = jnp.where(pad_rows == 0, 1.0, 0.0).astype(vt_ref.dtype)
    for j in range(2):
        base = (2 * p + j) * VT_ROWS
        vt_ref[0, 0, base:base + hd, :] = vt[j * hd:(j + 1) * hd]
        vt_ref[0, 0, base + hd:base + VT_ROWS, :] = ones_then_zeros


def _fox_prep_kernel(fq_ref, fk_ref, fv_ref, ff_ref, bias_ref, tri_ref, pq_ref, pk_ref, cq_ref, ck_ref,
                     qt_ref, kx_ref, vt_ref, carry_scr):
    @pl.when(pl.program_id(1) == 0)
    def _():
        carry_scr[...] = jnp.zeros_like(carry_scr)

    ts = ff_ref.shape[0]
    z = ff_ref[...] + bias_ref[...]
    logf = jnp.minimum(z, 0.0) - jnp.log(1.0 + jnp.exp(-jnp.abs(z)))
    tri = tri_ref[...]
    cum = carry_scr[0:1, :]
    for part in _split3(logf):
        cum = cum + jnp.dot(tri, part, preferred_element_type=F32)
    carry_scr[0:1, :] = cum[ts - 1:ts, :]

    eq = cq_ref[...]
    ek = ck_ref[...]
    for j, part in enumerate(_split3(cum * LOG2E)):
        eq = eq + jnp.dot(part, pq_ref[j * LANE:(j + 1) * LANE, :], preferred_element_type=F32)
        ek = ek + jnp.dot(part, pk_ref[j * LANE:(j + 1) * LANE, :], preferred_element_type=F32)
    scale = FOX_HEAD_DIM ** -0.5 * LOG2E
    for p in range(FOX_HEADS // 2):
        src = slice(p * LANE, (p + 1) * LANE)
        _store_t(qt_ref, (0, slice(2 * p * LANE, (2 * p + 1) * LANE)), fq_ref[:, src].astype(F32) * scale)
        _store_t(qt_ref, (0, slice((2 * p + 1) * LANE, (2 * p + 2) * LANE)), eq[:, src])
        kx_ref[:, 2 * p * LANE:(2 * p + 1) * LANE] = fk_ref[:, src]
        kx_ref[:, (2 * p + 1) * LANE:(2 * p + 2) * LANE] = ek[:, src].astype(BF16)
        _store_vt_pair(vt_ref, p, fv_ref[:, src].astype(F32))


def _fox_prep_consts(ts):
    tri = np.tril(np.ones((ts, ts), np.float32))
    pq = np.zeros((3 * LANE, FOX_W), np.float32)
    pk = np.zeros((3 * LANE, FOX_W), np.float32)
    cq = np.zeros((1, FOX_W), np.float32)
    ck = np.zeros((1, FOX_W), np.float32)
    for h in range(FOX_HEADS):
        base = LANE * (h // 2) + FOX_EXT_STRIDE * (h % 2)
        for j in range(3):
            pq[j * LANE + h, base + j] = 1.0
            ck[0, base + j] = 1.0
            cq[0, base + 3 + j] = 1.0
            pk[j * LANE + h, base + 3 + j] = -1.0
    return (jnp.asarray(tri, BF16), jnp.asarray(pq, BF16), jnp.asarray(pk, BF16),
            jnp.asarray(cq), jnp.asarray(ck))


def _attn_operand_specs(B, S, t):
    T = B * S
    ns = S // t
    specs = [
        pl.BlockSpec((1, PAIRS * QK_W, t), lambda b, s: (b, 0, s)),
        pl.BlockSpec((t, PAIRS * QK_W), lambda b, s: (b * ns + s, 0)),
        pl.BlockSpec((1, 1, 2 * PAIRS * VT_ROWS, t), lambda b, s: (b, s, 0, 0)),
    ]
    shapes = [
        jax.ShapeDtypeStruct((B, PAIRS * QK_W, S), BF16),
        jax.ShapeDtypeStruct((T, PAIRS * QK_W), BF16),
        jax.ShapeDtypeStruct((B, ns, 2 * PAIRS * VT_ROWS, t), BF16),
    ]
    return specs, shapes


def _fox_prep(proj, ff32, bias, B, S, ts):
    ns = S // ts
    tri, pq, pk, cq, ck = _fox_prep_consts(ts)
    bias128 = jnp.pad(bias.astype(F32), (0, LANE - FOX_HEADS)).reshape(1, LANE)
    const = lambda shape: pl.BlockSpec(shape, lambda b, s: (0, 0))
    out_specs, out_shape = _attn_operand_specs(B, S, ts)
    return pl.pallas_call(
        _fox_prep_kernel,
        grid=(B, ns),
        in_specs=[
            pl.BlockSpec((ts, FOX_W), lambda b, s: (b * ns + s, U_FQ // 4)),
            pl.BlockSpec((ts, FOX_W), lambda b, s: (b * ns + s, U_FK // 4)),
            pl.BlockSpec((ts, FOX_W), lambda b, s: (b * ns + s, U_FV // 4)),
            pl.BlockSpec((ts, LANE), lambda b, s: (b * ns + s, 0)),
            const((1, LANE)), const((ts, ts)), const((3 * LANE, FOX_W)), const((3 * LANE, FOX_W)),
            const((1, FOX_W)), const((1, FOX_W)),
        ],
        out_specs=out_specs,
        out_shape=out_shape,
        scratch_shapes=[pltpu.VMEM((8, LANE), F32)],
        compiler_params=_cparams(("parallel", "arbitrary")),
        name="fox_prep",
    )(proj, proj, proj, ff32, bias128, tri, pq, pk, cq, ck)


def _row_mask(shape, ranges):
    row = lax.broadcasted_iota(jnp.int32, shape, 0)
    m = None
    for a, b in ranges:
        r = (row >= a) & (row < b)
        m = r if m is None else (m | r)
    return m


def _col_reduce(x, op, reduce_fn, parts=2):
    c = x.shape[0] // parts
    pieces = [x[i * c:(i + 1) * c] for i in range(parts)]
    while len(pieces) > 1:
        pieces = [op(pieces[i], pieces[i + 1]) for i in range(0, len(pieces), 2)]
    return reduce_fn(pieces[0], axis=0, keepdims=True)


def _attn_kernel(qt_ref, qn_ref, k_ref, vt_ref, o_ref, qh_scr, qn_scr, sa_scr, sb_scr, ma_scr, mb_scr, acc_scr,
                 *, tk, ranges):
    qi = pl.program_id(2)
    qt = qt_ref[0]
    tq = qt.shape[-1]
    for h in range(2):
        qh_scr[h] = jnp.where(_row_mask(qt.shape, ranges[h]), qt, jnp.zeros_like(qt))
    acc_scr[...] = jnp.zeros_like(acc_scr)

    def key_tile(kt):
        return k_ref[pl.ds(pl.multiple_of(kt * tk, tk), tk), :]

    strips = [(c, c + ATTN_STRIP) for c in range(0, tq, ATTN_STRIP)]

    def scores(kt, s_scr, m_scr):
        k = key_tile(kt)
        for h in range(2):
            for a, b in strips:
                s = jnp.dot(k, qh_scr[h, :, a:b], preferred_element_type=F32)
                s_scr[h, :, a:b] = s
                m_scr[h, :, a:b] = _col_reduce(s, jnp.maximum, jnp.max)

    def update(kt, h, s, m_tile, m_prev, lo):
        m_new = jnp.maximum(m_prev, m_tile)
        alpha = jnp.exp2(m_prev - m_new)
        hi = lo + s.shape[-1]
        pv = None
        for r in range(0, tk, ATTN_KEY_CHUNK):
            p = jnp.exp2((s[r:r + ATTN_KEY_CHUNK] - m_new).astype(BF16))
            vt = vt_ref[0, kt, h * VT_ROWS:(h + 1) * VT_ROWS, r:r + ATTN_KEY_CHUNK]
            d = jnp.dot(vt, p, preferred_element_type=F32)
            pv = d if pv is None else pv + d
        acc_scr[h, :, lo:hi] = alpha * acc_scr[h, :, lo:hi] + pv
        return m_new

    def diag_update(kt, h, s_scr, m_prev, a, b, lo):
        s = s_scr[h, :, a:b]
        if a < lo + tk:
            key = lax.broadcasted_iota(jnp.int32, s.shape, 0)
            qry = lax.broadcasted_iota(jnp.int32, s.shape, 1) + (a - lo)
            s = jnp.where(key <= qry, s, NEG)
        return update(kt, h, s, _col_reduce(s, jnp.maximum, jnp.max), m_prev, a)

    def diagonal_tiles(carry, lookahead):
        k_last = key_tile(2 * qi + 1)
        if lookahead:
            qn = qn_ref[0]
            k0 = key_tile(0)
            for h in range(2):
                qn_scr[h] = jnp.where(_row_mask(qn.shape, ranges[h]), qn, jnp.zeros_like(qn))
        out = []
        for h in range(2):
            ms = []
            for a, b in strips:
                if a >= tk:
                    sb_scr[h, :, a:b] = jnp.dot(k_last, qh_scr[h, :, a:b], preferred_element_type=F32)
                ms.append(diag_update(2 * qi, h, sa_scr, carry[h][:, a:b], a, b, 0))
                if lookahead:
                    s = jnp.dot(k0, qn_scr[h, :, a:b], preferred_element_type=F32)
                    sa_scr[h, :, a:b] = s
                    ma_scr[h, :, a:b] = _col_reduce(s, jnp.maximum, jnp.max)
            out.append(ms)
        for h in range(2):
            for i, (a, b) in enumerate(strips):
                if a >= tk:
                    diag_update(2 * qi + 1, h, sb_scr, out[h][i], a, b, tk)
        hd = LANE // 2
        ot = jnp.concatenate([acc_scr[h, :hd] / acc_scr[h, hd:hd + 1] for h in range(2)], axis=0)
        o_ref[...] = ot.T.astype(o_ref.dtype)

    def fused_step(kt, cur_s, cur_m, nxt_s, nxt_m, carry):
        k = key_tile(kt + 1)
        ms = [[], []]
        for a, b in strips:
            for h in range(2):
                s = jnp.dot(k, qh_scr[h, :, a:b], preferred_element_type=F32)
                nxt_s[h, :, a:b] = s
                nxt_m[h, :, a:b] = _col_reduce(s, jnp.maximum, jnp.max)
                ms[h].append(update(kt, h, cur_s[h, :, a:b], cur_m[h, :, a:b], carry[h][:, a:b], a))
        return tuple(jnp.concatenate(m, axis=-1) for m in ms)

    def body(j, carry):
        kt = 2 * j
        carry = fused_step(kt, sa_scr, ma_scr, sb_scr, mb_scr, carry)
        return fused_step(kt + 1, sb_scr, mb_scr, sa_scr, ma_scr, carry)

    init = tuple(jnp.full((1, tq), NEG, F32) for _ in range(2))

    @pl.when(qi == 0)
    def _():
        scores(0, sa_scr, ma_scr)

    carry = lax.fori_loop(0, qi, body, init)
    last = pl.num_programs(2) - 1

    @pl.when(qi < last)
    def _():
        diagonal_tiles(carry, True)

    @pl.when(qi == last)
    def _():
        diagonal_tiles(carry, False)


def _attention(qt, kx, vt, B, S, tk, ranges, name):
    T = B * S
    tq = 2 * tk
    nq = S // tq
    kern = functools.partial(_attn_kernel, tk=tk, ranges=ranges)
    return pl.pallas_call(
        kern,
        grid=(B, PAIRS, nq),
        in_specs=[
            pl.BlockSpec((1, QK_W, tq), lambda b, p, i: (b, p, i)),
            pl.BlockSpec((1, QK_W, tq), lambda b, p, i: (b, p, jnp.minimum(i + 1, nq - 1))),
            pl.BlockSpec((S, QK_W), lambda b, p, i: (b, p)),
            pl.BlockSpec((1, S // tk, 2 * VT_ROWS, tk), lambda b, p, i: (b, 0, p, 0)),
        ],
        out_specs=pl.BlockSpec((tq, LANE), lambda b, p, i: (b * nq + i, p)),
        out_shape=jax.ShapeDtypeStruct((T, PAIRS * LANE), BF16),
        scratch_shapes=[
            pltpu.VMEM((2, QK_W, tq), BF16),
            pltpu.VMEM((2, QK_W, tq), BF16),
            pltpu.VMEM((2, tk, tq), F32),
            pltpu.VMEM((2, tk, tq), F32),
            pltpu.VMEM((2, 1, tq), F32),
            pltpu.VMEM((2, 1, tq), F32),
            pltpu.VMEM((2, VT_ROWS, tq), F32),
        ],
        compiler_params=_cparams(("parallel", "parallel", "arbitrary")),
        name=name,
    )(qt, qt, kx, vt)


_E0 = LANE
FOX_RANGES = (((0, 64), (_E0, _E0 + FOX_EXT_STRIDE)),
              ((64, 128), (_E0 + FOX_EXT_STRIDE, _E0 + 2 * FOX_EXT_STRIDE)))
MLA_RANGES = (((0, LANE),), ((LANE, 2 * LANE),))


def _mla_prep_kernel(cq_ref, ckv_ref, kr_ref, krr_ref, qnw_ref, kvnw_ref, wq_ref, wqr_ref, wk_ref, wv_ref,
                     cq_tab, sq_tab, ck_tab, sk_tab, qt_ref, k_ref, vt_ref):
    def norm(x_ref, w_ref):
        x = x_ref[...].astype(F32)
        ms = jnp.mean(x * x, axis=-1, keepdims=True)
        return (x * lax.rsqrt(ms + RMS_EPS) * w_ref[...]).astype(BF16)

    c_q = norm(cq_ref, qnw_ref)
    c_kv = norm(ckv_ref, kvnw_ref)
    tile8 = lambda a: jnp.concatenate([a] * MLA_HEADS, axis=-1)
    q = (jnp.dot(c_q, wq_ref[...], preferred_element_type=F32) * tile8(cq_tab[...])
         + jnp.dot(c_q, wqr_ref[...], preferred_element_type=F32) * tile8(sq_tab[...]))
    for c in range(MLA_HEADS):
        _store_t(qt_ref, (0, slice(c * LANE, (c + 1) * LANE)), q[:, c * LANE:(c + 1) * LANE])
    k_rope = kr_ref[...].astype(F32) * ck_tab[...] + krr_ref[...].astype(F32) * sk_tab[...]
    k = jnp.dot(c_kv, wk_ref[...], preferred_element_type=F32) + tile8(k_rope)
    k_ref[...] = k.astype(BF16)
    v = jnp.dot(c_kv, wv_ref[...], preferred_element_type=F32)
    for p in range(PAIRS):
        _store_vt_pair(vt_ref, p, v[:, p * LANE:(p + 1) * LANE])


def _mla_weights(w_uq, w_uk, w_uv):
    q3 = w_uq.reshape(MLA_Q_RANK, MLA_HEADS, MLA_QK_DIM)
    nope, ropep = q3[..., :MLA_NOPE_DIM], q3[..., MLA_NOPE_DIM:]
    half = MLA_ROPE_DIM // 2
    rot = jnp.concatenate([-ropep[..., half:], ropep[..., :half]], axis=-1)
    z32 = jnp.zeros((MLA_Q_RANK, MLA_HEADS, LANE - MLA_QK_DIM), F32)
    z64 = jnp.zeros((MLA_Q_RANK, MLA_HEADS, MLA_NOPE_DIM), F32)
    wq = jnp.concatenate([nope, ropep, z32], axis=-1).reshape(MLA_Q_RANK, MLA_HEADS * LANE)
    wqr = jnp.concatenate([z64, rot, z32], axis=-1).reshape(MLA_Q_RANK, MLA_HEADS * LANE)
    k3 = w_uk.reshape(MLA_KV_RANK, MLA_HEADS, MLA_NOPE_DIM)
    wk = jnp.concatenate([k3, jnp.zeros_like(k3)], axis=-1).reshape(MLA_KV_RANK, MLA_HEADS * LANE)
    return wq.astype(BF16), wqr.astype(BF16), wk.astype(BF16), w_uv.astype(BF16)


def _mla_tables(S):
    pos = jnp.arange(S, dtype=F32)
    inv = ROPE_THETA ** (-jnp.arange(0, MLA_ROPE_DIM, 2, dtype=F32) / MLA_ROPE_DIM)
    ang = pos[:, None] * inv[None, :]
    cos, sin = jnp.cos(ang), jnp.sin(ang)
    ones = jnp.ones((S, MLA_NOPE_DIM), F32)
    z64 = jnp.zeros((S, MLA_NOPE_DIM), F32)
    z32 = jnp.zeros((S, LANE - MLA_QK_DIM), F32)
    ctab = jnp.concatenate([ones, cos, cos, z32], axis=-1)
    stab = jnp.concatenate([z64, sin, sin, z32], axis=-1)
    scale = MLA_QK_DIM ** -0.5 * LOG2E
    return ctab * scale, stab * scale, ctab, stab


def _mla_prep(proj, qnw, kvnw, wq, wqr, wk, wv, tabs, B, S, tm):
    ns = S // tm
    HW = MLA_HEADS * LANE
    const = lambda shape: pl.BlockSpec(shape, lambda b, s: (0, 0))
    tab = pl.BlockSpec((tm, LANE), lambda b, s: (s, 0))
    out_specs, out_shape = _attn_operand_specs(B, S, tm)
    return pl.pallas_call(
        _mla_prep_kernel,
        grid=(B, ns),
        in_specs=[
            pl.BlockSpec((tm, MLA_Q_RANK), lambda b, s: (b * ns + s, U_CQ // 3)),
            pl.BlockSpec((tm, MLA_KV_RANK), lambda b, s: (b * ns + s, U_CKV // 2)),
            pl.BlockSpec((tm, LANE), lambda b, s: (b * ns + s, U_KR)),
            pl.BlockSpec((tm, LANE), lambda b, s: (b * ns + s, U_KRR)),
            const((1, MLA_Q_RANK)), const((1, MLA_KV_RANK)),
            const((MLA_Q_RANK, HW)), const((MLA_Q_RANK, HW)), const((MLA_KV_RANK, HW)),
            const((MLA_KV_RANK, MLA_V_W)),
            tab, tab, tab, tab,
        ],
        out_specs=out_specs,
        out_shape=out_shape,
        compiler_params=_cparams(("parallel", "parallel")),
        name="mla_prep",
    )(proj, proj, proj, proj, qnw, kvnw, wq, wqr, wk, wv, *tabs)


def _ret_kernel(rq_ref, rqr_ref, rk_ref, rkr_ref, rv_ref, rg_ref, cos_ref, sin_ref,
                din_ref, xi_ref, zeta_ref, gch_ref, gnw_ref, o_ref, r_scr):
    @pl.when(pl.program_id(0) == 0)
    def _():
        r_scr[...] = jnp.zeros_like(r_scr)

    tile4 = lambda a: jnp.concatenate([a] * (RET_QK_W // LANE), axis=-1)
    cos4 = tile4(cos_ref[...])
    sin4 = tile4(sin_ref[...])
    for bi in range(rq_ref.shape[0]):
        q = rq_ref[bi].astype(F32) * cos4 + rqr_ref[bi].astype(F32) * sin4
        k = (rk_ref[bi].astype(F32) * cos4 + rkr_ref[bi].astype(F32) * sin4) * (RET_QK_DIM ** -0.5)
        for p in range(RET_HEADS // 2):
            q2 = q[:, p * LANE:(p + 1) * LANE]
            k2 = k[:, p * LANE:(p + 1) * LANE]
            k2b = k2.astype(BF16)
            kz2 = (k2 * zeta_ref[p]).astype(BF16)
            r2 = r_scr[bi, p]
            r2b = r2.astype(BF16)
            new_r = gch_ref[p] * r2
            for j in range(2):
                h = 2 * p + j
                half = ((j * RET_QK_DIM, (j + 1) * RET_QK_DIM),)
                qm = jnp.where(_lane_mask(q2.shape, half), q2, 0.0).astype(BF16)
                inner = lax.dot_general(qm, k2b, (((1,), (1,)), ((), ())), preferred_element_type=F32)
                inner = inner * din_ref[h]
                vh = rv_ref[bi, :, h * LANE:(h + 1) * LANE]
                o = (jnp.dot(inner.astype(BF16), vh, preferred_element_type=F32)
                     + jnp.dot(qm, r2b, preferred_element_type=F32) * xi_ref[h])
                upd = lax.dot_general(kz2, vh, (((0,), (0,)), ((), ())), preferred_element_type=F32)
                rows = lax.broadcasted_iota(jnp.int32, upd.shape, 0)
                new_r = new_r + jnp.where((rows >= half[0][0]) & (rows < half[0][1]), upd, 0.0)
                mu = jnp.mean(o, axis=-1, keepdims=True)
                d = o - mu
                var = jnp.mean(d * d, axis=-1, keepdims=True)
                on = d * lax.rsqrt(var + RMS_EPS) * gnw_ref[:, h * LANE:(h + 1) * LANE]
                g = rg_ref[bi, :, h * LANE:(h + 1) * LANE].astype(F32)
                o_ref[bi, :, h * LANE:(h + 1) * LANE] = (g * _sigmoid(g) * on).astype(o_ref.dtype)
            r_scr[bi, p] = new_r


def _ret_tables(S, C):
    pos = jnp.arange(S, dtype=F32)
    inv = ROPE_THETA ** (-jnp.arange(0, RET_QK_DIM, 2, dtype=F32) / RET_QK_DIM)
    ang = pos[:, None] * inv[None, :]
    cos = jnp.concatenate([jnp.cos(ang)] * 4, axis=-1)
    sin = jnp.concatenate([jnp.sin(ang)] * 4, axis=-1)
    gammas = 1.0 - 2.0 ** (-5.0 - jnp.arange(RET_HEADS, dtype=F32))
    log_g = jnp.log(gammas)
    j = jnp.arange(C, dtype=F32)
    diff = j[:, None] - j[None, :]
    din = jnp.where(diff[None] >= 0, jnp.exp(jnp.maximum(diff, 0.0)[None] * log_g[:, None, None]), 0.0)
    xi = jnp.exp((j[None, :] + 1.0) * log_g[:, None])
    zeta = jnp.exp((C - 1.0 - j[None, :]) * log_g[:, None])
    gch = jnp.exp(C * log_g)
    xi_t = jnp.broadcast_to(xi[:, :, None], (RET_HEADS, C, LANE))
    zeta_p = jnp.repeat(zeta.reshape(RET_HEADS // 2, 2, C).transpose(0, 2, 1), RET_QK_DIM, axis=-1)
    gch_p = jnp.broadcast_to(jnp.repeat(gch.reshape(RET_HEADS // 2, 2), RET_QK_DIM, axis=-1)[:, :, None],
                             (RET_HEADS // 2, LANE, LANE))
    return cos, sin, din, xi_t, zeta_p, gch_p


def _retention(proj, gnw, tabs, B, S, C):
    nc = S // C
    cos, sin, din, xi, zeta, gch = tabs
    proj3 = proj.reshape(B, S, N_PROJ)
    blk = lambda w, unit: pl.BlockSpec((B, C, w), lambda c: (0, c, unit * LANE // w))
    full = lambda a: pl.BlockSpec(a.shape, lambda c: (0,) * a.ndim)
    out = pl.pallas_call(
        _ret_kernel,
        grid=(nc,),
        in_specs=[
            blk(RET_QK_W, U_RQ), blk(RET_QK_W, U_RQR), blk(RET_QK_W, U_RK), blk(RET_QK_W, U_RKR),
            blk(RET_V_W, U_RV), blk(RET_V_W, U_RG),
            pl.BlockSpec((C, LANE), lambda c: (c, 0)),
            pl.BlockSpec((C, LANE), lambda c: (c, 0)),
            full(din), full(xi), full(zeta), full(gch), full(gnw),
        ],
        out_specs=pl.BlockSpec((B, C, RET_V_W), lambda c: (0, c, 0)),
        out_shape=jax.ShapeDtypeStruct((B, S, RET_V_W), BF16),
        scratch_shapes=[pltpu.VMEM((B, RET_HEADS // 2, LANE, LANE), F32)],
        compiler_params=_cparams(("arbitrary",)),
        name="retention",
    )(proj3, proj3, proj3, proj3, proj3, proj3, cos, sin, din, xi, zeta, gch, gnw)
    return out.reshape(B * S, RET_V_W)


def _merge_kernel(of_ref, or_ref, om_ref, g0_ref, g1_ref, g2_ref, x_ref,
                  wf_ref, wr_ref, wm_ref, wo_ref, nw_ref, xo_ref, h_ref):
    merged = (_sigmoid(g0_ref[...].astype(F32)) * jnp.dot(of_ref[...], wf_ref[...], preferred_element_type=F32)
              + _sigmoid(g1_ref[...].astype(F32)) * jnp.dot(or_ref[...], wr_ref[...], preferred_element_type=F32)
              + _sigmoid(g2_ref[...].astype(F32)) * jnp.dot(om_ref[...], wm_ref[...], preferred_element_type=F32))
    x = x_ref[...] + jnp.dot(merged.astype(BF16), wo_ref[...], preferred_element_type=F32)
    xo_ref[...] = x
    ms = jnp.mean(x * x, axis=-1, keepdims=True)
    h_ref[...] = (x * lax.rsqrt(ms + RMS_EPS) * nw_ref[...]).astype(h_ref.dtype)


def _merge(o_fox, o_ret, o_mla, proj, x2, wf, wr, wm, wo, nw, tm):
    T = x2.shape[0]
    row = lambda w, cb=0: pl.BlockSpec((tm, w), lambda i: (i, cb))
    const = lambda a: pl.BlockSpec(a.shape, lambda i: (0, 0))
    g_unit = U_GL // 8
    return pl.pallas_call(
        _merge_kernel,
        grid=(T // tm,),
        in_specs=[
            row(FOX_W), row(RET_V_W), row(MLA_V_W),
            row(D_MODEL, g_unit), row(D_MODEL, g_unit + 1), row(D_MODEL, g_unit + 2),
            row(D_MODEL),
            const(wf), const(wr), const(wm), const(wo), const(nw),
        ],
        out_specs=[row(D_MODEL), row(D_MODEL)],
        out_shape=[jax.ShapeDtypeStruct((T, D_MODEL), F32), jax.ShapeDtypeStruct((T, D_MODEL), BF16)],
        compiler_params=_cparams(("parallel",)),
        name="merge",
    )(o_fox, o_ret, o_mla, proj, proj, proj, x2, wf, wr, wm, wo, nw)


def _swiglu_tile(h, wg, wu, wd):
    a = jnp.dot(h, wg, preferred_element_type=F32)
    b = jnp.dot(h, wu, preferred_element_type=F32)
    return jnp.dot((a * _sigmoid(a) * b).astype(BF16), wd, preferred_element_type=F32)


def _dense_ffn_kernel(h_ref, wg_ref, wu_ref, wd_ref, x_ref, o_ref, acc_scr):
    f = pl.program_id(1)

    @pl.when(f == 0)
    def _():
        acc_scr[...] = jnp.zeros_like(acc_scr)

    acc_scr[...] += _swiglu_tile(h_ref[...], wg_ref[...], wu_ref[...], wd_ref[...])

    @pl.when(f == pl.num_programs(1) - 1)
    def _():
        o_ref[...] = x_ref[...] + acc_scr[...]


def _dense_ffn(h, x2, wg, wu, wd, tm, tf):
    T = x2.shape[0]
    return pl.pallas_call(
        _dense_ffn_kernel,
        grid=(T // tm, D_FF // tf),
        in_specs=[
            pl.BlockSpec((tm, D_MODEL), lambda i, f: (i, 0)),
            pl.BlockSpec((D_MODEL, tf), lambda i, f: (0, f)),
            pl.BlockSpec((D_MODEL, tf), lambda i, f: (0, f)),
            pl.BlockSpec((tf, D_MODEL), lambda i, f: (f, 0)),
            pl.BlockSpec((tm, D_MODEL), lambda i, f: (i, 0)),
        ],
        out_specs=pl.BlockSpec((tm, D_MODEL), lambda i, f: (i, 0)),
        out_shape=jax.ShapeDtypeStruct((T, D_MODEL), F32),
        scratch_shapes=[pltpu.VMEM((tm, D_MODEL), F32)],
        compiler_params=_cparams(("parallel", "arbitrary")),
        name="dense_ffn",
    )(h, wg, wu, wd, x2)


M_E0, M_E1, M_P0, M_P1, M_W0, M_W1 = 0, 1, 2, 3, 4, 5


def _split2(x):
    hi = x.astype(BF16)
    lo = (x - hi.astype(F32)).astype(BF16)
    return hi, lo


HALF_D = D_MODEL // 2


def _pack_rows(h):
    bits = pltpu.bitcast(h.astype(BF16).astype(F32), jnp.uint32)
    return (bits[:, :HALF_D] & jnp.uint32(0xFFFF0000)) | (bits[:, HALF_D:] >> 16)


def _unpack_rows(u):
    left = pltpu.bitcast(u & jnp.uint32(0xFFFF0000), F32)
    right = pltpu.bitcast(u << 16, F32)
    return jnp.concatenate([left, right], axis=-1).astype(BF16)


def _router_kernel(x_ref, nw_ref, whi_ref, wlo_ref, tri_ref, h_ref, meta_ref, cnt_ref, carry_scr):
    @pl.when(pl.program_id(0) == 0)
    def _():
        carry_scr[...] = jnp.zeros_like(carry_scr)

    x = x_ref[...]
    ms = jnp.mean(x * x, axis=-1, keepdims=True)
    h = x * lax.rsqrt(ms + RMS_EPS) * nw_ref[...]
    h_ref[...] = _pack_rows(h)
    hhi, hlo = _split2(h)
    logits = (jnp.dot(hhi, whi_ref[...], preferred_element_type=F32)
              + jnp.dot(hhi, wlo_ref[...], preferred_element_type=F32)
              + jnp.dot(hlo, whi_ref[...], preferred_element_type=F32))
    lane = lax.broadcasted_iota(jnp.int32, logits.shape, 1)
    lanef = lane.astype(F32)
    logits = jnp.where(lane < N_EXPERTS, logits, NEG)
    m0 = jnp.max(logits, axis=-1, keepdims=True)
    e0 = jnp.min(jnp.where(logits == m0, lanef, float(LANE)), axis=-1, keepdims=True)
    oh0 = lanef == e0
    rest = jnp.where(oh0, NEG, logits)
    m1 = jnp.max(rest, axis=-1, keepdims=True)
    e1 = jnp.min(jnp.where(rest == m1, lanef, float(LANE)), axis=-1, keepdims=True)
    oh1 = lanef == e1
    z = jnp.exp(m1 - m0)
    w0 = 1.0 / (1.0 + z)
    w1 = z / (1.0 + z)
    both = jnp.where(oh0 | oh1, 1.0, 0.0)
    before = carry_scr[0:1, :] + jnp.dot(tri_ref[...], both.astype(BF16), preferred_element_type=F32)
    p0 = jnp.sum(jnp.where(oh0, before, 0.0), axis=-1, keepdims=True)
    p1 = jnp.sum(jnp.where(oh1, before, 0.0), axis=-1, keepdims=True)
    total = carry_scr[0:1, :] + jnp.sum(both, axis=0, keepdims=True)
    carry_scr[0:1, :] = total
    cnt_ref[...] = jnp.broadcast_to(total, cnt_ref.shape)
    meta = jnp.zeros(logits.shape, F32)
    for idx, val in ((M_E0, e0), (M_E1, e1), (M_P0, p0), (M_P1, p1), (M_W0, w0), (M_W1, w1)):
        meta = jnp.where(lane == idx, val, meta)
    meta_ref[...] = meta


def _router(x2, nw, w_router, tr):
    T = x2.shape[0]
    wr = jnp.pad(w_router.astype(F32), ((0, 0), (0, LANE - N_EXPERTS)))
    whi = wr.astype(BF16)
    wlo = (wr - whi.astype(F32)).astype(BF16)
    tri = jnp.asarray(np.tril(np.ones((tr, tr), np.float32), -1), BF16)
    const = lambda a: pl.BlockSpec(a.shape, lambda i: (0, 0))
    return pl.pallas_call(
        _router_kernel,
        grid=(T // tr,),
        in_specs=[pl.BlockSpec((tr, D_MODEL), lambda i: (i, 0)), const(nw), const(whi), const(wlo), const(tri)],
        out_specs=[
            pl.BlockSpec((tr, HALF_D), lambda i: (i, 0)),
            pl.BlockSpec((tr, LANE), lambda i: (i, 0)),
            pl.BlockSpec((8, LANE), lambda i: (0, 0)),
        ],
        out_shape=[
            jax.ShapeDtypeStruct((T, HALF_D), jnp.uint32),
            jax.ShapeDtypeStruct((T, LANE), F32),
            jax.ShapeDtypeStruct((8, LANE), F32),
        ],
        scratch_shapes=[pltpu.VMEM((8, LANE), F32)],
        compiler_params=_cparams(("arbitrary",)),
        name="router",
    )(x2, nw, whi, wlo, tri)


def _scatter_kernel(dest_ref, h_ref, xs_in, xs_hbm, sem, *, tt):
    del xs_in

    def row_copy(t, d_row):
        return pltpu.make_async_copy(h_ref.at[pl.ds(t, 1)], xs_hbm.at[pl.ds(d_row, 1)], sem)

    for t in range(tt):
        for j in range(TOP_K):
            row_copy(t, dest_ref[0, 0, TOP_K * t + j]).start(priority=j)

    for _ in range(TOP_K):
        pltpu.make_async_copy(h_ref, xs_hbm.at[pl.ds(0, tt)], sem).wait()


def _scatter_rows(dest, h, n_rows, tt):
    T = h.shape[0]
    width = h.shape[1]
    xs0 = jnp.zeros((n_rows, width), h.dtype)
    dest3 = dest.reshape(T // tt, 1, 2 * tt)
    return pl.pallas_call(
        functools.partial(_scatter_kernel, tt=tt),
        grid=(T // tt,),
        in_specs=[
            pl.BlockSpec((1, 1, 2 * tt), lambda i: (i, 0, 0), memory_space=pltpu.SMEM),
            pl.BlockSpec((tt, width), lambda i: (i, 0)),
            pl.BlockSpec(memory_space=pl.ANY),
        ],
        out_specs=pl.BlockSpec(memory_space=pl.ANY),
        out_shape=jax.ShapeDtypeStruct((n_rows, width), h.dtype),
        scratch_shapes=[pltpu.SemaphoreType.DMA(())],
        input_output_aliases={2: 0},
        compiler_params=_cparams(("arbitrary",)),
        name="moe_scatter",
    )(dest3, h, xs0)


def _moe_ffn_kernel(te_ref, na_ref, xs_ref, wg_ref, wu_ref, wd_ref, ys_ref, acc_scr):
    i = pl.program_id(0)
    f = pl.program_id(1)

    @pl.when(i < na_ref[0])
    def _():
        @pl.when(f == 0)
        def _():
            acc_scr[...] = jnp.zeros_like(acc_scr)

        acc_scr[...] += _swiglu_tile(_unpack_rows(xs_ref[...]), wg_ref[0].astype(BF16),
                                     wu_ref[0].astype(BF16), wd_ref[0].astype(BF16))

        @pl.when(f == pl.num_programs(1) - 1)
        def _():
            ys_ref[...] = acc_scr[...]

    @pl.when((i >= na_ref[0]) & (f == 0))
    def _():
        ys_ref[...] = jnp.zeros_like(ys_ref)


def _moe_ffn(tile_expert, n_active, xs, wg, wu, wd, tm, tf):
    n_rows = xs.shape[0]
    n_tiles = n_rows // tm
    nf = D_FF // tf

    def row_map(i, f, te, na):
        return (jnp.minimum(i, na[0] - 1), 0)

    def fcol(i, f, na):
        return jnp.where(i < na[0], f, nf - 1)

    grid_spec = pltpu.PrefetchScalarGridSpec(
        num_scalar_prefetch=2,
        grid=(n_tiles, nf),
        in_specs=[
            pl.BlockSpec((tm, HALF_D), row_map),
            pl.BlockSpec((1, D_MODEL, tf), lambda i, f, te, na: (te[i], 0, fcol(i, f, na))),
            pl.BlockSpec((1, D_MODEL, tf), lambda i, f, te, na: (te[i], 0, fcol(i, f, na))),
            pl.BlockSpec((1, tf, D_MODEL), lambda i, f, te, na: (te[i], fcol(i, f, na), 0)),
        ],
        out_specs=pl.BlockSpec((tm, D_MODEL), lambda i, f, te, na: (i, 0)),
        scratch_shapes=[pltpu.VMEM((tm, D_MODEL), F32)],
    )
    return pl.pallas_call(
        _moe_ffn_kernel,
        grid_spec=grid_spec,
        out_shape=jax.ShapeDtypeStruct((n_rows, D_MODEL), F32),
        compiler_params=_cparams(("arbitrary", "arbitrary")),
        name="moe_ffn",
    )(tile_expert, n_active, xs, wg, wu, wd)


def _combine_kernel(dest_ref, dnext_ref, x_ref, meta_ref, fw_ref, ys_hbm, o_ref, g_scr, sems, *, tc):
    i = pl.program_id(0)

    def gather(d_ref, half, s):
        for t in range(tc):
            for j in range(TOP_K):
                row = d_ref[0, 0, TOP_K * (half * tc + t) + j]
                pltpu.make_async_copy(ys_hbm.at[pl.ds(row, 1)], g_scr.at[s, j, pl.ds(t, 1)],
                                      sems.at[s]).start(priority=j)

    def wait(s):
        for j in range(TOP_K):
            pltpu.make_async_copy(ys_hbm.at[pl.ds(0, tc)], g_scr.at[s, j], sems.at[s]).wait()

    def combine(s):
        rows = pl.ds(s * tc, tc)
        meta = meta_ref[rows, :]
        w0 = meta[:, M_W0:M_W0 + 1]
        w1 = meta[:, M_W1:M_W1 + 1]
        x = x_ref[rows, :] + w0 * g_scr[s, 0] + w1 * g_scr[s, 1]
        ms = jnp.mean(x * x, axis=-1, keepdims=True)
        o_ref[rows, :] = x * lax.rsqrt(ms + RMS_EPS) * fw_ref[...]

    @pl.when(i == 0)
    def _():
        gather(dest_ref, 0, 0)

    gather(dest_ref, 1, 1)
    wait(0)
    combine(0)
    gather(dnext_ref, 0, 0)
    wait(1)
    combine(1)

    @pl.when(i == pl.num_programs(0) - 1)
    def _():
        wait(0)


def _combine(dest, x2, meta, fw, ys, tc):
    T = x2.shape[0]
    n = T // (2 * tc)
    dest3 = dest.reshape(n, 1, 2 * TOP_K * tc)
    dspec = lambda imap: pl.BlockSpec((1, 1, 2 * TOP_K * tc), imap, memory_space=pltpu.SMEM)
    return pl.pallas_call(
        functools.partial(_combine_kernel, tc=tc),
        grid=(n,),
        in_specs=[
            dspec(lambda i: (i, 0, 0)),
            dspec(lambda i: (jnp.minimum(i + 1, n - 1), 0, 0)),
            pl.BlockSpec((2 * tc, D_MODEL), lambda i: (i, 0)),
            pl.BlockSpec((2 * tc, LANE), lambda i: (i, 0)),
            pl.BlockSpec((1, D_MODEL), lambda i: (0, 0)),
            pl.BlockSpec(memory_space=pl.ANY),
        ],
        out_specs=pl.BlockSpec((2 * tc, D_MODEL), lambda i: (i, 0)),
        out_shape=jax.ShapeDtypeStruct((T, D_MODEL), F32),
        scratch_shapes=[pltpu.VMEM((2, TOP_K, tc, D_MODEL), F32), pltpu.SemaphoreType.DMA((2,))],
        compiler_params=_cparams(("arbitrary",)),
        name="moe_combine",
    )(dest3, dest3, x2, meta, fw, ys)


def _moe_plan(meta, counts, tm, n_tiles):
    cnt = counts[0, :N_EXPERTS].astype(jnp.int32)
    padded = ((cnt + tm - 1) // tm) * tm
    ends = jnp.cumsum(padded)
    offs = ends - padded
    e = meta[:, M_E0:M_E1 + 1].astype(jnp.int32)
    pos = meta[:, M_P0:M_P1 + 1].astype(jnp.int32)
    dest = offs[e] + pos
    starts = jnp.arange(n_tiles, dtype=jnp.int32) * tm
    n_active = (ends[-1] // tm).astype(jnp.int32)
    te = jnp.sum(starts[:, None] >= ends[None, :], axis=-1).astype(jnp.int32)
    last_e = jnp.sum(jnp.maximum(ends[-1] - tm, 0) >= ends).astype(jnp.int32)
    te = jnp.where(starts < ends[-1], te, last_e)
    return dest, te, n_active.reshape(1)


def _moe_layer(x_mid, nw, w_router, wg, wu, wd, final_w, tiles):
    T = x_mid.shape[0]
    tm = tiles["moe_tm"]
    n_tiles = (TOP_K * T) // tm + N_EXPERTS
    h, meta, counts = _router(x_mid, nw, w_router, tiles["router_t"])
    dest, te, n_active = _moe_plan(meta, counts, tm, n_tiles)
    xs = _scatter_rows(dest, h, n_tiles * tm, tiles["scatter_t"])
    ys = _moe_ffn(te, n_active, xs, wg, wu, wd, tm, tiles["moe_tf"])
    return _combine(dest, x_mid, meta, final_w, ys, tiles["combine_t"])


def _tiles(S, T):
    return dict(
        inproj_tm=min(1024, T), inproj_tn=1920,
        attn_t=min(512, S // 2), ret_c=min(256, S),
        merge_tm=min(512, T), ffn_tm=min(1024, T), ffn_tf=512,
        router_t=min(256, T), scatter_t=min(256, T), combine_t=min(256, T),
        moe_tm=min(1024, T), moe_tf=512,
    )


def kernel(x, norm_mix_w, w_in, fox_f_bias, ret_gn_w, mla_q_norm_w, mla_kv_norm_w, mla_w_uq, mla_w_uk, mla_w_uv, w_br_fox, w_br_ret, w_br_mla, w_out, norm_ffn_w, dense_w_gate, dense_w_up, dense_w_down, moe_w_router, moe_w_gate, moe_w_up, moe_w_down, final_norm_w):
    B, S, D = x.shape
    assert D == D_MODEL and w_in.shape[0] == DEPTH == 2
    T = B * S
    tl = _tiles(S, T)
    row = lambda a: a.reshape(1, -1).astype(F32)
    ret_tabs = _ret_tables(S, tl["ret_c"])
    mla_tabs = _mla_tables(S)
    x2 = x.reshape(T, D)
    w_all, w_ff = _build_inproj_weights(w_in)
    for l in range(DEPTH):
        proj, ff32 = _inproj(x2, row(norm_mix_w[l]), w_all[l], w_ff[l], tl["inproj_tm"], tl["inproj_tn"])
        fqt, fkx, fvt = _fox_prep(proj, ff32, fox_f_bias[l], B, S, tl["attn_t"])
        o_fox = _attention(fqt, fkx, fvt, B, S, tl["attn_t"], FOX_RANGES, "fox_attn")
        o_ret = _retention(proj, row(ret_gn_w[l]), ret_tabs, B, S, tl["ret_c"])
        wq, wqr, wk, wv = _mla_weights(mla_w_uq[l], mla_w_uk[l], mla_w_uv[l])
        mqt, mkx, mvt = _mla_prep(proj, row(mla_q_norm_w[l]), row(mla_kv_norm_w[l]), wq, wqr, wk, wv,
                                  mla_tabs, B, S, tl["attn_t"])
        o_mla = _attention(mqt, mkx, mvt, B, S, tl["attn_t"], MLA_RANGES, "mla_attn")
        x_mid, h2 = _merge(o_fox, o_ret, o_mla, proj, x2,
                           w_br_fox[l].astype(BF16), w_br_ret[l].astype(BF16), w_br_mla[l].astype(BF16),
                           w_out[l].astype(BF16), row(norm_ffn_w[l]), tl["merge_tm"])
        if l % 2 == 0:
            i = l // 2
            x2 = _dense_ffn(h2, x_mid, dense_w_gate[i].astype(BF16), dense_w_up[i].astype(BF16),
                            dense_w_down[i].astype(BF16), tl["ffn_tm"], tl["ffn_tf"])
        else:
            i = l // 2
            x2 = _moe_layer(x_mid, row(norm_ffn_w[l]), moe_w_router[i], moe_w_gate[i],
                            moe_w_up[i], moe_w_down[i], row(final_norm_w), tl)
    return x2.reshape(B, S, D)
```

```python
import functools

import numpy as np
import jax
import jax.numpy as jnp
from jax import lax
from jax.experimental import pallas as pl
from jax.experimental.pallas import tpu as pltpu

F32 = jnp.float32
BF16 = jnp.bfloat16

D_MODEL = 1024
DEPTH = 2
FOX_HEADS = 8
FOX_HEAD_DIM = 64
FOX_W = FOX_HEADS * FOX_HEAD_DIM
RET_HEADS = 8
RET_QK_DIM = 64
RET_V_DIM = 128
RET_QK_W = RET_HEADS * RET_QK_DIM
RET_V_W = RET_HEADS * RET_V_DIM
MLA_HEADS = 8
MLA_Q_RANK = 384
MLA_KV_RANK = 256
MLA_NOPE_DIM = 64
MLA_ROPE_DIM = 32
MLA_V_DIM = 64
MLA_QK_DIM = MLA_NOPE_DIM + MLA_ROPE_DIM
MLA_V_W = MLA_HEADS * MLA_V_DIM
N_BRANCHES = 3
ROPE_THETA = 10000.0
RMS_EPS = 1e-6
D_FF = 3584
N_EXPERTS = 8
TOP_K = 2
IN_SPLITS = (FOX_W, FOX_W, FOX_W, FOX_HEADS,
             RET_QK_W, RET_QK_W, RET_V_W, RET_V_W,
             MLA_Q_RANK, MLA_KV_RANK, MLA_ROPE_DIM,
             N_BRANCHES * D_MODEL)

LANE = 128
NEG = -1e30
VMEM_LIMIT = 56 * 1024 * 1024

U_FQ, U_FK, U_FV = 0, 4, 8
U_RQ, U_RQR, U_RK, U_RKR = 12, 16, 20, 24
U_CKV, U_KR, U_KRR = 28, 30, 31
U_RV, U_RG, U_GL, U_CQ = 32, 40, 48, 72
N_PROJ = 75 * LANE
FOX_EXT_STRIDE = 8
PAIRS = 4
QK_W = 2 * LANE
ATTN_KEY_CHUNK = 256
ATTN_STRIP = 256
VT_ROWS = 80
LOG2E = 1.4426950408889634


def _cparams(sem, vmem=VMEM_LIMIT, flags=None):
    return pltpu.CompilerParams(dimension_semantics=sem, vmem_limit_bytes=vmem, flags=flags)


def _sigmoid(x):
    return 1.0 / (1.0 + jnp.exp(-x))


def _lane_mask(shape, ranges):
    lane = lax.broadcasted_iota(jnp.int32, shape, len(shape) - 1)
    m = None
    for a, b in ranges:
        r = (lane >= a) & (lane < b)
        m = r if m is None else (m | r)
    return m


def _inproj_kernel(x_ref, nw_ref, w_ref, wff_ref, out_ref, ff_ref, h_scr):
    @pl.when(pl.program_id(1) == 0)
    def _():
        x = x_ref[...]
        ms = jnp.mean(x * x, axis=-1, keepdims=True)
        h = (x * lax.rsqrt(ms + RMS_EPS) * nw_ref[...]).astype(BF16)
        h_scr[...] = h
        ff_ref[...] = jnp.dot(h, wff_ref[...], preferred_element_type=F32)

    out_ref[...] = jnp.dot(h_scr[...], w_ref[...], preferred_element_type=F32).astype(out_ref.dtype)


def _inproj(x2, nw, w_all, w_ff, tm, tn):
    T = x2.shape[0]
    return pl.pallas_call(
        _inproj_kernel,
        grid=(T // tm, N_PROJ // tn),
        in_specs=[
            pl.BlockSpec((tm, D_MODEL), lambda i, j: (i, 0)),
            pl.BlockSpec((1, D_MODEL), lambda i, j: (0, 0)),
            pl.BlockSpec((D_MODEL, tn), lambda i, j: (0, j)),
            pl.BlockSpec((D_MODEL, LANE), lambda i, j: (0, 0)),
        ],
        out_specs=[
            pl.BlockSpec((tm, tn), lambda i, j: (i, j)),
            pl.BlockSpec((tm, LANE), lambda i, j: (i, 0)),
        ],
        out_shape=[
            jax.ShapeDtypeStruct((T, N_PROJ), BF16),
            jax.ShapeDtypeStruct((T, LANE), F32),
        ],
        scratch_shapes=[pltpu.VMEM((tm, D_MODEL), BF16)],
        compiler_params=_cparams(("parallel", "arbitrary")),
        name="inproj",
    )(x2, nw, w_all, w_ff)


def _rot_half_lanes(x, d):
    lane = lax.broadcasted_iota(jnp.int32, x.shape, 1)
    from_right = pltpu.roll(x, LANE - d // 2, 1)
    from_left = pltpu.roll(x, d // 2, 1)
    return jnp.where(lane % d < d // 2, -from_right, from_left)


def _wprep_kernel(w_ref, *out_refs):
    offs = [int(i) for i in np.cumsum((0,) + IN_SPLITS)]
    o_fq, o_fk, o_fv, o_ff, o_rq, o_rk, o_rv, o_rg, o_cq, o_ckv, o_kr, o_gl, _ = offs
    depth = w_ref.shape[0]
    lane = lax.broadcasted_iota(jnp.int32, (w_ref.shape[1], LANE), 1)
    rope_lanes = (lane >= MLA_NOPE_DIM) & (lane < MLA_QK_DIM)
    for l in range(depth):
        out_ref, ff_ref = out_refs[l], out_refs[depth + l]

        def put(unit, piece, out_ref=out_ref):
            out_ref[:, unit * LANE:unit * LANE + piece.shape[-1]] = piece.astype(out_ref.dtype)

        put(U_FQ, w_ref[l, :, o_fq:o_ff])
        ff_ref[...] = jnp.where(lane < FOX_HEADS, w_ref[l, :, o_ff:o_ff + LANE], 0.0).astype(ff_ref.dtype)
        for unit, unit_rot, start in ((U_RQ, U_RQR, o_rq), (U_RK, U_RKR, o_rk)):
            for c in range(RET_QK_W // LANE):
                slab = w_ref[l, :, start + c * LANE:start + (c + 1) * LANE]
                put(unit + c, slab)
                put(unit_rot + c, _rot_half_lanes(slab, RET_QK_DIM))
        put(U_CKV, w_ref[l, :, o_ckv:o_kr])
        kr128 = jnp.where(rope_lanes, w_ref[l, :, o_kr - MLA_NOPE_DIM:o_kr - MLA_NOPE_DIM + LANE], 0.0)
        put(U_KR, kr128)
        put(U_KRR, jnp.where(rope_lanes, _rot_half_lanes(kr128, MLA_ROPE_DIM), 0.0))
        put(U_RV, w_ref[l, :, o_rv:o_rg])
        put(U_RG, w_ref[l, :, o_rg:o_cq])
        put(U_GL, w_ref[l, :, o_gl:o_gl + N_BRANCHES * D_MODEL])
        put(U_CQ, w_ref[l, :, o_cq:o_ckv])


def _build_inproj_weights(w_in, rows=128):
    depth, _, d_in = w_in.shape
    outs = pl.pallas_call(
        _wprep_kernel,
        grid=(D_MODEL // rows,),
        in_specs=[pl.BlockSpec((depth, rows, d_in), lambda r: (0, r, 0))],
        out_specs=([pl.BlockSpec((rows, N_PROJ), lambda r: (r, 0))] * depth
                   + [pl.BlockSpec((rows, LANE), lambda r: (r, 0))] * depth),
        out_shape=([jax.ShapeDtypeStruct((D_MODEL, N_PROJ), BF16)] * depth
                   + [jax.ShapeDtypeStruct((D_MODEL, LANE), BF16)] * depth),
        compiler_params=_cparams(("parallel",)),
        name="inproj_weight_prep",
    )(w_in)
    return outs[:depth], outs[depth:]


def _split3(x):
    hi = x.astype(BF16)
    r1 = x - hi.astype(F32)
    mid = r1.astype(BF16)
    lo = (r1 - mid.astype(F32)).astype(BF16)
    return hi, mid, lo


def _store_t(dst_ref, idx, x):
    dst_ref[idx] = x.T.astype(dst_ref.dtype)


def _store_vt_pair(vt_ref, p, v2):
    hd = LANE // 2
    vt = v2.T.astype(vt_ref.dtype)
    pad_rows = lax.broadcasted_iota(jnp.int32, (VT_ROWS - hd, vt.shape[1]), 0)
    ones_then_zeros = jnp.where(pad_rows == 0, 1.0, 0.0).astype(vt_ref.dtype)
    for j in range(2):
        base = (2 * p + j) * VT_ROWS
        vt_ref[0, 0, base:base + hd, :] = vt[j * hd:(j + 1) * hd]
        vt_ref[0, 0, base + hd:base + VT_ROWS, :] = ones_then_zeros


def _fox_prep_kernel(fq_ref, fk_ref, fv_ref, ff_ref, bias_ref, tri_ref, pq_ref, pk_ref, cq_ref, ck_ref,
                     qt_ref, kx_ref, vt_ref, carry_scr):
    @pl.when(pl.program_id(1) == 0)
    def _():
        carry_scr[...] = jnp.zeros_like(carry_scr)

    ts = ff_ref.shape[0]
    z = ff_ref[...] + bias_ref[...]
    logf = jnp.minimum(z, 0.0) - jnp.log(1.0 + jnp.exp(-jnp.abs(z)))
    tri = tri_ref[...]
    cum = carry_scr[0:1, :]
    for part in _split3(logf):
        cum = cum + jnp.dot(tri, part, preferred_element_type=F32)
    carry_scr[0:1, :] = cum[ts - 1:ts, :]

    eq = cq_ref[...]
    ek = ck_ref[...]
    for j, part in enumerate(_split3(cum * LOG2E)):
        eq = eq + jnp.dot(part, pq_ref[j * LANE:(j + 1) * LANE, :], preferred_element_type=F32)
        ek = ek + jnp.dot(part, pk_ref[j * LANE:(j + 1) * LANE, :], preferred_element_type=F32)
    scale = FOX_HEAD_DIM ** -0.5 * LOG2E
    for p in range(FOX_HEADS // 2):
        src = slice(p * LANE, (p + 1) * LANE)
        _store_t(qt_ref, (0, slice(2 * p * LANE, (2 * p + 1) * LANE)), fq_ref[:, src].astype(F32) * scale)
        _store_t(qt_ref, (0, slice((2 * p + 1) * LANE, (2 * p + 2) * LANE)), eq[:, src])
        kx_ref[:, 2 * p * LANE:(2 * p + 1) * LANE] = fk_ref[:, src]
        kx_ref[:, (2 * p + 1) * LANE:(2 * p + 2) * LANE] = ek[:, src].astype(BF16)
        _store_vt_pair(vt_ref, p, fv_ref[:, src].astype(F32))


def _fox_prep_consts(ts):
    tri = np.tril(np.ones((ts, ts), np.float32))
    pq = np.zeros((3 * LANE, FOX_W), np.float32)
    pk = np.zeros((3 * LANE, FOX_W), np.float32)
    cq = np.zeros((1, FOX_W), np.float32)
    ck = np.zeros((1, FOX_W), np.float32)
    for h in range(FOX_HEADS):
        base = LANE * (h // 2) + FOX_EXT_STRIDE * (h % 2)
        for j in range(3):
            pq[j * LANE + h, base + j] = 1.0
            ck[0, base + j] = 1.0
            cq[0, base + 3 + j] = 1.0
            pk[j * LANE + h, base + 3 + j] = -1.0
    return (jnp.asarray(tri, BF16), jnp.asarray(pq, BF16), jnp.asarray(pk, BF16),
            jnp.asarray(cq), jnp.asarray(ck))


def _attn_operand_specs(B, S, t):
    T = B * S
    ns = S // t
    specs = [
        pl.BlockSpec((1, PAIRS * QK_W, t), lambda b, s: (b, 0, s)),
        pl.BlockSpec((t, PAIRS * QK_W), lambda b, s: (b * ns + s, 0)),
        pl.BlockSpec((1, 1, 2 * PAIRS * VT_ROWS, t), lambda b, s: (b, s, 0, 0)),
    ]
    shapes = [
        jax.ShapeDtypeStruct((B, PAIRS * QK_W, S), BF16),
        jax.ShapeDtypeStruct((T, PAIRS * QK_W), BF16),
        jax.ShapeDtypeStruct((B, ns, 2 * PAIRS * VT_ROWS, t), BF16),
    ]
    return specs, shapes


def _fox_prep(proj, ff32, bias, B, S, ts):
    ns = S // ts
    tri, pq, pk, cq, ck = _fox_prep_consts(ts)
    bias128 = jnp.pad(bias.astype(F32), (0, LANE - FOX_HEADS)).reshape(1, LANE)
    const = lambda shape: pl.BlockSpec(shape, lambda b, s: (0, 0))
    out_specs, out_shape = _attn_operand_specs(B, S, ts)
    return pl.pallas_call(
        _fox_prep_kernel,
        grid=(B, ns),
        in_specs=[
            pl.BlockSpec((ts, FOX_W), lambda b, s: (b * ns + s, U_FQ // 4)),
            pl.BlockSpec((ts, FOX_W), lambda b, s: (b * ns + s, U_FK // 4)),
            pl.BlockSpec((ts, FOX_W), lambda b, s: (b * ns + s, U_FV // 4)),
            pl.BlockSpec((ts, LANE), lambda b, s: (b * ns + s, 0)),
            const((1, LANE)), const((ts, ts)), const((3 * LANE, FOX_W)), const((3 * LANE, FOX_W)),
            const((1, FOX_W)), const((1, FOX_W)),
        ],
        out_specs=out_specs,
        out_shape=out_shape,
        scratch_shapes=[pltpu.VMEM((8, LANE), F32)],
        compiler_params=_cparams(("parallel", "arbitrary")),
        name="fox_prep",
    )(proj, proj, proj, ff32, bias128, tri, pq, pk, cq, ck)


def _row_mask(shape, ranges):
    row = lax.broadcasted_iota(jnp.int32, shape, 0)
    m = None
    for a, b in ranges:
        r = (row >= a) & (row < b)
        m = r if m is None else (m | r)
    return m


def _col_reduce(x, op, reduce_fn, parts=2):
    c = x.shape[0] // parts
    pieces = [x[i * c:(i + 1) * c] for i in range(parts)]
    while len(pieces) > 1:
        pieces = [op(pieces[i], pieces[i + 1]) for i in range(0, len(pieces), 2)]
    return reduce_fn(pieces[0], axis=0, keepdims=True)


def _attn_kernel(qt_ref, qn_ref, k_ref, vt_ref, o_ref, qh_scr, qn_scr, sa_scr, sb_scr, ma_scr, mb_scr, acc_scr,
                 *, tk, ranges):
    qi = pl.program_id(2)
    qt = qt_ref[0]
    tq = qt.shape[-1]
    for h in range(2):
        qh_scr[h] = jnp.where(_row_mask(qt.shape, ranges[h]), qt, jnp.zeros_like(qt))
    acc_scr[...] = jnp.zeros_like(acc_scr)

    def key_tile(kt):
        return k_ref[pl.ds(pl.multiple_of(kt * tk, tk), tk), :]

    strips = [(c, c + ATTN_STRIP) for c in range(0, tq, ATTN_STRIP)]

    def scores(kt, s_scr, m_scr):
        k = key_tile(kt)
        for h in range(2):
            for a, b in strips:
                s = jnp.dot(k, qh_scr[h, :, a:b], preferred_element_type=F32)
                s_scr[h, :, a:b] = s
                m_scr[h, :, a:b] = _col_reduce(s, jnp.maximum, jnp.max)

    def update(kt, h, s, m_tile, m_prev, lo):
        m_new = jnp.maximum(m_prev, m_tile)
        alpha = jnp.exp2(m_prev - m_new)
        hi = lo + s.shape[-1]
        pv = None
        for r in range(0, tk, ATTN_KEY_CHUNK):
            p = jnp.exp2((s[r:r + ATTN_KEY_CHUNK] - m_new).astype(BF16))
            vt = vt_ref[0, kt, h * VT_ROWS:(h + 1) * VT_ROWS, r:r + ATTN_KEY_CHUNK]
            d = jnp.dot(vt, p, preferred_element_type=F32)
            pv = d if pv is None else pv + d
        acc_scr[h, :, lo:hi] = alpha * acc_scr[h, :, lo:hi] + pv
        return m_new

    def diag_update(kt, h, s_scr, m_prev, a, b, lo):
        s = s_scr[h, :, a:b]
        if a < lo + tk:
            key = lax.broadcasted_iota(jnp.int32, s.shape, 0)
            qry = lax.broadcasted_iota(jnp.int32, s.shape, 1) + (a - lo)
            s = jnp.where(key <= qry, s, NEG)
        return update(kt, h, s, _col_reduce(s, jnp.maximum, jnp.max), m_prev, a)

    def diagonal_tiles(carry, lookahead):
        k_last = key_tile(2 * qi + 1)
        if lookahead:
            qn = qn_ref[0]
            k0 = key_tile(0)
            for h in range(2):
                qn_scr[h] = jnp.where(_row_mask(qn.shape, ranges[h]), qn, jnp.zeros_like(qn))
        out = []
        for h in range(2):
            ms = []
            for a, b in strips:
                if a >= tk:
                    sb_scr[h, :, a:b] = jnp.dot(k_last, qh_scr[h, :, a:b], preferred_element_type=F32)
                ms.append(diag_update(2 * qi, h, sa_scr, carry[h][:, a:b], a, b, 0))
                if lookahead:
                    s = jnp.dot(k0, qn_scr[h, :, a:b], preferred_element_type=F32)
                    sa_scr[h, :, a:b] = s
                    ma_scr[h, :, a:b] = _col_reduce(s, jnp.maximum, jnp.max)
            out.append(ms)
        for h in range(2):
            for i, (a, b) in enumerate(strips):
                if a >= tk:
                    diag_update(2 * qi + 1, h, sb_scr, out[h][i], a, b, tk)
        hd = LANE // 2
        ot = jnp.concatenate([acc_scr[h, :hd] / acc_scr[h, hd:hd + 1] for h in range(2)], axis=0)
        o_ref[...] = ot.T.astype(o_ref.dtype)

    def fused_step(kt, cur_s, cur_m, nxt_s, nxt_m, carry):
        k = key_tile(kt + 1)
        ms = [[], []]
        for a, b in strips:
            for h in range(2):
                s = jnp.dot(k, qh_scr[h, :, a:b], preferred_element_type=F32)
                nxt_s[h, :, a:b] = s
                nxt_m[h, :, a:b] = _col_reduce(s, jnp.maximum, jnp.max)
                ms[h].append(update(kt, h, cur_s[h, :, a:b], cur_m[h, :, a:b], carry[h][:, a:b], a))
        return tuple(jnp.concatenate(m, axis=-1) for m in ms)

    def body(j, carry):
        kt = 2 * j
        carry = fused_step(kt, sa_scr, ma_scr, sb_scr, mb_scr, carry)
        return fused_step(kt + 1, sb_scr, mb_scr, sa_scr, ma_scr, carry)

    init = tuple(jnp.full((1, tq), NEG, F32) for _ in range(2))

    @pl.when(qi == 0)
    def _():
        scores(0, sa_scr, ma_scr)

    carry = lax.fori_loop(0, qi, body, init)
    last = pl.num_programs(2) - 1

    @pl.when(qi < last)
    def _():
        diagonal_tiles(carry, True)

    @pl.when(qi == last)
    def _():
        diagonal_tiles(carry, False)


def _attention(qt, kx, vt, B, S, tk, ranges, name):
    T = B * S
    tq = 2 * tk
    nq = S // tq
    kern = functools.partial(_attn_kernel, tk=tk, ranges=ranges)
    return pl.pallas_call(
        kern,
        grid=(B, PAIRS, nq),
        in_specs=[
            pl.BlockSpec((1, QK_W, tq), lambda b, p, i: (b, p, i)),
            pl.BlockSpec((1, QK_W, tq), lambda b, p, i: (b, p, jnp.minimum(i + 1, nq - 1))),
            pl.BlockSpec((S, QK_W), lambda b, p, i: (b, p)),
            pl.BlockSpec((1, S // tk, 2 * VT_ROWS, tk), lambda b, p, i: (b, 0, p, 0)),
        ],
        out_specs=pl.BlockSpec((tq, LANE), lambda b, p, i: (b * nq + i, p)),
        out_shape=jax.ShapeDtypeStruct((T, PAIRS * LANE), BF16),
        scratch_shapes=[
            pltpu.VMEM((2, QK_W, tq), BF16),
            pltpu.VMEM((2, QK_W, tq), BF16),
            pltpu.VMEM((2, tk, tq), F32),
            pltpu.VMEM((2, tk, tq), F32),
            pltpu.VMEM((2, 1, tq), F32),
            pltpu.VMEM((2, 1, tq), F32),
            pltpu.VMEM((2, VT_ROWS, tq), F32),
        ],
        compiler_params=_cparams(("parallel", "parallel", "arbitrary")),
        name=name,
    )(qt, qt, kx, vt)


_E0 = LANE
FOX_RANGES = (((0, 64), (_E0, _E0 + FOX_EXT_STRIDE)),
              ((64, 128), (_E0 + FOX_EXT_STRIDE, _E0 + 2 * FOX_EXT_STRIDE)))
MLA_RANGES = (((0, LANE),), ((LANE, 2 * LANE),))


def _mla_prep_kernel(cq_ref, ckv_ref, kr_ref, krr_ref, qnw_ref, kvnw_ref, wq_ref, wqr_ref, wk_ref, wv_ref,
                     cq_tab, sq_tab, ck_tab, sk_tab, qt_ref, k_ref, vt_ref):
    def norm(x_ref, w_ref):
        x = x_ref[...].astype(F32)
        ms = jnp.mean(x * x, axis=-1, keepdims=True)
        return (x * lax.rsqrt(ms + RMS_EPS) * w_ref[...]).astype(BF16)

    c_q = norm(cq_ref, qnw_ref)
    c_kv = norm(ckv_ref, kvnw_ref)
    tile8 = lambda a: jnp.concatenate([a] * MLA_HEADS, axis=-1)
    q = (jnp.dot(c_q, wq_ref[...], preferred_element_type=F32) * tile8(cq_tab[...])
         + jnp.dot(c_q, wqr_ref[...], preferred_element_type=F32) * tile8(sq_tab[...]))
    for c in range(MLA_HEADS):
        _store_t(qt_ref, (0, slice(c * LANE, (c + 1) * LANE)), q[:, c * LANE:(c + 1) * LANE])
    k_rope = kr_ref[...].astype(F32) * ck_tab[...] + krr_ref[...].astype(F32) * sk_tab[...]
    k = jnp.dot(c_kv, wk_ref[...], preferred_element_type=F32) + tile8(k_rope)
    k_ref[...] = k.astype(BF16)
    v = jnp.dot(c_kv, wv_ref[...], preferred_element_type=F32)
    for p in range(PAIRS):
        _store_vt_pair(vt_ref, p, v[:, p * LANE:(p + 1) * LANE])


def _mla_weights(w_uq, w_uk, w_uv):
    q3 = w_uq.reshape(MLA_Q_RANK, MLA_HEADS, MLA_QK_DIM)
    nope, ropep = q3[..., :MLA_NOPE_DIM], q3[..., MLA_NOPE_DIM:]
    half = MLA_ROPE_DIM // 2
    rot = jnp.concatenate([-ropep[..., half:], ropep[..., :half]], axis=-1)
    z32 = jnp.zeros((MLA_Q_RANK, MLA_HEADS, LANE - MLA_QK_DIM), F32)
    z64 = jnp.zeros((MLA_Q_RANK, MLA_HEADS, MLA_NOPE_DIM), F32)
    wq = jnp.concatenate([nope, ropep, z32], axis=-1).reshape(MLA_Q_RANK, MLA_HEADS * LANE)
    wqr = jnp.concatenate([z64, rot, z32], axis=-1).reshape(MLA_Q_RANK, MLA_HEADS * LANE)
    k3 = w_uk.reshape(MLA_KV_RANK, MLA_HEADS, MLA_NOPE_DIM)
    wk = jnp.concatenate([k3, jnp.zeros_like(k3)], axis=-1).reshape(MLA_KV_RANK, MLA_HEADS * LANE)
    return wq.astype(BF16), wqr.astype(BF16), wk.astype(BF16), w_uv.astype(BF16)


def _rope_inv_freq(d):
    return (np.float32(ROPE_THETA) ** (-np.arange(0, d, 2, dtype=np.float32) / np.float32(d))).astype(np.float32)


def _mla_tables(S):
    inv = _rope_inv_freq(MLA_ROPE_DIM)
    inv128 = np.concatenate([np.zeros(MLA_NOPE_DIM, np.float32), inv, inv, np.zeros(LANE - MLA_QK_DIM, np.float32)])
    keep = (np.arange(LANE) < MLA_QK_DIM).astype(np.float32)
    ang = jnp.arange(S, dtype=F32)[:, None] * jnp.asarray(inv128)[None, :]
    ctab = jnp.cos(ang) * jnp.asarray(keep)[None, :]
    stab = jnp.sin(ang)
    scale = MLA_QK_DIM ** -0.5 * LOG2E
    return ctab * scale, stab * scale, ctab, stab


def _mla_prep(proj, qnw, kvnw, wq, wqr, wk, wv, tabs, B, S, tm):
    ns = S // tm
    HW = MLA_HEADS * LANE
    const = lambda shape: pl.BlockSpec(shape, lambda b, s: (0, 0))
    tab = pl.BlockSpec((tm, LANE), lambda b, s: (s, 0))
    out_specs, out_shape = _attn_operand_specs(B, S, tm)
    return pl.pallas_call(
        _mla_prep_kernel,
        grid=(B, ns),
        in_specs=[
            pl.BlockSpec((tm, MLA_Q_RANK), lambda b, s: (b * ns + s, U_CQ // 3)),
            pl.BlockSpec((tm, MLA_KV_RANK), lambda b, s: (b * ns + s, U_CKV // 2)),
            pl.BlockSpec((tm, LANE), lambda b, s: (b * ns + s, U_KR)),
            pl.BlockSpec((tm, LANE), lambda b, s: (b * ns + s, U_KRR)),
            const((1, MLA_Q_RANK)), const((1, MLA_KV_RANK)),
            const((MLA_Q_RANK, HW)), const((MLA_Q_RANK, HW)), const((MLA_KV_RANK, HW)),
            const((MLA_KV_RANK, MLA_V_W)),
            tab, tab, tab, tab,
        ],
        out_specs=out_specs,
        out_shape=out_shape,
        compiler_params=_cparams(("parallel", "parallel")),
        name="mla_prep",
    )(proj, proj, proj, proj, qnw, kvnw, wq, wqr, wk, wv, *tabs)


def _ret_kernel(rq_ref, rqr_ref, rk_ref, rkr_ref, rv_ref, rg_ref, cos_ref, sin_ref,
                din_ref, xi_ref, zeta_ref, gch_ref, gnw_ref, o_ref, r_scr):
    @pl.when(pl.program_id(0) == 0)
    def _():
        r_scr[...] = jnp.zeros_like(r_scr)

    tile4 = lambda a: jnp.concatenate([a] * (RET_QK_W // LANE), axis=-1)
    cos4 = tile4(cos_ref[...])
    sin4 = tile4(sin_ref[...])
    for bi in range(rq_ref.shape[0]):
        q = rq_ref[bi].astype(F32) * cos4 + rqr_ref[bi].astype(F32) * sin4
        k = (rk_ref[bi].astype(F32) * cos4 + rkr_ref[bi].astype(F32) * sin4) * (RET_QK_DIM ** -0.5)
        for p in range(RET_HEADS // 2):
            q2 = q[:, p * LANE:(p + 1) * LANE]
            k2 = k[:, p * LANE:(p + 1) * LANE]
            k2b = k2.astype(BF16)
            kz2 = (k2 * zeta_ref[p]).astype(BF16)
            r2 = r_scr[bi, p]
            r2b = r2.astype(BF16)
            new_r = gch_ref[p] * r2
            for j in range(2):
                h = 2 * p + j
                half = ((j * RET_QK_DIM, (j + 1) * RET_QK_DIM),)
                qm = jnp.where(_lane_mask(q2.shape, half), q2, 0.0).astype(BF16)
                inner = lax.dot_general(qm, k2b, (((1,), (1,)), ((), ())), preferred_element_type=F32)
                inner = inner * din_ref[h]
                vh = rv_ref[bi, :, h * LANE:(h + 1) * LANE]
                o = (jnp.dot(inner.astype(BF16), vh, preferred_element_type=F32)
                     + jnp.dot(qm, r2b, preferred_element_type=F32) * xi_ref[h])
                upd = lax.dot_general(kz2, vh, (((0,), (0,)), ((), ())), preferred_element_type=F32)
                rows = lax.broadcasted_iota(jnp.int32, upd.shape, 0)
                new_r = new_r + jnp.where((rows >= half[0][0]) & (rows < half[0][1]), upd, 0.0)
                mu = jnp.mean(o, axis=-1, keepdims=True)
                d = o - mu
                var = jnp.mean(d * d, axis=-1, keepdims=True)
                on = d * lax.rsqrt(var + RMS_EPS) * gnw_ref[:, h * LANE:(h + 1) * LANE]
                g = rg_ref[bi, :, h * LANE:(h + 1) * LANE].astype(F32)
                o_ref[bi, :, h * LANE:(h + 1) * LANE] = (g * _sigmoid(g) * on).astype(o_ref.dtype)
            r_scr[bi, p] = new_r


def _ret_tables(S, C):
    inv128 = np.tile(_rope_inv_freq(RET_QK_DIM), 2 * LANE // RET_QK_DIM)
    ang = jnp.arange(S, dtype=F32)[:, None] * jnp.asarray(inv128)[None, :]
    cos = jnp.cos(ang)
    sin = jnp.sin(ang)
    gammas = 1.0 - 2.0 ** (-5.0 - jnp.arange(RET_HEADS, dtype=F32))
    log_g = jnp.log(gammas)
    j = jnp.arange(C, dtype=F32)
    diff = j[:, None] - j[None, :]
    din = jnp.where(diff[None] >= 0, jnp.exp(jnp.maximum(diff, 0.0)[None] * log_g[:, None, None]), 0.0)
    xi = jnp.exp((j[None, :] + 1.0) * log_g[:, None])
    zeta = jnp.exp((C - 1.0 - j[None, :]) * log_g[:, None])
    gch = jnp.exp(C * log_g)
    xi_t = jnp.broadcast_to(xi[:, :, None], (RET_HEADS, C, LANE))
    zeta_p = jnp.repeat(zeta.reshape(RET_HEADS // 2, 2, C).transpose(0, 2, 1), RET_QK_DIM, axis=-1)
    gch_p = jnp.broadcast_to(jnp.repeat(gch.reshape(RET_HEADS // 2, 2), RET_QK_DIM, axis=-1)[:, :, None],
                             (RET_HEADS // 2, LANE, LANE))
    return cos, sin, din, xi_t, zeta_p, gch_p


def _retention(proj, gnw, tabs, B, S, C):
    nc = S // C
    cos, sin, din, xi, zeta, gch = tabs
    proj3 = proj.reshape(B, S, N_PROJ)
    blk = lambda w, unit: pl.BlockSpec((B, C, w), lambda c: (0, c, unit * LANE // w))
    full = lambda a: pl.BlockSpec(a.shape, lambda c: (0,) * a.ndim)
    out = pl.pallas_call(
        _ret_kernel,
        grid=(nc,),
        in_specs=[
            blk(RET_QK_W, U_RQ), blk(RET_QK_W, U_RQR), blk(RET_QK_W, U_RK), blk(RET_QK_W, U_RKR),
            blk(RET_V_W, U_RV), blk(RET_V_W, U_RG),
            pl.BlockSpec((C, LANE), lambda c: (c, 0)),
            pl.BlockSpec((C, LANE), lambda c: (c, 0)),
            full(din), full(xi), full(zeta), full(gch), full(gnw),
        ],
        out_specs=pl.BlockSpec((B, C, RET_V_W), lambda c: (0, c, 0)),
        out_shape=jax.ShapeDtypeStruct((B, S, RET_V_W), BF16),
        scratch_shapes=[pltpu.VMEM((B, RET_HEADS // 2, LANE, LANE), F32)],
        compiler_params=_cparams(("arbitrary",)),
        name="retention",
    )(proj3, proj3, proj3, proj3, proj3, proj3, cos, sin, din, xi, zeta, gch, gnw)
    return out.reshape(B * S, RET_V_W)


def _merge_kernel(of_ref, or_ref, om_ref, g0_ref, g1_ref, g2_ref, x_ref,
                  wf_ref, wr_ref, wm_ref, wo_ref, nw_ref, xo_ref, h_ref):
    merged = (_sigmoid(g0_ref[...].astype(F32)) * jnp.dot(of_ref[...], wf_ref[...], preferred_element_type=F32)
              + _sigmoid(g1_ref[...].astype(F32)) * jnp.dot(or_ref[...], wr_ref[...], preferred_element_type=F32)
              + _sigmoid(g2_ref[...].astype(F32)) * jnp.dot(om_ref[...], wm_ref[...], preferred_element_type=F32))
    x = x_ref[...] + jnp.dot(merged.astype(BF16), wo_ref[...], preferred_element_type=F32)
    xo_ref[...] = x
    ms = jnp.mean(x * x, axis=-1, keepdims=True)
    h_ref[...] = (x * lax.rsqrt(ms + RMS_EPS) * nw_ref[...]).astype(h_ref.dtype)


def _merge(o_fox, o_ret, o_mla, proj, x2, wf, wr, wm, wo, nw, tm):
    T = x2.shape[0]
    row = lambda w, cb=0: pl.BlockSpec((tm, w), lambda i: (i, cb))
    const = lambda a: pl.BlockSpec(a.shape, lambda i: (0, 0))
    g_unit = U_GL // 8
    return pl.pallas_call(
        _merge_kernel,
        grid=(T // tm,),
        in_specs=[
            row(FOX_W), row(RET_V_W), row(MLA_V_W),
            row(D_MODEL, g_unit), row(D_MODEL, g_unit + 1), row(D_MODEL, g_unit + 2),
            row(D_MODEL),
            const(wf), const(wr), const(wm), const(wo), const(nw),
        ],
        out_specs=[row(D_MODEL), row(D_MODEL)],
        out_shape=[jax.ShapeDtypeStruct((T, D_MODEL), F32), jax.ShapeDtypeStruct((T, D_MODEL), BF16)],
        compiler_params=_cparams(("parallel",)),
        name="merge",
    )(o_fox, o_ret, o_mla, proj, proj, proj, x2, wf, wr, wm, wo, nw)


def _swiglu_tile(h, wg, wu, wd):
    a = jnp.dot(h, wg, preferred_element_type=F32)
    b = jnp.dot(h, wu, preferred_element_type=F32)
    return jnp.dot((a * _sigmoid(a) * b).astype(BF16), wd, preferred_element_type=F32)


def _dense_ffn_kernel(h_ref, wg_ref, wu_ref, wd_ref, x_ref, o_ref, acc_scr):
    f = pl.program_id(1)

    @pl.when(f == 0)
    def _():
        acc_scr[...] = jnp.zeros_like(acc_scr)

    acc_scr[...] += _swiglu_tile(h_ref[...], wg_ref[...], wu_ref[...], wd_ref[...])

    @pl.when(f == pl.num_programs(1) - 1)
    def _():
        o_ref[...] = x_ref[...] + acc_scr[...]


def _dense_ffn(h, x2, wg, wu, wd, tm, tf):
    T = x2.shape[0]
    return pl.pallas_call(
        _dense_ffn_kernel,
        grid=(T // tm, D_FF // tf),
        in_specs=[
            pl.BlockSpec((tm, D_MODEL), lambda i, f: (i, 0)),
            pl.BlockSpec((D_MODEL, tf), lambda i, f: (0, f)),
            pl.BlockSpec((D_MODEL, tf), lambda i, f: (0, f)),
            pl.BlockSpec((tf, D_MODEL), lambda i, f: (f, 0)),
            pl.BlockSpec((tm, D_MODEL), lambda i, f: (i, 0)),
        ],
        out_specs=pl.BlockSpec((tm, D_MODEL), lambda i, f: (i, 0)),
        out_shape=jax.ShapeDtypeStruct((T, D_MODEL), F32),
        scratch_shapes=[pltpu.VMEM((tm, D_MODEL), F32)],
        compiler_params=_cparams(("parallel", "arbitrary")),
        name="dense_ffn",
    )(h, wg, wu, wd, x2)


M_E0, M_E1, M_P0, M_P1, M_W0, M_W1 = 0, 1, 2, 3, 4, 5


def _split2(x):
    hi = x.astype(BF16)
    lo = (x - hi.astype(F32)).astype(BF16)
    return hi, lo


HALF_D = D_MODEL // 2


def _pack_rows(h):
    bits = pltpu.bitcast(h.astype(BF16).astype(F32), jnp.uint32)
    return (bits[:, :HALF_D] & jnp.uint32(0xFFFF0000)) | (bits[:, HALF_D:] >> 16)


def _unpack_rows(u):
    left = pltpu.bitcast(u & jnp.uint32(0xFFFF0000), F32)
    right = pltpu.bitcast(u << 16, F32)
    return jnp.concatenate([left, right], axis=-1).astype(BF16)


def _router_kernel(x_ref, nw_ref, whi_ref, wlo_ref, tri_ref, h_ref, meta_ref, cnt_ref, carry_scr):
    @pl.when(pl.program_id(0) == 0)
    def _():
        carry_scr[...] = jnp.zeros_like(carry_scr)

    x = x_ref[...]
    ms = jnp.mean(x * x, axis=-1, keepdims=True)
    h = x * lax.rsqrt(ms + RMS_EPS) * nw_ref[...]
    h_ref[...] = _pack_rows(h)
    hhi, hlo = _split2(h)
    logits = (jnp.dot(hhi, whi_ref[...], preferred_element_type=F32)
              + jnp.dot(hhi, wlo_ref[...], preferred_element_type=F32)
              + jnp.dot(hlo, whi_ref[...], preferred_element_type=F32))
    lane = lax.broadcasted_iota(jnp.int32, logits.shape, 1)
    lanef = lane.astype(F32)
    logits = jnp.where(lane < N_EXPERTS, logits, NEG)
    m0 = jnp.max(logits, axis=-1, keepdims=True)
    e0 = jnp.min(jnp.where(logits == m0, lanef, float(LANE)), axis=-1, keepdims=True)
    oh0 = lanef == e0
    rest = jnp.where(oh0, NEG, logits)
    m1 = jnp.max(rest, axis=-1, keepdims=True)
    e1 = jnp.min(jnp.where(rest == m1, lanef, float(LANE)), axis=-1, keepdims=True)
    oh1 = lanef == e1
    z = jnp.exp(m1 - m0)
    w0 = 1.0 / (1.0 + z)
    w1 = z / (1.0 + z)
    both = jnp.where(oh0 | oh1, 1.0, 0.0)
    before = carry_scr[0:1, :] + jnp.dot(tri_ref[...], both.astype(BF16), preferred_element_type=F32)
    p0 = jnp.sum(jnp.where(oh0, before, 0.0), axis=-1, keepdims=True)
    p1 = jnp.sum(jnp.where(oh1, before, 0.0), axis=-1, keepdims=True)
    total = carry_scr[0:1, :] + jnp.sum(both, axis=0, keepdims=True)
    carry_scr[0:1, :] = total
    cnt_ref[...] = jnp.broadcast_to(total, cnt_ref.shape)
    meta = jnp.zeros(logits.shape, F32)
    for idx, val in ((M_E0, e0), (M_E1, e1), (M_P0, p0), (M_P1, p1), (M_W0, w0), (M_W1, w1)):
        meta = jnp.where(lane == idx, val, meta)
    meta_ref[...] = meta


def _router(x2, nw, w_router, tr):
    T = x2.shape[0]
    wr = jnp.pad(w_router.astype(F32), ((0, 0), (0, LANE - N_EXPERTS)))
    whi = wr.astype(BF16)
    wlo = (wr - whi.astype(F32)).astype(BF16)
    tri = jnp.asarray(np.tril(np.ones((tr, tr), np.float32), -1), BF16)
    const = lambda a: pl.BlockSpec(a.shape, lambda i: (0, 0))
    return pl.pallas_call(
        _router_kernel,
        grid=(T // tr,),
        in_specs=[pl.BlockSpec((tr, D_MODEL), lambda i: (i, 0)), const(nw), const(whi), const(wlo), const(tri)],
        out_specs=[
            pl.BlockSpec((tr, HALF_D), lambda i: (i, 0)),
            pl.BlockSpec((tr, LANE), lambda i: (i, 0)),
            pl.BlockSpec((8, LANE), lambda i: (0, 0)),
        ],
        out_shape=[
            jax.ShapeDtypeStruct((T, HALF_D), jnp.uint32),
            jax.ShapeDtypeStruct((T, LANE), F32),
            jax.ShapeDtypeStruct((8, LANE), F32),
        ],
        scratch_shapes=[pltpu.VMEM((8, LANE), F32)],
        compiler_params=_cparams(("arbitrary",)),
        name="router",
    )(x2, nw, whi, wlo, tri)


def _scatter_kernel(dest_ref, h_ref, xs_in, xs_hbm, sem, *, tt):
    del xs_in

    def row_copy(t, d_row):
        return pltpu.make_async_copy(h_ref.at[pl.ds(t, 1)], xs_hbm.at[pl.ds(d_row, 1)], sem)

    for t in range(tt):
        for j in range(TOP_K):
            row_copy(t, dest_ref[0, 0, TOP_K * t + j]).start(priority=j)

    for _ in range(TOP_K):
        pltpu.make_async_copy(h_ref, xs_hbm.at[pl.ds(0, tt)], sem).wait()


def _scatter_rows(dest, h, n_rows, tt):
    T = h.shape[0]
    width = h.shape[1]
    xs0 = jnp.zeros((n_rows, width), h.dtype)
    dest3 = dest.reshape(T // tt, 1, 2 * tt)
    return pl.pallas_call(
        functools.partial(_scatter_kernel, tt=tt),
        grid=(T // tt,),
        in_specs=[
            pl.BlockSpec((1, 1, 2 * tt), lambda i: (i, 0, 0), memory_space=pltpu.SMEM),
            pl.BlockSpec((tt, width), lambda i: (i, 0)),
            pl.BlockSpec(memory_space=pl.ANY),
        ],
        out_specs=pl.BlockSpec(memory_space=pl.ANY),
        out_shape=jax.ShapeDtypeStruct((n_rows, width), h.dtype),
        scratch_shapes=[pltpu.SemaphoreType.DMA(())],
        input_output_aliases={2: 0},
        compiler_params=_cparams(("arbitrary",)),
        name="moe_scatter",
    )(dest3, h, xs0)


def _moe_ffn_kernel(te_ref, na_ref, xs_ref, wg_ref, wu_ref, wd_ref, ys_ref, acc_scr):
    i = pl.program_id(0)
    f = pl.program_id(1)

    @pl.when(i < na_ref[0])
    def _():
        @pl.when(f == 0)
        def _():
            acc_scr[...] = jnp.zeros_like(acc_scr)

        acc_scr[...] += _swiglu_tile(_unpack_rows(xs_ref[...]), wg_ref[0].astype(BF16),
                                     wu_ref[0].astype(BF16), wd_ref[0].astype(BF16))

        @pl.when(f == pl.num_programs(1) - 1)
        def _():
            ys_ref[...] = acc_scr[...]

    @pl.when((i >= na_ref[0]) & (f == 0))
    def _():
        ys_ref[...] = jnp.zeros_like(ys_ref)


def _moe_ffn(tile_expert, n_active, xs, wg, wu, wd, tm, tf):
    n_rows = xs.shape[0]
    n_tiles = n_rows // tm
    nf = D_FF // tf

    def row_map(i, f, te, na):
        return (jnp.minimum(i, na[0] - 1), 0)

    def fcol(i, f, na):
        return jnp.where(i < na[0], f, nf - 1)

    grid_spec = pltpu.PrefetchScalarGridSpec(
        num_scalar_prefetch=2,
        grid=(n_tiles, nf),
        in_specs=[
            pl.BlockSpec((tm, HALF_D), row_map),
            pl.BlockSpec((1, D_MODEL, tf), lambda i, f, te, na: (te[i], 0, fcol(i, f, na))),
            pl.BlockSpec((1, D_MODEL, tf), lambda i, f, te, na: (te[i], 0, fcol(i, f, na))),
            pl.BlockSpec((1, tf, D_MODEL), lambda i, f, te, na: (te[i], fcol(i, f, na), 0)),
        ],
        out_specs=pl.BlockSpec((tm, D_MODEL), lambda i, f, te, na: (i, 0)),
        scratch_shapes=[pltpu.VMEM((tm, D_MODEL), F32)],
    )
    return pl.pallas_call(
        _moe_ffn_kernel,
        grid_spec=grid_spec,
        out_shape=jax.ShapeDtypeStruct((n_rows, D_MODEL), F32),
        compiler_params=_cparams(("arbitrary", "arbitrary")),
        name="moe_ffn",
    )(tile_expert, n_active, xs, wg, wu, wd)


def _combine_kernel(dest_ref, dnext_ref, x_ref, meta_ref, fw_ref, ys_hbm, o_ref, g_scr, sems, *, tc):
    i = pl.program_id(0)

    def gather(d_ref, half, s):
        for t in range(tc):
            for j in range(TOP_K):
                row = d_ref[0, 0, TOP_K * (half * tc + t) + j]
                pltpu.make_async_copy(ys_hbm.at[pl.ds(row, 1)], g_scr.at[s, j, pl.ds(t, 1)],
                                      sems.at[s]).start(priority=j)

    def wait(s):
        for j in range(TOP_K):
            pltpu.make_async_copy(ys_hbm.at[pl.ds(0, tc)], g_scr.at[s, j], sems.at[s]).wait()

    def combine(s):
        rows = pl.ds(s * tc, tc)
        meta = meta_ref[rows, :]
        w0 = meta[:, M_W0:M_W0 + 1]
        w1 = meta[:, M_W1:M_W1 + 1]
        x = x_ref[rows, :] + w0 * g_scr[s, 0] + w1 * g_scr[s, 1]
        ms = jnp.mean(x * x, axis=-1, keepdims=True)
        o_ref[rows, :] = x * lax.rsqrt(ms + RMS_EPS) * fw_ref[...]

    @pl.when(i == 0)
    def _():
        gather(dest_ref, 0, 0)

    gather(dest_ref, 1, 1)
    wait(0)
    combine(0)
    gather(dnext_ref, 0, 0)
    wait(1)
    combine(1)

    @pl.when(i == pl.num_programs(0) - 1)
    def _():
        wait(0)


def _combine(dest, x2, meta, fw, ys, tc):
    T = x2.shape[0]
    n = T // (2 * tc)
    dest3 = dest.reshape(n, 1, 2 * TOP_K * tc)
    dspec = lambda imap: pl.BlockSpec((1, 1, 2 * TOP_K * tc), imap, memory_space=pltpu.SMEM)
    return pl.pallas_call(
        functools.partial(_combine_kernel, tc=tc),
        grid=(n,),
        in_specs=[
            dspec(lambda i: (i, 0, 0)),
            dspec(lambda i: (jnp.minimum(i + 1, n - 1), 0, 0)),
            pl.BlockSpec((2 * tc, D_MODEL), lambda i: (i, 0)),
            pl.BlockSpec((2 * tc, LANE), lambda i: (i, 0)),
            pl.BlockSpec((1, D_MODEL), lambda i: (0, 0)),
            pl.BlockSpec(memory_space=pl.ANY),
        ],
        out_specs=pl.BlockSpec((2 * tc, D_MODEL), lambda i: (i, 0)),
        out_shape=jax.ShapeDtypeStruct((T, D_MODEL), F32),
        scratch_shapes=[pltpu.VMEM((2, TOP_K, tc, D_MODEL), F32), pltpu.SemaphoreType.DMA((2,))],
        compiler_params=_cparams(("arbitrary",)),
        name="moe_combine",
    )(dest3, dest3, x2, meta, fw, ys)


def _moe_plan(meta, counts, tm, n_tiles):
    cnt = counts[0, :N_EXPERTS].astype(jnp.int32)
    padded = ((cnt + tm - 1) // tm) * tm
    ends = jnp.cumsum(padded)
    offs = ends - padded
    e = meta[:, M_E0:M_E1 + 1].astype(jnp.int32)
    pos = meta[:, M_P0:M_P1 + 1].astype(jnp.int32)
    dest = offs[e] + pos
    starts = jnp.arange(n_tiles, dtype=jnp.int32) * tm
    n_active = (ends[-1] // tm).astype(jnp.int32)
    te = jnp.sum(starts[:, None] >= ends[None, :], axis=-1).astype(jnp.int32)
    last_e = jnp.sum(jnp.maximum(ends[-1] - tm, 0) >= ends).astype(jnp.int32)
    te = jnp.where(starts < ends[-1], te, last_e)
    return dest, te, n_active.reshape(1)


def _moe_layer(x_mid, nw, w_router, wg, wu, wd, final_w, tiles):
    T = x_mid.shape[0]
    tm = tiles["moe_tm"]
    n_tiles = (TOP_K * T) // tm + N_EXPERTS
    h, meta, counts = _router(x_mid, nw, w_router, tiles["router_t"])
    dest, te, n_active = _moe_plan(meta, counts, tm, n_tiles)
    xs = _scatter_rows(dest, h, n_tiles * tm, tiles["scatter_t"])
    ys = _moe_ffn(te, n_active, xs, wg, wu, wd, tm, tiles["moe_tf"])
    return _combine(dest, x_mid, meta, final_w, ys, tiles["combine_t"])


def _tiles(S, T):
    return dict(
        inproj_tm=min(1024, T), inproj_tn=1920,
        attn_t=min(512, S // 2), ret_c=min(256, S),
        merge_tm=min(512, T), ffn_tm=min(1024, T), ffn_tf=512,
        router_t=min(256, T), scatter_t=min(256, T), combine_t=min(256, T),
        moe_tm=min(1024, T), moe_tf=512,
    )


def kernel(x, norm_mix_w, w_in, fox_f_bias, ret_gn_w, mla_q_norm_w, mla_kv_norm_w, mla_w_uq, mla_w_uk, mla_w_uv, w_br_fox, w_br_ret, w_br_mla, w_out, norm_ffn_w, dense_w_gate, dense_w_up, dense_w_down, moe_w_router, moe_w_gate, moe_w_up, moe_w_down, final_norm_w):
    B, S, D = x.shape
    assert D == D_MODEL and w_in.shape[0] == DEPTH == 2
    T = B * S
    tl = _tiles(S, T)
    row = lambda a: a.reshape(1, -1).astype(F32)
    ret_tabs = _ret_tables(S, tl["ret_c"])
    mla_tabs = _mla_tables(S)
    x2 = x.reshape(T, D)
    w_all, w_ff = _build_inproj_weights(w_in)
    for l in range(DEPTH):
        proj, ff32 = _inproj(x2, row(norm_mix_w[l]), w_all[l], w_ff[l], tl["inproj_tm"], tl["inproj_tn"])
        fqt, fkx, fvt = _fox_prep(proj, ff32, fox_f_bias[l], B, S, tl["attn_t"])
        o_fox = _attention(fqt, fkx, fvt, B, S, tl["attn_t"], FOX_RANGES, "fox_attn")
        o_ret = _retention(proj, row(ret_gn_w[l]), ret_tabs, B, S, tl["ret_c"])
        wq, wqr, wk, wv = _mla_weights(mla_w_uq[l], mla_w_uk[l], mla_w_uv[l])
        mqt, mkx, mvt = _mla_prep(proj, row(mla_q_norm_w[l]), row(mla_kv_norm_w[l]), wq, wqr, wk, wv,
                                  mla_tabs, B, S, tl["attn_t"])
        o_mla = _attention(mqt, mkx, mvt, B, S, tl["attn_t"], MLA_RANGES, "mla_attn")
        x_mid, h2 = _merge(o_fox, o_ret, o_mla, proj, x2,
                           w_br_fox[l].astype(BF16), w_br_ret[l].astype(BF16), w_br_mla[l].astype(BF16),
                           w_out[l].astype(BF16), row(norm_ffn_w[l]), tl["merge_tm"])
        if l % 2 == 0:
            i = l // 2
            x2 = _dense_ffn(h2, x_mid, dense_w_gate[i].astype(BF16), dense_w_up[i].astype(BF16),
                            dense_w_down[i].astype(BF16), tl["ffn_tm"], tl["ffn_tf"])
        else:
            i = l // 2
            x2 = _moe_layer(x_mid, row(norm_ffn_w[l]), moe_w_router[i], moe_w_gate[i],
                            moe_w_up[i], moe_w_down[i], row(final_norm_w), tl)
    return x2.reshape(B, S, D)
```

```python
import functools

import numpy as np
import jax
import jax.numpy as jnp
from jax import lax
from jax.experimental import pallas as pl
from jax.experimental.pallas import tpu as pltpu

F32 = jnp.float32
BF16 = jnp.bfloat16

D_MODEL = 1024
DEPTH = 2
FOX_HEADS = 8
FOX_HEAD_DIM = 64
FOX_W = FOX_HEADS * FOX_HEAD_DIM
RET_HEADS = 8
RET_QK_DIM = 64
RET_V_DIM = 128
RET_QK_W = RET_HEADS * RET_QK_DIM
RET_V_W = RET_HEADS * RET_V_DIM
MLA_HEADS = 8
MLA_Q_RANK = 384
MLA_KV_RANK = 256
MLA_NOPE_DIM = 64
MLA_ROPE_DIM = 32
MLA_V_DIM = 64
MLA_QK_DIM = MLA_NOPE_DIM + MLA_ROPE_DIM
MLA_V_W = MLA_HEADS * MLA_V_DIM
N_BRANCHES = 3
ROPE_THETA = 10000.0
RMS_EPS = 1e-6
D_FF = 3584
N_EXPERTS = 8
TOP_K = 2
IN_SPLITS = (FOX_W, FOX_W, FOX_W, FOX_HEADS,
             RET_QK_W, RET_QK_W, RET_V_W, RET_V_W,
             MLA_Q_RANK, MLA_KV_RANK, MLA_ROPE_DIM,
             N_BRANCHES * D_MODEL)

LANE = 128
NEG = -1e30
VMEM_LIMIT = 56 * 1024 * 1024

U_FQ, U_FK, U_FV = 0, 4, 8
U_RQ, U_RQR, U_RK, U_RKR = 12, 16, 20, 24
U_CKV, U_KR, U_KRR = 28, 30, 31
U_RV, U_RG, U_GL, U_CQ = 32, 40, 48, 72
N_PROJ = 75 * LANE
FOX_EXT_STRIDE = 8
PAIRS = 4
QK_W = 2 * LANE
ATTN_KEY_CHUNK = 256
ATTN_STRIP = 256
VT_ROWS = 80
LOG2E = 1.4426950408889634


def _cparams(sem, vmem=VMEM_LIMIT, flags=None):
    return pltpu.CompilerParams(dimension_semantics=sem, vmem_limit_bytes=vmem, flags=flags)


def _sigmoid(x):
    return 1.0 / (1.0 + jnp.exp(-x))


def _lane_mask(shape, ranges):
    lane = lax.broadcasted_iota(jnp.int32, shape, len(shape) - 1)
    m = None
    for a, b in ranges:
        r = (lane >= a) & (lane < b)
        m = r if m is None else (m | r)
    return m


def _inproj_kernel(x_ref, nw_ref, w_ref, wff_ref, out_ref, ff_ref, h_scr):
    @pl.when(pl.program_id(1) == 0)
    def _():
        x = x_ref[...]
        ms = jnp.mean(x * x, axis=-1, keepdims=True)
        h = (x * lax.rsqrt(ms + RMS_EPS) * nw_ref[...]).astype(BF16)
        h_scr[...] = h
        ff_ref[...] = jnp.dot(h, wff_ref[...], preferred_element_type=F32)

    out_ref[...] = jnp.dot(h_scr[...], w_ref[...], preferred_element_type=F32).astype(out_ref.dtype)


def _inproj(x2, nw, w_all, w_ff, tm, tn):
    T = x2.shape[0]
    return pl.pallas_call(
        _inproj_kernel,
        grid=(T // tm, N_PROJ // tn),
        in_specs=[
            pl.BlockSpec((tm, D_MODEL), lambda i, j: (i, 0)),
            pl.BlockSpec((1, D_MODEL), lambda i, j: (0, 0)),
            pl.BlockSpec((D_MODEL, tn), lambda i, j: (0, j)),
            pl.BlockSpec((D_MODEL, LANE), lambda i, j: (0, 0)),
        ],
        out_specs=[
            pl.BlockSpec((tm, tn), lambda i, j: (i, j)),
            pl.BlockSpec((tm, LANE), lambda i, j: (i, 0)),
        ],
        out_shape=[
            jax.ShapeDtypeStruct((T, N_PROJ), BF16),
            jax.ShapeDtypeStruct((T, LANE), F32),
        ],
        scratch_shapes=[pltpu.VMEM((tm, D_MODEL), BF16)],
        compiler_params=_cparams(("parallel", "arbitrary")),
        name="inproj",
    )(x2, nw, w_all, w_ff)


def _rot_half_lanes(x, d):
    lane = lax.broadcasted_iota(jnp.int32, x.shape, 1)
    from_right = pltpu.roll(x, LANE - d // 2, 1)
    from_left = pltpu.roll(x, d // 2, 1)
    return jnp.where(lane % d < d // 2, -from_right, from_left)


def _wprep_kernel(w_ref, *out_refs):
    offs = [int(i) for i in np.cumsum((0,) + IN_SPLITS)]
    o_fq, o_fk, o_fv, o_ff, o_rq, o_rk, o_rv, o_rg, o_cq, o_ckv, o_kr, o_gl, _ = offs
    depth = w_ref.shape[0]
    lane = lax.broadcasted_iota(jnp.int32, (w_ref.shape[1], LANE), 1)
    rope_lanes = (lane >= MLA_NOPE_DIM) & (lane < MLA_QK_DIM)
    for l in range(depth):
        out_ref, ff_ref = out_refs[l], out_refs[depth + l]

        def put(unit, piece, out_ref=out_ref):
            out_ref[:, unit * LANE:unit * LANE + piece.shape[-1]] = piece.astype(out_ref.dtype)

        put(U_FQ, w_ref[l, :, o_fq:o_ff])
        ff_ref[...] = jnp.where(lane < FOX_HEADS, w_ref[l, :, o_ff:o_ff + LANE], 0.0).astype(ff_ref.dtype)
        for unit, unit_rot, start in ((U_RQ, U_RQR, o_rq), (U_RK, U_RKR, o_rk)):
            for c in range(RET_QK_W // LANE):
                slab = w_ref[l, :, start + c * LANE:start + (c + 1) * LANE]
                put(unit + c, slab)
                put(unit_rot + c, _rot_half_lanes(slab, RET_QK_DIM))
        put(U_CKV, w_ref[l, :, o_ckv:o_kr])
        kr128 = jnp.where(rope_lanes, w_ref[l, :, o_kr - MLA_NOPE_DIM:o_kr - MLA_NOPE_DIM + LANE], 0.0)
        put(U_KR, kr128)
        put(U_KRR, jnp.where(rope_lanes, _rot_half_lanes(kr128, MLA_ROPE_DIM), 0.0))
        put(U_RV, w_ref[l, :, o_rv:o_rg])
        put(U_RG, w_ref[l, :, o_rg:o_cq])
        put(U_GL, w_ref[l, :, o_gl:o_gl + N_BRANCHES * D_MODEL])
        put(U_CQ, w_ref[l, :, o_cq:o_ckv])


def _build_inproj_weights(w_in, rows=128):
    depth, _, d_in = w_in.shape
    outs = pl.pallas_call(
        _wprep_kernel,
        grid=(D_MODEL // rows,),
        in_specs=[pl.BlockSpec((depth, rows, d_in), lambda r: (0, r, 0))],
        out_specs=([pl.BlockSpec((rows, N_PROJ), lambda r: (r, 0))] * depth
                   + [pl.BlockSpec((rows, LANE), lambda r: (r, 0))] * depth),
        out_shape=([jax.ShapeDtypeStruct((D_MODEL, N_PROJ), BF16)] * depth
                   + [jax.ShapeDtypeStruct((D_MODEL, LANE), BF16)] * depth),
        compiler_params=_cparams(("parallel",)),
        name="inproj_weight_prep",
    )(w_in)
    return outs[:depth], outs[depth:]


def _split3(x):
    hi = x.astype(BF16)
    r1 = x - hi.astype(F32)
    mid = r1.astype(BF16)
    lo = (r1 - mid.astype(F32)).astype(BF16)
    return hi, mid, lo


def _store_t(dst_ref, idx, x):
    dst_ref[idx] = x.T.astype(dst_ref.dtype)


def _store_vt_pair(vt_ref, p, v2):
    hd = LANE // 2
    vt = v2.T.astype(vt_ref.dtype)
    pad_rows = lax.broadcasted_iota(jnp.int32, (VT_ROWS - hd, vt.shape[1]), 0)
    ones_then_zeros = jnp.where(pad_rows == 0, 1.0, 0.0).astype(vt_ref.dtype)
    for j in range(2):
        base = (2 * p + j) * VT_ROWS
        vt_ref[0, 0, base:base + hd, :] = vt[j * hd:(j + 1) * hd]
        vt_ref[0, 0, base + hd:base + VT_ROWS, :] = ones_then_zeros


def _fox_prep_kernel(fq_ref, fk_ref, fv_ref, ff_ref, bias_ref, tri_ref, pq_ref, pk_ref, cq_ref, ck_ref,
                     qt_ref, kx_ref, vt_ref, carry_scr):
    @pl.when(pl.program_id(1) == 0)
    def _():
        carry_scr[...] = jnp.zeros_like(carry_scr)

    ts = ff_ref.shape[0]
    z = ff_ref[...] + bias_ref[...]
    logf = jnp.minimum(z, 0.0) - jnp.log(1.0 + jnp.exp(-jnp.abs(z)))
    tri = tri_ref[...]
    cum = carry_scr[0:1, :]
    for part in _split3(logf):
        cum = cum + jnp.dot(tri, part, preferred_element_type=F32)
    carry_scr[0:1, :] = cum[ts - 1:ts, :]

    eq = cq_ref[...]
    ek = ck_ref[...]
    for j, part in enumerate(_split3(cum * LOG2E)):
        eq = eq + jnp.dot(part, pq_ref[j * LANE:(j + 1) * LANE, :], preferred_element_type=F32)
        ek = ek + jnp.dot(part, pk_ref[j * LANE:(j + 1) * LANE, :], preferred_element_type=F32)
    scale = FOX_HEAD_DIM ** -0.5 * LOG2E
    for p in range(FOX_HEADS // 2):
        src = slice(p * LANE, (p + 1) * LANE)
        _store_t(qt_ref, (0, slice(2 * p * LANE, (2 * p + 1) * LANE)), fq_ref[:, src].astype(F32) * scale)
        _store_t(qt_ref, (0, slice((2 * p + 1) * LANE, (2 * p + 2) * LANE)), eq[:, src])
        kx_ref[:, 2 * p * LANE:(2 * p + 1) * LANE] = fk_ref[:, src]
        kx_ref[:, (2 * p + 1) * LANE:(2 * p + 2) * LANE] = ek[:, src].astype(BF16)
        _store_vt_pair(vt_ref, p, fv_ref[:, src].astype(F32))


def _fox_prep_consts(ts):
    tri = np.tril(np.ones((ts, ts), np.float32))
    pq = np.zeros((3 * LANE, FOX_W), np.float32)
    pk = np.zeros((3 * LANE, FOX_W), np.float32)
    cq = np.zeros((1, FOX_W), np.float32)
    ck = np.zeros((1, FOX_W), np.float32)
    for h in range(FOX_HEADS):
        base = LANE * (h // 2) + FOX_EXT_STRIDE * (h % 2)
        for j in range(3):
            pq[j * LANE + h, base + j] = 1.0
            ck[0, base + j] = 1.0
            cq[0, base + 3 + j] = 1.0
            pk[j * LANE + h, base + 3 + j] = -1.0
    return (jnp.asarray(tri, BF16), jnp.asarray(pq, BF16), jnp.asarray(pk, BF16),
            jnp.asarray(cq), jnp.asarray(ck))


def _attn_operand_specs(B, S, t):
    T = B * S
    ns = S // t
    specs = [
        pl.BlockSpec((1, PAIRS * QK_W, t), lambda b, s: (b, 0, s)),
        pl.BlockSpec((t, PAIRS * QK_W), lambda b, s: (b * ns + s, 0)),
        pl.BlockSpec((1, 1, 2 * PAIRS * VT_ROWS, t), lambda b, s: (b, s, 0, 0)),
    ]
    shapes = [
        jax.ShapeDtypeStruct((B, PAIRS * QK_W, S), BF16),
        jax.ShapeDtypeStruct((T, PAIRS * QK_W), BF16),
        jax.ShapeDtypeStruct((B, ns, 2 * PAIRS * VT_ROWS, t), BF16),
    ]
    return specs, shapes


def _fox_prep(proj, ff32, bias, B, S, ts):
    ns = S // ts
    tri, pq, pk, cq, ck = _fox_prep_consts(ts)
    bias128 = jnp.pad(bias.astype(F32), (0, LANE - FOX_HEADS)).reshape(1, LANE)
    const = lambda shape: pl.BlockSpec(shape, lambda b, s: (0, 0))
    out_specs, out_shape = _attn_operand_specs(B, S, ts)
    return pl.pallas_call(
        _fox_prep_kernel,
        grid=(B, ns),
        in_specs=[
            pl.BlockSpec((ts, FOX_W), lambda b, s: (b * ns + s, U_FQ // 4)),
            pl.BlockSpec((ts, FOX_W), lambda b, s: (b * ns + s, U_FK // 4)),
            pl.BlockSpec((ts, FOX_W), lambda b, s: (b * ns + s, U_FV // 4)),
            pl.BlockSpec((ts, LANE), lambda b, s: (b * ns + s, 0)),
            const((1, LANE)), const((ts, ts)), const((3 * LANE, FOX_W)), const((3 * LANE, FOX_W)),
            const((1, FOX_W)), const((1, FOX_W)),
        ],
        out_specs=out_specs,
        out_shape=out_shape,
        scratch_shapes=[pltpu.VMEM((8, LANE), F32)],
        compiler_params=_cparams(("parallel", "arbitrary")),
        name="fox_prep",
    )(proj, proj, proj, ff32, bias128, tri, pq, pk, cq, ck)


def _row_mask(shape, ranges):
    row = lax.broadcasted_iota(jnp.int32, shape, 0)
    m = None
    for a, b in ranges:
        r = (row >= a) & (row < b)
        m = r if m is None else (m | r)
    return m


def _col_reduce(x, op, reduce_fn, parts=2):
    c = x.shape[0] // parts
    pieces = [x[i * c:(i + 1) * c] for i in range(parts)]
    while len(pieces) > 1:
        pieces = [op(pieces[i], pieces[i + 1]) for i in range(0, len(pieces), 2)]
    return reduce_fn(pieces[0], axis=0, keepdims=True)


def _attn_kernel(qt_ref, qn_ref, k_ref, vt_ref, o_ref, qh_scr, qn_scr, sa_scr, sb_scr, ma_scr, mb_scr, acc_scr,
                 *, tk, ranges):
    qi = pl.program_id(2)
    qt = qt_ref[0]
    tq = qt.shape[-1]
    for h in range(2):
        qh_scr[h] = jnp.where(_row_mask(qt.shape, ranges[h]), qt, jnp.zeros_like(qt))
    acc_scr[...] = jnp.zeros_like(acc_scr)

    def key_tile(kt):
        return k_ref[pl.ds(pl.multiple_of(kt * tk, tk), tk), :]

    strips = [(c, c + ATTN_STRIP) for c in range(0, tq, ATTN_STRIP)]

    def scores(kt, s_scr, m_scr):
        k = key_tile(kt)
        for h in range(2):
            for a, b in strips:
                s = jnp.dot(k, qh_scr[h, :, a:b], preferred_element_type=F32)
                s_scr[h, :, a:b] = s
                m_scr[h, :, a:b] = _col_reduce(s, jnp.maximum, jnp.max)

    def update(kt, h, s, m_tile, m_prev, lo):
        m_new = jnp.maximum(m_prev, m_tile)
        alpha = jnp.exp2(m_prev - m_new)
        hi = lo + s.shape[-1]
        pv = None
        for r in range(0, tk, ATTN_KEY_CHUNK):
            p = jnp.exp2((s[r:r + ATTN_KEY_CHUNK] - m_new).astype(BF16))
            vt = vt_ref[0, kt, h * VT_ROWS:(h + 1) * VT_ROWS, r:r + ATTN_KEY_CHUNK]
            d = jnp.dot(vt, p, preferred_element_type=F32)
            pv = d if pv is None else pv + d
        acc_scr[h, :, lo:hi] = alpha * acc_scr[h, :, lo:hi] + pv
        return m_new

    def diag_update(kt, h, s_scr, m_prev, a, b, lo):
        s = s_scr[h, :, a:b]
        if a < lo + tk:
            key = lax.broadcasted_iota(jnp.int32, s.shape, 0)
            qry = lax.broadcasted_iota(jnp.int32, s.shape, 1) + (a - lo)
            s = jnp.where(key <= qry, s, NEG)
        return update(kt, h, s, _col_reduce(s, jnp.maximum, jnp.max), m_prev, a)

    def diagonal_tiles(carry, lookahead):
        k_last = key_tile(2 * qi + 1)
        if lookahead:
            qn = qn_ref[0]
            k0 = key_tile(0)
            for h in range(2):
                qn_scr[h] = jnp.where(_row_mask(qn.shape, ranges[h]), qn, jnp.zeros_like(qn))
        out = []
        for h in range(2):
            ms = []
            for a, b in strips:
                if a >= tk:
                    sb_scr[h, :, a:b] = jnp.dot(k_last, qh_scr[h, :, a:b], preferred_element_type=F32)
                ms.append(diag_update(2 * qi, h, sa_scr, carry[h][:, a:b], a, b, 0))
                if lookahead:
                    s = jnp.dot(k0, qn_scr[h, :, a:b], preferred_element_type=F32)
                    sa_scr[h, :, a:b] = s
                    ma_scr[h, :, a:b] = _col_reduce(s, jnp.maximum, jnp.max)
            out.append(ms)
        for h in range(2):
            for i, (a, b) in enumerate(strips):
                if a >= tk:
                    diag_update(2 * qi + 1, h, sb_scr, out[h][i], a, b, tk)
        hd = LANE // 2
        ot = jnp.concatenate([acc_scr[h, :hd] / acc_scr[h, hd:hd + 1] for h in range(2)], axis=0)
        o_ref[...] = ot.T.astype(o_ref.dtype)

    def fused_step(kt, cur_s, cur_m, nxt_s, nxt_m, carry):
        k = key_tile(kt + 1)
        ms = [[], []]
        for a, b in strips:
            for h in range(2):
                s = jnp.dot(k, qh_scr[h, :, a:b], preferred_element_type=F32)
                nxt_s[h, :, a:b] = s
                nxt_m[h, :, a:b] = _col_reduce(s, jnp.maximum, jnp.max)
                ms[h].append(update(kt, h, cur_s[h, :, a:b], cur_m[h, :, a:b], carry[h][:, a:b], a))
        return tuple(jnp.concatenate(m, axis=-1) for m in ms)

    def body(j, carry):
        kt = 2 * j
        carry = fused_step(kt, sa_scr, ma_scr, sb_scr, mb_scr, carry)
        return fused_step(kt + 1, sb_scr, mb_scr, sa_scr, ma_scr, carry)

    init = tuple(jnp.full((1, tq), NEG, F32) for _ in range(2))

    @pl.when(qi == 0)
    def _():
        scores(0, sa_scr, ma_scr)

    carry = lax.fori_loop(0, qi, body, init)
    last = pl.num_programs(2) - 1

    @pl.when(qi < last)
    def _():
        diagonal_tiles(carry, True)

    @pl.when(qi == last)
    def _():
        diagonal_tiles(carry, False)


def _attention(qt, kx, vt, B, S, tk, ranges, name):
    T = B * S
    tq = 2 * tk
    nq = S // tq
    kern = functools.partial(_attn_kernel, tk=tk, ranges=ranges)
    return pl.pallas_call(
        kern,
        grid=(B, PAIRS, nq),
        in_specs=[
            pl.BlockSpec((1, QK_W, tq), lambda b, p, i: (b, p, i)),
            pl.BlockSpec((1, QK_W, tq), lambda b, p, i: (b, p, jnp.minimum(i + 1, nq - 1))),
            pl.BlockSpec((S, QK_W), lambda b, p, i: (b, p)),
            pl.BlockSpec((1, S // tk, 2 * VT_ROWS, tk), lambda b, p, i: (b, 0, p, 0)),
        ],
        out_specs=pl.BlockSpec((tq, LANE), lambda b, p, i: (b * nq + i, p)),
        out_shape=jax.ShapeDtypeStruct((T, PAIRS * LANE), BF16),
        scratch_shapes=[
            pltpu.VMEM((2, QK_W, tq), BF16),
            pltpu.VMEM((2, QK_W, tq), BF16),
            pltpu.VMEM((2, tk, tq), F32),
            pltpu.VMEM((2, tk, tq), F32),
            pltpu.VMEM((2, 1, tq), F32),
            pltpu.VMEM((2, 1, tq), F32),
            pltpu.VMEM((2, VT_ROWS, tq), F32),
        ],
        compiler_params=_cparams(("parallel", "parallel", "arbitrary")),
        name=name,
    )(qt, qt, kx, vt)


_E0 = LANE
FOX_RANGES = (((0, 64), (_E0, _E0 + FOX_EXT_STRIDE)),
              ((64, 128), (_E0 + FOX_EXT_STRIDE, _E0 + 2 * FOX_EXT_STRIDE)))
MLA_RANGES = (((0, LANE),), ((LANE, 2 * LANE),))


def _mla_prep_kernel(cq_ref, ckv_ref, kr_ref, krr_ref, qnw_ref, kvnw_ref, wq_ref, wqr_ref, wk_ref, wv_ref,
                     cq_tab, sq_tab, ck_tab, sk_tab, qt_ref, k_ref, vt_ref):
    def norm(x_ref, w_ref):
        x = x_ref[...].astype(F32)
        ms = jnp.mean(x * x, axis=-1, keepdims=True)
        return (x * lax.rsqrt(ms + RMS_EPS) * w_ref[...]).astype(BF16)

    c_q = norm(cq_ref, qnw_ref)
    c_kv = norm(ckv_ref, kvnw_ref)
    tile8 = lambda a: jnp.concatenate([a] * MLA_HEADS, axis=-1)
    q = (jnp.dot(c_q, wq_ref[...], preferred_element_type=F32) * tile8(cq_tab[...])
         + jnp.dot(c_q, wqr_ref[...], preferred_element_type=F32) * tile8(sq_tab[...]))
    for c in range(MLA_HEADS):
        _store_t(qt_ref, (0, slice(c * LANE, (c + 1) * LANE)), q[:, c * LANE:(c + 1) * LANE])
    k_rope = kr_ref[...].astype(F32) * ck_tab[...] + krr_ref[...].astype(F32) * sk_tab[...]
    k = jnp.dot(c_kv, wk_ref[...], preferred_element_type=F32) + tile8(k_rope)
    k_ref[...] = k.astype(BF16)
    v = jnp.dot(c_kv, wv_ref[...], preferred_element_type=F32)
    for p in range(PAIRS):
        _store_vt_pair(vt_ref, p, v[:, p * LANE:(p + 1) * LANE])


def _mla_weights(w_uq, w_uk, w_uv):
    q3 = w_uq.reshape(MLA_Q_RANK, MLA_HEADS, MLA_QK_DIM)
    nope, ropep = q3[..., :MLA_NOPE_DIM], q3[..., MLA_NOPE_DIM:]
    half = MLA_ROPE_DIM // 2
    rot = jnp.concatenate([-ropep[..., half:], ropep[..., :half]], axis=-1)
    z32 = jnp.zeros((MLA_Q_RANK, MLA_HEADS, LANE - MLA_QK_DIM), F32)
    z64 = jnp.zeros((MLA_Q_RANK, MLA_HEADS, MLA_NOPE_DIM), F32)
    wq = jnp.concatenate([nope, ropep, z32], axis=-1).reshape(MLA_Q_RANK, MLA_HEADS * LANE)
    wqr = jnp.concatenate([z64, rot, z32], axis=-1).reshape(MLA_Q_RANK, MLA_HEADS * LANE)
    k3 = w_uk.reshape(MLA_KV_RANK, MLA_HEADS, MLA_NOPE_DIM)
    wk = jnp.concatenate([k3, jnp.zeros_like(k3)], axis=-1).reshape(MLA_KV_RANK, MLA_HEADS * LANE)
    return wq.astype(BF16), wqr.astype(BF16), wk.astype(BF16), w_uv.astype(BF16)


def _rope_inv_freq(d):
    return (np.float32(ROPE_THETA) ** (-np.arange(0, d, 2, dtype=np.float32) / np.float32(d))).astype(np.float32)


def _mla_tables(S):
    inv = _rope_inv_freq(MLA_ROPE_DIM)
    inv128 = np.concatenate([np.zeros(MLA_NOPE_DIM, np.float32), inv, inv, np.zeros(LANE - MLA_QK_DIM, np.float32)])
    keep = (np.arange(LANE) < MLA_QK_DIM).astype(np.float32)
    ang = jnp.arange(S, dtype=F32)[:, None] * jnp.asarray(inv128)[None, :]
    ctab = jnp.cos(ang) * jnp.asarray(keep)[None, :]
    stab = jnp.sin(ang)
    scale = MLA_QK_DIM ** -0.5 * LOG2E
    return ctab * scale, stab * scale, ctab, stab


def _mla_prep(proj, qnw, kvnw, wq, wqr, wk, wv, tabs, B, S, tm):
    ns = S // tm
    HW = MLA_HEADS * LANE
    const = lambda shape: pl.BlockSpec(shape, lambda b, s: (0, 0))
    tab = pl.BlockSpec((tm, LANE), lambda b, s: (s, 0))
    out_specs, out_shape = _attn_operand_specs(B, S, tm)
    return pl.pallas_call(
        _mla_prep_kernel,
        grid=(B, ns),
        in_specs=[
            pl.BlockSpec((tm, MLA_Q_RANK), lambda b, s: (b * ns + s, U_CQ // 3)),
            pl.BlockSpec((tm, MLA_KV_RANK), lambda b, s: (b * ns + s, U_CKV // 2)),
            pl.BlockSpec((tm, LANE), lambda b, s: (b * ns + s, U_KR)),
            pl.BlockSpec((tm, LANE), lambda b, s: (b * ns + s, U_KRR)),
            const((1, MLA_Q_RANK)), const((1, MLA_KV_RANK)),
            const((MLA_Q_RANK, HW)), const((MLA_Q_RANK, HW)), const((MLA_KV_RANK, HW)),
            const((MLA_KV_RANK, MLA_V_W)),
            tab, tab, tab, tab,
        ],
        out_specs=out_specs,
        out_shape=out_shape,
        compiler_params=_cparams(("parallel", "parallel")),
        name="mla_prep",
    )(proj, proj, proj, proj, qnw, kvnw, wq, wqr, wk, wv, *tabs)


def _ret_kernel(rq_ref, rqr_ref, rk_ref, rkr_ref, rv_ref, rg_ref, cos_ref, sin_ref,
                din_ref, xi_ref, zeta_ref, gch_ref, gnw_ref, o_ref, r_scr):
    @pl.when(pl.program_id(0) == 0)
    def _():
        r_scr[...] = jnp.zeros_like(r_scr)

    tile4 = lambda a: jnp.concatenate([a] * (RET_QK_W // LANE), axis=-1)
    cos4 = tile4(cos_ref[...])
    sin4 = tile4(sin_ref[...])
    for bi in range(rq_ref.shape[0]):
        q = rq_ref[bi].astype(F32) * cos4 + rqr_ref[bi].astype(F32) * sin4
        k = (rk_ref[bi].astype(F32) * cos4 + rkr_ref[bi].astype(F32) * sin4) * (RET_QK_DIM ** -0.5)
        for p in range(RET_HEADS // 2):
            q2 = q[:, p * LANE:(p + 1) * LANE]
            k2 = k[:, p * LANE:(p + 1) * LANE]
            k2b = k2.astype(BF16)
            kz2 = (k2 * zeta_ref[p]).astype(BF16)
            r2 = r_scr[bi, p]
            r2b = r2.astype(BF16)
            new_r = gch_ref[p] * r2
            for j in range(2):
                h = 2 * p + j
                half = ((j * RET_QK_DIM, (j + 1) * RET_QK_DIM),)
                qm = jnp.where(_lane_mask(q2.shape, half), q2, 0.0).astype(BF16)
                inner = lax.dot_general(qm, k2b, (((1,), (1,)), ((), ())), preferred_element_type=F32)
                inner = inner * din_ref[h]
                vh = rv_ref[bi, :, h * LANE:(h + 1) * LANE]
                o = (jnp.dot(inner.astype(BF16), vh, preferred_element_type=F32)
                     + jnp.dot(qm, r2b, preferred_element_type=F32) * xi_ref[h])
                upd = lax.dot_general(kz2, vh, (((0,), (0,)), ((), ())), preferred_element_type=F32)
                rows = lax.broadcasted_iota(jnp.int32, upd.shape, 0)
                new_r = new_r + jnp.where((rows >= half[0][0]) & (rows < half[0][1]), upd, 0.0)
                mu = jnp.mean(o, axis=-1, keepdims=True)
                d = o - mu
                var = jnp.mean(d * d, axis=-1, keepdims=True)
                on = d * lax.rsqrt(var + RMS_EPS) * gnw_ref[:, h * LANE:(h + 1) * LANE]
                g = rg_ref[bi, :, h * LANE:(h + 1) * LANE].astype(F32)
                o_ref[bi, :, h * LANE:(h + 1) * LANE] = (g * _sigmoid(g) * on).astype(o_ref.dtype)
            r_scr[bi, p] = new_r


def _ret_tables(S, C):
    inv128 = np.tile(_rope_inv_freq(RET_QK_DIM), 2 * LANE // RET_QK_DIM)
    ang = jnp.arange(S, dtype=F32)[:, None] * jnp.asarray(inv128)[None, :]
    cos = jnp.cos(ang)
    sin = jnp.sin(ang)
    gammas = 1.0 - 2.0 ** (-5.0 - jnp.arange(RET_HEADS, dtype=F32))
    log_g = jnp.log(gammas)
    j = jnp.arange(C, dtype=F32)
    diff = j[:, None] - j[None, :]
    din = jnp.where(diff[None] >= 0, jnp.exp(jnp.maximum(diff, 0.0)[None] * log_g[:, None, None]), 0.0)
    xi = jnp.exp((j[None, :] + 1.0) * log_g[:, None])
    zeta = jnp.exp((C - 1.0 - j[None, :]) * log_g[:, None])
    gch = jnp.exp(C * log_g)
    xi_t = jnp.broadcast_to(xi[:, :, None], (RET_HEADS, C, LANE))
    zeta_p = jnp.repeat(zeta.reshape(RET_HEADS // 2, 2, C).transpose(0, 2, 1), RET_QK_DIM, axis=-1)
    gch_p = jnp.broadcast_to(jnp.repeat(gch.reshape(RET_HEADS // 2, 2), RET_QK_DIM, axis=-1)[:, :, None],
                             (RET_HEADS // 2, LANE, LANE))
    return cos, sin, din, xi_t, zeta_p, gch_p


def _retention(proj, gnw, tabs, B, S, C):
    nc = S // C
    cos, sin, din, xi, zeta, gch = tabs
    proj3 = proj.reshape(B, S, N_PROJ)
    blk = lambda w, unit: pl.BlockSpec((B, C, w), lambda c: (0, c, unit * LANE // w))
    full = lambda a: pl.BlockSpec(a.shape, lambda c: (0,) * a.ndim)
    out = pl.pallas_call(
        _ret_kernel,
        grid=(nc,),
        in_specs=[
            blk(RET_QK_W, U_RQ), blk(RET_QK_W, U_RQR), blk(RET_QK_W, U_RK), blk(RET_QK_W, U_RKR),
            blk(RET_V_W, U_RV), blk(RET_V_W, U_RG),
            pl.BlockSpec((C, LANE), lambda c: (c, 0)),
            pl.BlockSpec((C, LANE), lambda c: (c, 0)),
            full(din), full(xi), full(zeta), full(gch), full(gnw),
        ],
        out_specs=pl.BlockSpec((B, C, RET_V_W), lambda c: (0, c, 0)),
        out_shape=jax.ShapeDtypeStruct((B, S, RET_V_W), BF16),
        scratch_shapes=[pltpu.VMEM((B, RET_HEADS // 2, LANE, LANE), F32)],
        compiler_params=_cparams(("arbitrary",)),
        name="retention",
    )(proj3, proj3, proj3, proj3, proj3, proj3, cos, sin, din, xi, zeta, gch, gnw)
    return out.reshape(B * S, RET_V_W)


def _merge_kernel(of_ref, or_ref, om_ref, g0_ref, g1_ref, g2_ref, x_ref,
                  wf_ref, wr_ref, wm_ref, wo_ref, nw_ref, xo_ref, h_ref):
    merged = (_sigmoid(g0_ref[...].astype(F32)) * jnp.dot(of_ref[...], wf_ref[...], preferred_element_type=F32)
              + _sigmoid(g1_ref[...].astype(F32)) * jnp.dot(or_ref[...], wr_ref[...], preferred_element_type=F32)
              + _sigmoid(g2_ref[...].astype(F32)) * jnp.dot(om_ref[...], wm_ref[...], preferred_element_type=F32))
    x = x_ref[...] + jnp.dot(merged.astype(BF16), wo_ref[...], preferred_element_type=F32)
    xo_ref[...] = x
    ms = jnp.mean(x * x, axis=-1, keepdims=True)
    h_ref[...] = (x * lax.rsqrt(ms + RMS_EPS) * nw_ref[...]).astype(h_ref.dtype)


def _merge(o_fox, o_ret, o_mla, proj, x2, wf, wr, wm, wo, nw, tm):
    T = x2.shape[0]
    row = lambda w, cb=0: pl.BlockSpec((tm, w), lambda i: (i, cb))
    const = lambda a: pl.BlockSpec(a.shape, lambda i: (0, 0))
    g_unit = U_GL // 8
    return pl.pallas_call(
        _merge_kernel,
        grid=(T // tm,),
        in_specs=[
            row(FOX_W), row(RET_V_W), row(MLA_V_W),
            row(D_MODEL, g_unit), row(D_MODEL, g_unit + 1), row(D_MODEL, g_unit + 2),
            row(D_MODEL),
            const(wf), const(wr), const(wm), const(wo), const(nw),
        ],
        out_specs=[row(D_MODEL), row(D_MODEL)],
        out_shape=[jax.ShapeDtypeStruct((T, D_MODEL), F32), jax.ShapeDtypeStruct((T, D_MODEL), BF16)],
        compiler_params=_cparams(("parallel",)),
        name="merge",
    )(o_fox, o_ret, o_mla, proj, proj, proj, x2, wf, wr, wm, wo, nw)


def _swiglu_tile(h, wg, wu, wd):
    a = jnp.dot(h, wg, preferred_element_type=F32)
    b = jnp.dot(h, wu, preferred_element_type=F32)
    return jnp.dot((a * _sigmoid(a) * b).astype(BF16), wd, preferred_element_type=F32)


def _dense_ffn_kernel(h_ref, wg_ref, wu_ref, wd_ref, x_ref, o_ref, acc_scr):
    f = pl.program_id(1)

    @pl.when(f == 0)
    def _():
        acc_scr[...] = jnp.zeros_like(acc_scr)

    acc_scr[...] += _swiglu_tile(h_ref[...], wg_ref[...], wu_ref[...], wd_ref[...])

    @pl.when(f == pl.num_programs(1) - 1)
    def _():
        o_ref[...] = x_ref[...] + acc_scr[...]


def _dense_ffn(h, x2, wg, wu, wd, tm, tf):
    T = x2.shape[0]
    return pl.pallas_call(
        _dense_ffn_kernel,
        grid=(T // tm, D_FF // tf),
        in_specs=[
            pl.BlockSpec((tm, D_MODEL), lambda i, f: (i, 0)),
            pl.BlockSpec((D_MODEL, tf), lambda i, f: (0, f)),
            pl.BlockSpec((D_MODEL, tf), lambda i, f: (0, f)),
            pl.BlockSpec((tf, D_MODEL), lambda i, f: (f, 0)),
            pl.BlockSpec((tm, D_MODEL), lambda i, f: (i, 0)),
        ],
        out_specs=pl.BlockSpec((tm, D_MODEL), lambda i, f: (i, 0)),
        out_shape=jax.ShapeDtypeStruct((T, D_MODEL), F32),
        scratch_shapes=[pltpu.VMEM((tm, D_MODEL), F32)],
        compiler_params=_cparams(("parallel", "arbitrary")),
        name="dense_ffn",
    )(h, wg, wu, wd, x2)


M_E0, M_E1, M_P0, M_P1, M_W0, M_W1 = 0, 1, 2, 3, 4, 5
ROUTE_ROWS = 8


def _split2(x):
    hi = x.astype(BF16)
    lo = (x - hi.astype(F32)).astype(BF16)
    return hi, lo


HALF_D = D_MODEL // 2


def _pack_rows(h):
    bits = pltpu.bitcast(h.astype(BF16).astype(F32), jnp.uint32)
    return (bits[:, :HALF_D] & jnp.uint32(0xFFFF0000)) | (bits[:, HALF_D:] >> 16)


def _unpack_rows(u):
    left = pltpu.bitcast(u & jnp.uint32(0xFFFF0000), F32)
    right = pltpu.bitcast(u << 16, F32)
    return jnp.concatenate([left, right], axis=-1).astype(BF16)


def _router_kernel(x_ref, nw_ref, whi_ref, wlo_ref, tri_ref, h_ref, meta_ref, route_ref, cnt_ref, carry_scr):
    @pl.when(pl.program_id(0) == 0)
    def _():
        carry_scr[...] = jnp.zeros_like(carry_scr)

    x = x_ref[...]
    ms = jnp.mean(x * x, axis=-1, keepdims=True)
    h = x * lax.rsqrt(ms + RMS_EPS) * nw_ref[...]
    h_ref[...] = _pack_rows(h)
    hhi, hlo = _split2(h)
    logits = (jnp.dot(hhi, whi_ref[...], preferred_element_type=F32)
              + jnp.dot(hhi, wlo_ref[...], preferred_element_type=F32)
              + jnp.dot(hlo, whi_ref[...], preferred_element_type=F32))
    lane = lax.broadcasted_iota(jnp.int32, logits.shape, 1)
    lanef = lane.astype(F32)
    logits = jnp.where(lane < N_EXPERTS, logits, NEG)
    m0 = jnp.max(logits, axis=-1, keepdims=True)
    e0 = jnp.min(jnp.where(logits == m0, lanef, float(LANE)), axis=-1, keepdims=True)
    oh0 = lanef == e0
    rest = jnp.where(oh0, NEG, logits)
    m1 = jnp.max(rest, axis=-1, keepdims=True)
    e1 = jnp.min(jnp.where(rest == m1, lanef, float(LANE)), axis=-1, keepdims=True)
    oh1 = lanef == e1
    z = jnp.exp(m1 - m0)
    w0 = 1.0 / (1.0 + z)
    w1 = z / (1.0 + z)
    both = jnp.where(oh0 | oh1, 1.0, 0.0)
    before = carry_scr[0:1, :] + jnp.dot(tri_ref[...], both.astype(BF16), preferred_element_type=F32)
    p0 = jnp.sum(jnp.where(oh0, before, 0.0), axis=-1, keepdims=True)
    p1 = jnp.sum(jnp.where(oh1, before, 0.0), axis=-1, keepdims=True)
    total = carry_scr[0:1, :] + jnp.sum(both, axis=0, keepdims=True)
    carry_scr[0:1, :] = total
    cnt_ref[...] = jnp.broadcast_to(total, cnt_ref.shape)
    meta = jnp.zeros(logits.shape, F32)
    for idx, val in ((M_E0, e0), (M_E1, e1), (M_P0, p0), (M_P1, p1), (M_W0, w0), (M_W1, w1)):
        meta = jnp.where(lane == idx, val, meta)
    meta_ref[...] = meta
    route_ref[0] = meta.T[:ROUTE_ROWS, :]


def _router(x2, nw, w_router, tr):
    T = x2.shape[0]
    wr = jnp.pad(w_router.astype(F32), ((0, 0), (0, LANE - N_EXPERTS)))
    whi = wr.astype(BF16)
    wlo = (wr - whi.astype(F32)).astype(BF16)
    tri = jnp.asarray(np.tril(np.ones((tr, tr), np.float32), -1), BF16)
    const = lambda a: pl.BlockSpec(a.shape, lambda i: (0, 0))
    return pl.pallas_call(
        _router_kernel,
        grid=(T // tr,),
        in_specs=[pl.BlockSpec((tr, D_MODEL), lambda i: (i, 0)), const(nw), const(whi), const(wlo), const(tri)],
        out_specs=[
            pl.BlockSpec((tr, HALF_D), lambda i: (i, 0)),
            pl.BlockSpec((tr, LANE), lambda i: (i, 0)),
            pl.BlockSpec((1, ROUTE_ROWS, tr), lambda i: (i, 0, 0)),
            pl.BlockSpec((8, LANE), lambda i: (0, 0)),
        ],
        out_shape=[
            jax.ShapeDtypeStruct((T, HALF_D), jnp.uint32),
            jax.ShapeDtypeStruct((T, LANE), F32),
            jax.ShapeDtypeStruct((T // tr, ROUTE_ROWS, tr), F32),
            jax.ShapeDtypeStruct((8, LANE), F32),
        ],
        scratch_shapes=[pltpu.VMEM((8, LANE), F32)],
        compiler_params=_cparams(("arbitrary",)),
        name="router",
    )(x2, nw, whi, wlo, tri)


def _scatter_kernel(dest_ref, h_ref, xs_in, xs_hbm, sem, *, tt):
    del xs_in

    def row_copy(t, d_row):
        return pltpu.make_async_copy(h_ref.at[pl.ds(t, 1)], xs_hbm.at[pl.ds(d_row, 1)], sem)

    for t in range(tt):
        for j in range(TOP_K):
            row_copy(t, dest_ref[0, 0, j * tt + t]).start(priority=j)

    for _ in range(TOP_K):
        pltpu.make_async_copy(h_ref, xs_hbm.at[pl.ds(0, tt)], sem).wait()


def _scatter_rows(dest, h, n_rows, tt):
    T = h.shape[0]
    width = h.shape[1]
    xs0 = jnp.zeros((n_rows, width), h.dtype)
    assert dest.shape == (T // tt, TOP_K, tt)
    dest3 = dest.reshape(T // tt, 1, TOP_K * tt)
    return pl.pallas_call(
        functools.partial(_scatter_kernel, tt=tt),
        grid=(T // tt,),
        in_specs=[
            pl.BlockSpec((1, 1, 2 * tt), lambda i: (i, 0, 0), memory_space=pltpu.SMEM),
            pl.BlockSpec((tt, width), lambda i: (i, 0)),
            pl.BlockSpec(memory_space=pl.ANY),
        ],
        out_specs=pl.BlockSpec(memory_space=pl.ANY),
        out_shape=jax.ShapeDtypeStruct((n_rows, width), h.dtype),
        scratch_shapes=[pltpu.SemaphoreType.DMA(())],
        input_output_aliases={2: 0},
        compiler_params=_cparams(("arbitrary",)),
        name="moe_scatter",
    )(dest3, h, xs0)


def _moe_ffn_kernel(te_ref, na_ref, xs_ref, wg_ref, wu_ref, wd_ref, ys_ref, acc_scr):
    i = pl.program_id(0)
    f = pl.program_id(1)

    @pl.when(i < na_ref[0])
    def _():
        @pl.when(f == 0)
        def _():
            acc_scr[...] = jnp.zeros_like(acc_scr)

        acc_scr[...] += _swiglu_tile(_unpack_rows(xs_ref[...]), wg_ref[0].astype(BF16),
                                     wu_ref[0].astype(BF16), wd_ref[0].astype(BF16))

        @pl.when(f == pl.num_programs(1) - 1)
        def _():
            ys_ref[...] = acc_scr[...]

    @pl.when((i >= na_ref[0]) & (f == 0))
    def _():
        ys_ref[...] = jnp.zeros_like(ys_ref)


def _moe_ffn(tile_expert, n_active, xs, wg, wu, wd, tm, tf):
    n_rows = xs.shape[0]
    n_tiles = n_rows // tm
    nf = D_FF // tf

    def row_map(i, f, te, na):
        return (jnp.minimum(i, na[0] - 1), 0)

    def fcol(i, f, na):
        return jnp.where(i < na[0], f, nf - 1)

    grid_spec = pltpu.PrefetchScalarGridSpec(
        num_scalar_prefetch=2,
        grid=(n_tiles, nf),
        in_specs=[
            pl.BlockSpec((tm, HALF_D), row_map),
            pl.BlockSpec((1, D_MODEL, tf), lambda i, f, te, na: (te[i], 0, fcol(i, f, na))),
            pl.BlockSpec((1, D_MODEL, tf), lambda i, f, te, na: (te[i], 0, fcol(i, f, na))),
            pl.BlockSpec((1, tf, D_MODEL), lambda i, f, te, na: (te[i], fcol(i, f, na), 0)),
        ],
        out_specs=pl.BlockSpec((tm, D_MODEL), lambda i, f, te, na: (i, 0)),
        scratch_shapes=[pltpu.VMEM((tm, D_MODEL), F32)],
    )
    return pl.pallas_call(
        _moe_ffn_kernel,
        grid_spec=grid_spec,
        out_shape=jax.ShapeDtypeStruct((n_rows, D_MODEL), F32),
        compiler_params=_cparams(("arbitrary", "arbitrary")),
        name="moe_ffn",
    )(tile_expert, n_active, xs, wg, wu, wd)


def _combine_kernel(dest_ref, dnext_ref, x_ref, meta_ref, fw_ref, ys_hbm, o_ref, g_scr, sems, *, tc):
    i = pl.program_id(0)

    def gather(d_ref, half, s):
        for t in range(tc):
            for j in range(TOP_K):
                row = d_ref[0, 0, (half * TOP_K + j) * tc + t]
                pltpu.make_async_copy(ys_hbm.at[pl.ds(row, 1)], g_scr.at[s, j, pl.ds(t, 1)],
                                      sems.at[s]).start(priority=j)

    def wait(s):
        for j in range(TOP_K):
            pltpu.make_async_copy(ys_hbm.at[pl.ds(0, tc)], g_scr.at[s, j], sems.at[s]).wait()

    def combine(s):
        rows = pl.ds(s * tc, tc)
        meta = meta_ref[rows, :]
        w0 = meta[:, M_W0:M_W0 + 1]
        w1 = meta[:, M_W1:M_W1 + 1]
        x = x_ref[rows, :] + w0 * g_scr[s, 0] + w1 * g_scr[s, 1]
        ms = jnp.mean(x * x, axis=-1, keepdims=True)
        o_ref[rows, :] = x * lax.rsqrt(ms + RMS_EPS) * fw_ref[...]

    @pl.when(i == 0)
    def _():
        gather(dest_ref, 0, 0)

    gather(dest_ref, 1, 1)
    wait(0)
    combine(0)
    gather(dnext_ref, 0, 0)
    wait(1)
    combine(1)

    @pl.when(i == pl.num_programs(0) - 1)
    def _():
        wait(0)


def _combine(dest, x2, meta, fw, ys, tc):
    T = x2.shape[0]
    n = T // (2 * tc)
    assert dest.shape == (2 * n, TOP_K, tc)
    dest3 = dest.reshape(n, 1, 2 * TOP_K * tc)
    dspec = lambda imap: pl.BlockSpec((1, 1, 2 * TOP_K * tc), imap, memory_space=pltpu.SMEM)
    return pl.pallas_call(
        functools.partial(_combine_kernel, tc=tc),
        grid=(n,),
        in_specs=[
            dspec(lambda i: (i, 0, 0)),
            dspec(lambda i: (jnp.minimum(i + 1, n - 1), 0, 0)),
            pl.BlockSpec((2 * tc, D_MODEL), lambda i: (i, 0)),
            pl.BlockSpec((2 * tc, LANE), lambda i: (i, 0)),
            pl.BlockSpec((1, D_MODEL), lambda i: (0, 0)),
            pl.BlockSpec(memory_space=pl.ANY),
        ],
        out_specs=pl.BlockSpec((2 * tc, D_MODEL), lambda i: (i, 0)),
        out_shape=jax.ShapeDtypeStruct((T, D_MODEL), F32),
        scratch_shapes=[pltpu.VMEM((2, TOP_K, tc, D_MODEL), F32), pltpu.SemaphoreType.DMA((2,))],
        compiler_params=_cparams(("arbitrary",)),
        name="moe_combine",
    )(dest3, dest3, x2, meta, fw, ys)


def _moe_plan(route, counts, tm, n_tiles):
    cnt = counts[0, :N_EXPERTS].astype(jnp.int32)
    padded = ((cnt + tm - 1) // tm) * tm
    ends = jnp.cumsum(padded)
    offs = ends - padded
    e = route[:, M_E0:M_E1 + 1, :].astype(jnp.int32)
    pos = route[:, M_P0:M_P1 + 1, :].astype(jnp.int32)
    dest = offs[e] + pos
    starts = jnp.arange(n_tiles, dtype=jnp.int32) * tm
    n_active = (ends[-1] // tm).astype(jnp.int32)
    te = jnp.sum(starts[:, None] >= ends[None, :], axis=-1).astype(jnp.int32)
    last_e = jnp.sum(jnp.maximum(ends[-1] - tm, 0) >= ends).astype(jnp.int32)
    te = jnp.where(starts < ends[-1], te, last_e)
    return dest, te, n_active.reshape(1)


def _moe_layer(x_mid, nw, w_router, wg, wu, wd, final_w, tiles):
    T = x_mid.shape[0]
    tm = tiles["moe_tm"]
    n_tiles = (TOP_K * T) // tm + N_EXPERTS
    tr = tiles["moe_row_t"]
    h, meta, route, counts = _router(x_mid, nw, w_router, tr)
    dest, te, n_active = _moe_plan(route, counts, tm, n_tiles)
    xs = _scatter_rows(dest, h, n_tiles * tm, tr)
    ys = _moe_ffn(te, n_active, xs, wg, wu, wd, tm, tiles["moe_tf"])
    return _combine(dest, x_mid, meta, final_w, ys, tr)


def _tiles(S, T):
    return dict(
        inproj_tm=min(1024, T), inproj_tn=1920,
        attn_t=min(512, S // 2), ret_c=min(256, S),
        merge_tm=min(512, T), ffn_tm=min(1024, T), ffn_tf=512,
        moe_row_t=min(256, T // 2),
        moe_tm=min(1024, T), moe_tf=512,
    )


def kernel(x, norm_mix_w, w_in, fox_f_bias, ret_gn_w, mla_q_norm_w, mla_kv_norm_w, mla_w_uq, mla_w_uk, mla_w_uv, w_br_fox, w_br_ret, w_br_mla, w_out, norm_ffn_w, dense_w_gate, dense_w_up, dense_w_down, moe_w_router, moe_w_gate, moe_w_up, moe_w_down, final_norm_w):
    B, S, D = x.shape
    assert D == D_MODEL and w_in.shape[0] == DEPTH == 2
    T = B * S
    tl = _tiles(S, T)
    row = lambda a: a.reshape(1, -1).astype(F32)
    ret_tabs = _ret_tables(S, tl["ret_c"])
    mla_tabs = _mla_tables(S)
    x2 = x.reshape(T, D)
    w_all, w_ff = _build_inproj_weights(w_in)
    for l in range(DEPTH):
        proj, ff32 = _inproj(x2, row(norm_mix_w[l]), w_all[l], w_ff[l], tl["inproj_tm"], tl["inproj_tn"])
        fqt, fkx, fvt = _fox_prep(proj, ff32, fox_f_bias[l], B, S, tl["attn_t"])
        o_fox = _attention(fqt, fkx, fvt, B, S, tl["attn_t"], FOX_RANGES, "fox_attn")
        o_ret = _retention(proj, row(ret_gn_w[l]), ret_tabs, B, S, tl["ret_c"])
        wq, wqr, wk, wv = _mla_weights(mla_w_uq[l], mla_w_uk[l], mla_w_uv[l])
        mqt, mkx, mvt = _mla_prep(proj, row(mla_q_norm_w[l]), row(mla_kv_norm_w[l]), wq, wqr, wk, wv,
                                  mla_tabs, B, S, tl["attn_t"])
        o_mla = _attention(mqt, mkx, mvt, B, S, tl["attn_t"], MLA_RANGES, "mla_attn")
        x_mid, h2 = _merge(o_fox, o_ret, o_mla, proj, x2,
                           w_br_fox[l].astype(BF16), w_br_ret[l].astype(BF16), w_br_mla[l].astype(BF16),
                           w_out[l].astype(BF16), row(norm_ffn_w[l]), tl["merge_tm"])
        if l % 2 == 0:
            i = l // 2
            x2 = _dense_ffn(h2, x_mid, dense_w_gate[i].astype(BF16), dense_w_up[i].astype(BF16),
                            dense_w_down[i].astype(BF16), tl["ffn_tm"], tl["ffn_tf"])
        else:
            i = l // 2
            x2 = _moe_layer(x_mid, row(norm_ffn_w[l]), moe_w_router[i], moe_w_gate[i],
                            moe_w_up[i], moe_w_down[i], row(final_norm_w), tl)
    return x2.reshape(B, S, D)
```

```python
import functools

import numpy as np
import jax
import jax.numpy as jnp
from jax import lax
from jax.experimental import pallas as pl
from jax.experimental.pallas import tpu as pltpu

F32 = jnp.float32
BF16 = jnp.bfloat16

D_MODEL = 1024
DEPTH = 2
FOX_HEADS = 8
FOX_HEAD_DIM = 64
FOX_W = FOX_HEADS * FOX_HEAD_DIM
RET_HEADS = 8
RET_QK_DIM = 64
RET_V_DIM = 128
RET_QK_W = RET_HEADS * RET_QK_DIM
RET_V_W = RET_HEADS * RET_V_DIM
MLA_HEADS = 8
MLA_Q_RANK = 384
MLA_KV_RANK = 256
MLA_NOPE_DIM = 64
MLA_ROPE_DIM = 32
MLA_V_DIM = 64
MLA_QK_DIM = MLA_NOPE_DIM + MLA_ROPE_DIM
MLA_V_W = MLA_HEADS * MLA_V_DIM
N_BRANCHES = 3
ROPE_THETA = 10000.0
RMS_EPS = 1e-6
D_FF = 3584
N_EXPERTS = 8
TOP_K = 2
IN_SPLITS = (FOX_W, FOX_W, FOX_W, FOX_HEADS,
             RET_QK_W, RET_QK_W, RET_V_W, RET_V_W,
             MLA_Q_RANK, MLA_KV_RANK, MLA_ROPE_DIM,
             N_BRANCHES * D_MODEL)

LANE = 128
NEG = -1e30
VMEM_LIMIT = 56 * 1024 * 1024

U_FQ, U_FK, U_FV = 0, 4, 8
U_RQ, U_RQR, U_RK, U_RKR = 12, 16, 20, 24
U_CKV, U_KR, U_KRR = 28, 30, 31
U_RV, U_RG, U_GL, U_CQ = 32, 40, 48, 72
N_PROJ = 75 * LANE
FOX_EXT_STRIDE = 8
PAIRS = 4
QK_W = 2 * LANE
ATTN_KEY_CHUNK = 256
ATTN_STRIP = 256
VT_ROWS = 80
LOG2E = 1.4426950408889634


def _cparams(sem, vmem=VMEM_LIMIT, flags=None):
    return pltpu.CompilerParams(dimension_semantics=sem, vmem_limit_bytes=vmem, flags=flags)


def _sigmoid(x):
    return 1.0 / (1.0 + jnp.exp(-x))


def _lane_mask(shape, ranges):
    lane = lax.broadcasted_iota(jnp.int32, shape, len(shape) - 1)
    m = None
    for a, b in ranges:
        r = (lane >= a) & (lane < b)
        m = r if m is None else (m | r)
    return m


def _inproj_kernel(x_ref, nw_ref, w_ref, wff_ref, out_ref, ff_ref, h_scr):
    @pl.when(pl.program_id(1) == 0)
    def _():
        x = x_ref[...]
        ms = jnp.mean(x * x, axis=-1, keepdims=True)
        h = (x * lax.rsqrt(ms + RMS_EPS) * nw_ref[...]).astype(BF16)
        h_scr[...] = h
        ff_ref[...] = jnp.dot(h, wff_ref[...], preferred_element_type=F32)

    out_ref[...] = jnp.dot(h_scr[...], w_ref[...], preferred_element_type=F32).astype(out_ref.dtype)


def _inproj(x2, nw, w_all, w_ff, tm, tn):
    T = x2.shape[0]
    return pl.pallas_call(
        _inproj_kernel,
        grid=(T // tm, N_PROJ // tn),
        in_specs=[
            pl.BlockSpec((tm, D_MODEL), lambda i, j: (i, 0)),
            pl.BlockSpec((1, D_MODEL), lambda i, j: (0, 0)),
            pl.BlockSpec((D_MODEL, tn), lambda i, j: (0, j)),
            pl.BlockSpec((D_MODEL, LANE), lambda i, j: (0, 0)),
        ],
        out_specs=[
            pl.BlockSpec((tm, tn), lambda i, j: (i, j)),
            pl.BlockSpec((tm, LANE), lambda i, j: (i, 0)),
        ],
        out_shape=[
            jax.ShapeDtypeStruct((T, N_PROJ), BF16),
            jax.ShapeDtypeStruct((T, LANE), F32),
        ],
        scratch_shapes=[pltpu.VMEM((tm, D_MODEL), BF16)],
        compiler_params=_cparams(("parallel", "arbitrary")),
        name="inproj",
    )(x2, nw, w_all, w_ff)


def _rot_half_lanes(x, d):
    lane = lax.broadcasted_iota(jnp.int32, x.shape, 1)
    from_right = pltpu.roll(x, LANE - d // 2, 1)
    from_left = pltpu.roll(x, d // 2, 1)
    return jnp.where(lane % d < d // 2, -from_right, from_left)


def _wprep_kernel(w_ref, *out_refs):
    offs = [int(i) for i in np.cumsum((0,) + IN_SPLITS)]
    o_fq, o_fk, o_fv, o_ff, o_rq, o_rk, o_rv, o_rg, o_cq, o_ckv, o_kr, o_gl, _ = offs
    depth = w_ref.shape[0]
    lane = lax.broadcasted_iota(jnp.int32, (w_ref.shape[1], LANE), 1)
    rope_lanes = (lane >= MLA_NOPE_DIM) & (lane < MLA_QK_DIM)
    for l in range(depth):
        out_ref, ff_ref = out_refs[l], out_refs[depth + l]

        def put(unit, piece, out_ref=out_ref):
            out_ref[:, unit * LANE:unit * LANE + piece.shape[-1]] = piece.astype(out_ref.dtype)

        put(U_FQ, w_ref[l, :, o_fq:o_ff])
        ff_ref[...] = jnp.where(lane < FOX_HEADS, w_ref[l, :, o_ff:o_ff + LANE], 0.0).astype(ff_ref.dtype)
        for unit, unit_rot, start in ((U_RQ, U_RQR, o_rq), (U_RK, U_RKR, o_rk)):
            for c in range(RET_QK_W // LANE):
                slab = w_ref[l, :, start + c * LANE:start + (c + 1) * LANE]
                put(unit + c, slab)
                put(unit_rot + c, _rot_half_lanes(slab, RET_QK_DIM))
        put(U_CKV, w_ref[l, :, o_ckv:o_kr])
        kr128 = jnp.where(rope_lanes, w_ref[l, :, o_kr - MLA_NOPE_DIM:o_kr - MLA_NOPE_DIM + LANE], 0.0)
        put(U_KR, kr128)
        put(U_KRR, jnp.where(rope_lanes, _rot_half_lanes(kr128, MLA_ROPE_DIM), 0.0))
        put(U_RV, w_ref[l, :, o_rv:o_rg])
        put(U_RG, w_ref[l, :, o_rg:o_cq])
        put(U_GL, w_ref[l, :, o_gl:o_gl + N_BRANCHES * D_MODEL])
        put(U_CQ, w_ref[l, :, o_cq:o_ckv])


def _build_inproj_weights(w_in, rows=128):
    depth, _, d_in = w_in.shape
    outs = pl.pallas_call(
        _wprep_kernel,
        grid=(D_MODEL // rows,),
        in_specs=[pl.BlockSpec((depth, rows, d_in), lambda r: (0, r, 0))],
        out_specs=([pl.BlockSpec((rows, N_PROJ), lambda r: (r, 0))] * depth
                   + [pl.BlockSpec((rows, LANE), lambda r: (r, 0))] * depth),
        out_shape=([jax.ShapeDtypeStruct((D_MODEL, N_PROJ), BF16)] * depth
                   + [jax.ShapeDtypeStruct((D_MODEL, LANE), BF16)] * depth),
        compiler_params=_cparams(("parallel",)),
        name="inproj_weight_prep",
    )(w_in)
    return outs[:depth], outs[depth:]


def _split3(x):
    hi = x.astype(BF16)
    r1 = x - hi.astype(F32)
    mid = r1.astype(BF16)
    lo = (r1 - mid.astype(F32)).astype(BF16)
    return hi, mid, lo


def _store_t(dst_ref, idx, x):
    dst_ref[idx] = x.T.astype(dst_ref.dtype)


def _store_vt_pair(vt_ref, p, v2):
    hd = LANE // 2
    vt = v2.T.astype(vt_ref.dtype)
    pad_rows = lax.broadcasted_iota(jnp.int32, (VT_ROWS - hd, vt.shape[1]), 0)
    ones_then_zeros = jnp.where(pad_rows == 0, 1.0, 0.0).astype(vt_ref.dtype)
    for j in range(2):
        base = (2 * p + j) * VT_ROWS
        vt_ref[0, 0, base:base + hd, :] = vt[j * hd:(j + 1) * hd]
        vt_ref[0, 0, base + hd:base + VT_ROWS, :] = ones_then_zeros


def _fox_prep_kernel(fq_ref, fk_ref, fv_ref, ff_ref, bias_ref, tri_ref, pq_ref, pk_ref, cq_ref, ck_ref,
                     qt_ref, kx_ref, vt_ref, carry_scr):
    @pl.when(pl.program_id(1) == 0)
    def _():
        carry_scr[...] = jnp.zeros_like(carry_scr)

    ts = ff_ref.shape[0]
    z = ff_ref[...] + bias_ref[...]
    logf = jnp.minimum(z, 0.0) - jnp.log(1.0 + jnp.exp(-jnp.abs(z)))
    tri = tri_ref[...]
    cum = carry_scr[0:1, :]
    for part in _split3(logf):
        cum = cum + jnp.dot(tri, part, preferred_element_type=F32)
    carry_scr[0:1, :] = cum[ts - 1:ts, :]

    cum3 = jnp.concatenate(_split3(cum * LOG2E), axis=-1)
    eq = cq_ref[...] + jnp.dot(cum3, pq_ref[...], preferred_element_type=F32)
    ek = ck_ref[...] + jnp.dot(cum3, pk_ref[...], preferred_element_type=F32)
    scale = FOX_HEAD_DIM ** -0.5 * LOG2E
    lane = lax.broadcasted_iota(jnp.int32, (ts, LANE), 1)
    for p in range(FOX_HEADS // 2):
        src = slice(p * LANE, (p + 1) * LANE)
        q2 = fq_ref[:, src].astype(F32) * scale
        k2 = fk_ref[:, src].astype(F32)
        for j in range(2):
            h = 2 * p + j
            dst = slice(h * LANE, (h + 1) * LANE)
            qh = q2 if j == 0 else pltpu.roll(q2, LANE // 2, 1)
            kh = k2 if j == 0 else pltpu.roll(k2, LANE // 2, 1)
            own = (lane >= FOX_HEAD_DIM + FOX_EXT_STRIDE * h) & (lane < FOX_HEAD_DIM + FOX_EXT_STRIDE * (h + 1))
            _store_t(qt_ref, (0, dst), jnp.where(lane < FOX_HEAD_DIM, qh, jnp.where(own, eq, 0.0)))
            kx_ref[:, dst] = jnp.where(lane < FOX_HEAD_DIM, kh, ek).astype(BF16)
        _store_vt_pair(vt_ref, p, fv_ref[:, src].astype(F32))


def _fox_prep_consts(ts):
    tri = np.tril(np.ones((ts, ts), np.float32))
    pq = np.zeros((3 * LANE, LANE), np.float32)
    pk = np.zeros((3 * LANE, LANE), np.float32)
    cq = np.zeros((1, LANE), np.float32)
    ck = np.zeros((1, LANE), np.float32)
    for h in range(FOX_HEADS):
        base = FOX_HEAD_DIM + FOX_EXT_STRIDE * h
        for j in range(3):
            pq[j * LANE + h, base + j] = 1.0
            ck[0, base + j] = 1.0
            cq[0, base + 3 + j] = 1.0
            pk[j * LANE + h, base + 3 + j] = -1.0
    return (jnp.asarray(tri, BF16), jnp.asarray(pq, BF16), jnp.asarray(pk, BF16),
            jnp.asarray(cq), jnp.asarray(ck))


def _attn_operand_specs(B, S, t):
    T = B * S
    ns = S // t
    specs = [
        pl.BlockSpec((1, PAIRS * QK_W, t), lambda b, s: (b, 0, s)),
        pl.BlockSpec((t, PAIRS * QK_W), lambda b, s: (b * ns + s, 0)),
        pl.BlockSpec((1, 1, 2 * PAIRS * VT_ROWS, t), lambda b, s: (b, s, 0, 0)),
    ]
    shapes = [
        jax.ShapeDtypeStruct((B, PAIRS * QK_W, S), BF16),
        jax.ShapeDtypeStruct((T, PAIRS * QK_W), BF16),
        jax.ShapeDtypeStruct((B, ns, 2 * PAIRS * VT_ROWS, t), BF16),
    ]
    return specs, shapes


def _fox_prep(proj, ff32, bias, B, S, ts):
    ns = S // ts
    tri, pq, pk, cq, ck = _fox_prep_consts(ts)
    bias128 = jnp.pad(bias.astype(F32), (0, LANE - FOX_HEADS)).reshape(1, LANE)
    const = lambda shape: pl.BlockSpec(shape, lambda b, s: (0, 0))
    out_specs, out_shape = _attn_operand_specs(B, S, ts)
    return pl.pallas_call(
        _fox_prep_kernel,
        grid=(B, ns),
        in_specs=[
            pl.BlockSpec((ts, FOX_W), lambda b, s: (b * ns + s, U_FQ // 4)),
            pl.BlockSpec((ts, FOX_W), lambda b, s: (b * ns + s, U_FK // 4)),
            pl.BlockSpec((ts, FOX_W), lambda b, s: (b * ns + s, U_FV // 4)),
            pl.BlockSpec((ts, LANE), lambda b, s: (b * ns + s, 0)),
            const((1, LANE)), const((ts, ts)), const(pq.shape), const(pk.shape), const(cq.shape), const(ck.shape),
        ],
        out_specs=out_specs,
        out_shape=out_shape,
        scratch_shapes=[pltpu.VMEM((8, LANE), F32)],
        compiler_params=_cparams(("parallel", "arbitrary")),
        name="fox_prep",
    )(proj, proj, proj, ff32, bias128, tri, pq, pk, cq, ck)


def _col_reduce(x, op, reduce_fn, parts=2):
    c = x.shape[0] // parts
    pieces = [x[i * c:(i + 1) * c] for i in range(parts)]
    while len(pieces) > 1:
        pieces = [op(pieces[i], pieces[i + 1]) for i in range(0, len(pieces), 2)]
    return reduce_fn(pieces[0], axis=0, keepdims=True)


def _attn_kernel(qt_ref, qn_ref, k_ref, vt_ref, o_ref, sa_scr, sb_scr, ma_scr, mb_scr, acc_scr, *, tk):
    qi = pl.program_id(2)
    tq = qt_ref.shape[-1]
    acc_scr[...] = jnp.zeros_like(acc_scr)

    def key_tile(kt):
        return k_ref[pl.ds(pl.multiple_of(kt * tk, tk), tk), :]

    strips = [(c, c + ATTN_STRIP) for c in range(0, tq, ATTN_STRIP)]

    def scores(kt, s_scr, m_scr):
        k = key_tile(kt)
        for h in range(2):
            for a, b in strips:
                s = jnp.dot(k[:, h * LANE:(h + 1) * LANE], qt_ref[0, h * LANE:(h + 1) * LANE, a:b],
                            preferred_element_type=F32)
                s_scr[h, :, a:b] = s
                m_scr[h, :, a:b] = _col_reduce(s, jnp.maximum, jnp.max)

    def update(kt, h, s, m_tile, m_prev, lo):
        m_new = jnp.maximum(m_prev, m_tile)
        alpha = jnp.exp2(m_prev - m_new)
        hi = lo + s.shape[-1]
        pv = None
        for r in range(0, tk, ATTN_KEY_CHUNK):
            p = jnp.exp2((s[r:r + ATTN_KEY_CHUNK] - m_new).astype(BF16))
            vt = vt_ref[0, kt, h * VT_ROWS:(h + 1) * VT_ROWS, r:r + ATTN_KEY_CHUNK]
            d = jnp.dot(vt, p, preferred_element_type=F32)
            pv = d if pv is None else pv + d
        acc_scr[h, :, lo:hi] = alpha * acc_scr[h, :, lo:hi] + pv
        return m_new

    def diag_update(kt, h, s_scr, m_prev, a, b, lo):
        s = s_scr[h, :, a:b]
        if a < lo + tk:
            key = lax.broadcasted_iota(jnp.int32, s.shape, 0)
            qry = lax.broadcasted_iota(jnp.int32, s.shape, 1) + (a - lo)
            s = jnp.where(key <= qry, s, NEG)
        return update(kt, h, s, _col_reduce(s, jnp.maximum, jnp.max), m_prev, a)

    def diagonal_tiles(carry, lookahead):
        k_last = key_tile(2 * qi + 1)
        if lookahead:
            k0 = key_tile(0)
        out = []
        for h in range(2):
            ms = []
            for a, b in strips:
                if a >= tk:
                    sb_scr[h, :, a:b] = jnp.dot(k_last[:, h * LANE:(h + 1) * LANE],
                                                qt_ref[0, h * LANE:(h + 1) * LANE, a:b],
                                                preferred_element_type=F32)
                ms.append(diag_update(2 * qi, h, sa_scr, carry[h][:, a:b], a, b, 0))
                if lookahead:
                    s = jnp.dot(k0[:, h * LANE:(h + 1) * LANE], qn_ref[0, h * LANE:(h + 1) * LANE, a:b],
                                preferred_element_type=F32)
                    sa_scr[h, :, a:b] = s
                    ma_scr[h, :, a:b] = _col_reduce(s, jnp.maximum, jnp.max)
            out.append(ms)
        for h in range(2):
            for i, (a, b) in enumerate(strips):
                if a >= tk:
                    diag_update(2 * qi + 1, h, sb_scr, out[h][i], a, b, tk)
        hd = LANE // 2
        ot = jnp.concatenate([acc_scr[h, :hd] / acc_scr[h, hd:hd + 1] for h in range(2)], axis=0)
        o_ref[...] = ot.T.astype(o_ref.dtype)

    def fused_step(kt, cur_s, cur_m, nxt_s, nxt_m, carry):
        k = key_tile(kt + 1)
        ms = [[], []]
        for a, b in strips:
            for h in range(2):
                s = jnp.dot(k[:, h * LANE:(h + 1) * LANE], qt_ref[0, h * LANE:(h + 1) * LANE, a:b],
                            preferred_element_type=F32)
                nxt_s[h, :, a:b] = s
                nxt_m[h, :, a:b] = _col_reduce(s, jnp.maximum, jnp.max)
                ms[h].append(update(kt, h, cur_s[h, :, a:b], cur_m[h, :, a:b], carry[h][:, a:b], a))
        return tuple(jnp.concatenate(m, axis=-1) for m in ms)

    def body(j, carry):
        kt = 2 * j
        carry = fused_step(kt, sa_scr, ma_scr, sb_scr, mb_scr, carry)
        return fused_step(kt + 1, sb_scr, mb_scr, sa_scr, ma_scr, carry)

    init = tuple(jnp.full((1, tq), NEG, F32) for _ in range(2))

    @pl.when(qi == 0)
    def _():
        scores(0, sa_scr, ma_scr)

    carry = lax.fori_loop(0, qi, body, init)
    last = pl.num_programs(2) - 1

    @pl.when(qi < last)
    def _():
        diagonal_tiles(carry, True)

    @pl.when(qi == last)
    def _():
        diagonal_tiles(carry, False)


def _attention(qt, kx, vt, B, S, tk, name):
    T = B * S
    tq = 2 * tk
    nq = S // tq
    kern = functools.partial(_attn_kernel, tk=tk)
    return pl.pallas_call(
        kern,
        grid=(B, PAIRS, nq),
        in_specs=[
            pl.BlockSpec((1, QK_W, tq), lambda b, p, i: (b, p, i)),
            pl.BlockSpec((1, QK_W, tq), lambda b, p, i: (b, p, jnp.minimum(i + 1, nq - 1))),
            pl.BlockSpec((S, QK_W), lambda b, p, i: (b, p)),
            pl.BlockSpec((1, S // tk, 2 * VT_ROWS, tk), lambda b, p, i: (b, 0, p, 0)),
        ],
        out_specs=pl.BlockSpec((tq, LANE), lambda b, p, i: (b * nq + i, p)),
        out_shape=jax.ShapeDtypeStruct((T, PAIRS * LANE), BF16),
        scratch_shapes=[
            pltpu.VMEM((2, tk, tq), F32),
            pltpu.VMEM((2, tk, tq), F32),
            pltpu.VMEM((2, 1, tq), F32),
            pltpu.VMEM((2, 1, tq), F32),
            pltpu.VMEM((2, VT_ROWS, tq), F32),
        ],
        compiler_params=_cparams(("parallel", "parallel", "arbitrary")),
        name=name,
    )(qt, qt, kx, vt)


def _mla_prep_kernel(cq_ref, ckv_ref, kr_ref, krr_ref, qnw_ref, kvnw_ref, wq_ref, wqr_ref, wk_ref, wv_ref,
                     cq_tab, sq_tab, ck_tab, sk_tab, qt_ref, k_ref, vt_ref):
    def norm(x_ref, w_ref):
        x = x_ref[...].astype(F32)
        ms = jnp.mean(x * x, axis=-1, keepdims=True)
        return (x * lax.rsqrt(ms + RMS_EPS) * w_ref[...]).astype(BF16)

    c_q = norm(cq_ref, qnw_ref)
    c_kv = norm(ckv_ref, kvnw_ref)
    tile8 = lambda a: jnp.concatenate([a] * MLA_HEADS, axis=-1)
    q = (jnp.dot(c_q, wq_ref[...], preferred_element_type=F32) * tile8(cq_tab[...])
         + jnp.dot(c_q, wqr_ref[...], preferred_element_type=F32) * tile8(sq_tab[...]))
    for c in range(MLA_HEADS):
        _store_t(qt_ref, (0, slice(c * LANE, (c + 1) * LANE)), q[:, c * LANE:(c + 1) * LANE])
    k_rope = kr_ref[...].astype(F32) * ck_tab[...] + krr_ref[...].astype(F32) * sk_tab[...]
    k = jnp.dot(c_kv, wk_ref[...], preferred_element_type=F32) + tile8(k_rope)
    k_ref[...] = k.astype(BF16)
    v = jnp.dot(c_kv, wv_ref[...], preferred_element_type=F32)
    for p in range(PAIRS):
        _store_vt_pair(vt_ref, p, v[:, p * LANE:(p + 1) * LANE])


def _mla_weights(w_uq, w_uk, w_uv):
    q3 = w_uq.reshape(MLA_Q_RANK, MLA_HEADS, MLA_QK_DIM)
    nope, ropep = q3[..., :MLA_NOPE_DIM], q3[..., MLA_NOPE_DIM:]
    half = MLA_ROPE_DIM // 2
    rot = jnp.concatenate([-ropep[..., half:], ropep[..., :half]], axis=-1)
    z32 = jnp.zeros((MLA_Q_RANK, MLA_HEADS, LANE - MLA_QK_DIM), F32)
    z64 = jnp.zeros((MLA_Q_RANK, MLA_HEADS, MLA_NOPE_DIM), F32)
    wq = jnp.concatenate([nope, ropep, z32], axis=-1).reshape(MLA_Q_RANK, MLA_HEADS * LANE)
    wqr = jnp.concatenate([z64, rot, z32], axis=-1).reshape(MLA_Q_RANK, MLA_HEADS * LANE)
    k3 = w_uk.reshape(MLA_KV_RANK, MLA_HEADS, MLA_NOPE_DIM)
    wk = jnp.concatenate([k3, jnp.zeros_like(k3)], axis=-1).reshape(MLA_KV_RANK, MLA_HEADS * LANE)
    return wq.astype(BF16), wqr.astype(BF16), wk.astype(BF16), w_uv.astype(BF16)


def _rope_inv_freq(d):
    return (np.float32(ROPE_THETA) ** (-np.arange(0, d, 2, dtype=np.float32) / np.float32(d))).astype(np.float32)


def _mla_tables(S):
    inv = _rope_inv_freq(MLA_ROPE_DIM)
    inv128 = np.concatenate([np.zeros(MLA_NOPE_DIM, np.float32), inv, inv, np.zeros(LANE - MLA_QK_DIM, np.float32)])
    keep = (np.arange(LANE) < MLA_QK_DIM).astype(np.float32)
    ang = jnp.arange(S, dtype=F32)[:, None] * jnp.asarray(inv128)[None, :]
    ctab = jnp.cos(ang) * jnp.asarray(keep)[None, :]
    stab = jnp.sin(ang)
    scale = MLA_QK_DIM ** -0.5 * LOG2E
    return ctab * scale, stab * scale, ctab, stab


def _mla_prep(proj, qnw, kvnw, wq, wqr, wk, wv, tabs, B, S, tm):
    ns = S // tm
    HW = MLA_HEADS * LANE
    const = lambda shape: pl.BlockSpec(shape, lambda b, s: (0, 0))
    tab = pl.BlockSpec((tm, LANE), lambda b, s: (s, 0))
    out_specs, out_shape = _attn_operand_specs(B, S, tm)
    return pl.pallas_call(
        _mla_prep_kernel,
        grid=(B, ns),
        in_specs=[
            pl.BlockSpec((tm, MLA_Q_RANK), lambda b, s: (b * ns + s, U_CQ // 3)),
            pl.BlockSpec((tm, MLA_KV_RANK), lambda b, s: (b * ns + s, U_CKV // 2)),
            pl.BlockSpec((tm, LANE), lambda b, s: (b * ns + s, U_KR)),
            pl.BlockSpec((tm, LANE), lambda b, s: (b * ns + s, U_KRR)),
            const((1, MLA_Q_RANK)), const((1, MLA_KV_RANK)),
            const((MLA_Q_RANK, HW)), const((MLA_Q_RANK, HW)), const((MLA_KV_RANK, HW)),
            const((MLA_KV_RANK, MLA_V_W)),
            tab, tab, tab, tab,
        ],
        out_specs=out_specs,
        out_shape=out_shape,
        compiler_params=_cparams(("parallel", "parallel")),
        name="mla_prep",
    )(proj, proj, proj, proj, qnw, kvnw, wq, wqr, wk, wv, *tabs)


def _ret_kernel(rq_ref, rqr_ref, rk_ref, rkr_ref, rv_ref, rg_ref, cos_ref, sin_ref,
                din_ref, xi_ref, zeta_ref, gch_ref, gnw_ref, o_ref, r_scr):
    @pl.when(pl.program_id(0) == 0)
    def _():
        r_scr[...] = jnp.zeros_like(r_scr)

    tile4 = lambda a: jnp.concatenate([a] * (RET_QK_W // LANE), axis=-1)
    cos4 = tile4(cos_ref[...])
    sin4 = tile4(sin_ref[...])
    for bi in range(rq_ref.shape[0]):
        q = rq_ref[bi].astype(F32) * cos4 + rqr_ref[bi].astype(F32) * sin4
        k = (rk_ref[bi].astype(F32) * cos4 + rkr_ref[bi].astype(F32) * sin4) * (RET_QK_DIM ** -0.5)
        for p in range(RET_HEADS // 2):
            q2 = q[:, p * LANE:(p + 1) * LANE]
            k2 = k[:, p * LANE:(p + 1) * LANE]
            k2b = k2.astype(BF16)
            kz2 = (k2 * zeta_ref[p]).astype(BF16)
            r2 = r_scr[bi, p]
            r2b = r2.astype(BF16)
            new_r = gch_ref[p] * r2
            for j in range(2):
                h = 2 * p + j
                half = ((j * RET_QK_DIM, (j + 1) * RET_QK_DIM),)
                qm = jnp.where(_lane_mask(q2.shape, half), q2, 0.0).astype(BF16)
                inner = lax.dot_general(qm, k2b, (((1,), (1,)), ((), ())), preferred_element_type=F32)
                inner = inner * din_ref[h]
                vh = rv_ref[bi, :, h * LANE:(h + 1) * LANE]
                o = (jnp.dot(inner.astype(BF16), vh, preferred_element_type=F32)
                     + jnp.dot(qm, r2b, preferred_element_type=F32) * xi_ref[h])
                upd = lax.dot_general(kz2, vh, (((0,), (0,)), ((), ())), preferred_element_type=F32)
                rows = lax.broadcasted_iota(jnp.int32, upd.shape, 0)
                new_r = new_r + jnp.where((rows >= half[0][0]) & (rows < half[0][1]), upd, 0.0)
                mu = jnp.mean(o, axis=-1, keepdims=True)
                d = o - mu
                var = jnp.mean(d * d, axis=-1, keepdims=True)
                on = d * lax.rsqrt(var + RMS_EPS) * gnw_ref[:, h * LANE:(h + 1) * LANE]
                g = rg_ref[bi, :, h * LANE:(h + 1) * LANE].astype(F32)
                o_ref[bi, :, h * LANE:(h + 1) * LANE] = (g * _sigmoid(g) * on).astype(o_ref.dtype)
            r_scr[bi, p] = new_r


def _ret_tables(S, C):
    inv128 = np.tile(_rope_inv_freq(RET_QK_DIM), 2 * LANE // RET_QK_DIM)
    ang = jnp.arange(S, dtype=F32)[:, None] * jnp.asarray(inv128)[None, :]
    cos = jnp.cos(ang)
    sin = jnp.sin(ang)
    gammas = 1.0 - 2.0 ** (-5.0 - jnp.arange(RET_HEADS, dtype=F32))
    log_g = jnp.log(gammas)
    j = jnp.arange(C, dtype=F32)
    diff = j[:, None] - j[None, :]
    din = jnp.where(diff[None] >= 0, jnp.exp(jnp.maximum(diff, 0.0)[None] * log_g[:, None, None]), 0.0)
    xi = jnp.exp((j[None, :] + 1.0) * log_g[:, None])
    zeta = jnp.exp((C - 1.0 - j[None, :]) * log_g[:, None])
    gch = jnp.exp(C * log_g)
    xi_t = jnp.broadcast_to(xi[:, :, None], (RET_HEADS, C, LANE))
    zeta_p = jnp.repeat(zeta.reshape(RET_HEADS // 2, 2, C).transpose(0, 2, 1), RET_QK_DIM, axis=-1)
    gch_p = jnp.broadcast_to(jnp.repeat(gch.reshape(RET_HEADS // 2, 2), RET_QK_DIM, axis=-1)[:, :, None],
                             (RET_HEADS // 2, LANE, LANE))
    return cos, sin, din, xi_t, zeta_p, gch_p


def _retention(proj, gnw, tabs, B, S, C):
    nc = S // C
    cos, sin, din, xi, zeta, gch = tabs
    proj3 = proj.reshape(B, S, N_PROJ)
    blk = lambda w, unit: pl.BlockSpec((B, C, w), lambda c: (0, c, unit * LANE // w))
    full = lambda a: pl.BlockSpec(a.shape, lambda c: (0,) * a.ndim)
    out = pl.pallas_call(
        _ret_kernel,
        grid=(nc,),
        in_specs=[
            blk(RET_QK_W, U_RQ), blk(RET_QK_W, U_RQR), blk(RET_QK_W, U_RK), blk(RET_QK_W, U_RKR),
            blk(RET_V_W, U_RV), blk(RET_V_W, U_RG),
            pl.BlockSpec((C, LANE), lambda c: (c, 0)),
            pl.BlockSpec((C, LANE), lambda c: (c, 0)),
            full(din), full(xi), full(zeta), full(gch), full(gnw),
        ],
        out_specs=pl.BlockSpec((B, C, RET_V_W), lambda c: (0, c, 0)),
        out_shape=jax.ShapeDtypeStruct((B, S, RET_V_W), BF16),
        scratch_shapes=[pltpu.VMEM((B, RET_HEADS // 2, LANE, LANE), F32)],
        compiler_params=_cparams(("arbitrary",)),
        name="retention",
    )(proj3, proj3, proj3, proj3, proj3, proj3, cos, sin, din, xi, zeta, gch, gnw)
    return out.reshape(B * S, RET_V_W)


def _merge_kernel(of_ref, or_ref, om_ref, g0_ref, g1_ref, g2_ref, x_ref,
                  wf_ref, wr_ref, wm_ref, wo_ref, nw_ref, xo_ref, h_ref):
    merged = (_sigmoid(g0_ref[...].astype(F32)) * jnp.dot(of_ref[...], wf_ref[...], preferred_element_type=F32)
              + _sigmoid(g1_ref[...].astype(F32)) * jnp.dot(or_ref[...], wr_ref[...], preferred_element_type=F32)
              + _sigmoid(g2_ref[...].astype(F32)) * jnp.dot(om_ref[...], wm_ref[...], preferred_element_type=F32))
    x = x_ref[...] + jnp.dot(merged.astype(BF16), wo_ref[...], preferred_element_type=F32)
    xo_ref[...] = x
    ms = jnp.mean(x * x, axis=-1, keepdims=True)
    h_ref[...] = (x * lax.rsqrt(ms + RMS_EPS) * nw_ref[...]).astype(h_ref.dtype)


def _merge(o_fox, o_ret, o_mla, proj, x2, wf, wr, wm, wo, nw, tm):
    T = x2.shape[0]
    row = lambda w, cb=0: pl.BlockSpec((tm, w), lambda i: (i, cb))
    const = lambda a: pl.BlockSpec(a.shape, lambda i: (0, 0))
    g_unit = U_GL // 8
    return pl.pallas_call(
        _merge_kernel,
        grid=(T // tm,),
        in_specs=[
            row(FOX_W), row(RET_V_W), row(MLA_V_W),
            row(D_MODEL, g_unit), row(D_MODEL, g_unit + 1), row(D_MODEL, g_unit + 2),
            row(D_MODEL),
            const(wf), const(wr), const(wm), const(wo), const(nw),
        ],
        out_specs=[row(D_MODEL), row(D_MODEL)],
        out_shape=[jax.ShapeDtypeStruct((T, D_MODEL), F32), jax.ShapeDtypeStruct((T, D_MODEL), BF16)],
        compiler_params=_cparams(("parallel",)),
        name="merge",
    )(o_fox, o_ret, o_mla, proj, proj, proj, x2, wf, wr, wm, wo, nw)


def _swiglu_tile(h, wg, wu, wd):
    a = jnp.dot(h, wg, preferred_element_type=F32)
    b = jnp.dot(h, wu, preferred_element_type=F32)
    return jnp.dot((a * _sigmoid(a) * b).astype(BF16), wd, preferred_element_type=F32)


def _dense_ffn_kernel(h_ref, wg_ref, wu_ref, wd_ref, x_ref, o_ref, acc_scr):
    f = pl.program_id(1)

    @pl.when(f == 0)
    def _():
        acc_scr[...] = jnp.zeros_like(acc_scr)

    acc_scr[...] += _swiglu_tile(h_ref[...], wg_ref[...], wu_ref[...], wd_ref[...])

    @pl.when(f == pl.num_programs(1) - 1)
    def _():
        o_ref[...] = x_ref[...] + acc_scr[...]


def _dense_ffn(h, x2, wg, wu, wd, tm, tf):
    T = x2.shape[0]
    return pl.pallas_call(
        _dense_ffn_kernel,
        grid=(T // tm, D_FF // tf),
        in_specs=[
            pl.BlockSpec((tm, D_MODEL), lambda i, f: (i, 0)),
            pl.BlockSpec((D_MODEL, tf), lambda i, f: (0, f)),
            pl.BlockSpec((D_MODEL, tf), lambda i, f: (0, f)),
            pl.BlockSpec((tf, D_MODEL), lambda i, f: (f, 0)),
            pl.BlockSpec((tm, D_MODEL), lambda i, f: (i, 0)),
        ],
        out_specs=pl.BlockSpec((tm, D_MODEL), lambda i, f: (i, 0)),
        out_shape=jax.ShapeDtypeStruct((T, D_MODEL), F32),
        scratch_shapes=[pltpu.VMEM((tm, D_MODEL), F32)],
        compiler_params=_cparams(("parallel", "arbitrary")),
        name="dense_ffn",
    )(h, wg, wu, wd, x2)


M_E0, M_E1, M_P0, M_P1, M_W0, M_W1 = 0, 1, 2, 3, 4, 5
ROUTE_ROWS = 8


def _split2(x):
    hi = x.astype(BF16)
    lo = (x - hi.astype(F32)).astype(BF16)
    return hi, lo


HALF_D = D_MODEL // 2


def _pack_rows(h):
    bits = pltpu.bitcast(h.astype(BF16).astype(F32), jnp.uint32)
    return (bits[:, :HALF_D] & jnp.uint32(0xFFFF0000)) | (bits[:, HALF_D:] >> 16)


def _unpack_rows(u):
    left = pltpu.bitcast(u & jnp.uint32(0xFFFF0000), F32)
    right = pltpu.bitcast(u << 16, F32)
    return jnp.concatenate([left, right], axis=-1).astype(BF16)


def _router_kernel(x_ref, nw_ref, whi_ref, wlo_ref, tri_ref, h_ref, meta_ref, route_ref, cnt_ref, carry_scr):
    @pl.when(pl.program_id(0) == 0)
    def _():
        carry_scr[...] = jnp.zeros_like(carry_scr)

    x = x_ref[...]
    ms = jnp.mean(x * x, axis=-1, keepdims=True)
    h = x * lax.rsqrt(ms + RMS_EPS) * nw_ref[...]
    h_ref[...] = _pack_rows(h)
    hhi, hlo = _split2(h)
    logits = (jnp.dot(hhi, whi_ref[...], preferred_element_type=F32)
              + jnp.dot(hhi, wlo_ref[...], preferred_element_type=F32)
              + jnp.dot(hlo, whi_ref[...], preferred_element_type=F32))
    lane = lax.broadcasted_iota(jnp.int32, logits.shape, 1)
    lanef = lane.astype(F32)
    logits = jnp.where(lane < N_EXPERTS, logits, NEG)
    m0 = jnp.max(logits, axis=-1, keepdims=True)
    e0 = jnp.min(jnp.where(logits == m0, lanef, float(LANE)), axis=-1, keepdims=True)
    oh0 = lanef == e0
    rest = jnp.where(oh0, NEG, logits)
    m1 = jnp.max(rest, axis=-1, keepdims=True)
    e1 = jnp.min(jnp.where(rest == m1, lanef, float(LANE)), axis=-1, keepdims=True)
    oh1 = lanef == e1
    z = jnp.exp(m1 - m0)
    w0 = 1.0 / (1.0 + z)
    w1 = z / (1.0 + z)
    both = jnp.where(oh0 | oh1, 1.0, 0.0)
    before = carry_scr[0:1, :] + jnp.dot(tri_ref[...], both.astype(BF16), preferred_element_type=F32)
    p0 = jnp.sum(jnp.where(oh0, before, 0.0), axis=-1, keepdims=True)
    p1 = jnp.sum(jnp.where(oh1, before, 0.0), axis=-1, keepdims=True)
    total = carry_scr[0:1, :] + jnp.sum(both, axis=0, keepdims=True)
    carry_scr[0:1, :] = total
    cnt_ref[...] = jnp.broadcast_to(total, cnt_ref.shape)
    meta = jnp.zeros(logits.shape, F32)
    for idx, val in ((M_E0, e0), (M_E1, e1), (M_P0, p0), (M_P1, p1), (M_W0, w0), (M_W1, w1)):
        meta = jnp.where(lane == idx, val, meta)
    meta_ref[...] = meta
    route_ref[0] = meta.T[:ROUTE_ROWS, :]


def _router(x2, nw, w_router, tr):
    T = x2.shape[0]
    wr = jnp.pad(w_router.astype(F32), ((0, 0), (0, LANE - N_EXPERTS)))
    whi = wr.astype(BF16)
    wlo = (wr - whi.astype(F32)).astype(BF16)
    tri = jnp.asarray(np.tril(np.ones((tr, tr), np.float32), -1), BF16)
    const = lambda a: pl.BlockSpec(a.shape, lambda i: (0, 0))
    return pl.pallas_call(
        _router_kernel,
        grid=(T // tr,),
        in_specs=[pl.BlockSpec((tr, D_MODEL), lambda i: (i, 0)), const(nw), const(whi), const(wlo), const(tri)],
        out_specs=[
            pl.BlockSpec((tr, HALF_D), lambda i: (i, 0)),
            pl.BlockSpec((tr, LANE), lambda i: (i, 0)),
            pl.BlockSpec((1, ROUTE_ROWS, tr), lambda i: (i, 0, 0)),
            pl.BlockSpec((8, LANE), lambda i: (0, 0)),
        ],
        out_shape=[
            jax.ShapeDtypeStruct((T, HALF_D), jnp.uint32),
            jax.ShapeDtypeStruct((T, LANE), F32),
            jax.ShapeDtypeStruct((T // tr, ROUTE_ROWS, tr), F32),
            jax.ShapeDtypeStruct((8, LANE), F32),
        ],
        scratch_shapes=[pltpu.VMEM((8, LANE), F32)],
        compiler_params=_cparams(("arbitrary",)),
        name="router",
    )(x2, nw, whi, wlo, tri)


def _scatter_kernel(dest_ref, h_ref, xs_in, xs_hbm, sem, *, tt):
    del xs_in

    def row_copy(t, d_row):
        return pltpu.make_async_copy(h_ref.at[pl.ds(t, 1)], xs_hbm.at[pl.ds(d_row, 1)], sem)

    for t in range(tt):
        for j in range(TOP_K):
            row_copy(t, dest_ref[0, 0, j * tt + t]).start(priority=j)

    for _ in range(TOP_K):
        pltpu.make_async_copy(h_ref, xs_hbm.at[pl.ds(0, tt)], sem).wait()


def _scatter_rows(dest, h, n_rows, tt):
    T = h.shape[0]
    width = h.shape[1]
    xs0 = jnp.zeros((n_rows, width), h.dtype)
    assert dest.shape == (T // tt, TOP_K, tt)
    dest3 = dest.reshape(T // tt, 1, TOP_K * tt)
    return pl.pallas_call(
        functools.partial(_scatter_kernel, tt=tt),
        grid=(T // tt,),
        in_specs=[
            pl.BlockSpec((1, 1, 2 * tt), lambda i: (i, 0, 0), memory_space=pltpu.SMEM),
            pl.BlockSpec((tt, width), lambda i: (i, 0)),
            pl.BlockSpec(memory_space=pl.ANY),
        ],
        out_specs=pl.BlockSpec(memory_space=pl.ANY),
        out_shape=jax.ShapeDtypeStruct((n_rows, width), h.dtype),
        scratch_shapes=[pltpu.SemaphoreType.DMA(())],
        input_output_aliases={2: 0},
        compiler_params=_cparams(("arbitrary",)),
        name="moe_scatter",
    )(dest3, h, xs0)


def _moe_ffn_kernel(te_ref, na_ref, xs_ref, wg_ref, wu_ref, wd_ref, ys_ref, acc_scr):
    i = pl.program_id(0)
    f = pl.program_id(1)

    @pl.when(i < na_ref[0])
    def _():
        @pl.when(f == 0)
        def _():
            acc_scr[...] = jnp.zeros_like(acc_scr)

        acc_scr[...] += _swiglu_tile(_unpack_rows(xs_ref[...]), wg_ref[0].astype(BF16),
                                     wu_ref[0].astype(BF16), wd_ref[0].astype(BF16))

        @pl.when(f == pl.num_programs(1) - 1)
        def _():
            ys_ref[...] = acc_scr[...]

    @pl.when((i >= na_ref[0]) & (f == 0))
    def _():
        ys_ref[...] = jnp.zeros_like(ys_ref)


def _moe_ffn(tile_expert, n_active, xs, wg, wu, wd, tm, tf):
    n_rows = xs.shape[0]
    n_tiles = n_rows // tm
    nf = D_FF // tf

    def row_map(i, f, te, na):
        return (jnp.minimum(i, na[0] - 1), 0)

    def fcol(i, f, na):
        return jnp.where(i < na[0], f, nf - 1)

    grid_spec = pltpu.PrefetchScalarGridSpec(
        num_scalar_prefetch=2,
        grid=(n_tiles, nf),
        in_specs=[
            pl.BlockSpec((tm, HALF_D), row_map),
            pl.BlockSpec((1, D_MODEL, tf), lambda i, f, te, na: (te[i], 0, fcol(i, f, na))),
            pl.BlockSpec((1, D_MODEL, tf), lambda i, f, te, na: (te[i], 0, fcol(i, f, na))),
            pl.BlockSpec((1, tf, D_MODEL), lambda i, f, te, na: (te[i], fcol(i, f, na), 0)),
        ],
        out_specs=pl.BlockSpec((tm, D_MODEL), lambda i, f, te, na: (i, 0)),
        scratch_shapes=[pltpu.VMEM((tm, D_MODEL), F32)],
    )
    return pl.pallas_call(
        _moe_ffn_kernel,
        grid_spec=grid_spec,
        out_shape=jax.ShapeDtypeStruct((n_rows, D_MODEL), F32),
        compiler_params=_cparams(("arbitrary", "arbitrary")),
        name="moe_ffn",
    )(tile_expert, n_active, xs, wg, wu, wd)


def _combine_kernel(dest_ref, dnext_ref, x_ref, meta_ref, fw_ref, ys_hbm, o_ref, g_scr, sems, *, tc):
    i = pl.program_id(0)

    def gather(d_ref, half, s):
        for t in range(tc):
            for j in range(TOP_K):
                row = d_ref[0, 0, (half * TOP_K + j) * tc + t]
                pltpu.make_async_copy(ys_hbm.at[pl.ds(row, 1)], g_scr.at[s, j, pl.ds(t, 1)],
                                      sems.at[s]).start(priority=j)

    def wait(s):
        for j in range(TOP_K):
            pltpu.make_async_copy(ys_hbm.at[pl.ds(0, tc)], g_scr.at[s, j], sems.at[s]).wait()

    def combine(s):
        rows = pl.ds(s * tc, tc)
        meta = meta_ref[rows, :]
        w0 = meta[:, M_W0:M_W0 + 1]
        w1 = meta[:, M_W1:M_W1 + 1]
        x = x_ref[rows, :] + w0 * g_scr[s, 0] + w1 * g_scr[s, 1]
        ms = jnp.mean(x * x, axis=-1, keepdims=True)
        o_ref[rows, :] = x * lax.rsqrt(ms + RMS_EPS) * fw_ref[...]

    @pl.when(i == 0)
    def _():
        gather(dest_ref, 0, 0)

    gather(dest_ref, 1, 1)
    wait(0)
    combine(0)
    gather(dnext_ref, 0, 0)
    wait(1)
    combine(1)

    @pl.when(i == pl.num_programs(0) - 1)
    def _():
        wait(0)


def _combine(dest, x2, meta, fw, ys, tc):
    T = x2.shape[0]
    n = T // (2 * tc)
    assert dest.shape == (2 * n, TOP_K, tc)
    dest3 = dest.reshape(n, 1, 2 * TOP_K * tc)
    dspec = lambda imap: pl.BlockSpec((1, 1, 2 * TOP_K * tc), imap, memory_space=pltpu.SMEM)
    return pl.pallas_call(
        functools.partial(_combine_kernel, tc=tc),
        grid=(n,),
        in_specs=[
            dspec(lambda i: (i, 0, 0)),
            dspec(lambda i: (jnp.minimum(i + 1, n - 1), 0, 0)),
            pl.BlockSpec((2 * tc, D_MODEL), lambda i: (i, 0)),
            pl.BlockSpec((2 * tc, LANE), lambda i: (i, 0)),
            pl.BlockSpec((1, D_MODEL), lambda i: (0, 0)),
            pl.BlockSpec(memory_space=pl.ANY),
        ],
        out_specs=pl.BlockSpec((2 * tc, D_MODEL), lambda i: (i, 0)),
        out_shape=jax.ShapeDtypeStruct((T, D_MODEL), F32),
        scratch_shapes=[pltpu.VMEM((2, TOP_K, tc, D_MODEL), F32), pltpu.SemaphoreType.DMA((2,))],
        compiler_params=_cparams(("arbitrary",)),
        name="moe_combine",
    )(dest3, dest3, x2, meta, fw, ys)


def _moe_plan(route, counts, tm, n_tiles):
    cnt = counts[0, :N_EXPERTS].astype(jnp.int32)
    padded = ((cnt + tm - 1) // tm) * tm
    ends = jnp.cumsum(padded)
    offs = ends - padded
    e = route[:, M_E0:M_E1 + 1, :].astype(jnp.int32)
    pos = route[:, M_P0:M_P1 + 1, :].astype(jnp.int32)
    dest = offs[e] + pos
    starts = jnp.arange(n_tiles, dtype=jnp.int32) * tm
    n_active = (ends[-1] // tm).astype(jnp.int32)
    te = jnp.sum(starts[:, None] >= ends[None, :], axis=-1).astype(jnp.int32)
    last_e = jnp.sum(jnp.maximum(ends[-1] - tm, 0) >= ends).astype(jnp.int32)
    te = jnp.where(starts < ends[-1], te, last_e)
    return dest, te, n_active.reshape(1)


def _moe_layer(x_mid, nw, w_router, wg, wu, wd, final_w, tiles):
    T = x_mid.shape[0]
    tm = tiles["moe_tm"]
    n_tiles = (TOP_K * T) // tm + N_EXPERTS
    tr = tiles["moe_row_t"]
    h, meta, route, counts = _router(x_mid, nw, w_router, tr)
    dest, te, n_active = _moe_plan(route, counts, tm, n_tiles)
    xs = _scatter_rows(dest, h, n_tiles * tm, tr)
    ys = _moe_ffn(te, n_active, xs, wg, wu, wd, tm, tiles["moe_tf"])
    return _combine(dest, x_mid, meta, final_w, ys, tr)


def _tiles(S, T):
    return dict(
        inproj_tm=min(1024, T), inproj_tn=1920,
        attn_t=min(512, S // 2), ret_c=min(256, S),
        merge_tm=min(512, T), ffn_tm=min(1024, T), ffn_tf=512,
        moe_row_t=min(256, T // 2),
        moe_tm=min(1024, T), moe_tf=512,
    )


def kernel(x, norm_mix_w, w_in, fox_f_bias, ret_gn_w, mla_q_norm_w, mla_kv_norm_w, mla_w_uq, mla_w_uk, mla_w_uv, w_br_fox, w_br_ret, w_br_mla, w_out, norm_ffn_w, dense_w_gate, dense_w_up, dense_w_down, moe_w_router, moe_w_gate, moe_w_up, moe_w_down, final_norm_w):
    B, S, D = x.shape
    assert D == D_MODEL and w_in.shape[0] == DEPTH == 2
    T = B * S
    tl = _tiles(S, T)
    row = lambda a: a.reshape(1, -1).astype(F32)
    ret_tabs = _ret_tables(S, tl["ret_c"])
    mla_tabs = _mla_tables(S)
    x2 = x.reshape(T, D)
    w_all, w_ff = _build_inproj_weights(w_in)
    for l in range(DEPTH):
        proj, ff32 = _inproj(x2, row(norm_mix_w[l]), w_all[l], w_ff[l], tl["inproj_tm"], tl["inproj_tn"])
        fqt, fkx, fvt = _fox_prep(proj, ff32, fox_f_bias[l], B, S, tl["attn_t"])
        o_fox = _attention(fqt, fkx, fvt, B, S, tl["attn_t"], "fox_attn")
        o_ret = _retention(proj, row(ret_gn_w[l]), ret_tabs, B, S, tl["ret_c"])
        wq, wqr, wk, wv = _mla_weights(mla_w_uq[l], mla_w_uk[l], mla_w_uv[l])
        mqt, mkx, mvt = _mla_prep(proj, row(mla_q_norm_w[l]), row(mla_kv_norm_w[l]), wq, wqr, wk, wv,
                                  mla_tabs, B, S, tl["attn_t"])
        o_mla = _attention(mqt, mkx, mvt, B, S, tl["attn_t"], "mla_attn")
        x_mid, h2 = _merge(o_fox, o_ret, o_mla, proj, x2,
                           w_br_fox[l].astype(BF16), w_br_ret[l].astype(BF16), w_br_mla[l].astype(BF16),
                           w_out[l].astype(BF16), row(norm_ffn_w[l]), tl["merge_tm"])
        if l % 2 == 0:
            i = l // 2
            x2 = _dense_ffn(h2, x_mid, dense_w_gate[i].astype(BF16), dense_w_up[i].astype(BF16),
                            dense_w_down[i].astype(BF16), tl["ffn_tm"], tl["ffn_tf"])
        else:
            i = l // 2
            x2 = _moe_layer(x_mid, row(norm_ffn_w[l]), moe_w_router[i], moe_w_gate[i],
                            moe_w_up[i], moe_w_down[i], row(final_norm_w), tl)
    return x2.reshape(B, S, D)
```

```python
import functools

import numpy as np
import jax
import jax.numpy as jnp
from jax import lax
from jax.experimental import pallas as pl
from jax.experimental.pallas import tpu as pltpu

F32 = jnp.float32
BF16 = jnp.bfloat16

D_MODEL = 1024
DEPTH = 2
FOX_HEADS = 8
FOX_HEAD_DIM = 64
FOX_W = FOX_HEADS * FOX_HEAD_DIM
RET_HEADS = 8
RET_QK_DIM = 64
RET_V_DIM = 128
RET_QK_W = RET_HEADS * RET_QK_DIM
RET_V_W = RET_HEADS * RET_V_DIM
MLA_HEADS = 8
MLA_Q_RANK = 384
MLA_KV_RANK = 256
MLA_NOPE_DIM = 64
MLA_ROPE_DIM = 32
MLA_V_DIM = 64
MLA_QK_DIM = MLA_NOPE_DIM + MLA_ROPE_DIM
MLA_V_W = MLA_HEADS * MLA_V_DIM
N_BRANCHES = 3
ROPE_THETA = 10000.0
RMS_EPS = 1e-6
D_FF = 3584
N_EXPERTS = 8
TOP_K = 2
IN_SPLITS = (FOX_W, FOX_W, FOX_W, FOX_HEADS,
             RET_QK_W, RET_QK_W, RET_V_W, RET_V_W,
             MLA_Q_RANK, MLA_KV_RANK, MLA_ROPE_DIM,
             N_BRANCHES * D_MODEL)

LANE = 128
NEG = -1e30
VMEM_LIMIT = 56 * 1024 * 1024

U_FQ, U_FK, U_FV = 0, 4, 8
U_RQ, U_RK = 12, 16
U_KR, U_CQ = 20, 21
U_RV, U_RG, U_GL, U_CKV = 24, 32, 40, 64
N_PROJ = 66 * LANE
FOX_EXT_STRIDE = 8
PAIRS = 4
QK_W = 2 * LANE
ATTN_KEY_CHUNK = 256
ATTN_STRIP = 256
VT_ROWS = 80
LOG2E = 1.4426950408889634


def _cparams(sem, vmem=VMEM_LIMIT, flags=None):
    return pltpu.CompilerParams(dimension_semantics=sem, vmem_limit_bytes=vmem, flags=flags)


def _sigmoid(x):
    return 1.0 / (1.0 + jnp.exp(-x))


def _lane_mask(shape, ranges):
    lane = lax.broadcasted_iota(jnp.int32, shape, len(shape) - 1)
    m = None
    for a, b in ranges:
        r = (lane >= a) & (lane < b)
        m = r if m is None else (m | r)
    return m


def _inproj_kernel(x_ref, nw_ref, w_ref, wff_ref, out_ref, ff_ref, h_scr):
    @pl.when(pl.program_id(1) == 0)
    def _():
        x = x_ref[...]
        ms = jnp.mean(x * x, axis=-1, keepdims=True)
        h = (x * lax.rsqrt(ms + RMS_EPS) * nw_ref[...]).astype(BF16)
        h_scr[...] = h
        ff_ref[...] = jnp.dot(h, wff_ref[...], preferred_element_type=F32)

    out_ref[...] = jnp.dot(h_scr[...], w_ref[...], preferred_element_type=F32).astype(out_ref.dtype)


def _inproj(x2, nw, w_all, w_ff, tm, tn):
    T = x2.shape[0]
    return pl.pallas_call(
        _inproj_kernel,
        grid=(T // tm, N_PROJ // tn),
        in_specs=[
            pl.BlockSpec((tm, D_MODEL), lambda i, j: (i, 0)),
            pl.BlockSpec((1, D_MODEL), lambda i, j: (0, 0)),
            pl.BlockSpec((D_MODEL, tn), lambda i, j: (0, j)),
            pl.BlockSpec((D_MODEL, LANE), lambda i, j: (0, 0)),
        ],
        out_specs=[
            pl.BlockSpec((tm, tn), lambda i, j: (i, j)),
            pl.BlockSpec((tm, LANE), lambda i, j: (i, 0)),
        ],
        out_shape=[
            jax.ShapeDtypeStruct((T, N_PROJ), BF16),
            jax.ShapeDtypeStruct((T, LANE), F32),
        ],
        scratch_shapes=[pltpu.VMEM((tm, D_MODEL), BF16)],
        compiler_params=_cparams(("parallel", "arbitrary")),
        name="inproj",
    )(x2, nw, w_all, w_ff)


def _rot_half_lanes(x, d):
    lane = lax.broadcasted_iota(jnp.int32, x.shape, 1)
    from_right = pltpu.roll(x, LANE - d // 2, 1)
    from_left = pltpu.roll(x, d // 2, 1)
    return jnp.where(lane % d < d // 2, -from_right, from_left)


def _wprep_kernel(w_ref, *out_refs):
    offs = [int(i) for i in np.cumsum((0,) + IN_SPLITS)]
    o_fq, o_fk, o_fv, o_ff, o_rq, o_rk, o_rv, o_rg, o_cq, o_ckv, o_kr, o_gl, _ = offs
    depth = w_ref.shape[0]
    lane = lax.broadcasted_iota(jnp.int32, (w_ref.shape[1], LANE), 1)
    rope_lanes = (lane >= MLA_NOPE_DIM) & (lane < MLA_QK_DIM)
    for l in range(depth):
        out_ref, ff_ref = out_refs[l], out_refs[depth + l]

        def put(unit, piece, out_ref=out_ref):
            out_ref[:, unit * LANE:unit * LANE + piece.shape[-1]] = piece.astype(out_ref.dtype)

        put(U_FQ, w_ref[l, :, o_fq:o_ff])
        ff_ref[...] = jnp.where(lane < FOX_HEADS, w_ref[l, :, o_ff:o_ff + LANE], 0.0).astype(ff_ref.dtype)
        put(U_RQ, w_ref[l, :, o_rq:o_rv])
        put(U_CKV, w_ref[l, :, o_ckv:o_kr])
        put(U_KR, jnp.where(rope_lanes, w_ref[l, :, o_kr - MLA_NOPE_DIM:o_kr - MLA_NOPE_DIM + LANE], 0.0))
        put(U_RV, w_ref[l, :, o_rv:o_rg])
        put(U_RG, w_ref[l, :, o_rg:o_cq])
        put(U_GL, w_ref[l, :, o_gl:o_gl + N_BRANCHES * D_MODEL])
        put(U_CQ, w_ref[l, :, o_cq:o_ckv])


def _build_inproj_weights(w_in, rows=128):
    depth, _, d_in = w_in.shape
    outs = pl.pallas_call(
        _wprep_kernel,
        grid=(D_MODEL // rows,),
        in_specs=[pl.BlockSpec((depth, rows, d_in), lambda r: (0, r, 0))],
        out_specs=([pl.BlockSpec((rows, N_PROJ), lambda r: (r, 0))] * depth
                   + [pl.BlockSpec((rows, LANE), lambda r: (r, 0))] * depth),
        out_shape=([jax.ShapeDtypeStruct((D_MODEL, N_PROJ), BF16)] * depth
                   + [jax.ShapeDtypeStruct((D_MODEL, LANE), BF16)] * depth),
        compiler_params=_cparams(("parallel",)),
        name="inproj_weight_prep",
    )(w_in)
    return outs[:depth], outs[depth:]


def _split3(x):
    hi = x.astype(BF16)
    r1 = x - hi.astype(F32)
    mid = r1.astype(BF16)
    lo = (r1 - mid.astype(F32)).astype(BF16)
    return hi, mid, lo


def _store_t(dst_ref, idx, x):
    dst_ref[idx] = x.T.astype(dst_ref.dtype)


def _store_vt_pair(vt_ref, p, v2):
    hd = LANE // 2
    vt = v2.T.astype(vt_ref.dtype)
    pad_rows = lax.broadcasted_iota(jnp.int32, (VT_ROWS - hd, vt.shape[1]), 0)
    ones_then_zeros = jnp.where(pad_rows == 0, 1.0, 0.0).astype(vt_ref.dtype)
    for j in range(2):
        base = (2 * p + j) * VT_ROWS
        vt_ref[0, 0, base:base + hd, :] = vt[j * hd:(j + 1) * hd]
        vt_ref[0, 0, base + hd:base + VT_ROWS, :] = ones_then_zeros


def _fox_prep_kernel(fq_ref, fk_ref, fv_ref, ff_ref, bias_ref, tri_ref, pq_ref, pk_ref, cq_ref, ck_ref,
                     qt_ref, kx_ref, vt_ref, carry_scr):
    @pl.when(pl.program_id(1) == 0)
    def _():
        carry_scr[...] = jnp.zeros_like(carry_scr)

    ts = ff_ref.shape[0]
    z = ff_ref[...] + bias_ref[...]
    logf = jnp.minimum(z, 0.0) - jnp.log(1.0 + jnp.exp(-jnp.abs(z)))
    tri = tri_ref[...]
    cum = carry_scr[0:1, :]
    for part in _split3(logf):
        cum = cum + jnp.dot(tri, part, preferred_element_type=F32)
    carry_scr[0:1, :] = cum[ts - 1:ts, :]

    cum3 = jnp.concatenate(_split3(cum * LOG2E), axis=-1)
    eq = cq_ref[...] + jnp.dot(cum3, pq_ref[...], preferred_element_type=F32)
    ek = ck_ref[...] + jnp.dot(cum3, pk_ref[...], preferred_element_type=F32)
    scale = FOX_HEAD_DIM ** -0.5 * LOG2E
    lane = lax.broadcasted_iota(jnp.int32, (ts, LANE), 1)
    for p in range(FOX_HEADS // 2):
        src = slice(p * LANE, (p + 1) * LANE)
        q2 = fq_ref[:, src].astype(F32) * scale
        k2 = fk_ref[:, src].astype(F32)
        for j in range(2):
            h = 2 * p + j
            dst = slice(h * LANE, (h + 1) * LANE)
            qh = q2 if j == 0 else pltpu.roll(q2, LANE // 2, 1)
            kh = k2 if j == 0 else pltpu.roll(k2, LANE // 2, 1)
            own = (lane >= FOX_HEAD_DIM + FOX_EXT_STRIDE * h) & (lane < FOX_HEAD_DIM + FOX_EXT_STRIDE * (h + 1))
            _store_t(qt_ref, (0, dst), jnp.where(lane < FOX_HEAD_DIM, qh, jnp.where(own, eq, 0.0)))
            kx_ref[:, dst] = jnp.where(lane < FOX_HEAD_DIM, kh, ek).astype(BF16)
        _store_vt_pair(vt_ref, p, fv_ref[:, src].astype(F32))


def _fox_prep_consts(ts):
    tri = np.tril(np.ones((ts, ts), np.float32))
    pq = np.zeros((3 * LANE, LANE), np.float32)
    pk = np.zeros((3 * LANE, LANE), np.float32)
    cq = np.zeros((1, LANE), np.float32)
    ck = np.zeros((1, LANE), np.float32)
    for h in range(FOX_HEADS):
        base = FOX_HEAD_DIM + FOX_EXT_STRIDE * h
        for j in range(3):
            pq[j * LANE + h, base + j] = 1.0
            ck[0, base + j] = 1.0
            cq[0, base + 3 + j] = 1.0
            pk[j * LANE + h, base + 3 + j] = -1.0
    return (jnp.asarray(tri, BF16), jnp.asarray(pq, BF16), jnp.asarray(pk, BF16),
            jnp.asarray(cq), jnp.asarray(ck))


def _attn_operand_specs(B, S, t):
    T = B * S
    ns = S // t
    specs = [
        pl.BlockSpec((1, PAIRS * QK_W, t), lambda b, s: (b, 0, s)),
        pl.BlockSpec((t, PAIRS * QK_W), lambda b, s: (b * ns + s, 0)),
        pl.BlockSpec((1, 1, 2 * PAIRS * VT_ROWS, t), lambda b, s: (b, s, 0, 0)),
    ]
    shapes = [
        jax.ShapeDtypeStruct((B, PAIRS * QK_W, S), BF16),
        jax.ShapeDtypeStruct((T, PAIRS * QK_W), BF16),
        jax.ShapeDtypeStruct((B, ns, 2 * PAIRS * VT_ROWS, t), BF16),
    ]
    return specs, shapes


def _fox_prep(proj, ff32, bias, B, S, ts):
    ns = S // ts
    tri, pq, pk, cq, ck = _fox_prep_consts(ts)
    bias128 = jnp.pad(bias.astype(F32), (0, LANE - FOX_HEADS)).reshape(1, LANE)
    const = lambda shape: pl.BlockSpec(shape, lambda b, s: (0, 0))
    out_specs, out_shape = _attn_operand_specs(B, S, ts)
    return pl.pallas_call(
        _fox_prep_kernel,
        grid=(B, ns),
        in_specs=[
            pl.BlockSpec((ts, FOX_W), lambda b, s: (b * ns + s, U_FQ // 4)),
            pl.BlockSpec((ts, FOX_W), lambda b, s: (b * ns + s, U_FK // 4)),
            pl.BlockSpec((ts, FOX_W), lambda b, s: (b * ns + s, U_FV // 4)),
            pl.BlockSpec((ts, LANE), lambda b, s: (b * ns + s, 0)),
            const((1, LANE)), const((ts, ts)), const(pq.shape), const(pk.shape), const(cq.shape), const(ck.shape),
        ],
        out_specs=out_specs,
        out_shape=out_shape,
        scratch_shapes=[pltpu.VMEM((8, LANE), F32)],
        compiler_params=_cparams(("parallel", "arbitrary")),
        name="fox_prep",
    )(proj, proj, proj, ff32, bias128, tri, pq, pk, cq, ck)


def _col_reduce(x, op, reduce_fn, parts=2):
    c = x.shape[0] // parts
    pieces = [x[i * c:(i + 1) * c] for i in range(parts)]
    while len(pieces) > 1:
        pieces = [op(pieces[i], pieces[i + 1]) for i in range(0, len(pieces), 2)]
    return reduce_fn(pieces[0], axis=0, keepdims=True)


def _attn_kernel(qt_ref, qn_ref, k_ref, vt_ref, o_ref, sa_scr, sb_scr, ma_scr, mb_scr, acc_scr, *, tk):
    qi = pl.program_id(2)
    tq = qt_ref.shape[-1]
    acc_scr[...] = jnp.zeros_like(acc_scr)

    def key_tile(kt):
        return k_ref[pl.ds(pl.multiple_of(kt * tk, tk), tk), :]

    strips = [(c, c + ATTN_STRIP) for c in range(0, tq, ATTN_STRIP)]

    def scores(kt, s_scr, m_scr):
        k = key_tile(kt)
        for h in range(2):
            for a, b in strips:
                s = jnp.dot(k[:, h * LANE:(h + 1) * LANE], qt_ref[0, h * LANE:(h + 1) * LANE, a:b],
                            preferred_element_type=F32)
                s_scr[h, :, a:b] = s
                m_scr[h, :, a:b] = _col_reduce(s, jnp.maximum, jnp.max)

    def update(kt, h, s, m_tile, m_prev, lo):
        m_new = jnp.maximum(m_prev, m_tile)
        alpha = jnp.exp2(m_prev - m_new)
        hi = lo + s.shape[-1]
        pv = None
        for r in range(0, tk, ATTN_KEY_CHUNK):
            p = jnp.exp2((s[r:r + ATTN_KEY_CHUNK] - m_new).astype(BF16))
            vt = vt_ref[0, kt, h * VT_ROWS:(h + 1) * VT_ROWS, r:r + ATTN_KEY_CHUNK]
            d = jnp.dot(vt, p, preferred_element_type=F32)
            pv = d if pv is None else pv + d
        acc_scr[h, :, lo:hi] = alpha * acc_scr[h, :, lo:hi] + pv
        return m_new

    def diag_update(kt, h, s_scr, m_prev, a, b, lo):
        s = s_scr[h, :, a:b]
        if a < lo + tk:
            key = lax.broadcasted_iota(jnp.int32, s.shape, 0)
            qry = lax.broadcasted_iota(jnp.int32, s.shape, 1) + (a - lo)
            s = jnp.where(key <= qry, s, NEG)
        return update(kt, h, s, _col_reduce(s, jnp.maximum, jnp.max), m_prev, a)

    def diagonal_tiles(carry, lookahead):
        k_last = key_tile(2 * qi + 1)
        if lookahead:
            k0 = key_tile(0)
        out = []
        for h in range(2):
            ms = []
            for a, b in strips:
                if a >= tk:
                    sb_scr[h, :, a:b] = jnp.dot(k_last[:, h * LANE:(h + 1) * LANE],
                                                qt_ref[0, h * LANE:(h + 1) * LANE, a:b],
                                                preferred_element_type=F32)
                ms.append(diag_update(2 * qi, h, sa_scr, carry[h][:, a:b], a, b, 0))
                if lookahead:
                    s = jnp.dot(k0[:, h * LANE:(h + 1) * LANE], qn_ref[0, h * LANE:(h + 1) * LANE, a:b],
                                preferred_element_type=F32)
                    sa_scr[h, :, a:b] = s
                    ma_scr[h, :, a:b] = _col_reduce(s, jnp.maximum, jnp.max)
            out.append(ms)
        for h in range(2):
            for i, (a, b) in enumerate(strips):
                if a >= tk:
                    diag_update(2 * qi + 1, h, sb_scr, out[h][i], a, b, tk)
        hd = LANE // 2
        ot = jnp.concatenate([acc_scr[h, :hd] / acc_scr[h, hd:hd + 1] for h in range(2)], axis=0)
        o_ref[...] = ot.T.astype(o_ref.dtype)

    def fused_step(kt, cur_s, cur_m, nxt_s, nxt_m, carry):
        k = key_tile(kt + 1)
        ms = [[], []]
        for a, b in strips:
            for h in range(2):
                s = jnp.dot(k[:, h * LANE:(h + 1) * LANE], qt_ref[0, h * LANE:(h + 1) * LANE, a:b],
                            preferred_element_type=F32)
                nxt_s[h, :, a:b] = s
                nxt_m[h, :, a:b] = _col_reduce(s, jnp.maximum, jnp.max)
                ms[h].append(update(kt, h, cur_s[h, :, a:b], cur_m[h, :, a:b], carry[h][:, a:b], a))
        return tuple(jnp.concatenate(m, axis=-1) for m in ms)

    def body(j, carry):
        kt = 2 * j
        carry = fused_step(kt, sa_scr, ma_scr, sb_scr, mb_scr, carry)
        return fused_step(kt + 1, sb_scr, mb_scr, sa_scr, ma_scr, carry)

    init = tuple(jnp.full((1, tq), NEG, F32) for _ in range(2))

    @pl.when(qi == 0)
    def _():
        scores(0, sa_scr, ma_scr)

    carry = lax.fori_loop(0, qi, body, init)
    last = pl.num_programs(2) - 1

    @pl.when(qi < last)
    def _():
        diagonal_tiles(carry, True)

    @pl.when(qi == last)
    def _():
        diagonal_tiles(carry, False)


def _attention(qt, kx, vt, B, S, tk, name):
    T = B * S
    tq = 2 * tk
    nq = S // tq
    kern = functools.partial(_attn_kernel, tk=tk)
    return pl.pallas_call(
        kern,
        grid=(B, PAIRS, nq),
        in_specs=[
            pl.BlockSpec((1, QK_W, tq), lambda b, p, i: (b, p, i)),
            pl.BlockSpec((1, QK_W, tq), lambda b, p, i: (b, p, jnp.minimum(i + 1, nq - 1))),
            pl.BlockSpec((S, QK_W), lambda b, p, i: (b, p)),
            pl.BlockSpec((1, S // tk, 2 * VT_ROWS, tk), lambda b, p, i: (b, 0, p, 0)),
        ],
        out_specs=pl.BlockSpec((tq, LANE), lambda b, p, i: (b * nq + i, p)),
        out_shape=jax.ShapeDtypeStruct((T, PAIRS * LANE), BF16),
        scratch_shapes=[
            pltpu.VMEM((2, tk, tq), F32),
            pltpu.VMEM((2, tk, tq), F32),
            pltpu.VMEM((2, 1, tq), F32),
            pltpu.VMEM((2, 1, tq), F32),
            pltpu.VMEM((2, VT_ROWS, tq), F32),
        ],
        compiler_params=_cparams(("parallel", "parallel", "arbitrary")),
        name=name,
    )(qt, qt, kx, vt)


def _mla_prep_kernel(cq_ref, ckv_ref, kr_ref, qnw_ref, kvnw_ref, wq_ref, wk_ref, wv_ref,
                     cq_tab, sq_tab, ck_tab, sk_tab, qt_ref, k_ref, vt_ref):
    def norm(x_ref, w_ref):
        x = x_ref[...].astype(F32)
        ms = jnp.mean(x * x, axis=-1, keepdims=True)
        return (x * lax.rsqrt(ms + RMS_EPS) * w_ref[...]).astype(BF16)

    c_q = norm(cq_ref, qnw_ref)
    c_kv = norm(ckv_ref, kvnw_ref)
    tile8 = lambda a: jnp.concatenate([a] * MLA_HEADS, axis=-1)
    q_raw = jnp.dot(c_q, wq_ref[...], preferred_element_type=F32)
    for c in range(MLA_HEADS):
        qs = q_raw[:, c * LANE:(c + 1) * LANE]
        q = qs * cq_tab[...] + _rot_half_lanes(qs, MLA_ROPE_DIM) * sq_tab[...]
        _store_t(qt_ref, (0, slice(c * LANE, (c + 1) * LANE)), q)
    kr = kr_ref[...].astype(F32)
    k_rope = kr * ck_tab[...] + _rot_half_lanes(kr, MLA_ROPE_DIM) * sk_tab[...]
    k = jnp.dot(c_kv, wk_ref[...], preferred_element_type=F32) + tile8(k_rope)
    k_ref[...] = k.astype(BF16)
    v = jnp.dot(c_kv, wv_ref[...], preferred_element_type=F32)
    for p in range(PAIRS):
        _store_vt_pair(vt_ref, p, v[:, p * LANE:(p + 1) * LANE])


def _mla_weights(w_uq, w_uk, w_uv):
    q3 = w_uq.reshape(MLA_Q_RANK, MLA_HEADS, MLA_QK_DIM)
    nope, ropep = q3[..., :MLA_NOPE_DIM], q3[..., MLA_NOPE_DIM:]
    z32 = jnp.zeros((MLA_Q_RANK, MLA_HEADS, LANE - MLA_QK_DIM), F32)
    wq = jnp.concatenate([nope, ropep, z32], axis=-1).reshape(MLA_Q_RANK, MLA_HEADS * LANE)
    k3 = w_uk.reshape(MLA_KV_RANK, MLA_HEADS, MLA_NOPE_DIM)
    wk = jnp.concatenate([k3, jnp.zeros_like(k3)], axis=-1).reshape(MLA_KV_RANK, MLA_HEADS * LANE)
    return wq.astype(BF16), wk.astype(BF16), w_uv.astype(BF16)


def _rope_inv_freq(d):
    return (np.float32(ROPE_THETA) ** (-np.arange(0, d, 2, dtype=np.float32) / np.float32(d))).astype(np.float32)


def _mla_tables(S):
    inv = _rope_inv_freq(MLA_ROPE_DIM)
    inv128 = np.concatenate([np.zeros(MLA_NOPE_DIM, np.float32), inv, inv, np.zeros(LANE - MLA_QK_DIM, np.float32)])
    keep = (np.arange(LANE) < MLA_QK_DIM).astype(np.float32)
    ang = jnp.arange(S, dtype=F32)[:, None] * jnp.asarray(inv128)[None, :]
    ctab = jnp.cos(ang) * jnp.asarray(keep)[None, :]
    stab = jnp.sin(ang)
    scale = MLA_QK_DIM ** -0.5 * LOG2E
    return ctab * scale, stab * scale, ctab, stab


def _mla_prep(proj, qnw, kvnw, wq, wk, wv, tabs, B, S, tm):
    ns = S // tm
    HW = MLA_HEADS * LANE
    const = lambda shape: pl.BlockSpec(shape, lambda b, s: (0, 0))
    tab = pl.BlockSpec((tm, LANE), lambda b, s: (s, 0))
    out_specs, out_shape = _attn_operand_specs(B, S, tm)
    return pl.pallas_call(
        _mla_prep_kernel,
        grid=(B, ns),
        in_specs=[
            pl.BlockSpec((tm, MLA_Q_RANK), lambda b, s: (b * ns + s, U_CQ // 3)),
            pl.BlockSpec((tm, MLA_KV_RANK), lambda b, s: (b * ns + s, U_CKV // 2)),
            pl.BlockSpec((tm, LANE), lambda b, s: (b * ns + s, U_KR)),
            const((1, MLA_Q_RANK)), const((1, MLA_KV_RANK)),
            const((MLA_Q_RANK, HW)), const((MLA_KV_RANK, HW)),
            const((MLA_KV_RANK, MLA_V_W)),
            tab, tab, tab, tab,
        ],
        out_specs=out_specs,
        out_shape=out_shape,
        compiler_params=_cparams(("parallel", "parallel")),
        name="mla_prep",
    )(proj, proj, proj, qnw, kvnw, wq, wk, wv, *tabs)


def _ret_kernel(rq_ref, rk_ref, rv_ref, rg_ref, cos_ref, sin_ref,
                din_ref, xi_ref, zeta_ref, gch_ref, gnw_ref, o_ref, r_scr):
    @pl.when(pl.program_id(0) == 0)
    def _():
        r_scr[...] = jnp.zeros_like(r_scr)

    tile4 = lambda a: jnp.concatenate([a] * (RET_QK_W // LANE), axis=-1)
    cos4 = tile4(cos_ref[...])
    sin4 = tile4(sin_ref[...])
    for bi in range(rq_ref.shape[0]):
        def rope(x_ref):
            x = x_ref[bi].astype(F32)
            rot = jnp.concatenate([_rot_half_lanes(x[:, c * LANE:(c + 1) * LANE], RET_QK_DIM)
                                   for c in range(RET_QK_W // LANE)], axis=-1)
            return x * cos4 + rot * sin4

        q = rope(rq_ref)
        k = rope(rk_ref) * (RET_QK_DIM ** -0.5)
        for p in range(RET_HEADS // 2):
            q2 = q[:, p * LANE:(p + 1) * LANE]
            k2 = k[:, p * LANE:(p + 1) * LANE]
            k2b = k2.astype(BF16)
            kz2 = (k2 * zeta_ref[p]).astype(BF16)
            r2 = r_scr[bi, p]
            r2b = r2.astype(BF16)
            new_r = gch_ref[p] * r2
            for j in range(2):
                h = 2 * p + j
                half = ((j * RET_QK_DIM, (j + 1) * RET_QK_DIM),)
                qm = jnp.where(_lane_mask(q2.shape, half), q2, 0.0).astype(BF16)
                inner = lax.dot_general(qm, k2b, (((1,), (1,)), ((), ())), preferred_element_type=F32)
                inner = inner * din_ref[h]
                vh = rv_ref[bi, :, h * LANE:(h + 1) * LANE]
                o = (jnp.dot(inner.astype(BF16), vh, preferred_element_type=F32)
                     + jnp.dot(qm, r2b, preferred_element_type=F32) * xi_ref[h])
                upd = lax.dot_general(kz2, vh, (((0,), (0,)), ((), ())), preferred_element_type=F32)
                rows = lax.broadcasted_iota(jnp.int32, upd.shape, 0)
                new_r = new_r + jnp.where((rows >= half[0][0]) & (rows < half[0][1]), upd, 0.0)
                mu = jnp.mean(o, axis=-1, keepdims=True)
                d = o - mu
                var = jnp.mean(d * d, axis=-1, keepdims=True)
                on = d * lax.rsqrt(var + RMS_EPS) * gnw_ref[:, h * LANE:(h + 1) * LANE]
                g = rg_ref[bi, :, h * LANE:(h + 1) * LANE].astype(F32)
                o_ref[bi, :, h * LANE:(h + 1) * LANE] = (g * _sigmoid(g) * on).astype(o_ref.dtype)
            r_scr[bi, p] = new_r


def _ret_tables(S, C):
    inv128 = np.tile(_rope_inv_freq(RET_QK_DIM), 2 * LANE // RET_QK_DIM)
    ang = jnp.arange(S, dtype=F32)[:, None] * jnp.asarray(inv128)[None, :]
    cos = jnp.cos(ang)
    sin = jnp.sin(ang)
    gammas = 1.0 - 2.0 ** (-5.0 - jnp.arange(RET_HEADS, dtype=F32))
    log_g = jnp.log(gammas)
    j = jnp.arange(C, dtype=F32)
    diff = j[:, None] - j[None, :]
    din = jnp.where(diff[None] >= 0, jnp.exp(jnp.maximum(diff, 0.0)[None] * log_g[:, None, None]), 0.0)
    xi = jnp.exp((j[None, :] + 1.0) * log_g[:, None])
    zeta = jnp.exp((C - 1.0 - j[None, :]) * log_g[:, None])
    gch = jnp.exp(C * log_g)
    xi_t = jnp.broadcast_to(xi[:, :, None], (RET_HEADS, C, LANE))
    zeta_p = jnp.repeat(zeta.reshape(RET_HEADS // 2, 2, C).transpose(0, 2, 1), RET_QK_DIM, axis=-1)
    gch_p = jnp.broadcast_to(jnp.repeat(gch.reshape(RET_HEADS // 2, 2), RET_QK_DIM, axis=-1)[:, :, None],
                             (RET_HEADS // 2, LANE, LANE))
    return cos, sin, din, xi_t, zeta_p, gch_p


def _retention(proj, gnw, tabs, B, S, C):
    nc = S // C
    cos, sin, din, xi, zeta, gch = tabs
    proj3 = proj.reshape(B, S, N_PROJ)
    blk = lambda w, unit: pl.BlockSpec((B, C, w), lambda c: (0, c, unit * LANE // w))
    full = lambda a: pl.BlockSpec(a.shape, lambda c: (0,) * a.ndim)
    out = pl.pallas_call(
        _ret_kernel,
        grid=(nc,),
        in_specs=[
            blk(RET_QK_W, U_RQ), blk(RET_QK_W, U_RK),
            blk(RET_V_W, U_RV), blk(RET_V_W, U_RG),
            pl.BlockSpec((C, LANE), lambda c: (c, 0)),
            pl.BlockSpec((C, LANE), lambda c: (c, 0)),
            full(din), full(xi), full(zeta), full(gch), full(gnw),
        ],
        out_specs=pl.BlockSpec((B, C, RET_V_W), lambda c: (0, c, 0)),
        out_shape=jax.ShapeDtypeStruct((B, S, RET_V_W), BF16),
        scratch_shapes=[pltpu.VMEM((B, RET_HEADS // 2, LANE, LANE), F32)],
        compiler_params=_cparams(("arbitrary",)),
        name="retention",
    )(proj3, proj3, proj3, proj3, cos, sin, din, xi, zeta, gch, gnw)
    return out.reshape(B * S, RET_V_W)


def _merge_kernel(of_ref, or_ref, om_ref, g0_ref, g1_ref, g2_ref, x_ref,
                  wf_ref, wr_ref, wm_ref, wo_ref, nw_ref, xo_ref, h_ref):
    merged = (_sigmoid(g0_ref[...].astype(F32)) * jnp.dot(of_ref[...], wf_ref[...], preferred_element_type=F32)
              + _sigmoid(g1_ref[...].astype(F32)) * jnp.dot(or_ref[...], wr_ref[...], preferred_element_type=F32)
              + _sigmoid(g2_ref[...].astype(F32)) * jnp.dot(om_ref[...], wm_ref[...], preferred_element_type=F32))
    x = x_ref[...] + jnp.dot(merged.astype(BF16), wo_ref[...], preferred_element_type=F32)
    xo_ref[...] = x
    ms = jnp.mean(x * x, axis=-1, keepdims=True)
    h_ref[...] = (x * lax.rsqrt(ms + RMS_EPS) * nw_ref[...]).astype(h_ref.dtype)


def _merge(o_fox, o_ret, o_mla, proj, x2, wf, wr, wm, wo, nw, tm):
    T = x2.shape[0]
    row = lambda w, cb=0: pl.BlockSpec((tm, w), lambda i: (i, cb))
    const = lambda a: pl.BlockSpec(a.shape, lambda i: (0, 0))
    g_unit = U_GL // 8
    return pl.pallas_call(
        _merge_kernel,
        grid=(T // tm,),
        in_specs=[
            row(FOX_W), row(RET_V_W), row(MLA_V_W),
            row(D_MODEL, g_unit), row(D_MODEL, g_unit + 1), row(D_MODEL, g_unit + 2),
            row(D_MODEL),
            const(wf), const(wr), const(wm), const(wo), const(nw),
        ],
        out_specs=[row(D_MODEL), row(D_MODEL)],
        out_shape=[jax.ShapeDtypeStruct((T, D_MODEL), F32), jax.ShapeDtypeStruct((T, D_MODEL), BF16)],
        compiler_params=_cparams(("parallel",)),
        name="merge",
    )(o_fox, o_ret, o_mla, proj, proj, proj, x2, wf, wr, wm, wo, nw)


def _swiglu_tile(h, wg, wu, wd):
    a = jnp.dot(h, wg, preferred_element_type=F32)
    b = jnp.dot(h, wu, preferred_element_type=F32)
    return jnp.dot((a * _sigmoid(a) * b).astype(BF16), wd, preferred_element_type=F32)


def _dense_ffn_kernel(h_ref, wg_ref, wu_ref, wd_ref, x_ref, o_ref, acc_scr):
    f = pl.program_id(1)

    @pl.when(f == 0)
    def _():
        acc_scr[...] = jnp.zeros_like(acc_scr)

    acc_scr[...] += _swiglu_tile(h_ref[...], wg_ref[...], wu_ref[...], wd_ref[...])

    @pl.when(f == pl.num_programs(1) - 1)
    def _():
        o_ref[...] = x_ref[...] + acc_scr[...]


def _dense_ffn(h, x2, wg, wu, wd, tm, tf):
    T = x2.shape[0]
    return pl.pallas_call(
        _dense_ffn_kernel,
        grid=(T // tm, D_FF // tf),
        in_specs=[
            pl.BlockSpec((tm, D_MODEL), lambda i, f: (i, 0)),
            pl.BlockSpec((D_MODEL, tf), lambda i, f: (0, f)),
            pl.BlockSpec((D_MODEL, tf), lambda i, f: (0, f)),
            pl.BlockSpec((tf, D_MODEL), lambda i, f: (f, 0)),
            pl.BlockSpec((tm, D_MODEL), lambda i, f: (i, 0)),
        ],
        out_specs=pl.BlockSpec((tm, D_MODEL), lambda i, f: (i, 0)),
        out_shape=jax.ShapeDtypeStruct((T, D_MODEL), F32),
        scratch_shapes=[pltpu.VMEM((tm, D_MODEL), F32)],
        compiler_params=_cparams(("parallel", "arbitrary")),
        name="dense_ffn",
    )(h, wg, wu, wd, x2)


M_E0, M_E1, M_P0, M_P1, M_W0, M_W1 = 0, 1, 2, 3, 4, 5
ROUTE_ROWS = 8


def _split2(x):
    hi = x.astype(BF16)
    lo = (x - hi.astype(F32)).astype(BF16)
    return hi, lo


HALF_D = D_MODEL // 2


def _pack_rows(h):
    bits = pltpu.bitcast(h.astype(BF16).astype(F32), jnp.uint32)
    return (bits[:, :HALF_D] & jnp.uint32(0xFFFF0000)) | (bits[:, HALF_D:] >> 16)


def _unpack_rows(u):
    left = pltpu.bitcast(u & jnp.uint32(0xFFFF0000), F32)
    right = pltpu.bitcast(u << 16, F32)
    return jnp.concatenate([left, right], axis=-1).astype(BF16)


def _router_kernel(x_ref, nw_ref, whi_ref, wlo_ref, tri_ref, h_ref, meta_ref, route_ref, cnt_ref, carry_scr):
    @pl.when(pl.program_id(0) == 0)
    def _():
        carry_scr[...] = jnp.zeros_like(carry_scr)

    x = x_ref[...]
    ms = jnp.mean(x * x, axis=-1, keepdims=True)
    h = x * lax.rsqrt(ms + RMS_EPS) * nw_ref[...]
    h_ref[...] = _pack_rows(h)
    hhi, hlo = _split2(h)
    logits = (jnp.dot(hhi, whi_ref[...], preferred_element_type=F32)
              + jnp.dot(hhi, wlo_ref[...], preferred_element_type=F32)
              + jnp.dot(hlo, whi_ref[...], preferred_element_type=F32))
    lane = lax.broadcasted_iota(jnp.int32, logits.shape, 1)
    lanef = lane.astype(F32)
    logits = jnp.where(lane < N_EXPERTS, logits, NEG)
    m0 = jnp.max(logits, axis=-1, keepdims=True)
    e0 = jnp.min(jnp.where(logits == m0, lanef, float(LANE)), axis=-1, keepdims=True)
    oh0 = lanef == e0
    rest = jnp.where(oh0, NEG, logits)
    m1 = jnp.max(rest, axis=-1, keepdims=True)
    e1 = jnp.min(jnp.where(rest == m1, lanef, float(LANE)), axis=-1, keepdims=True)
    oh1 = lanef == e1
    z = jnp.exp(m1 - m0)
    w0 = 1.0 / (1.0 + z)
    w1 = z / (1.0 + z)
    both = jnp.where(oh0 | oh1, 1.0, 0.0)
    before = carry_scr[0:1, :] + jnp.dot(tri_ref[...], both.astype(BF16), preferred_element_type=F32)
    p0 = jnp.sum(jnp.where(oh0, before, 0.0), axis=-1, keepdims=True)
    p1 = jnp.sum(jnp.where(oh1, before, 0.0), axis=-1, keepdims=True)
    total = carry_scr[0:1, :] + jnp.sum(both, axis=0, keepdims=True)
    carry_scr[0:1, :] = total
    cnt_ref[...] = jnp.broadcast_to(total, cnt_ref.shape)
    meta = jnp.zeros(logits.shape, F32)
    for idx, val in ((M_E0, e0), (M_E1, e1), (M_P0, p0), (M_P1, p1), (M_W0, w0), (M_W1, w1)):
        meta = jnp.where(lane == idx, val, meta)
    meta_ref[...] = meta
    route_ref[0] = meta.T[:ROUTE_ROWS, :]


def _router(x2, nw, w_router, tr):
    T = x2.shape[0]
    wr = jnp.pad(w_router.astype(F32), ((0, 0), (0, LANE - N_EXPERTS)))
    whi = wr.astype(BF16)
    wlo = (wr - whi.astype(F32)).astype(BF16)
    tri = jnp.asarray(np.tril(np.ones((tr, tr), np.float32), -1), BF16)
    const = lambda a: pl.BlockSpec(a.shape, lambda i: (0, 0))
    return pl.pallas_call(
        _router_kernel,
        grid=(T // tr,),
        in_specs=[pl.BlockSpec((tr, D_MODEL), lambda i: (i, 0)), const(nw), const(whi), const(wlo), const(tri)],
        out_specs=[
            pl.BlockSpec((tr, HALF_D), lambda i: (i, 0)),
            pl.BlockSpec((tr, LANE), lambda i: (i, 0)),
            pl.BlockSpec((1, ROUTE_ROWS, tr), lambda i: (i, 0, 0)),
            pl.BlockSpec((8, LANE), lambda i: (0, 0)),
        ],
        out_shape=[
            jax.ShapeDtypeStruct((T, HALF_D), jnp.uint32),
            jax.ShapeDtypeStruct((T, LANE), F32),
            jax.ShapeDtypeStruct((T // tr, ROUTE_ROWS, tr), F32),
            jax.ShapeDtypeStruct((8, LANE), F32),
        ],
        scratch_shapes=[pltpu.VMEM((8, LANE), F32)],
        compiler_params=_cparams(("arbitrary",)),
        name="router",
    )(x2, nw, whi, wlo, tri)


def _scatter_kernel(dest_ref, h_ref, xs_in, xs_hbm, sem, *, tt):
    del xs_in

    def row_copy(t, d_row):
        return pltpu.make_async_copy(h_ref.at[pl.ds(t, 1)], xs_hbm.at[pl.ds(d_row, 1)], sem)

    for t in range(tt):
        for j in range(TOP_K):
            row_copy(t, dest_ref[0, 0, j * tt + t]).start(priority=j)

    for _ in range(TOP_K):
        pltpu.make_async_copy(h_ref, xs_hbm.at[pl.ds(0, tt)], sem).wait()


def _scatter_rows(dest, h, n_rows, tt):
    T = h.shape[0]
    width = h.shape[1]
    xs0 = jnp.zeros((n_rows, width), h.dtype)
    assert dest.shape == (T // tt, TOP_K, tt)
    dest3 = dest.reshape(T // tt, 1, TOP_K * tt)
    return pl.pallas_call(
        functools.partial(_scatter_kernel, tt=tt),
        grid=(T // tt,),
        in_specs=[
            pl.BlockSpec((1, 1, 2 * tt), lambda i: (i, 0, 0), memory_space=pltpu.SMEM),
            pl.BlockSpec((tt, width), lambda i: (i, 0)),
            pl.BlockSpec(memory_space=pl.ANY),
        ],
        out_specs=pl.BlockSpec(memory_space=pl.ANY),
        out_shape=jax.ShapeDtypeStruct((n_rows, width), h.dtype),
        scratch_shapes=[pltpu.SemaphoreType.DMA(())],
        input_output_aliases={2: 0},
        compiler_params=_cparams(("arbitrary",)),
        name="moe_scatter",
    )(dest3, h, xs0)


def _moe_ffn_kernel(te_ref, na_ref, xs_ref, wg_ref, wu_ref, wd_ref, ys_ref, acc_scr):
    i = pl.program_id(0)
    f = pl.program_id(1)

    @pl.when(i < na_ref[0])
    def _():
        @pl.when(f == 0)
        def _():
            acc_scr[...] = jnp.zeros_like(acc_scr)

        acc_scr[...] += _swiglu_tile(_unpack_rows(xs_ref[...]), wg_ref[0].astype(BF16),
                                     wu_ref[0].astype(BF16), wd_ref[0].astype(BF16))

        @pl.when(f == pl.num_programs(1) - 1)
        def _():
            ys_ref[...] = acc_scr[...]

    @pl.when((i >= na_ref[0]) & (f == 0))
    def _():
        ys_ref[...] = jnp.zeros_like(ys_ref)


def _moe_ffn(tile_expert, n_active, xs, wg, wu, wd, tm, tf):
    n_rows = xs.shape[0]
    n_tiles = n_rows // tm
    nf = D_FF // tf

    def row_map(i, f, te, na):
        return (jnp.minimum(i, na[0] - 1), 0)

    def fcol(i, f, na):
        return jnp.where(i < na[0], f, nf - 1)

    grid_spec = pltpu.PrefetchScalarGridSpec(
        num_scalar_prefetch=2,
        grid=(n_tiles, nf),
        in_specs=[
            pl.BlockSpec((tm, HALF_D), row_map),
            pl.BlockSpec((1, D_MODEL, tf), lambda i, f, te, na: (te[i], 0, fcol(i, f, na))),
            pl.BlockSpec((1, D_MODEL, tf), lambda i, f, te, na: (te[i], 0, fcol(i, f, na))),
            pl.BlockSpec((1, tf, D_MODEL), lambda i, f, te, na: (te[i], fcol(i, f, na), 0)),
        ],
        out_specs=pl.BlockSpec((tm, D_MODEL), lambda i, f, te, na: (i, 0)),
        scratch_shapes=[pltpu.VMEM((tm, D_MODEL), F32)],
    )
    return pl.pallas_call(
        _moe_ffn_kernel,
        grid_spec=grid_spec,
        out_shape=jax.ShapeDtypeStruct((n_rows, D_MODEL), F32),
        compiler_params=_cparams(("arbitrary", "arbitrary")),
        name="moe_ffn",
    )(tile_expert, n_active, xs, wg, wu, wd)


def _combine_kernel(dest_ref, dnext_ref, x_ref, meta_ref, fw_ref, ys_hbm, o_ref, g_scr, sems, *, tc):
    i = pl.program_id(0)

    def gather(d_ref, half, s):
        for t in range(tc):
            for j in range(TOP_K):
                row = d_ref[0, 0, (half * TOP_K + j) * tc + t]
                pltpu.make_async_copy(ys_hbm.at[pl.ds(row, 1)], g_scr.at[s, j, pl.ds(t, 1)],
                                      sems.at[s]).start(priority=j)

    def wait(s):
        for j in range(TOP_K):
            pltpu.make_async_copy(ys_hbm.at[pl.ds(0, tc)], g_scr.at[s, j], sems.at[s]).wait()

    def combine(s):
        rows = pl.ds(s * tc, tc)
        meta = meta_ref[rows, :]
        w0 = meta[:, M_W0:M_W0 + 1]
        w1 = meta[:, M_W1:M_W1 + 1]
        x = x_ref[rows, :] + w0 * g_scr[s, 0] + w1 * g_scr[s, 1]
        ms = jnp.mean(x * x, axis=-1, keepdims=True)
        o_ref[rows, :] = x * lax.rsqrt(ms + RMS_EPS) * fw_ref[...]

    @pl.when(i == 0)
    def _():
        gather(dest_ref, 0, 0)

    gather(dest_ref, 1, 1)
    wait(0)
    combine(0)
    gather(dnext_ref, 0, 0)
    wait(1)
    combine(1)

    @pl.when(i == pl.num_programs(0) - 1)
    def _():
        wait(0)


def _combine(dest, x2, meta, fw, ys, tc):
    T = x2.shape[0]
    n = T // (2 * tc)
    assert dest.shape == (2 * n, TOP_K, tc)
    dest3 = dest.reshape(n, 1, 2 * TOP_K * tc)
    dspec = lambda imap: pl.BlockSpec((1, 1, 2 * TOP_K * tc), imap, memory_space=pltpu.SMEM)
    return pl.pallas_call(
        functools.partial(_combine_kernel, tc=tc),
        grid=(n,),
        in_specs=[
            dspec(lambda i: (i, 0, 0)),
            dspec(lambda i: (jnp.minimum(i + 1, n - 1), 0, 0)),
            pl.BlockSpec((2 * tc, D_MODEL), lambda i: (i, 0)),
            pl.BlockSpec((2 * tc, LANE), lambda i: (i, 0)),
            pl.BlockSpec((1, D_MODEL), lambda i: (0, 0)),
            pl.BlockSpec(memory_space=pl.ANY),
        ],
        out_specs=pl.BlockSpec((2 * tc, D_MODEL), lambda i: (i, 0)),
        out_shape=jax.ShapeDtypeStruct((T, D_MODEL), F32),
        scratch_shapes=[pltpu.VMEM((2, TOP_K, tc, D_MODEL), F32), pltpu.SemaphoreType.DMA((2,))],
        compiler_params=_cparams(("arbitrary",)),
        name="moe_combine",
    )(dest3, dest3, x2, meta, fw, ys)


def _moe_plan(route, counts, tm, n_tiles):
    cnt = counts[0, :N_EXPERTS].astype(jnp.int32)
    padded = ((cnt + tm - 1) // tm) * tm
    ends = jnp.cumsum(padded)
    offs = ends - padded
    e = route[:, M_E0:M_E1 + 1, :].astype(jnp.int32)
    pos = route[:, M_P0:M_P1 + 1, :].astype(jnp.int32)
    dest = offs[e] + pos
    starts = jnp.arange(n_tiles, dtype=jnp.int32) * tm
    n_active = (ends[-1] // tm).astype(jnp.int32)
    te = jnp.sum(starts[:, None] >= ends[None, :], axis=-1).astype(jnp.int32)
    last_e = jnp.sum(jnp.maximum(ends[-1] - tm, 0) >= ends).astype(jnp.int32)
    te = jnp.where(starts < ends[-1], te, last_e)
    return dest, te, n_active.reshape(1)


def _moe_layer(x_mid, nw, w_router, wg, wu, wd, final_w, tiles):
    T = x_mid.shape[0]
    tm = tiles["moe_tm"]
    n_tiles = (TOP_K * T) // tm + N_EXPERTS
    tr = tiles["moe_row_t"]
    h, meta, route, counts = _router(x_mid, nw, w_router, tr)
    dest, te, n_active = _moe_plan(route, counts, tm, n_tiles)
    xs = _scatter_rows(dest, h, n_tiles * tm, tr)
    ys = _moe_ffn(te, n_active, xs, wg, wu, wd, tm, tiles["moe_tf"])
    return _combine(dest, x_mid, meta, final_w, ys, tr)


def _tiles(S, T):
    return dict(
        inproj_tm=min(1024, T), inproj_tn=2816,
        attn_t=min(512, S // 2), ret_c=min(256, S),
        merge_tm=min(512, T), ffn_tm=min(1024, T), ffn_tf=512,
        moe_row_t=min(256, T // 2),
        moe_tm=min(1024, T), moe_tf=512,
    )


def kernel(x, norm_mix_w, w_in, fox_f_bias, ret_gn_w, mla_q_norm_w, mla_kv_norm_w, mla_w_uq, mla_w_uk, mla_w_uv, w_br_fox, w_br_ret, w_br_mla, w_out, norm_ffn_w, dense_w_gate, dense_w_up, dense_w_down, moe_w_router, moe_w_gate, moe_w_up, moe_w_down, final_norm_w):
    B, S, D = x.shape
    assert D == D_MODEL and w_in.shape[0] == DEPTH == 2
    T = B * S
    tl = _tiles(S, T)
    row = lambda a: a.reshape(1, -1).astype(F32)
    ret_tabs = _ret_tables(S, tl["ret_c"])
    mla_tabs = _mla_tables(S)
    x2 = x.reshape(T, D)
    w_all, w_ff = _build_inproj_weights(w_in)
    for l in range(DEPTH):
        proj, ff32 = _inproj(x2, row(norm_mix_w[l]), w_all[l], w_ff[l], tl["inproj_tm"], tl["inproj_tn"])
        fqt, fkx, fvt = _fox_prep(proj, ff32, fox_f_bias[l], B, S, tl["attn_t"])
        o_fox = _attention(fqt, fkx, fvt, B, S, tl["attn_t"], "fox_attn")
        o_ret = _retention(proj, row(ret_gn_w[l]), ret_tabs, B, S, tl["ret_c"])
        wq, wk, wv = _mla_weights(mla_w_uq[l], mla_w_uk[l], mla_w_uv[l])
        mqt, mkx, mvt = _mla_prep(proj, row(mla_q_norm_w[l]), row(mla_kv_norm_w[l]), wq, wk, wv,
                                  mla_tabs, B, S, tl["attn_t"])
        o_mla = _attention(mqt, mkx, mvt, B, S, tl["attn_t"], "mla_attn")
        x_mid, h2 = _merge(o_fox, o_ret, o_mla, proj, x2,
                           w_br_fox[l].astype(BF16), w_br_ret[l].astype(BF16), w_br_mla[l].astype(BF16),
                           w_out[l].astype(BF16), row(norm_ffn_w[l]), tl["merge_tm"])
        if l % 2 == 0:
            i = l // 2
            x2 = _dense_ffn(h2, x_mid, dense_w_gate[i].astype(BF16), dense_w_up[i].astype(BF16),
                            dense_w_down[i].astype(BF16), tl["ffn_tm"], tl["ffn_tf"])
        else:
            i = l // 2
            x2 = _moe_layer(x_mid, row(norm_ffn_w[l]), moe_w_router[i], moe_w_gate[i],
                            moe_w_up[i], moe_w_down[i], row(final_norm_w), tl)
    return x2.reshape(B, S, D)
```

```python
import functools

import numpy as np
import jax
import jax.numpy as jnp
from jax import lax
from jax.experimental import pallas as pl
from jax.experimental.pallas import tpu as pltpu

F32 = jnp.float32
BF16 = jnp.bfloat16

D_MODEL = 1024
DEPTH = 2
FOX_HEADS = 8
FOX_HEAD_DIM = 64
FOX_W = FOX_HEADS * FOX_HEAD_DIM
RET_HEADS = 8
RET_QK_DIM = 64
RET_V_DIM = 128
RET_QK_W = RET_HEADS * RET_QK_DIM
RET_V_W = RET_HEADS * RET_V_DIM
MLA_HEADS = 8
MLA_Q_RANK = 384
MLA_KV_RANK = 256
MLA_NOPE_DIM = 64
MLA_ROPE_DIM = 32
MLA_V_DIM = 64
MLA_QK_DIM = MLA_NOPE_DIM + MLA_ROPE_DIM
MLA_V_W = MLA_HEADS * MLA_V_DIM
N_BRANCHES = 3
ROPE_THETA = 10000.0
RMS_EPS = 1e-6
D_FF = 3584
N_EXPERTS = 8
TOP_K = 2
IN_SPLITS = (FOX_W, FOX_W, FOX_W, FOX_HEADS,
             RET_QK_W, RET_QK_W, RET_V_W, RET_V_W,
             MLA_Q_RANK, MLA_KV_RANK, MLA_ROPE_DIM,
             N_BRANCHES * D_MODEL)

LANE = 128
NEG = -1e30
VMEM_LIMIT = 56 * 1024 * 1024

U_FQ, U_FK, U_FV = 0, 4, 8
U_RQ, U_RK = 12, 16
U_KR, U_CQ = 20, 21
U_RV, U_RG, U_GL, U_CKV = 24, 32, 40, 64
N_PROJ = 66 * LANE
FOX_EXT_STRIDE = 8
PAIRS = 4
QK_W = 2 * LANE
ATTN_KEY_CHUNK = 256
ATTN_STRIP = 256
VT_ROWS = 80
LOG2E = 1.4426950408889634


def _cparams(sem, vmem=VMEM_LIMIT, flags=None):
    return pltpu.CompilerParams(dimension_semantics=sem, vmem_limit_bytes=vmem, flags=flags)


def _sigmoid(x):
    return 1.0 / (1.0 + jnp.exp(-x))


def _lane_mask(shape, ranges):
    lane = lax.broadcasted_iota(jnp.int32, shape, len(shape) - 1)
    m = None
    for a, b in ranges:
        r = (lane >= a) & (lane < b)
        m = r if m is None else (m | r)
    return m


def _inproj_kernel(x_ref, nw_ref, w_ref, wff_ref, out_ref, ff_ref, h_scr):
    @pl.when(pl.program_id(1) == 0)
    def _():
        x = x_ref[...]
        ms = jnp.mean(x * x, axis=-1, keepdims=True)
        h = (x * lax.rsqrt(ms + RMS_EPS) * nw_ref[...]).astype(BF16)
        h_scr[...] = h
        ff_ref[...] = jnp.dot(h, wff_ref[...], preferred_element_type=F32)

    out_ref[...] = jnp.dot(h_scr[...], w_ref[...], preferred_element_type=F32).astype(out_ref.dtype)


def _inproj(x2, nw, w_all, w_ff, tm, tn):
    T = x2.shape[0]
    return pl.pallas_call(
        _inproj_kernel,
        grid=(T // tm, N_PROJ // tn),
        in_specs=[
            pl.BlockSpec((tm, D_MODEL), lambda i, j: (i, 0)),
            pl.BlockSpec((1, D_MODEL), lambda i, j: (0, 0)),
            pl.BlockSpec((D_MODEL, tn), lambda i, j: (0, j)),
            pl.BlockSpec((D_MODEL, LANE), lambda i, j: (0, 0)),
        ],
        out_specs=[
            pl.BlockSpec((tm, tn), lambda i, j: (i, j)),
            pl.BlockSpec((tm, LANE), lambda i, j: (i, 0)),
        ],
        out_shape=[
            jax.ShapeDtypeStruct((T, N_PROJ), BF16),
            jax.ShapeDtypeStruct((T, LANE), F32),
        ],
        scratch_shapes=[pltpu.VMEM((tm, D_MODEL), BF16)],
        compiler_params=_cparams(("parallel", "arbitrary")),
        name="inproj",
    )(x2, nw, w_all, w_ff)


def _rot_half_lanes(x, d):
    lane = lax.broadcasted_iota(jnp.int32, x.shape, 1)
    from_right = pltpu.roll(x, LANE - d // 2, 1)
    from_left = pltpu.roll(x, d // 2, 1)
    return jnp.where(lane % d < d // 2, -from_right, from_left)


def _wprep_kernel(w_ref, *out_refs):
    offs = [int(i) for i in np.cumsum((0,) + IN_SPLITS)]
    o_fq, o_fk, o_fv, o_ff, o_rq, o_rk, o_rv, o_rg, o_cq, o_ckv, o_kr, o_gl, _ = offs
    depth = w_ref.shape[0]
    lane = lax.broadcasted_iota(jnp.int32, (w_ref.shape[1], LANE), 1)
    rope_lanes = (lane >= MLA_NOPE_DIM) & (lane < MLA_QK_DIM)
    for l in range(depth):
        out_ref, ff_ref = out_refs[l], out_refs[depth + l]

        def put(unit, piece, out_ref=out_ref):
            out_ref[:, unit * LANE:unit * LANE + piece.shape[-1]] = piece.astype(out_ref.dtype)

        put(U_FQ, w_ref[l, :, o_fq:o_ff])
        ff_ref[...] = jnp.where(lane < FOX_HEADS, w_ref[l, :, o_ff:o_ff + LANE], 0.0).astype(ff_ref.dtype)
        put(U_RQ, w_ref[l, :, o_rq:o_rv])
        put(U_CKV, w_ref[l, :, o_ckv:o_kr])
        put(U_KR, jnp.where(rope_lanes, w_ref[l, :, o_kr - MLA_NOPE_DIM:o_kr - MLA_NOPE_DIM + LANE], 0.0))
        put(U_RV, w_ref[l, :, o_rv:o_rg])
        put(U_RG, w_ref[l, :, o_rg:o_cq])
        put(U_GL, w_ref[l, :, o_gl:o_gl + N_BRANCHES * D_MODEL])
        put(U_CQ, w_ref[l, :, o_cq:o_ckv])


def _build_inproj_weights(w_in, rows=128):
    depth, _, d_in = w_in.shape
    outs = pl.pallas_call(
        _wprep_kernel,
        grid=(D_MODEL // rows,),
        in_specs=[pl.BlockSpec((depth, rows, d_in), lambda r: (0, r, 0))],
        out_specs=([pl.BlockSpec((rows, N_PROJ), lambda r: (r, 0))] * depth
                   + [pl.BlockSpec((rows, LANE), lambda r: (r, 0))] * depth),
        out_shape=([jax.ShapeDtypeStruct((D_MODEL, N_PROJ), BF16)] * depth
                   + [jax.ShapeDtypeStruct((D_MODEL, LANE), BF16)] * depth),
        compiler_params=_cparams(("parallel",)),
        name="inproj_weight_prep",
    )(w_in)
    return outs[:depth], outs[depth:]


def _split3(x):
    hi = x.astype(BF16)
    r1 = x - hi.astype(F32)
    mid = r1.astype(BF16)
    lo = (r1 - mid.astype(F32)).astype(BF16)
    return hi, mid, lo


def _store_t(dst_ref, idx, x):
    dst_ref[idx] = x.T.astype(dst_ref.dtype)


def _store_vt_pair(vt_ref, p, v2):
    hd = LANE // 2
    vt = v2.T.astype(vt_ref.dtype)
    pad_rows = lax.broadcasted_iota(jnp.int32, (VT_ROWS - hd, vt.shape[1]), 0)
    ones_then_zeros = jnp.where(pad_rows == 0, 1.0, 0.0).astype(vt_ref.dtype)
    for j in range(2):
        base = (2 * p + j) * VT_ROWS
        vt_ref[0, 0, base:base + hd, :] = vt[j * hd:(j + 1) * hd]
        vt_ref[0, 0, base + hd:base + VT_ROWS, :] = ones_then_zeros


def _fox_prep_kernel(fq_ref, fk_ref, fv_ref, ff_ref, bias_ref, tri_ref, pq_ref, pk_ref, cq_ref, ck_ref,
                     qt_ref, kx_ref, vt_ref, carry_scr):
    @pl.when(pl.program_id(1) == 0)
    def _():
        carry_scr[...] = jnp.zeros_like(carry_scr)

    ts = ff_ref.shape[0]
    z = ff_ref[...] + bias_ref[...]
    logf = jnp.minimum(z, 0.0) - jnp.log(1.0 + jnp.exp(-jnp.abs(z)))
    tri = tri_ref[...]
    cum = carry_scr[0:1, :]
    for part in _split3(logf):
        cum = cum + jnp.dot(tri, part, preferred_element_type=F32)
    carry_scr[0:1, :] = cum[ts - 1:ts, :]

    cum3 = jnp.concatenate(_split3(cum * LOG2E), axis=-1)
    eq = cq_ref[...] + jnp.dot(cum3, pq_ref[...], preferred_element_type=F32)
    ek = ck_ref[...] + jnp.dot(cum3, pk_ref[...], preferred_element_type=F32)
    scale = FOX_HEAD_DIM ** -0.5 * LOG2E
    lane = lax.broadcasted_iota(jnp.int32, (ts, LANE), 1)
    for p in range(FOX_HEADS // 2):
        src = slice(p * LANE, (p + 1) * LANE)
        q2 = fq_ref[:, src].astype(F32) * scale
        k2 = fk_ref[:, src].astype(F32)
        for j in range(2):
            h = 2 * p + j
            dst = slice(h * LANE, (h + 1) * LANE)
            qh = q2 if j == 0 else pltpu.roll(q2, LANE // 2, 1)
            kh = k2 if j == 0 else pltpu.roll(k2, LANE // 2, 1)
            own = (lane >= FOX_HEAD_DIM + FOX_EXT_STRIDE * h) & (lane < FOX_HEAD_DIM + FOX_EXT_STRIDE * (h + 1))
            _store_t(qt_ref, (0, dst), jnp.where(lane < FOX_HEAD_DIM, qh, jnp.where(own, eq, 0.0)))
            kx_ref[:, dst] = jnp.where(lane < FOX_HEAD_DIM, kh, ek).astype(BF16)
        _store_vt_pair(vt_ref, p, fv_ref[:, src].astype(F32))


def _fox_prep_consts(ts):
    tri = np.tril(np.ones((ts, ts), np.float32))
    pq = np.zeros((3 * LANE, LANE), np.float32)
    pk = np.zeros((3 * LANE, LANE), np.float32)
    cq = np.zeros((1, LANE), np.float32)
    ck = np.zeros((1, LANE), np.float32)
    for h in range(FOX_HEADS):
        base = FOX_HEAD_DIM + FOX_EXT_STRIDE * h
        for j in range(3):
            pq[j * LANE + h, base + j] = 1.0
            ck[0, base + j] = 1.0
            cq[0, base + 3 + j] = 1.0
            pk[j * LANE + h, base + 3 + j] = -1.0
    return (jnp.asarray(tri, BF16), jnp.asarray(pq, BF16), jnp.asarray(pk, BF16),
            jnp.asarray(cq), jnp.asarray(ck))


def _attn_operand_specs(B, S, t):
    T = B * S
    ns = S // t
    specs = [
        pl.BlockSpec((1, PAIRS * QK_W, t), lambda b, s: (b, 0, s)),
        pl.BlockSpec((t, PAIRS * QK_W), lambda b, s: (b * ns + s, 0)),
        pl.BlockSpec((1, 1, 2 * PAIRS * VT_ROWS, t), lambda b, s: (b, s, 0, 0)),
    ]
    shapes = [
        jax.ShapeDtypeStruct((B, PAIRS * QK_W, S), BF16),
        jax.ShapeDtypeStruct((T, PAIRS * QK_W), BF16),
        jax.ShapeDtypeStruct((B, ns, 2 * PAIRS * VT_ROWS, t), BF16),
    ]
    return specs, shapes


def _fox_prep(proj, ff32, bias, B, S, ts):
    ns = S // ts
    tri, pq, pk, cq, ck = _fox_prep_consts(ts)
    bias128 = jnp.pad(bias.astype(F32), (0, LANE - FOX_HEADS)).reshape(1, LANE)
    const = lambda shape: pl.BlockSpec(shape, lambda b, s: (0, 0))
    out_specs, out_shape = _attn_operand_specs(B, S, ts)
    return pl.pallas_call(
        _fox_prep_kernel,
        grid=(B, ns),
        in_specs=[
            pl.BlockSpec((ts, FOX_W), lambda b, s: (b * ns + s, U_FQ // 4)),
            pl.BlockSpec((ts, FOX_W), lambda b, s: (b * ns + s, U_FK // 4)),
            pl.BlockSpec((ts, FOX_W), lambda b, s: (b * ns + s, U_FV // 4)),
            pl.BlockSpec((ts, LANE), lambda b, s: (b * ns + s, 0)),
            const((1, LANE)), const((ts, ts)), const(pq.shape), const(pk.shape), const(cq.shape), const(ck.shape),
        ],
        out_specs=out_specs,
        out_shape=out_shape,
        scratch_shapes=[pltpu.VMEM((8, LANE), F32)],
        compiler_params=_cparams(("parallel", "arbitrary")),
        name="fox_prep",
    )(proj, proj, proj, ff32, bias128, tri, pq, pk, cq, ck)


def _col_reduce(x, op, reduce_fn, parts=2):
    c = x.shape[0] // parts
    pieces = [x[i * c:(i + 1) * c] for i in range(parts)]
    while len(pieces) > 1:
        pieces = [op(pieces[i], pieces[i + 1]) for i in range(0, len(pieces), 2)]
    return reduce_fn(pieces[0], axis=0, keepdims=True)


def _attn_kernel(qt_ref, qn_ref, k_ref, vt_ref, o_ref, sa_scr, sb_scr, ma_scr, mb_scr, acc_scr, *, tk):
    qi = pl.program_id(2)
    tq = qt_ref.shape[-1]
    acc_scr[...] = jnp.zeros_like(acc_scr)

    def key_tile(kt):
        return k_ref[pl.ds(pl.multiple_of(kt * tk, tk), tk), :]

    strips = [(c, c + ATTN_STRIP) for c in range(0, tq, ATTN_STRIP)]

    def scores(kt, s_scr, m_scr):
        k = key_tile(kt)
        for h in range(2):
            for a, b in strips:
                s = jnp.dot(k[:, h * LANE:(h + 1) * LANE], qt_ref[0, h * LANE:(h + 1) * LANE, a:b],
                            preferred_element_type=F32)
                s_scr[h, :, a:b] = s
                m_scr[h, :, a:b] = _col_reduce(s, jnp.maximum, jnp.max)

    def update(kt, h, s, m_tile, m_prev, lo):
        m_new = jnp.maximum(m_prev, m_tile)
        alpha = jnp.exp2(m_prev - m_new)
        hi = lo + s.shape[-1]
        pv = None
        for r in range(0, tk, ATTN_KEY_CHUNK):
            p = jnp.exp2((s[r:r + ATTN_KEY_CHUNK] - m_new).astype(BF16))
            vt = vt_ref[0, kt, h * VT_ROWS:(h + 1) * VT_ROWS, r:r + ATTN_KEY_CHUNK]
            d = jnp.dot(vt, p, preferred_element_type=F32)
            pv = d if pv is None else pv + d
        acc_scr[h, :, lo:hi] = alpha * acc_scr[h, :, lo:hi] + pv
        return m_new

    def diag_update(kt, h, s_scr, m_prev, a, b, lo):
        s = s_scr[h, :, a:b]
        if a < lo + tk:
            key = lax.broadcasted_iota(jnp.int32, s.shape, 0)
            qry = lax.broadcasted_iota(jnp.int32, s.shape, 1) + (a - lo)
            s = jnp.where(key <= qry, s, NEG)
        return update(kt, h, s, _col_reduce(s, jnp.maximum, jnp.max), m_prev, a)

    def diagonal_tiles(carry, lookahead):
        k_last = key_tile(2 * qi + 1)
        if lookahead:
            k0 = key_tile(0)
        out = []
        for h in range(2):
            ms = []
            for a, b in strips:
                if a >= tk:
                    sb_scr[h, :, a:b] = jnp.dot(k_last[:, h * LANE:(h + 1) * LANE],
                                                qt_ref[0, h * LANE:(h + 1) * LANE, a:b],
                                                preferred_element_type=F32)
                ms.append(diag_update(2 * qi, h, sa_scr, carry[h][:, a:b], a, b, 0))
                if lookahead:
                    s = jnp.dot(k0[:, h * LANE:(h + 1) * LANE], qn_ref[0, h * LANE:(h + 1) * LANE, a:b],
                                preferred_element_type=F32)
                    sa_scr[h, :, a:b] = s
                    ma_scr[h, :, a:b] = _col_reduce(s, jnp.maximum, jnp.max)
            out.append(ms)
        for h in range(2):
            for i, (a, b) in enumerate(strips):
                if a >= tk:
                    diag_update(2 * qi + 1, h, sb_scr, out[h][i], a, b, tk)
        hd = LANE // 2
        ot = jnp.concatenate([acc_scr[h, :hd] / acc_scr[h, hd:hd + 1] for h in range(2)], axis=0)
        o_ref[...] = ot.T.astype(o_ref.dtype)

    def fused_step(kt, cur_s, cur_m, nxt_s, nxt_m, carry):
        k = key_tile(kt + 1)
        ms = [[], []]
        for a, b in strips:
            for h in range(2):
                s = jnp.dot(k[:, h * LANE:(h + 1) * LANE], qt_ref[0, h * LANE:(h + 1) * LANE, a:b],
                            preferred_element_type=F32)
                nxt_s[h, :, a:b] = s
                nxt_m[h, :, a:b] = _col_reduce(s, jnp.maximum, jnp.max)
                ms[h].append(update(kt, h, cur_s[h, :, a:b], cur_m[h, :, a:b], carry[h][:, a:b], a))
        return tuple(jnp.concatenate(m, axis=-1) for m in ms)

    def body(j, carry):
        kt = 2 * j
        carry = fused_step(kt, sa_scr, ma_scr, sb_scr, mb_scr, carry)
        return fused_step(kt + 1, sb_scr, mb_scr, sa_scr, ma_scr, carry)

    init = tuple(jnp.full((1, tq), NEG, F32) for _ in range(2))

    @pl.when(qi == 0)
    def _():
        scores(0, sa_scr, ma_scr)

    carry = lax.fori_loop(0, qi, body, init)
    last = pl.num_programs(2) - 1

    @pl.when(qi < last)
    def _():
        diagonal_tiles(carry, True)

    @pl.when(qi == last)
    def _():
        diagonal_tiles(carry, False)


def _attention(qt, kx, vt, B, S, tk, name):
    T = B * S
    tq = 2 * tk
    nq = S // tq
    kern = functools.partial(_attn_kernel, tk=tk)
    return pl.pallas_call(
        kern,
        grid=(B, PAIRS, nq),
        in_specs=[
            pl.BlockSpec((1, QK_W, tq), lambda b, p, i: (b, p, i)),
            pl.BlockSpec((1, QK_W, tq), lambda b, p, i: (b, p, jnp.minimum(i + 1, nq - 1))),
            pl.BlockSpec((S, QK_W), lambda b, p, i: (b, p)),
            pl.BlockSpec((1, S // tk, 2 * VT_ROWS, tk), lambda b, p, i: (b, 0, p, 0)),
        ],
        out_specs=pl.BlockSpec((tq, LANE), lambda b, p, i: (b * nq + i, p)),
        out_shape=jax.ShapeDtypeStruct((T, PAIRS * LANE), BF16),
        scratch_shapes=[
            pltpu.VMEM((2, tk, tq), F32),
            pltpu.VMEM((2, tk, tq), F32),
            pltpu.VMEM((2, 1, tq), F32),
            pltpu.VMEM((2, 1, tq), F32),
            pltpu.VMEM((2, VT_ROWS, tq), F32),
        ],
        compiler_params=_cparams(("parallel", "parallel", "arbitrary")),
        name=name,
    )(qt, qt, kx, vt)


def _mla_prep_kernel(cq_ref, ckv_ref, kr_ref, qnw_ref, kvnw_ref, wq_ref, wk_ref, wv_ref,
                     cq_tab, sq_tab, ck_tab, sk_tab, qt_ref, k_ref, vt_ref):
    def norm(x_ref, w_ref):
        x = x_ref[...].astype(F32)
        ms = jnp.mean(x * x, axis=-1, keepdims=True)
        return (x * lax.rsqrt(ms + RMS_EPS) * w_ref[...]).astype(BF16)

    c_q = norm(cq_ref, qnw_ref)
    c_kv = norm(ckv_ref, kvnw_ref)
    tile8 = lambda a: jnp.concatenate([a] * MLA_HEADS, axis=-1)
    q_raw = jnp.dot(c_q, wq_ref[...], preferred_element_type=F32)
    for c in range(MLA_HEADS):
        qs = q_raw[:, c * LANE:(c + 1) * LANE]
        q = qs * cq_tab[...] + _rot_half_lanes(qs, MLA_ROPE_DIM) * sq_tab[...]
        _store_t(qt_ref, (0, slice(c * LANE, (c + 1) * LANE)), q)
    kr = kr_ref[...].astype(F32)
    k_rope = kr * ck_tab[...] + _rot_half_lanes(kr, MLA_ROPE_DIM) * sk_tab[...]
    k = jnp.dot(c_kv, wk_ref[...], preferred_element_type=F32) + tile8(k_rope)
    k_ref[...] = k.astype(BF16)
    v = jnp.dot(c_kv, wv_ref[...], preferred_element_type=F32)
    for p in range(PAIRS):
        _store_vt_pair(vt_ref, p, v[:, p * LANE:(p + 1) * LANE])


def _mla_weights(w_uq, w_uk, w_uv):
    q3 = w_uq.reshape(MLA_Q_RANK, MLA_HEADS, MLA_QK_DIM)
    nope, ropep = q3[..., :MLA_NOPE_DIM], q3[..., MLA_NOPE_DIM:]
    z32 = jnp.zeros((MLA_Q_RANK, MLA_HEADS, LANE - MLA_QK_DIM), F32)
    wq = jnp.concatenate([nope, ropep, z32], axis=-1).reshape(MLA_Q_RANK, MLA_HEADS * LANE)
    k3 = w_uk.reshape(MLA_KV_RANK, MLA_HEADS, MLA_NOPE_DIM)
    wk = jnp.concatenate([k3, jnp.zeros_like(k3)], axis=-1).reshape(MLA_KV_RANK, MLA_HEADS * LANE)
    return wq.astype(BF16), wk.astype(BF16), w_uv.astype(BF16)


def _rope_inv_freq(d):
    return (np.float32(ROPE_THETA) ** (-np.arange(0, d, 2, dtype=np.float32) / np.float32(d))).astype(np.float32)


def _mla_tables(S):
    inv = _rope_inv_freq(MLA_ROPE_DIM)
    inv128 = np.concatenate([np.zeros(MLA_NOPE_DIM, np.float32), inv, inv, np.zeros(LANE - MLA_QK_DIM, np.float32)])
    keep = (np.arange(LANE) < MLA_QK_DIM).astype(np.float32)
    ang = jnp.arange(S, dtype=F32)[:, None] * jnp.asarray(inv128)[None, :]
    ctab = jnp.cos(ang) * jnp.asarray(keep)[None, :]
    stab = jnp.sin(ang)
    scale = MLA_QK_DIM ** -0.5 * LOG2E
    return ctab * scale, stab * scale, ctab, stab


def _mla_prep(proj, qnw, kvnw, wq, wk, wv, tabs, B, S, tm):
    ns = S // tm
    HW = MLA_HEADS * LANE
    const = lambda shape: pl.BlockSpec(shape, lambda b, s: (0, 0))
    tab = pl.BlockSpec((tm, LANE), lambda b, s: (s, 0))
    out_specs, out_shape = _attn_operand_specs(B, S, tm)
    return pl.pallas_call(
        _mla_prep_kernel,
        grid=(B, ns),
        in_specs=[
            pl.BlockSpec((tm, MLA_Q_RANK), lambda b, s: (b * ns + s, U_CQ // 3)),
            pl.BlockSpec((tm, MLA_KV_RANK), lambda b, s: (b * ns + s, U_CKV // 2)),
            pl.BlockSpec((tm, LANE), lambda b, s: (b * ns + s, U_KR)),
            const((1, MLA_Q_RANK)), const((1, MLA_KV_RANK)),
            const((MLA_Q_RANK, HW)), const((MLA_KV_RANK, HW)),
            const((MLA_KV_RANK, MLA_V_W)),
            tab, tab, tab, tab,
        ],
        out_specs=out_specs,
        out_shape=out_shape,
        compiler_params=_cparams(("parallel", "parallel")),
        name="mla_prep",
    )(proj, proj, proj, qnw, kvnw, wq, wk, wv, *tabs)


def _ret_kernel(rq_ref, rk_ref, rv_ref, rg_ref, cos_ref, sin_ref, perm_ref,
                din_ref, xi_ref, zeta_ref, gch_ref, gnw_ref, o_ref, r_scr):
    @pl.when(pl.program_id(0) == 0)
    def _():
        r_scr[...] = jnp.zeros_like(r_scr)

    tile4 = lambda a: jnp.concatenate([a] * (RET_QK_W // LANE), axis=-1)
    cos4 = tile4(cos_ref[...])
    sin4 = tile4(sin_ref[...])
    for bi in range(rq_ref.shape[0]):
        def rope(x_ref):
            x = x_ref[bi]
            rot = jnp.dot(x, perm_ref[...], preferred_element_type=F32)
            return x.astype(F32) * cos4 + rot * sin4

        q = rope(rq_ref)
        k = rope(rk_ref) * (RET_QK_DIM ** -0.5)
        for p in range(RET_HEADS // 2):
            q2 = q[:, p * LANE:(p + 1) * LANE]
            k2 = k[:, p * LANE:(p + 1) * LANE]
            k2b = k2.astype(BF16)
            kz2 = (k2 * zeta_ref[p]).astype(BF16)
            r2 = r_scr[bi, p]
            r2b = r2.astype(BF16)
            new_r = gch_ref[p] * r2
            for j in range(2):
                h = 2 * p + j
                half = ((j * RET_QK_DIM, (j + 1) * RET_QK_DIM),)
                qm = jnp.where(_lane_mask(q2.shape, half), q2, 0.0).astype(BF16)
                inner = lax.dot_general(qm, k2b, (((1,), (1,)), ((), ())), preferred_element_type=F32)
                inner = inner * din_ref[h]
                vh = rv_ref[bi, :, h * LANE:(h + 1) * LANE]
                o = (jnp.dot(inner.astype(BF16), vh, preferred_element_type=F32)
                     + jnp.dot(qm, r2b, preferred_element_type=F32) * xi_ref[h])
                upd = lax.dot_general(kz2, vh, (((0,), (0,)), ((), ())), preferred_element_type=F32)
                rows = lax.broadcasted_iota(jnp.int32, upd.shape, 0)
                new_r = new_r + jnp.where((rows >= half[0][0]) & (rows < half[0][1]), upd, 0.0)
                mu = jnp.mean(o, axis=-1, keepdims=True)
                d = o - mu
                var = jnp.mean(d * d, axis=-1, keepdims=True)
                on = d * lax.rsqrt(var + RMS_EPS) * gnw_ref[:, h * LANE:(h + 1) * LANE]
                g = rg_ref[bi, :, h * LANE:(h + 1) * LANE].astype(F32)
                o_ref[bi, :, h * LANE:(h + 1) * LANE] = (g * _sigmoid(g) * on).astype(o_ref.dtype)
            r_scr[bi, p] = new_r


def _ret_tables(S, C):
    inv128 = np.tile(_rope_inv_freq(RET_QK_DIM), 2 * LANE // RET_QK_DIM)
    ang = jnp.arange(S, dtype=F32)[:, None] * jnp.asarray(inv128)[None, :]
    cos = jnp.cos(ang)
    sin = jnp.sin(ang)
    gammas = 1.0 - 2.0 ** (-5.0 - jnp.arange(RET_HEADS, dtype=F32))
    log_g = jnp.log(gammas)
    j = jnp.arange(C, dtype=F32)
    diff = j[:, None] - j[None, :]
    din = jnp.where(diff[None] >= 0, jnp.exp(jnp.maximum(diff, 0.0)[None] * log_g[:, None, None]), 0.0)
    xi = jnp.exp((j[None, :] + 1.0) * log_g[:, None])
    zeta = jnp.exp((C - 1.0 - j[None, :]) * log_g[:, None])
    gch = jnp.exp(C * log_g)
    xi_t = jnp.broadcast_to(xi[:, :, None], (RET_HEADS, C, LANE))
    zeta_p = jnp.repeat(zeta.reshape(RET_HEADS // 2, 2, C).transpose(0, 2, 1), RET_QK_DIM, axis=-1)
    gch_p = jnp.broadcast_to(jnp.repeat(gch.reshape(RET_HEADS // 2, 2), RET_QK_DIM, axis=-1)[:, :, None],
                             (RET_HEADS // 2, LANE, LANE))
    return cos, sin, din, xi_t, zeta_p, gch_p


def _retention(proj, gnw, tabs, B, S, C):
    nc = S // C
    cos, sin, din, xi, zeta, gch = tabs
    proj3 = proj.reshape(B, S, N_PROJ)
    perm = np.zeros((RET_QK_W, RET_QK_W), np.float32)
    half = RET_QK_DIM // 2
    for h in range(RET_HEADS):
        for j in range(half):
            perm[h * RET_QK_DIM + half + j, h * RET_QK_DIM + j] = -1.0
            perm[h * RET_QK_DIM + j, h * RET_QK_DIM + half + j] = 1.0
    perm = jnp.asarray(perm, BF16)
    blk = lambda w, unit: pl.BlockSpec((B, C, w), lambda c: (0, c, unit * LANE // w))
    full = lambda a: pl.BlockSpec(a.shape, lambda c: (0,) * a.ndim)
    out = pl.pallas_call(
        _ret_kernel,
        grid=(nc,),
        in_specs=[
            blk(RET_QK_W, U_RQ), blk(RET_QK_W, U_RK),
            blk(RET_V_W, U_RV), blk(RET_V_W, U_RG),
            pl.BlockSpec((C, LANE), lambda c: (c, 0)),
            pl.BlockSpec((C, LANE), lambda c: (c, 0)),
            full(perm), full(din), full(xi), full(zeta), full(gch), full(gnw),
        ],
        out_specs=pl.BlockSpec((B, C, RET_V_W), lambda c: (0, c, 0)),
        out_shape=jax.ShapeDtypeStruct((B, S, RET_V_W), BF16),
        scratch_shapes=[pltpu.VMEM((B, RET_HEADS // 2, LANE, LANE), F32)],
        compiler_params=_cparams(("arbitrary",)),
        name="retention",
    )(proj3, proj3, proj3, proj3, cos, sin, perm, din, xi, zeta, gch, gnw)
    return out.reshape(B * S, RET_V_W)


def _merge_kernel(of_ref, or_ref, om_ref, g0_ref, g1_ref, g2_ref, x_ref,
                  wf_ref, wr_ref, wm_ref, wo_ref, nw_ref, xo_ref, h_ref):
    merged = (_sigmoid(g0_ref[...].astype(F32)) * jnp.dot(of_ref[...], wf_ref[...], preferred_element_type=F32)
              + _sigmoid(g1_ref[...].astype(F32)) * jnp.dot(or_ref[...], wr_ref[...], preferred_element_type=F32)
              + _sigmoid(g2_ref[...].astype(F32)) * jnp.dot(om_ref[...], wm_ref[...], preferred_element_type=F32))
    x = x_ref[...] + jnp.dot(merged.astype(BF16), wo_ref[...], preferred_element_type=F32)
    xo_ref[...] = x
    ms = jnp.mean(x * x, axis=-1, keepdims=True)
    h_ref[...] = (x * lax.rsqrt(ms + RMS_EPS) * nw_ref[...]).astype(h_ref.dtype)


def _merge(o_fox, o_ret, o_mla, proj, x2, wf, wr, wm, wo, nw, tm):
    T = x2.shape[0]
    row = lambda w, cb=0: pl.BlockSpec((tm, w), lambda i: (i, cb))
    const = lambda a: pl.BlockSpec(a.shape, lambda i: (0, 0))
    g_unit = U_GL // 8
    return pl.pallas_call(
        _merge_kernel,
        grid=(T // tm,),
        in_specs=[
            row(FOX_W), row(RET_V_W), row(MLA_V_W),
            row(D_MODEL, g_unit), row(D_MODEL, g_unit + 1), row(D_MODEL, g_unit + 2),
            row(D_MODEL),
            const(wf), const(wr), const(wm), const(wo), const(nw),
        ],
        out_specs=[row(D_MODEL), row(D_MODEL)],
        out_shape=[jax.ShapeDtypeStruct((T, D_MODEL), F32), jax.ShapeDtypeStruct((T, D_MODEL), BF16)],
        compiler_params=_cparams(("parallel",)),
        name="merge",
    )(o_fox, o_ret, o_mla, proj, proj, proj, x2, wf, wr, wm, wo, nw)


def _swiglu_tile(h, wg, wu, wd):
    a = jnp.dot(h, wg, preferred_element_type=F32)
    b = jnp.dot(h, wu, preferred_element_type=F32)
    return jnp.dot((a * _sigmoid(a) * b).astype(BF16), wd, preferred_element_type=F32)


def _dense_ffn_kernel(h_ref, wg_ref, wu_ref, wd_ref, x_ref, o_ref, acc_scr):
    f = pl.program_id(1)

    @pl.when(f == 0)
    def _():
        acc_scr[...] = jnp.zeros_like(acc_scr)

    acc_scr[...] += _swiglu_tile(h_ref[...], wg_ref[...], wu_ref[...], wd_ref[...])

    @pl.when(f == pl.num_programs(1) - 1)
    def _():
        o_ref[...] = x_ref[...] + acc_scr[...]


def _dense_ffn(h, x2, wg, wu, wd, tm, tf):
    T = x2.shape[0]
    return pl.pallas_call(
        _dense_ffn_kernel,
        grid=(T // tm, D_FF // tf),
        in_specs=[
            pl.BlockSpec((tm, D_MODEL), lambda i, f: (i, 0)),
            pl.BlockSpec((D_MODEL, tf), lambda i, f: (0, f)),
            pl.BlockSpec((D_MODEL, tf), lambda i, f: (0, f)),
            pl.BlockSpec((tf, D_MODEL), lambda i, f: (f, 0)),
            pl.BlockSpec((tm, D_MODEL), lambda i, f: (i, 0)),
        ],
        out_specs=pl.BlockSpec((tm, D_MODEL), lambda i, f: (i, 0)),
        out_shape=jax.ShapeDtypeStruct((T, D_MODEL), F32),
        scratch_shapes=[pltpu.VMEM((tm, D_MODEL), F32)],
        compiler_params=_cparams(("parallel", "arbitrary")),
        name="dense_ffn",
    )(h, wg, wu, wd, x2)


M_E0, M_E1, M_P0, M_P1, M_W0, M_W1 = 0, 1, 2, 3, 4, 5
ROUTE_ROWS = 8


def _split2(x):
    hi = x.astype(BF16)
    lo = (x - hi.astype(F32)).astype(BF16)
    return hi, lo


HALF_D = D_MODEL // 2


def _pack_rows(h):
    bits = pltpu.bitcast(h.astype(BF16).astype(F32), jnp.uint32)
    return (bits[:, :HALF_D] & jnp.uint32(0xFFFF0000)) | (bits[:, HALF_D:] >> 16)


def _unpack_rows(u):
    left = pltpu.bitcast(u & jnp.uint32(0xFFFF0000), F32)
    right = pltpu.bitcast(u << 16, F32)
    return jnp.concatenate([left, right], axis=-1).astype(BF16)


def _router_kernel(x_ref, nw_ref, whi_ref, wlo_ref, tri_ref, h_ref, meta_ref, route_ref, cnt_ref, carry_scr):
    @pl.when(pl.program_id(0) == 0)
    def _():
        carry_scr[...] = jnp.zeros_like(carry_scr)

    x = x_ref[...]
    ms = jnp.mean(x * x, axis=-1, keepdims=True)
    h = x * lax.rsqrt(ms + RMS_EPS) * nw_ref[...]
    h_ref[...] = _pack_rows(h)
    hhi, hlo = _split2(h)
    logits = (jnp.dot(hhi, whi_ref[...], preferred_element_type=F32)
              + jnp.dot(hhi, wlo_ref[...], preferred_element_type=F32)
              + jnp.dot(hlo, whi_ref[...], preferred_element_type=F32))
    lane = lax.broadcasted_iota(jnp.int32, logits.shape, 1)
    lanef = lane.astype(F32)
    logits = jnp.where(lane < N_EXPERTS, logits, NEG)
    m0 = jnp.max(logits, axis=-1, keepdims=True)
    e0 = jnp.min(jnp.where(logits == m0, lanef, float(LANE)), axis=-1, keepdims=True)
    oh0 = lanef == e0
    rest = jnp.where(oh0, NEG, logits)
    m1 = jnp.max(rest, axis=-1, keepdims=True)
    e1 = jnp.min(jnp.where(rest == m1, lanef, float(LANE)), axis=-1, keepdims=True)
    oh1 = lanef == e1
    z = jnp.exp(m1 - m0)
    w0 = 1.0 / (1.0 + z)
    w1 = z / (1.0 + z)
    both = jnp.where(oh0 | oh1, 1.0, 0.0)
    before = carry_scr[0:1, :] + jnp.dot(tri_ref[...], both.astype(BF16), preferred_element_type=F32)
    p0 = jnp.sum(jnp.where(oh0, before, 0.0), axis=-1, keepdims=True)
    p1 = jnp.sum(jnp.where(oh1, before, 0.0), axis=-1, keepdims=True)
    total = carry_scr[0:1, :] + jnp.sum(both, axis=0, keepdims=True)
    carry_scr[0:1, :] = total
    cnt_ref[...] = jnp.broadcast_to(total, cnt_ref.shape)
    meta = jnp.zeros(logits.shape, F32)
    for idx, val in ((M_E0, e0), (M_E1, e1), (M_P0, p0), (M_P1, p1), (M_W0, w0), (M_W1, w1)):
        meta = jnp.where(lane == idx, val, meta)
    meta_ref[...] = meta
    route_ref[0] = meta.T[:ROUTE_ROWS, :]


def _router(x2, nw, w_router, tr):
    T = x2.shape[0]
    wr = jnp.pad(w_router.astype(F32), ((0, 0), (0, LANE - N_EXPERTS)))
    whi = wr.astype(BF16)
    wlo = (wr - whi.astype(F32)).astype(BF16)
    tri = jnp.asarray(np.tril(np.ones((tr, tr), np.float32), -1), BF16)
    const = lambda a: pl.BlockSpec(a.shape, lambda i: (0, 0))
    return pl.pallas_call(
        _router_kernel,
        grid=(T // tr,),
        in_specs=[pl.BlockSpec((tr, D_MODEL), lambda i: (i, 0)), const(nw), const(whi), const(wlo), const(tri)],
        out_specs=[
            pl.BlockSpec((tr, HALF_D), lambda i: (i, 0)),
            pl.BlockSpec((tr, LANE), lambda i: (i, 0)),
            pl.BlockSpec((1, ROUTE_ROWS, tr), lambda i: (i, 0, 0)),
            pl.BlockSpec((8, LANE), lambda i: (0, 0)),
        ],
        out_shape=[
            jax.ShapeDtypeStruct((T, HALF_D), jnp.uint32),
            jax.ShapeDtypeStruct((T, LANE), F32),
            jax.ShapeDtypeStruct((T // tr, ROUTE_ROWS, tr), F32),
            jax.ShapeDtypeStruct((8, LANE), F32),
        ],
        scratch_shapes=[pltpu.VMEM((8, LANE), F32)],
        compiler_params=_cparams(("arbitrary",)),
        name="router",
    )(x2, nw, whi, wlo, tri)


def _scatter_kernel(dest_ref, h_ref, xs_in, xs_hbm, sem, *, tt):
    del xs_in

    def row_copy(t, d_row):
        return pltpu.make_async_copy(h_ref.at[pl.ds(t, 1)], xs_hbm.at[pl.ds(d_row, 1)], sem)

    for t in range(tt):
        for j in range(TOP_K):
            row_copy(t, dest_ref[0, 0, j * tt + t]).start(priority=j)

    for _ in range(TOP_K):
        pltpu.make_async_copy(h_ref, xs_hbm.at[pl.ds(0, tt)], sem).wait()


def _scatter_rows(dest, h, n_rows, tt):
    T = h.shape[0]
    width = h.shape[1]
    xs0 = jnp.zeros((n_rows, width), h.dtype)
    assert dest.shape == (T // tt, TOP_K, tt)
    dest3 = dest.reshape(T // tt, 1, TOP_K * tt)
    return pl.pallas_call(
        functools.partial(_scatter_kernel, tt=tt),
        grid=(T // tt,),
        in_specs=[
            pl.BlockSpec((1, 1, 2 * tt), lambda i: (i, 0, 0), memory_space=pltpu.SMEM),
            pl.BlockSpec((tt, width), lambda i: (i, 0)),
            pl.BlockSpec(memory_space=pl.ANY),
        ],
        out_specs=pl.BlockSpec(memory_space=pl.ANY),
        out_shape=jax.ShapeDtypeStruct((n_rows, width), h.dtype),
        scratch_shapes=[pltpu.SemaphoreType.DMA(())],
        input_output_aliases={2: 0},
        compiler_params=_cparams(("arbitrary",)),
        name="moe_scatter",
    )(dest3, h, xs0)


def _moe_ffn_kernel(te_ref, na_ref, xs_ref, wg_ref, wu_ref, wd_ref, ys_ref, acc_scr):
    i = pl.program_id(0)
    f = pl.program_id(1)

    @pl.when(i < na_ref[0])
    def _():
        @pl.when(f == 0)
        def _():
            acc_scr[...] = jnp.zeros_like(acc_scr)

        acc_scr[...] += _swiglu_tile(_unpack_rows(xs_ref[...]), wg_ref[0].astype(BF16),
                                     wu_ref[0].astype(BF16), wd_ref[0].astype(BF16))

        @pl.when(f == pl.num_programs(1) - 1)
        def _():
            ys_ref[...] = acc_scr[...]

    @pl.when((i >= na_ref[0]) & (f == 0))
    def _():
        ys_ref[...] = jnp.zeros_like(ys_ref)


def _moe_ffn(tile_expert, n_active, xs, wg, wu, wd, tm, tf):
    n_rows = xs.shape[0]
    n_tiles = n_rows // tm
    nf = D_FF // tf

    def row_map(i, f, te, na):
        return (jnp.minimum(i, na[0] - 1), 0)

    def fcol(i, f, na):
        return jnp.where(i < na[0], f, nf - 1)

    grid_spec = pltpu.PrefetchScalarGridSpec(
        num_scalar_prefetch=2,
        grid=(n_tiles, nf),
        in_specs=[
            pl.BlockSpec((tm, HALF_D), row_map),
            pl.BlockSpec((1, D_MODEL, tf), lambda i, f, te, na: (te[i], 0, fcol(i, f, na))),
            pl.BlockSpec((1, D_MODEL, tf), lambda i, f, te, na: (te[i], 0, fcol(i, f, na))),
            pl.BlockSpec((1, tf, D_MODEL), lambda i, f, te, na: (te[i], fcol(i, f, na), 0)),
        ],
        out_specs=pl.BlockSpec((tm, D_MODEL), lambda i, f, te, na: (i, 0)),
        scratch_shapes=[pltpu.VMEM((tm, D_MODEL), F32)],
    )
    return pl.pallas_call(
        _moe_ffn_kernel,
        grid_spec=grid_spec,
        out_shape=jax.ShapeDtypeStruct((n_rows, D_MODEL), F32),
        compiler_params=_cparams(("arbitrary", "arbitrary")),
        name="moe_ffn",
    )(tile_expert, n_active, xs, wg, wu, wd)


def _combine_kernel(dest_ref, dnext_ref, x_ref, meta_ref, fw_ref, ys_hbm, o_ref, g_scr, sems, *, tc):
    i = pl.program_id(0)

    def gather(d_ref, half, s):
        for t in range(tc):
            for j in range(TOP_K):
                row = d_ref[0, 0, (half * TOP_K + j) * tc + t]
                pltpu.make_async_copy(ys_hbm.at[pl.ds(row, 1)], g_scr.at[s, j, pl.ds(t, 1)],
                                      sems.at[s]).start(priority=j)

    def wait(s):
        for j in range(TOP_K):
            pltpu.make_async_copy(ys_hbm.at[pl.ds(0, tc)], g_scr.at[s, j], sems.at[s]).wait()

    def combine(s):
        rows = pl.ds(s * tc, tc)
        meta = meta_ref[rows, :]
        w0 = meta[:, M_W0:M_W0 + 1]
        w1 = meta[:, M_W1:M_W1 + 1]
        x = x_ref[rows, :] + w0 * g_scr[s, 0] + w1 * g_scr[s, 1]
        ms = jnp.mean(x * x, axis=-1, keepdims=True)
        o_ref[rows, :] = x * lax.rsqrt(ms + RMS_EPS) * fw_ref[...]

    @pl.when(i == 0)
    def _():
        gather(dest_ref, 0, 0)

    gather(dest_ref, 1, 1)
    wait(0)
    combine(0)
    gather(dnext_ref, 0, 0)
    wait(1)
    combine(1)

    @pl.when(i == pl.num_programs(0) - 1)
    def _():
        wait(0)


def _combine(dest, x2, meta, fw, ys, tc):
    T = x2.shape[0]
    n = T // (2 * tc)
    assert dest.shape == (2 * n, TOP_K, tc)
    dest3 = dest.reshape(n, 1, 2 * TOP_K * tc)
    dspec = lambda imap: pl.BlockSpec((1, 1, 2 * TOP_K * tc), imap, memory_space=pltpu.SMEM)
    return pl.pallas_call(
        functools.partial(_combine_kernel, tc=tc),
        grid=(n,),
        in_specs=[
            dspec(lambda i: (i, 0, 0)),
            dspec(lambda i: (jnp.minimum(i + 1, n - 1), 0, 0)),
            pl.BlockSpec((2 * tc, D_MODEL), lambda i: (i, 0)),
            pl.BlockSpec((2 * tc, LANE), lambda i: (i, 0)),
            pl.BlockSpec((1, D_MODEL), lambda i: (0, 0)),
            pl.BlockSpec(memory_space=pl.ANY),
        ],
        out_specs=pl.BlockSpec((2 * tc, D_MODEL), lambda i: (i, 0)),
        out_shape=jax.ShapeDtypeStruct((T, D_MODEL), F32),
        scratch_shapes=[pltpu.VMEM((2, TOP_K, tc, D_MODEL), F32), pltpu.SemaphoreType.DMA((2,))],
        compiler_params=_cparams(("arbitrary",)),
        name="moe_combine",
    )(dest3, dest3, x2, meta, fw, ys)


def _moe_plan(route, counts, tm, n_tiles):
    cnt = counts[0, :N_EXPERTS].astype(jnp.int32)
    padded = ((cnt + tm - 1) // tm) * tm
    ends = jnp.cumsum(padded)
    offs = ends - padded
    e = route[:, M_E0:M_E1 + 1, :].astype(jnp.int32)
    pos = route[:, M_P0:M_P1 + 1, :].astype(jnp.int32)
    dest = offs[e] + pos
    starts = jnp.arange(n_tiles, dtype=jnp.int32) * tm
    n_active = (ends[-1] // tm).astype(jnp.int32)
    te = jnp.sum(starts[:, None] >= ends[None, :], axis=-1).astype(jnp.int32)
    last_e = jnp.sum(jnp.maximum(ends[-1] - tm, 0) >= ends).astype(jnp.int32)
    te = jnp.where(starts < ends[-1], te, last_e)
    return dest, te, n_active.reshape(1)


def _moe_layer(x_mid, nw, w_router, wg, wu, wd, final_w, tiles):
    T = x_mid.shape[0]
    tm = tiles["moe_tm"]
    n_tiles = (TOP_K * T) // tm + N_EXPERTS
    tr = tiles["moe_row_t"]
    h, meta, route, counts = _router(x_mid, nw, w_router, tr)
    dest, te, n_active = _moe_plan(route, counts, tm, n_tiles)
    xs = _scatter_rows(dest, h, n_tiles * tm, tr)
    ys = _moe_ffn(te, n_active, xs, wg, wu, wd, tm, tiles["moe_tf"])
    return _combine(dest, x_mid, meta, final_w, ys, tr)


def _tiles(S, T):
    return dict(
        inproj_tm=min(1024, T), inproj_tn=2816,
        attn_t=min(512, S // 2), ret_c=min(256, S),
        merge_tm=min(512, T), ffn_tm=min(1024, T), ffn_tf=512,
        moe_row_t=min(256, T // 2),
        moe_tm=min(1024, T), moe_tf=512,
    )


def kernel(x, norm_mix_w, w_in, fox_f_bias, ret_gn_w, mla_q_norm_w, mla_kv_norm_w, mla_w_uq, mla_w_uk, mla_w_uv, w_br_fox, w_br_ret, w_br_mla, w_out, norm_ffn_w, dense_w_gate, dense_w_up, dense_w_down, moe_w_router, moe_w_gate, moe_w_up, moe_w_down, final_norm_w):
    B, S, D = x.shape
    assert D == D_MODEL and w_in.shape[0] == DEPTH == 2
    T = B * S
    tl = _tiles(S, T)
    row = lambda a: a.reshape(1, -1).astype(F32)
    ret_tabs = _ret_tables(S, tl["ret_c"])
    mla_tabs = _mla_tables(S)
    x2 = x.reshape(T, D)
    w_all, w_ff = _build_inproj_weights(w_in)
    for l in range(DEPTH):
        proj, ff32 = _inproj(x2, row(norm_mix_w[l]), w_all[l], w_ff[l], tl["inproj_tm"], tl["inproj_tn"])
        fqt, fkx, fvt = _fox_prep(proj, ff32, fox_f_bias[l], B, S, tl["attn_t"])
        o_fox = _attention(fqt, fkx, fvt, B, S, tl["attn_t"], "fox_attn")
        o_ret = _retention(proj, row(ret_gn_w[l]), ret_tabs, B, S, tl["ret_c"])
        wq, wk, wv = _mla_weights(mla_w_uq[l], mla_w_uk[l], mla_w_uv[l])
        mqt, mkx, mvt = _mla_prep(proj, row(mla_q_norm_w[l]), row(mla_kv_norm_w[l]), wq, wk, wv,
                                  mla_tabs, B, S, tl["attn_t"])
        o_mla = _attention(mqt, mkx, mvt, B, S, tl["attn_t"], "mla_attn")
        x_mid, h2 = _merge(o_fox, o_ret, o_mla, proj, x2,
                           w_br_fox[l].astype(BF16), w_br_ret[l].astype(BF16), w_br_mla[l].astype(BF16),
                           w_out[l].astype(BF16), row(norm_ffn_w[l]), tl["merge_tm"])
        if l % 2 == 0:
            i = l // 2
            x2 = _dense_ffn(h2, x_mid, dense_w_gate[i].astype(BF16), dense_w_up[i].astype(BF16),
                            dense_w_down[i].astype(BF16), tl["ffn_tm"], tl["ffn_tf"])
        else:
            i = l // 2
            x2 = _moe_layer(x_mid, row(norm_ffn_w[l]), moe_w_router[i], moe_w_gate[i],
                            moe_w_up[i], moe_w_down[i], row(final_norm_w), tl)
    return x2.reshape(B, S, D)
```

```python
import functools

import numpy as np
import jax
import jax.numpy as jnp
from jax import lax
from jax.experimental import pallas as pl
from jax.experimental.pallas import tpu as pltpu

F32 = jnp.float32
BF16 = jnp.bfloat16

D_MODEL = 1024
DEPTH = 2
FOX_HEADS = 8
FOX_HEAD_DIM = 64
FOX_W = FOX_HEADS * FOX_HEAD_DIM
RET_HEADS = 8
RET_QK_DIM = 64
RET_V_DIM = 128
RET_QK_W = RET_HEADS * RET_QK_DIM
RET_V_W = RET_HEADS * RET_V_DIM
MLA_HEADS = 8
MLA_Q_RANK = 384
MLA_KV_RANK = 256
MLA_NOPE_DIM = 64
MLA_ROPE_DIM = 32
MLA_V_DIM = 64
MLA_QK_DIM = MLA_NOPE_DIM + MLA_ROPE_DIM
MLA_V_W = MLA_HEADS * MLA_V_DIM
N_BRANCHES = 3
ROPE_THETA = 10000.0
RMS_EPS = 1e-6
D_FF = 3584
N_EXPERTS = 8
TOP_K = 2
IN_SPLITS = (FOX_W, FOX_W, FOX_W, FOX_HEADS,
             RET_QK_W, RET_QK_W, RET_V_W, RET_V_W,
             MLA_Q_RANK, MLA_KV_RANK, MLA_ROPE_DIM,
             N_BRANCHES * D_MODEL)

LANE = 128
NEG = -1e30
VMEM_LIMIT = 56 * 1024 * 1024

U_FQ, U_FK, U_FV = 0, 4, 8
U_RQ, U_RK = 12, 16
U_KR, U_CQ = 20, 21
U_RV, U_RG, U_GL, U_CKV = 24, 32, 40, 64
N_PROJ = 66 * LANE
FOX_EXT_STRIDE = 8
PAIRS = 4
QK_W = 2 * LANE
ATTN_KEY_CHUNK = 256
ATTN_STRIP = 256
VT_ROWS = 80
LOG2E = 1.4426950408889634


def _cparams(sem, vmem=VMEM_LIMIT):
    return pltpu.CompilerParams(dimension_semantics=sem, vmem_limit_bytes=vmem)


def _sigmoid(x):
    return 1.0 / (1.0 + jnp.exp(-x))


def _lane_mask(shape, ranges):
    lane = lax.broadcasted_iota(jnp.int32, shape, len(shape) - 1)
    m = None
    for a, b in ranges:
        r = (lane >= a) & (lane < b)
        m = r if m is None else (m | r)
    return m


def _inproj_kernel(x_ref, nw_ref, w_ref, wff_ref, out_ref, ff_ref, h_scr):
    @pl.when(pl.program_id(1) == 0)
    def _():
        x = x_ref[...]
        ms = jnp.mean(x * x, axis=-1, keepdims=True)
        h = (x * lax.rsqrt(ms + RMS_EPS) * nw_ref[...]).astype(BF16)
        h_scr[...] = h
        ff_ref[...] = jnp.dot(h, wff_ref[...], preferred_element_type=F32)

    out_ref[...] = jnp.dot(h_scr[...], w_ref[...], preferred_element_type=F32).astype(out_ref.dtype)


def _inproj(x2, nw, w_all, w_ff, tm, tn):
    T = x2.shape[0]
    return pl.pallas_call(
        _inproj_kernel,
        grid=(T // tm, N_PROJ // tn),
        in_specs=[
            pl.BlockSpec((tm, D_MODEL), lambda i, j: (i, 0)),
            pl.BlockSpec((1, D_MODEL), lambda i, j: (0, 0)),
            pl.BlockSpec((D_MODEL, tn), lambda i, j: (0, j)),
            pl.BlockSpec((D_MODEL, LANE), lambda i, j: (0, 0)),
        ],
        out_specs=[
            pl.BlockSpec((tm, tn), lambda i, j: (i, j)),
            pl.BlockSpec((tm, LANE), lambda i, j: (i, 0)),
        ],
        out_shape=[
            jax.ShapeDtypeStruct((T, N_PROJ), BF16),
            jax.ShapeDtypeStruct((T, LANE), F32),
        ],
        scratch_shapes=[pltpu.VMEM((tm, D_MODEL), BF16)],
        compiler_params=_cparams(("parallel", "arbitrary")),
        name="inproj",
    )(x2, nw, w_all, w_ff)


def _rot_half_lanes(x, d):
    lane = lax.broadcasted_iota(jnp.int32, x.shape, 1)
    from_right = pltpu.roll(x, LANE - d // 2, 1)
    from_left = pltpu.roll(x, d // 2, 1)
    return jnp.where(lane % d < d // 2, -from_right, from_left)


def _wprep_kernel(w_ref, *out_refs):
    offs = [int(i) for i in np.cumsum((0,) + IN_SPLITS)]
    o_fq, o_fk, o_fv, o_ff, o_rq, o_rk, o_rv, o_rg, o_cq, o_ckv, o_kr, o_gl, _ = offs
    depth = w_ref.shape[0]
    lane = lax.broadcasted_iota(jnp.int32, (w_ref.shape[1], LANE), 1)
    rope_lanes = (lane >= MLA_NOPE_DIM) & (lane < MLA_QK_DIM)
    for l in range(depth):
        out_ref, ff_ref = out_refs[l], out_refs[depth + l]

        def put(unit, piece, out_ref=out_ref):
            out_ref[:, unit * LANE:unit * LANE + piece.shape[-1]] = piece.astype(out_ref.dtype)

        put(U_FQ, w_ref[l, :, o_fq:o_ff])
        ff_ref[...] = jnp.where(lane < FOX_HEADS, w_ref[l, :, o_ff:o_ff + LANE], 0.0).astype(ff_ref.dtype)
        put(U_RQ, w_ref[l, :, o_rq:o_rv])
        put(U_CKV, w_ref[l, :, o_ckv:o_kr])
        put(U_KR, jnp.where(rope_lanes, w_ref[l, :, o_kr - MLA_NOPE_DIM:o_kr - MLA_NOPE_DIM + LANE], 0.0))
        put(U_RV, w_ref[l, :, o_rv:o_rg])
        put(U_RG, w_ref[l, :, o_rg:o_cq])
        put(U_GL, w_ref[l, :, o_gl:o_gl + N_BRANCHES * D_MODEL])
        put(U_CQ, w_ref[l, :, o_cq:o_ckv])


def _build_inproj_weights(w_in, rows=128):
    depth, _, d_in = w_in.shape
    outs = pl.pallas_call(
        _wprep_kernel,
        grid=(D_MODEL // rows,),
        in_specs=[pl.BlockSpec((depth, rows, d_in), lambda r: (0, r, 0))],
        out_specs=([pl.BlockSpec((rows, N_PROJ), lambda r: (r, 0))] * depth
                   + [pl.BlockSpec((rows, LANE), lambda r: (r, 0))] * depth),
        out_shape=([jax.ShapeDtypeStruct((D_MODEL, N_PROJ), BF16)] * depth
                   + [jax.ShapeDtypeStruct((D_MODEL, LANE), BF16)] * depth),
        compiler_params=_cparams(("parallel",)),
        name="inproj_weight_prep",
    )(w_in)
    return outs[:depth], outs[depth:]


def _split3(x):
    hi = x.astype(BF16)
    r1 = x - hi.astype(F32)
    mid = r1.astype(BF16)
    lo = (r1 - mid.astype(F32)).astype(BF16)
    return hi, mid, lo


def _store_t(dst_ref, idx, x):
    dst_ref[idx] = x.T.astype(dst_ref.dtype)


def _store_vt_pair(vt_ref, p, v2):
    hd = LANE // 2
    vt = v2.T.astype(vt_ref.dtype)
    pad_rows = lax.broadcasted_iota(jnp.int32, (VT_ROWS - hd, vt.shape[1]), 0)
    ones_then_zeros = jnp.where(pad_rows == 0, 1.0, 0.0).astype(vt_ref.dtype)
    for j in range(2):
        base = (2 * p + j) * VT_ROWS
        vt_ref[0, 0, base:base + hd, :] = vt[j * hd:(j + 1) * hd]
        vt_ref[0, 0, base + hd:base + VT_ROWS, :] = ones_then_zeros


def _fox_prep_kernel(fq_ref, fk_ref, fv_ref, ff_ref, bias_ref, tri_ref, pq_ref, pk_ref, cq_ref, ck_ref,
                     qt_ref, kx_ref, vt_ref, carry_scr):
    @pl.when(pl.program_id(1) == 0)
    def _():
        carry_scr[...] = jnp.zeros_like(carry_scr)

    ts = ff_ref.shape[0]
    z = ff_ref[...] + bias_ref[...]
    logf = jnp.minimum(z, 0.0) - jnp.log(1.0 + jnp.exp(-jnp.abs(z)))
    tri = tri_ref[...]
    cum = carry_scr[0:1, :]
    for part in _split3(logf):
        cum = cum + jnp.dot(tri, part, preferred_element_type=F32)
    carry_scr[0:1, :] = cum[ts - 1:ts, :]

    cum3 = jnp.concatenate(_split3(cum * LOG2E), axis=-1)
    eq = cq_ref[...] + jnp.dot(cum3, pq_ref[...], preferred_element_type=F32)
    ek = ck_ref[...] + jnp.dot(cum3, pk_ref[...], preferred_element_type=F32)
    scale = FOX_HEAD_DIM ** -0.5 * LOG2E
    lane = lax.broadcasted_iota(jnp.int32, (ts, LANE), 1)
    for p in range(FOX_HEADS // 2):
        src = slice(p * LANE, (p + 1) * LANE)
        q2 = fq_ref[:, src].astype(F32) * scale
        k2 = fk_ref[:, src].astype(F32)
        for j in range(2):
            h = 2 * p + j
            dst = slice(h * LANE, (h + 1) * LANE)
            qh = q2 if j == 0 else pltpu.roll(q2, LANE // 2, 1)
            kh = k2 if j == 0 else pltpu.roll(k2, LANE // 2, 1)
            own = (lane >= FOX_HEAD_DIM + FOX_EXT_STRIDE * h) & (lane < FOX_HEAD_DIM + FOX_EXT_STRIDE * (h + 1))
            _store_t(qt_ref, (0, dst), jnp.where(lane < FOX_HEAD_DIM, qh, jnp.where(own, eq, 0.0)))
            kx_ref[:, dst] = jnp.where(lane < FOX_HEAD_DIM, kh, ek).astype(BF16)
        _store_vt_pair(vt_ref, p, fv_ref[:, src].astype(F32))


def _fox_prep_consts(ts):
    tri = np.tril(np.ones((ts, ts), np.float32))
    pq = np.zeros((3 * LANE, LANE), np.float32)
    pk = np.zeros((3 * LANE, LANE), np.float32)
    cq = np.zeros((1, LANE), np.float32)
    ck = np.zeros((1, LANE), np.float32)
    for h in range(FOX_HEADS):
        base = FOX_HEAD_DIM + FOX_EXT_STRIDE * h
        for j in range(3):
            pq[j * LANE + h, base + j] = 1.0
            ck[0, base + j] = 1.0
            cq[0, base + 3 + j] = 1.0
            pk[j * LANE + h, base + 3 + j] = -1.0
    return (jnp.asarray(tri, BF16), jnp.asarray(pq, BF16), jnp.asarray(pk, BF16),
            jnp.asarray(cq), jnp.asarray(ck))


def _attn_operand_specs(B, S, t):
    T = B * S
    ns = S // t
    specs = [
        pl.BlockSpec((1, PAIRS * QK_W, t), lambda b, s: (b, 0, s)),
        pl.BlockSpec((t, PAIRS * QK_W), lambda b, s: (b * ns + s, 0)),
        pl.BlockSpec((1, 1, 2 * PAIRS * VT_ROWS, t), lambda b, s: (b, s, 0, 0)),
    ]
    shapes = [
        jax.ShapeDtypeStruct((B, PAIRS * QK_W, S), BF16),
        jax.ShapeDtypeStruct((T, PAIRS * QK_W), BF16),
        jax.ShapeDtypeStruct((B, ns, 2 * PAIRS * VT_ROWS, t), BF16),
    ]
    return specs, shapes


def _fox_prep(proj, ff32, bias, B, S, ts):
    ns = S // ts
    tri, pq, pk, cq, ck = _fox_prep_consts(ts)
    bias128 = jnp.pad(bias.astype(F32), (0, LANE - FOX_HEADS)).reshape(1, LANE)
    const = lambda shape: pl.BlockSpec(shape, lambda b, s: (0, 0))
    out_specs, out_shape = _attn_operand_specs(B, S, ts)
    return pl.pallas_call(
        _fox_prep_kernel,
        grid=(B, ns),
        in_specs=[
            pl.BlockSpec((ts, FOX_W), lambda b, s: (b * ns + s, U_FQ // 4)),
            pl.BlockSpec((ts, FOX_W), lambda b, s: (b * ns + s, U_FK // 4)),
            pl.BlockSpec((ts, FOX_W), lambda b, s: (b * ns + s, U_FV // 4)),
            pl.BlockSpec((ts, LANE), lambda b, s: (b * ns + s, 0)),
            const((1, LANE)), const((ts, ts)), const(pq.shape), const(pk.shape), const(cq.shape), const(ck.shape),
        ],
        out_specs=out_specs,
        out_shape=out_shape,
        scratch_shapes=[pltpu.VMEM((8, LANE), F32)],
        compiler_params=_cparams(("parallel", "arbitrary")),
        name="fox_prep",
    )(proj, proj, proj, ff32, bias128, tri, pq, pk, cq, ck)


def _col_reduce(x, op, reduce_fn, parts=2):
    c = x.shape[0] // parts
    pieces = [x[i * c:(i + 1) * c] for i in range(parts)]
    while len(pieces) > 1:
        pieces = [op(pieces[i], pieces[i + 1]) for i in range(0, len(pieces), 2)]
    return reduce_fn(pieces[0], axis=0, keepdims=True)


def _attn_kernel(qt_ref, qn_ref, k_ref, vt_ref, o_ref, sa_scr, sb_scr, ma_scr, mb_scr, acc_scr, *, tk):
    qi = pl.program_id(2)
    tq = qt_ref.shape[-1]
    acc_scr[...] = jnp.zeros_like(acc_scr)

    def key_tile(kt):
        return k_ref[pl.ds(pl.multiple_of(kt * tk, tk), tk), :]

    strips = [(c, c + ATTN_STRIP) for c in range(0, tq, ATTN_STRIP)]

    def scores(kt, s_scr, m_scr):
        k = key_tile(kt)
        for h in range(2):
            for a, b in strips:
                s = jnp.dot(k[:, h * LANE:(h + 1) * LANE], qt_ref[0, h * LANE:(h + 1) * LANE, a:b],
                            preferred_element_type=F32)
                s_scr[h, :, a:b] = s
                m_scr[h, :, a:b] = _col_reduce(s, jnp.maximum, jnp.max)

    def update(kt, h, s, m_tile, m_prev, lo):
        m_new = jnp.maximum(m_prev, m_tile)
        alpha = jnp.exp2(m_prev - m_new)
        hi = lo + s.shape[-1]
        pv = None
        for r in range(0, tk, ATTN_KEY_CHUNK):
            p = jnp.exp2((s[r:r + ATTN_KEY_CHUNK] - m_new).astype(BF16))
            vt = vt_ref[0, kt, h * VT_ROWS:(h + 1) * VT_ROWS, r:r + ATTN_KEY_CHUNK]
            d = jnp.dot(vt, p, preferred_element_type=F32)
            pv = d if pv is None else pv + d
        acc_scr[h, :, lo:hi] = alpha * acc_scr[h, :, lo:hi] + pv
        return m_new

    def diag_update(kt, h, s_scr, m_prev, a, b, lo):
        s = s_scr[h, :, a:b]
        if a < lo + tk:
            key = lax.broadcasted_iota(jnp.int32, s.shape, 0)
            qry = lax.broadcasted_iota(jnp.int32, s.shape, 1) + (a - lo)
            s = jnp.where(key <= qry, s, NEG)
        return update(kt, h, s, _col_reduce(s, jnp.maximum, jnp.max), m_prev, a)

    def diagonal_tiles(carry, lookahead):
        k_last = key_tile(2 * qi + 1)
        if lookahead:
            k0 = key_tile(0)
        out = []
        for h in range(2):
            ms = []
            for a, b in strips:
                if a >= tk:
                    sb_scr[h, :, a:b] = jnp.dot(k_last[:, h * LANE:(h + 1) * LANE],
                                                qt_ref[0, h * LANE:(h + 1) * LANE, a:b],
                                                preferred_element_type=F32)
                ms.append(diag_update(2 * qi, h, sa_scr, carry[h][:, a:b], a, b, 0))
                if lookahead:
                    s = jnp.dot(k0[:, h * LANE:(h + 1) * LANE], qn_ref[0, h * LANE:(h + 1) * LANE, a:b],
                                preferred_element_type=F32)
                    sa_scr[h, :, a:b] = s
                    ma_scr[h, :, a:b] = _col_reduce(s, jnp.maximum, jnp.max)
            out.append(ms)
        for h in range(2):
            for i, (a, b) in enumerate(strips):
                if a >= tk:
                    diag_update(2 * qi + 1, h, sb_scr, out[h][i], a, b, tk)
        hd = LANE // 2
        ot = jnp.concatenate([acc_scr[h, :hd] / acc_scr[h, hd:hd + 1] for h in range(2)], axis=0)
        o_ref[...] = ot.T.astype(o_ref.dtype)

    def fused_step(kt, cur_s, cur_m, nxt_s, nxt_m, carry):
        k = key_tile(kt + 1)
        ms = [[], []]
        for a, b in strips:
            for h in range(2):
                s = jnp.dot(k[:, h * LANE:(h + 1) * LANE], qt_ref[0, h * LANE:(h + 1) * LANE, a:b],
                            preferred_element_type=F32)
                nxt_s[h, :, a:b] = s
                nxt_m[h, :, a:b] = _col_reduce(s, jnp.maximum, jnp.max)
                ms[h].append(update(kt, h, cur_s[h, :, a:b], cur_m[h, :, a:b], carry[h][:, a:b], a))
        return tuple(jnp.concatenate(m, axis=-1) for m in ms)

    def body(j, carry):
        kt = 2 * j
        carry = fused_step(kt, sa_scr, ma_scr, sb_scr, mb_scr, carry)
        return fused_step(kt + 1, sb_scr, mb_scr, sa_scr, ma_scr, carry)

    init = tuple(jnp.full((1, tq), NEG, F32) for _ in range(2))

    @pl.when(qi == 0)
    def _():
        scores(0, sa_scr, ma_scr)

    carry = lax.fori_loop(0, qi, body, init)
    last = pl.num_programs(2) - 1

    @pl.when(qi < last)
    def _():
        diagonal_tiles(carry, True)

    @pl.when(qi == last)
    def _():
        diagonal_tiles(carry, False)


def _attention(qt, kx, vt, B, S, tk, name):
    T = B * S
    tq = 2 * tk
    nq = S // tq
    kern = functools.partial(_attn_kernel, tk=tk)
    return pl.pallas_call(
        kern,
        grid=(B, PAIRS, nq),
        in_specs=[
            pl.BlockSpec((1, QK_W, tq), lambda b, p, i: (b, p, i)),
            pl.BlockSpec((1, QK_W, tq), lambda b, p, i: (b, p, jnp.minimum(i + 1, nq - 1))),
            pl.BlockSpec((S, QK_W), lambda b, p, i: (b, p)),
            pl.BlockSpec((1, S // tk, 2 * VT_ROWS, tk), lambda b, p, i: (b, 0, p, 0)),
        ],
        out_specs=pl.BlockSpec((tq, LANE), lambda b, p, i: (b * nq + i, p)),
        out_shape=jax.ShapeDtypeStruct((T, PAIRS * LANE), BF16),
        scratch_shapes=[
            pltpu.VMEM((2, tk, tq), F32),
            pltpu.VMEM((2, tk, tq), F32),
            pltpu.VMEM((2, 1, tq), F32),
            pltpu.VMEM((2, 1, tq), F32),
            pltpu.VMEM((2, VT_ROWS, tq), F32),
        ],
        compiler_params=_cparams(("parallel", "parallel", "arbitrary")),
        name=name,
    )(qt, qt, kx, vt)


def _mla_prep_kernel(cq_ref, ckv_ref, kr_ref, qnw_ref, kvnw_ref, wq_ref, wk_ref, wv_ref,
                     cq_tab, sq_tab, ck_tab, sk_tab, qt_ref, k_ref, vt_ref):
    def norm(x_ref, w_ref):
        x = x_ref[...].astype(F32)
        ms = jnp.mean(x * x, axis=-1, keepdims=True)
        return (x * lax.rsqrt(ms + RMS_EPS) * w_ref[...]).astype(BF16)

    c_q = norm(cq_ref, qnw_ref)
    c_kv = norm(ckv_ref, kvnw_ref)
    tile8 = lambda a: jnp.concatenate([a] * MLA_HEADS, axis=-1)
    q_raw = jnp.dot(c_q, wq_ref[...], preferred_element_type=F32)
    for c in range(MLA_HEADS):
        qs = q_raw[:, c * LANE:(c + 1) * LANE]
        q = qs * cq_tab[...] + _rot_half_lanes(qs, MLA_ROPE_DIM) * sq_tab[...]
        _store_t(qt_ref, (0, slice(c * LANE, (c + 1) * LANE)), q)
    kr = kr_ref[...].astype(F32)
    k_rope = kr * ck_tab[...] + _rot_half_lanes(kr, MLA_ROPE_DIM) * sk_tab[...]
    k = jnp.dot(c_kv, wk_ref[...], preferred_element_type=F32) + tile8(k_rope)
    k_ref[...] = k.astype(BF16)
    v = jnp.dot(c_kv, wv_ref[...], preferred_element_type=F32)
    for p in range(PAIRS):
        _store_vt_pair(vt_ref, p, v[:, p * LANE:(p + 1) * LANE])


def _mla_weights(w_uq, w_uk, w_uv):
    q3 = w_uq.reshape(MLA_Q_RANK, MLA_HEADS, MLA_QK_DIM)
    nope, ropep = q3[..., :MLA_NOPE_DIM], q3[..., MLA_NOPE_DIM:]
    z32 = jnp.zeros((MLA_Q_RANK, MLA_HEADS, LANE - MLA_QK_DIM), F32)
    wq = jnp.concatenate([nope, ropep, z32], axis=-1).reshape(MLA_Q_RANK, MLA_HEADS * LANE)
    k3 = w_uk.reshape(MLA_KV_RANK, MLA_HEADS, MLA_NOPE_DIM)
    wk = jnp.concatenate([k3, jnp.zeros_like(k3)], axis=-1).reshape(MLA_KV_RANK, MLA_HEADS * LANE)
    return wq.astype(BF16), wk.astype(BF16), w_uv.astype(BF16)


def _rope_inv_freq(d):
    return (np.float32(ROPE_THETA) ** (-np.arange(0, d, 2, dtype=np.float32) / np.float32(d))).astype(np.float32)


def _mla_tables(S):
    inv = _rope_inv_freq(MLA_ROPE_DIM)
    inv128 = np.concatenate([np.zeros(MLA_NOPE_DIM, np.float32), inv, inv, np.zeros(LANE - MLA_QK_DIM, np.float32)])
    keep = (np.arange(LANE) < MLA_QK_DIM).astype(np.float32)
    ang = jnp.arange(S, dtype=F32)[:, None] * jnp.asarray(inv128)[None, :]
    ctab = jnp.cos(ang) * jnp.asarray(keep)[None, :]
    stab = jnp.sin(ang)
    scale = MLA_QK_DIM ** -0.5 * LOG2E
    return ctab * scale, stab * scale, ctab, stab


def _mla_prep(proj, qnw, kvnw, wq, wk, wv, tabs, B, S, tm):
    ns = S // tm
    HW = MLA_HEADS * LANE
    const = lambda shape: pl.BlockSpec(shape, lambda b, s: (0, 0))
    tab = pl.BlockSpec((tm, LANE), lambda b, s: (s, 0))
    out_specs, out_shape = _attn_operand_specs(B, S, tm)
    return pl.pallas_call(
        _mla_prep_kernel,
        grid=(B, ns),
        in_specs=[
            pl.BlockSpec((tm, MLA_Q_RANK), lambda b, s: (b * ns + s, U_CQ // 3)),
            pl.BlockSpec((tm, MLA_KV_RANK), lambda b, s: (b * ns + s, U_CKV // 2)),
            pl.BlockSpec((tm, LANE), lambda b, s: (b * ns + s, U_KR)),
            const((1, MLA_Q_RANK)), const((1, MLA_KV_RANK)),
            const((MLA_Q_RANK, HW)), const((MLA_KV_RANK, HW)),
            const((MLA_KV_RANK, MLA_V_W)),
            tab, tab, tab, tab,
        ],
        out_specs=out_specs,
        out_shape=out_shape,
        compiler_params=_cparams(("parallel", "parallel")),
        name="mla_prep",
    )(proj, proj, proj, qnw, kvnw, wq, wk, wv, *tabs)


def _ret_kernel(rq_ref, rk_ref, rv_ref, rg_ref, cos_ref, sin_ref, perm_ref,
                din_ref, xi_ref, zeta_ref, gch_ref, gnw_ref, o_ref, r_scr):
    @pl.when(pl.program_id(0) == 0)
    def _():
        r_scr[...] = jnp.zeros_like(r_scr)

    tile4 = lambda a: jnp.concatenate([a] * (RET_QK_W // LANE), axis=-1)
    cos4 = tile4(cos_ref[...])
    sin4 = tile4(sin_ref[...])
    for bi in range(rq_ref.shape[0]):
        def rope(x_ref):
            x = x_ref[bi]
            rot = jnp.dot(x, perm_ref[...], preferred_element_type=F32)
            return x.astype(F32) * cos4 + rot * sin4

        q = rope(rq_ref)
        k = rope(rk_ref) * (RET_QK_DIM ** -0.5)
        for p in range(RET_HEADS // 2):
            q2 = q[:, p * LANE:(p + 1) * LANE]
            k2 = k[:, p * LANE:(p + 1) * LANE]
            k2b = k2.astype(BF16)
            kz2 = (k2 * zeta_ref[p]).astype(BF16)
            r2 = r_scr[bi, p]
            r2b = r2.astype(BF16)
            new_r = gch_ref[p] * r2
            for j in range(2):
                h = 2 * p + j
                half = ((j * RET_QK_DIM, (j + 1) * RET_QK_DIM),)
                qm = jnp.where(_lane_mask(q2.shape, half), q2, 0.0).astype(BF16)
                inner = lax.dot_general(qm, k2b, (((1,), (1,)), ((), ())), preferred_element_type=F32)
                inner = inner * din_ref[h]
                vh = rv_ref[bi, :, h * LANE:(h + 1) * LANE]
                o = (jnp.dot(inner.astype(BF16), vh, preferred_element_type=F32)
                     + jnp.dot(qm, r2b, preferred_element_type=F32) * xi_ref[h])
                upd = lax.dot_general(kz2, vh, (((0,), (0,)), ((), ())), preferred_element_type=F32)
                rows = lax.broadcasted_iota(jnp.int32, upd.shape, 0)
                new_r = new_r + jnp.where((rows >= half[0][0]) & (rows < half[0][1]), upd, 0.0)
                mu = jnp.mean(o, axis=-1, keepdims=True)
                d = o - mu
                var = jnp.mean(d * d, axis=-1, keepdims=True)
                on = d * lax.rsqrt(var + RMS_EPS) * gnw_ref[:, h * LANE:(h + 1) * LANE]
                g = rg_ref[bi, :, h * LANE:(h + 1) * LANE].astype(F32)
                o_ref[bi, :, h * LANE:(h + 1) * LANE] = (g * _sigmoid(g) * on).astype(o_ref.dtype)
            r_scr[bi, p] = new_r


def _ret_tables(S, C):
    inv128 = np.tile(_rope_inv_freq(RET_QK_DIM), 2 * LANE // RET_QK_DIM)
    ang = jnp.arange(S, dtype=F32)[:, None] * jnp.asarray(inv128)[None, :]
    cos = jnp.cos(ang)
    sin = jnp.sin(ang)
    gammas = 1.0 - 2.0 ** (-5.0 - jnp.arange(RET_HEADS, dtype=F32))
    log_g = jnp.log(gammas)
    j = jnp.arange(C, dtype=F32)
    diff = j[:, None] - j[None, :]
    din = jnp.where(diff[None] >= 0, jnp.exp(jnp.maximum(diff, 0.0)[None] * log_g[:, None, None]), 0.0)
    xi = jnp.exp((j[None, :] + 1.0) * log_g[:, None])
    zeta = jnp.exp((C - 1.0 - j[None, :]) * log_g[:, None])
    gch = jnp.exp(C * log_g)
    xi_t = jnp.broadcast_to(xi[:, :, None], (RET_HEADS, C, LANE))
    zeta_p = jnp.repeat(zeta.reshape(RET_HEADS // 2, 2, C).transpose(0, 2, 1), RET_QK_DIM, axis=-1)
    gch_p = jnp.broadcast_to(jnp.repeat(gch.reshape(RET_HEADS // 2, 2), RET_QK_DIM, axis=-1)[:, :, None],
                             (RET_HEADS // 2, LANE, LANE))
    return cos, sin, din, xi_t, zeta_p, gch_p


def _retention(proj, gnw, tabs, B, S, C):
    nc = S // C
    cos, sin, din, xi, zeta, gch = tabs
    proj3 = proj.reshape(B, S, N_PROJ)
    perm = np.zeros((RET_QK_W, RET_QK_W), np.float32)
    half = RET_QK_DIM // 2
    for h in range(RET_HEADS):
        for j in range(half):
            perm[h * RET_QK_DIM + half + j, h * RET_QK_DIM + j] = -1.0
            perm[h * RET_QK_DIM + j, h * RET_QK_DIM + half + j] = 1.0
    perm = jnp.asarray(perm, BF16)
    blk = lambda w, unit: pl.BlockSpec((B, C, w), lambda c: (0, c, unit * LANE // w))
    full = lambda a: pl.BlockSpec(a.shape, lambda c: (0,) * a.ndim)
    out = pl.pallas_call(
        _ret_kernel,
        grid=(nc,),
        in_specs=[
            blk(RET_QK_W, U_RQ), blk(RET_QK_W, U_RK),
            blk(RET_V_W, U_RV), blk(RET_V_W, U_RG),
            pl.BlockSpec((C, LANE), lambda c: (c, 0)),
            pl.BlockSpec((C, LANE), lambda c: (c, 0)),
            full(perm), full(din), full(xi), full(zeta), full(gch), full(gnw),
        ],
        out_specs=pl.BlockSpec((B, C, RET_V_W), lambda c: (0, c, 0)),
        out_shape=jax.ShapeDtypeStruct((B, S, RET_V_W), BF16),
        scratch_shapes=[pltpu.VMEM((B, RET_HEADS // 2, LANE, LANE), F32)],
        compiler_params=_cparams(("arbitrary",)),
        name="retention",
    )(proj3, proj3, proj3, proj3, cos, sin, perm, din, xi, zeta, gch, gnw)
    return out.reshape(B * S, RET_V_W)


def _merge_kernel(of_ref, or_ref, om_ref, g0_ref, g1_ref, g2_ref, x_ref,
                  wf_ref, wr_ref, wm_ref, wo_ref, nw_ref, xo_ref, h_ref):
    merged = (_sigmoid(g0_ref[...].astype(F32)) * jnp.dot(of_ref[...], wf_ref[...], preferred_element_type=F32)
              + _sigmoid(g1_ref[...].astype(F32)) * jnp.dot(or_ref[...], wr_ref[...], preferred_element_type=F32)
              + _sigmoid(g2_ref[...].astype(F32)) * jnp.dot(om_ref[...], wm_ref[...], preferred_element_type=F32))
    x = x_ref[...] + jnp.dot(merged.astype(BF16), wo_ref[...], preferred_element_type=F32)
    xo_ref[...] = x
    ms = jnp.mean(x * x, axis=-1, keepdims=True)
    h_ref[...] = (x * lax.rsqrt(ms + RMS_EPS) * nw_ref[...]).astype(h_ref.dtype)


def _merge(o_fox, o_ret, o_mla, proj, x2, wf, wr, wm, wo, nw, tm):
    T = x2.shape[0]
    row = lambda w, cb=0: pl.BlockSpec((tm, w), lambda i: (i, cb))
    const = lambda a: pl.BlockSpec(a.shape, lambda i: (0, 0))
    g_unit = U_GL // 8
    return pl.pallas_call(
        _merge_kernel,
        grid=(T // tm,),
        in_specs=[
            row(FOX_W), row(RET_V_W), row(MLA_V_W),
            row(D_MODEL, g_unit), row(D_MODEL, g_unit + 1), row(D_MODEL, g_unit + 2),
            row(D_MODEL),
            const(wf), const(wr), const(wm), const(wo), const(nw),
        ],
        out_specs=[row(D_MODEL), row(D_MODEL)],
        out_shape=[jax.ShapeDtypeStruct((T, D_MODEL), F32), jax.ShapeDtypeStruct((T, D_MODEL), BF16)],
        compiler_params=_cparams(("parallel",)),
        name="merge",
    )(o_fox, o_ret, o_mla, proj, proj, proj, x2, wf, wr, wm, wo, nw)


def _swiglu_tile(h, wg, wu, wd):
    a = jnp.dot(h, wg, preferred_element_type=F32)
    b = jnp.dot(h, wu, preferred_element_type=F32)
    return jnp.dot((a * _sigmoid(a) * b).astype(BF16), wd, preferred_element_type=F32)


def _dense_ffn_kernel(h_ref, wg_ref, wu_ref, wd_ref, x_ref, o_ref, acc_scr):
    f = pl.program_id(1)

    @pl.when(f == 0)
    def _():
        acc_scr[...] = jnp.zeros_like(acc_scr)

    acc_scr[...] += _swiglu_tile(h_ref[...], wg_ref[...], wu_ref[...], wd_ref[...])

    @pl.when(f == pl.num_programs(1) - 1)
    def _():
        o_ref[...] = x_ref[...] + acc_scr[...]


def _dense_ffn(h, x2, wg, wu, wd, tm, tf):
    T = x2.shape[0]
    return pl.pallas_call(
        _dense_ffn_kernel,
        grid=(T // tm, D_FF // tf),
        in_specs=[
            pl.BlockSpec((tm, D_MODEL), lambda i, f: (i, 0)),
            pl.BlockSpec((D_MODEL, tf), lambda i, f: (0, f)),
            pl.BlockSpec((D_MODEL, tf), lambda i, f: (0, f)),
            pl.BlockSpec((tf, D_MODEL), lambda i, f: (f, 0)),
            pl.BlockSpec((tm, D_MODEL), lambda i, f: (i, 0)),
        ],
        out_specs=pl.BlockSpec((tm, D_MODEL), lambda i, f: (i, 0)),
        out_shape=jax.ShapeDtypeStruct((T, D_MODEL), F32),
        scratch_shapes=[pltpu.VMEM((tm, D_MODEL), F32)],
        compiler_params=_cparams(("parallel", "arbitrary")),
        name="dense_ffn",
    )(h, wg, wu, wd, x2)


M_E0, M_E1, M_P0, M_P1, M_W0, M_W1 = 0, 1, 2, 3, 4, 5
ROUTE_ROWS = 8


def _split2(x):
    hi = x.astype(BF16)
    lo = (x - hi.astype(F32)).astype(BF16)
    return hi, lo


HALF_D = D_MODEL // 2


def _pack_rows(h):
    bits = pltpu.bitcast(h.astype(BF16).astype(F32), jnp.uint32)
    return (bits[:, :HALF_D] & jnp.uint32(0xFFFF0000)) | (bits[:, HALF_D:] >> 16)


def _unpack_rows(u):
    left = pltpu.bitcast(u & jnp.uint32(0xFFFF0000), F32)
    right = pltpu.bitcast(u << 16, F32)
    return jnp.concatenate([left, right], axis=-1).astype(BF16)


def _router_kernel(x_ref, nw_ref, whi_ref, wlo_ref, tri_ref, h_ref, meta_ref, route_ref, cnt_ref, carry_scr):
    @pl.when(pl.program_id(0) == 0)
    def _():
        carry_scr[...] = jnp.zeros_like(carry_scr)

    x = x_ref[...]
    ms = jnp.mean(x * x, axis=-1, keepdims=True)
    h = x * lax.rsqrt(ms + RMS_EPS) * nw_ref[...]
    h_ref[...] = _pack_rows(h)
    hhi, hlo = _split2(h)
    logits = (jnp.dot(hhi, whi_ref[...], preferred_element_type=F32)
              + jnp.dot(hhi, wlo_ref[...], preferred_element_type=F32)
              + jnp.dot(hlo, whi_ref[...], preferred_element_type=F32))
    lane = lax.broadcasted_iota(jnp.int32, logits.shape, 1)
    lanef = lane.astype(F32)
    logits = jnp.where(lane < N_EXPERTS, logits, NEG)
    m0 = jnp.max(logits, axis=-1, keepdims=True)
    e0 = jnp.min(jnp.where(logits == m0, lanef, float(LANE)), axis=-1, keepdims=True)
    oh0 = lanef == e0
    rest = jnp.where(oh0, NEG, logits)
    m1 = jnp.max(rest, axis=-1, keepdims=True)
    e1 = jnp.min(jnp.where(rest == m1, lanef, float(LANE)), axis=-1, keepdims=True)
    oh1 = lanef == e1
    z = jnp.exp(m1 - m0)
    w0 = 1.0 / (1.0 + z)
    w1 = z / (1.0 + z)
    both = jnp.where(oh0 | oh1, 1.0, 0.0)
    before = carry_scr[0:1, :] + jnp.dot(tri_ref[...], both.astype(BF16), preferred_element_type=F32)
    p0 = jnp.sum(jnp.where(oh0, before, 0.0), axis=-1, keepdims=True)
    p1 = jnp.sum(jnp.where(oh1, before, 0.0), axis=-1, keepdims=True)
    total = carry_scr[0:1, :] + jnp.sum(both, axis=0, keepdims=True)
    carry_scr[0:1, :] = total
    cnt_ref[...] = jnp.broadcast_to(total, cnt_ref.shape)
    meta = jnp.zeros(logits.shape, F32)
    for idx, val in ((M_E0, e0), (M_E1, e1), (M_P0, p0), (M_P1, p1), (M_W0, w0), (M_W1, w1)):
        meta = jnp.where(lane == idx, val, meta)
    meta_ref[...] = meta
    route_ref[0] = meta.T[:ROUTE_ROWS, :]


def _router(x2, nw, w_router, tr):
    T = x2.shape[0]
    wr = jnp.pad(w_router.astype(F32), ((0, 0), (0, LANE - N_EXPERTS)))
    whi = wr.astype(BF16)
    wlo = (wr - whi.astype(F32)).astype(BF16)
    tri = jnp.asarray(np.tril(np.ones((tr, tr), np.float32), -1), BF16)
    const = lambda a: pl.BlockSpec(a.shape, lambda i: (0, 0))
    return pl.pallas_call(
        _router_kernel,
        grid=(T // tr,),
        in_specs=[pl.BlockSpec((tr, D_MODEL), lambda i: (i, 0)), const(nw), const(whi), const(wlo), const(tri)],
        out_specs=[
            pl.BlockSpec((tr, HALF_D), lambda i: (i, 0)),
            pl.BlockSpec((tr, LANE), lambda i: (i, 0)),
            pl.BlockSpec((1, ROUTE_ROWS, tr), lambda i: (i, 0, 0)),
            pl.BlockSpec((8, LANE), lambda i: (0, 0)),
        ],
        out_shape=[
            jax.ShapeDtypeStruct((T, HALF_D), jnp.uint32),
            jax.ShapeDtypeStruct((T, LANE), F32),
            jax.ShapeDtypeStruct((T // tr, ROUTE_ROWS, tr), F32),
            jax.ShapeDtypeStruct((8, LANE), F32),
        ],
        scratch_shapes=[pltpu.VMEM((8, LANE), F32)],
        compiler_params=_cparams(("arbitrary",)),
        name="router",
    )(x2, nw, whi, wlo, tri)


def _scatter_kernel(dest_ref, h_ref, xs_in, xs_hbm, sem, *, tt):
    del xs_in

    def row_copy(t, d_row):
        return pltpu.make_async_copy(h_ref.at[pl.ds(t, 1)], xs_hbm.at[pl.ds(d_row, 1)], sem)

    for t in range(tt):
        for j in range(TOP_K):
            row_copy(t, dest_ref[0, 0, j * tt + t]).start(priority=j)

    for _ in range(TOP_K):
        pltpu.make_async_copy(h_ref, xs_hbm.at[pl.ds(0, tt)], sem).wait()


def _scatter_rows(dest, h, n_rows, tt):
    T = h.shape[0]
    width = h.shape[1]
    xs0 = jnp.zeros((n_rows, width), h.dtype)
    assert dest.shape == (T // tt, TOP_K, tt)
    dest3 = dest.reshape(T // tt, 1, TOP_K * tt)
    return pl.pallas_call(
        functools.partial(_scatter_kernel, tt=tt),
        grid=(T // tt,),
        in_specs=[
            pl.BlockSpec((1, 1, 2 * tt), lambda i: (i, 0, 0), memory_space=pltpu.SMEM),
            pl.BlockSpec((tt, width), lambda i: (i, 0)),
            pl.BlockSpec(memory_space=pl.ANY),
        ],
        out_specs=pl.BlockSpec(memory_space=pl.ANY),
        out_shape=jax.ShapeDtypeStruct((n_rows, width), h.dtype),
        scratch_shapes=[pltpu.SemaphoreType.DMA(())],
        input_output_aliases={2: 0},
        compiler_params=_cparams(("arbitrary",)),
        name="moe_scatter",
    )(dest3, h, xs0)


def _moe_ffn_kernel(te_ref, na_ref, xs_ref, wg_ref, wu_ref, wd_ref, ys_ref, acc_scr):
    i = pl.program_id(0)
    f = pl.program_id(1)

    @pl.when(i < na_ref[0])
    def _():
        @pl.when(f == 0)
        def _():
            acc_scr[...] = jnp.zeros_like(acc_scr)

        acc_scr[...] += _swiglu_tile(_unpack_rows(xs_ref[...]), wg_ref[0].astype(BF16),
                                     wu_ref[0].astype(BF16), wd_ref[0].astype(BF16))

        @pl.when(f == pl.num_programs(1) - 1)
        def _():
            ys_ref[...] = acc_scr[...]

    @pl.when((i >= na_ref[0]) & (f == 0))
    def _():
        ys_ref[...] = jnp.zeros_like(ys_ref)


def _moe_ffn(tile_expert, n_active, xs, wg, wu, wd, tm, tf):
    n_rows = xs.shape[0]
    n_tiles = n_rows // tm
    nf = D_FF // tf

    def row_map(i, f, te, na):
        return (jnp.minimum(i, na[0] - 1), 0)

    def fcol(i, f, na):
        return jnp.where(i < na[0], f, nf - 1)

    grid_spec = pltpu.PrefetchScalarGridSpec(
        num_scalar_prefetch=2,
        grid=(n_tiles, nf),
        in_specs=[
            pl.BlockSpec((tm, HALF_D), row_map),
            pl.BlockSpec((1, D_MODEL, tf), lambda i, f, te, na: (te[i], 0, fcol(i, f, na))),
            pl.BlockSpec((1, D_MODEL, tf), lambda i, f, te, na: (te[i], 0, fcol(i, f, na))),
            pl.BlockSpec((1, tf, D_MODEL), lambda i, f, te, na: (te[i], fcol(i, f, na), 0)),
        ],
        out_specs=pl.BlockSpec((tm, D_MODEL), lambda i, f, te, na: (i, 0)),
        scratch_shapes=[pltpu.VMEM((tm, D_MODEL), F32)],
    )
    return pl.pallas_call(
        _moe_ffn_kernel,
        grid_spec=grid_spec,
        out_shape=jax.ShapeDtypeStruct((n_rows, D_MODEL), F32),
        compiler_params=_cparams(("arbitrary", "arbitrary")),
        name="moe_ffn",
    )(tile_expert, n_active, xs, wg, wu, wd)


def _combine_kernel(dest_ref, dnext_ref, x_ref, meta_ref, fw_ref, ys_hbm, o_ref, g_scr, sems, *, tc):
    i = pl.program_id(0)

    def gather(d_ref, half, s):
        for t in range(tc):
            for j in range(TOP_K):
                row = d_ref[0, 0, (half * TOP_K + j) * tc + t]
                pltpu.make_async_copy(ys_hbm.at[pl.ds(row, 1)], g_scr.at[s, j, pl.ds(t, 1)],
                                      sems.at[s]).start(priority=j)

    def wait(s):
        for j in range(TOP_K):
            pltpu.make_async_copy(ys_hbm.at[pl.ds(0, tc)], g_scr.at[s, j], sems.at[s]).wait()

    def combine(s):
        rows = pl.ds(s * tc, tc)
        meta = meta_ref[rows, :]
        w0 = meta[:, M_W0:M_W0 + 1]
        w1 = meta[:, M_W1:M_W1 + 1]
        x = x_ref[rows, :] + w0 * g_scr[s, 0] + w1 * g_scr[s, 1]
        ms = jnp.mean(x * x, axis=-1, keepdims=True)
        o_ref[rows, :] = x * lax.rsqrt(ms + RMS_EPS) * fw_ref[...]

    @pl.when(i == 0)
    def _():
        gather(dest_ref, 0, 0)

    gather(dest_ref, 1, 1)
    wait(0)
    combine(0)
    gather(dnext_ref, 0, 0)
    wait(1)
    combine(1)

    @pl.when(i == pl.num_programs(0) - 1)
    def _():
        wait(0)


def _combine(dest, x2, meta, fw, ys, tc):
    T = x2.shape[0]
    n = T // (2 * tc)
    assert dest.shape == (2 * n, TOP_K, tc)
    dest3 = dest.reshape(n, 1, 2 * TOP_K * tc)
    dspec = lambda imap: pl.BlockSpec((1, 1, 2 * TOP_K * tc), imap, memory_space=pltpu.SMEM)
    return pl.pallas_call(
        functools.partial(_combine_kernel, tc=tc),
        grid=(n,),
        in_specs=[
            dspec(lambda i: (i, 0, 0)),
            dspec(lambda i: (jnp.minimum(i + 1, n - 1), 0, 0)),
            pl.BlockSpec((2 * tc, D_MODEL), lambda i: (i, 0)),
            pl.BlockSpec((2 * tc, LANE), lambda i: (i, 0)),
            pl.BlockSpec((1, D_MODEL), lambda i: (0, 0)),
            pl.BlockSpec(memory_space=pl.ANY),
        ],
        out_specs=pl.BlockSpec((2 * tc, D_MODEL), lambda i: (i, 0)),
        out_shape=jax.ShapeDtypeStruct((T, D_MODEL), F32),
        scratch_shapes=[pltpu.VMEM((2, TOP_K, tc, D_MODEL), F32), pltpu.SemaphoreType.DMA((2,))],
        compiler_params=_cparams(("arbitrary",)),
        name="moe_combine",
    )(dest3, dest3, x2, meta, fw, ys)


def _moe_plan(route, counts, tm, n_tiles):
    cnt = counts[0, :N_EXPERTS].astype(jnp.int32)
    padded = ((cnt + tm - 1) // tm) * tm
    ends = jnp.cumsum(padded)
    offs = ends - padded
    e = route[:, M_E0:M_E1 + 1, :].astype(jnp.int32)
    pos = route[:, M_P0:M_P1 + 1, :].astype(jnp.int32)
    dest = offs[e] + pos
    starts = jnp.arange(n_tiles, dtype=jnp.int32) * tm
    n_active = (ends[-1] // tm).astype(jnp.int32)
    te = jnp.sum(starts[:, None] >= ends[None, :], axis=-1).astype(jnp.int32)
    last_e = jnp.sum(jnp.maximum(ends[-1] - tm, 0) >= ends).astype(jnp.int32)
    te = jnp.where(starts < ends[-1], te, last_e)
    return dest, te, n_active.reshape(1)


def _moe_layer(x_mid, nw, w_router, wg, wu, wd, final_w, tiles):
    T = x_mid.shape[0]
    tm = tiles["moe_tm"]
    n_tiles = (TOP_K * T) // tm + N_EXPERTS
    tr, tt = tiles["router_t"], tiles["moe_row_t"]
    h, meta, route, counts = _router(x_mid, nw, w_router, tr)
    dest, te, n_active = _moe_plan(route, counts, tm, n_tiles)
    dest = dest.reshape(T // tr, TOP_K, tr // tt, tt).transpose(0, 2, 1, 3).reshape(T // tt, TOP_K, tt)
    xs = _scatter_rows(dest, h, n_tiles * tm, tt)
    ys = _moe_ffn(te, n_active, xs, wg, wu, wd, tm, tiles["moe_tf"])
    return _combine(dest, x_mid, meta, final_w, ys, tt)


def _tiles(S, T):
    return dict(
        inproj_tm=min(1024, T), inproj_tn=2816,
        attn_t=min(512, S // 2), ret_c=min(256, S),
        merge_tm=min(512, T), ffn_tm=min(1024, T), ffn_tf=512,
        router_t=min(512, T // 2), moe_row_t=min(256, T // 2),
        moe_tm=min(1024, T), moe_tf=512,
    )


def kernel(x, norm_mix_w, w_in, fox_f_bias, ret_gn_w, mla_q_norm_w, mla_kv_norm_w, mla_w_uq, mla_w_uk, mla_w_uv, w_br_fox, w_br_ret, w_br_mla, w_out, norm_ffn_w, dense_w_gate, dense_w_up, dense_w_down, moe_w_router, moe_w_gate, moe_w_up, moe_w_down, final_norm_w):
    B, S, D = x.shape
    assert D == D_MODEL and w_in.shape[0] == DEPTH == 2
    T = B * S
    tl = _tiles(S, T)
    row = lambda a: a.reshape(1, -1).astype(F32)
    ret_tabs = _ret_tables(S, tl["ret_c"])
    mla_tabs = _mla_tables(S)
    x2 = x.reshape(T, D)
    w_all, w_ff = _build_inproj_weights(w_in)
    for l in range(DEPTH):
        proj, ff32 = _inproj(x2, row(norm_mix_w[l]), w_all[l], w_ff[l], tl["inproj_tm"], tl["inproj_tn"])
        fqt, fkx, fvt = _fox_prep(proj, ff32, fox_f_bias[l], B, S, tl["attn_t"])
        o_fox = _attention(fqt, fkx, fvt, B, S, tl["attn_t"], "fox_attn")
        o_ret = _retention(proj, row(ret_gn_w[l]), ret_tabs, B, S, tl["ret_c"])
        wq, wk, wv = _mla_weights(mla_w_uq[l], mla_w_uk[l], mla_w_uv[l])
        mqt, mkx, mvt = _mla_prep(proj, row(mla_q_norm_w[l]), row(mla_kv_norm_w[l]), wq, wk, wv,
                                  mla_tabs, B, S, tl["attn_t"])
        o_mla = _attention(mqt, mkx, mvt, B, S, tl["attn_t"], "mla_attn")
        x_mid, h2 = _merge(o_fox, o_ret, o_mla, proj, x2,
                           w_br_fox[l].astype(BF16), w_br_ret[l].astype(BF16), w_br_mla[l].astype(BF16),
                           w_out[l].astype(BF16), row(norm_ffn_w[l]), tl["merge_tm"])
        if l % 2 == 0:
            i = l // 2
            x2 = _dense_ffn(h2, x_mid, dense_w_gate[i].astype(BF16), dense_w_up[i].astype(BF16),
                            dense_w_down[i].astype(BF16), tl["ffn_tm"], tl["ffn_tf"])
        else:
            i = l // 2
            x2 = _moe_layer(x_mid, row(norm_ffn_w[l]), moe_w_router[i], moe_w_gate[i],
                            moe_w_up[i], moe_w_down[i], row(final_norm_w), tl)
    return x2.reshape(B, S, D)
```

```python
import functools

import numpy as np
import jax
import jax.numpy as jnp
from jax import lax
from jax.experimental import pallas as pl
from jax.experimental.pallas import tpu as pltpu

F32 = jnp.float32
BF16 = jnp.bfloat16

D_MODEL = 1024
DEPTH = 2
FOX_HEADS = 8
FOX_HEAD_DIM = 64
FOX_W = FOX_HEADS * FOX_HEAD_DIM
RET_HEADS = 8
RET_QK_DIM = 64
RET_V_DIM = 128
RET_QK_W = RET_HEADS * RET_QK_DIM
RET_V_W = RET_HEADS * RET_V_DIM
MLA_HEADS = 8
MLA_Q_RANK = 384
MLA_KV_RANK = 256
MLA_NOPE_DIM = 64
MLA_ROPE_DIM = 32
MLA_V_DIM = 64
MLA_QK_DIM = MLA_NOPE_DIM + MLA_ROPE_DIM
MLA_V_W = MLA_HEADS * MLA_V_DIM
N_BRANCHES = 3
ROPE_THETA = 10000.0
RMS_EPS = 1e-6
D_FF = 3584
N_EXPERTS = 8
TOP_K = 2
IN_SPLITS = (FOX_W, FOX_W, FOX_W, FOX_HEADS,
             RET_QK_W, RET_QK_W, RET_V_W, RET_V_W,
             MLA_Q_RANK, MLA_KV_RANK, MLA_ROPE_DIM,
             N_BRANCHES * D_MODEL)

LANE = 128
NEG = -1e30
VMEM_LIMIT = 56 * 1024 * 1024

U_FQ, U_FK, U_FV = 0, 4, 8
U_RQ, U_RK = 12, 16
U_KR, U_CQ = 20, 21
U_RV, U_RG, U_GL, U_CKV = 24, 32, 40, 64
N_PROJ = 66 * LANE
FOX_EXT_STRIDE = 8
PAIRS = 4
QK_W = 2 * LANE
ATTN_KEY_CHUNK = 256
ATTN_STRIP = 256
VT_ROWS = 80
LOG2E = 1.4426950408889634


def _cparams(sem, vmem=VMEM_LIMIT):
    return pltpu.CompilerParams(dimension_semantics=sem, vmem_limit_bytes=vmem)


def _sigmoid(x):
    return 1.0 / (1.0 + jnp.exp(-x))


def _lane_mask(shape, ranges):
    lane = lax.broadcasted_iota(jnp.int32, shape, len(shape) - 1)
    m = None
    for a, b in ranges:
        r = (lane >= a) & (lane < b)
        m = r if m is None else (m | r)
    return m


def _inproj_kernel(x_ref, nw_ref, w_ref, wff_ref, out_ref, ff_ref, h_scr):
    @pl.when(pl.program_id(1) == 0)
    def _():
        x = x_ref[...]
        ms = jnp.mean(x * x, axis=-1, keepdims=True)
        h = (x * lax.rsqrt(ms + RMS_EPS) * nw_ref[...]).astype(BF16)
        h_scr[...] = h
        ff_ref[...] = jnp.dot(h, wff_ref[...], preferred_element_type=F32)

    out_ref[...] = jnp.dot(h_scr[...], w_ref[...], preferred_element_type=F32).astype(out_ref.dtype)


def _inproj(x2, nw, w_all, w_ff, tm, tn):
    T = x2.shape[0]
    return pl.pallas_call(
        _inproj_kernel,
        grid=(T // tm, N_PROJ // tn),
        in_specs=[
            pl.BlockSpec((tm, D_MODEL), lambda i, j: (i, 0)),
            pl.BlockSpec((1, D_MODEL), lambda i, j: (0, 0)),
            pl.BlockSpec((D_MODEL, tn), lambda i, j: (0, j)),
            pl.BlockSpec((D_MODEL, LANE), lambda i, j: (0, 0)),
        ],
        out_specs=[
            pl.BlockSpec((tm, tn), lambda i, j: (i, j)),
            pl.BlockSpec((tm, LANE), lambda i, j: (i, 0)),
        ],
        out_shape=[
            jax.ShapeDtypeStruct((T, N_PROJ), BF16),
            jax.ShapeDtypeStruct((T, LANE), F32),
        ],
        scratch_shapes=[pltpu.VMEM((tm, D_MODEL), BF16)],
        compiler_params=_cparams(("parallel", "arbitrary")),
        name="inproj",
    )(x2, nw, w_all, w_ff)


def _rot_half_lanes(x, d):
    lane = lax.broadcasted_iota(jnp.int32, x.shape, 1)
    from_right = pltpu.roll(x, LANE - d // 2, 1)
    from_left = pltpu.roll(x, d // 2, 1)
    return jnp.where(lane % d < d // 2, -from_right, from_left)


def _wprep_kernel(w_ref, *out_refs):
    offs = [int(i) for i in np.cumsum((0,) + IN_SPLITS)]
    o_fq, o_fk, o_fv, o_ff, o_rq, o_rk, o_rv, o_rg, o_cq, o_ckv, o_kr, o_gl, _ = offs
    depth = w_ref.shape[0]
    lane = lax.broadcasted_iota(jnp.int32, (w_ref.shape[1], LANE), 1)
    rope_lanes = (lane >= MLA_NOPE_DIM) & (lane < MLA_QK_DIM)
    for l in range(depth):
        out_ref, ff_ref = out_refs[l], out_refs[depth + l]

        def put(unit, piece, out_ref=out_ref):
            out_ref[:, unit * LANE:unit * LANE + piece.shape[-1]] = piece.astype(out_ref.dtype)

        put(U_FQ, w_ref[l, :, o_fq:o_ff])
        ff_ref[...] = jnp.where(lane < FOX_HEADS, w_ref[l, :, o_ff:o_ff + LANE], 0.0).astype(ff_ref.dtype)
        put(U_RQ, w_ref[l, :, o_rq:o_rv])
        put(U_CKV, w_ref[l, :, o_ckv:o_kr])
        put(U_KR, jnp.where(rope_lanes, w_ref[l, :, o_kr - MLA_NOPE_DIM:o_kr - MLA_NOPE_DIM + LANE], 0.0))
        put(U_RV, w_ref[l, :, o_rv:o_rg])
        put(U_RG, w_ref[l, :, o_rg:o_cq])
        put(U_GL, w_ref[l, :, o_gl:o_gl + N_BRANCHES * D_MODEL])
        put(U_CQ, w_ref[l, :, o_cq:o_ckv])


def _build_inproj_weights(w_in, rows=128):
    depth, _, d_in = w_in.shape
    outs = pl.pallas_call(
        _wprep_kernel,
        grid=(D_MODEL // rows,),
        in_specs=[pl.BlockSpec((depth, rows, d_in), lambda r: (0, r, 0))],
        out_specs=([pl.BlockSpec((rows, N_PROJ), lambda r: (r, 0))] * depth
                   + [pl.BlockSpec((rows, LANE), lambda r: (r, 0))] * depth),
        out_shape=([jax.ShapeDtypeStruct((D_MODEL, N_PROJ), BF16)] * depth
                   + [jax.ShapeDtypeStruct((D_MODEL, LANE), BF16)] * depth),
        compiler_params=_cparams(("parallel",)),
        name="inproj_weight_prep",
    )(w_in)
    return outs[:depth], outs[depth:]


def _split3(x):
    hi = x.astype(BF16)
    r1 = x - hi.astype(F32)
    mid = r1.astype(BF16)
    lo = (r1 - mid.astype(F32)).astype(BF16)
    return hi, mid, lo


def _store_t(dst_ref, idx, x):
    dst_ref[idx] = x.T.astype(dst_ref.dtype)


def _store_vt_pair(vt_ref, p, v2):
    hd = LANE // 2
    vt = v2.T.astype(vt_ref.dtype)
    pad_rows = lax.broadcasted_iota(jnp.int32, (VT_ROWS - hd, vt.shape[1]), 0)
    ones_then_zeros = jnp.where(pad_rows == 0, 1.0, 0.0).astype(vt_ref.dtype)
    for j in range(2):
        base = (2 * p + j) * VT_ROWS
        vt_ref[0, 0, base:base + hd, :] = vt[j * hd:(j + 1) * hd]
        vt_ref[0, 0, base + hd:base + VT_ROWS, :] = ones_then_zeros


def _fox_prep_kernel(fq_ref, fk_ref, fv_ref, ff_ref, bias_ref, tri_ref, pq_ref, pk_ref, cq_ref, ck_ref,
                     qt_ref, kx_ref, vt_ref, carry_scr):
    @pl.when(pl.program_id(1) == 0)
    def _():
        carry_scr[...] = jnp.zeros_like(carry_scr)

    ts = ff_ref.shape[0]
    z = ff_ref[...] + bias_ref[...]
    logf = jnp.minimum(z, 0.0) - jnp.log(1.0 + jnp.exp(-jnp.abs(z)))
    tri = tri_ref[...]
    cum = carry_scr[0:1, :]
    for part in _split3(logf):
        cum = cum + jnp.dot(tri, part, preferred_element_type=F32)
    carry_scr[0:1, :] = cum[ts - 1:ts, :]

    cum3 = jnp.concatenate(_split3(cum * LOG2E), axis=-1)
    eq = cq_ref[...] + jnp.dot(cum3, pq_ref[...], preferred_element_type=F32)
    ek = ck_ref[...] + jnp.dot(cum3, pk_ref[...], preferred_element_type=F32)
    scale = FOX_HEAD_DIM ** -0.5 * LOG2E
    lane = lax.broadcasted_iota(jnp.int32, (ts, LANE), 1)
    for p in range(FOX_HEADS // 2):
        src = slice(p * LANE, (p + 1) * LANE)
        q2 = fq_ref[:, src].astype(F32) * scale
        k2 = fk_ref[:, src].astype(F32)
        for j in range(2):
            h = 2 * p + j
            dst = slice(h * LANE, (h + 1) * LANE)
            qh = q2 if j == 0 else pltpu.roll(q2, LANE // 2, 1)
            kh = k2 if j == 0 else pltpu.roll(k2, LANE // 2, 1)
            own = (lane >= FOX_HEAD_DIM + FOX_EXT_STRIDE * h) & (lane < FOX_HEAD_DIM + FOX_EXT_STRIDE * (h + 1))
            _store_t(qt_ref, (0, dst), jnp.where(lane < FOX_HEAD_DIM, qh, jnp.where(own, eq, 0.0)))
            kx_ref[:, dst] = jnp.where(lane < FOX_HEAD_DIM, kh, ek).astype(BF16)
        _store_vt_pair(vt_ref, p, fv_ref[:, src].astype(F32))


def _fox_prep_consts(ts):
    tri = np.tril(np.ones((ts, ts), np.float32))
    pq = np.zeros((3 * LANE, LANE), np.float32)
    pk = np.zeros((3 * LANE, LANE), np.float32)
    cq = np.zeros((1, LANE), np.float32)
    ck = np.zeros((1, LANE), np.float32)
    for h in range(FOX_HEADS):
        base = FOX_HEAD_DIM + FOX_EXT_STRIDE * h
        for j in range(3):
            pq[j * LANE + h, base + j] = 1.0
            ck[0, base + j] = 1.0
            cq[0, base + 3 + j] = 1.0
            pk[j * LANE + h, base + 3 + j] = -1.0
    return (jnp.asarray(tri, BF16), jnp.asarray(pq, BF16), jnp.asarray(pk, BF16),
            jnp.asarray(cq), jnp.asarray(ck))


def _attn_operand_specs(B, S, t):
    T = B * S
    ns = S // t
    specs = [
        pl.BlockSpec((1, PAIRS * QK_W, t), lambda b, s: (b, 0, s)),
        pl.BlockSpec((t, PAIRS * QK_W), lambda b, s: (b * ns + s, 0)),
        pl.BlockSpec((1, 1, 2 * PAIRS * VT_ROWS, t), lambda b, s: (b, s, 0, 0)),
    ]
    shapes = [
        jax.ShapeDtypeStruct((B, PAIRS * QK_W, S), BF16),
        jax.ShapeDtypeStruct((T, PAIRS * QK_W), BF16),
        jax.ShapeDtypeStruct((B, ns, 2 * PAIRS * VT_ROWS, t), BF16),
    ]
    return specs, shapes


def _fox_prep(proj, ff32, bias, B, S, ts):
    ns = S // ts
    tri, pq, pk, cq, ck = _fox_prep_consts(ts)
    bias128 = jnp.pad(bias.astype(F32), (0, LANE - FOX_HEADS)).reshape(1, LANE)
    const = lambda shape: pl.BlockSpec(shape, lambda b, s: (0, 0))
    out_specs, out_shape = _attn_operand_specs(B, S, ts)
    return pl.pallas_call(
        _fox_prep_kernel,
        grid=(B, ns),
        in_specs=[
            pl.BlockSpec((ts, FOX_W), lambda b, s: (b * ns + s, U_FQ // 4)),
            pl.BlockSpec((ts, FOX_W), lambda b, s: (b * ns + s, U_FK // 4)),
            pl.BlockSpec((ts, FOX_W), lambda b, s: (b * ns + s, U_FV // 4)),
            pl.BlockSpec((ts, LANE), lambda b, s: (b * ns + s, 0)),
            const((1, LANE)), const((ts, ts)), const(pq.shape), const(pk.shape), const(cq.shape), const(ck.shape),
        ],
        out_specs=out_specs,
        out_shape=out_shape,
        scratch_shapes=[pltpu.VMEM((8, LANE), F32)],
        compiler_params=_cparams(("parallel", "arbitrary")),
        name="fox_prep",
    )(proj, proj, proj, ff32, bias128, tri, pq, pk, cq, ck)


def _col_reduce(x, op, reduce_fn, parts=2):
    c = x.shape[0] // parts
    pieces = [x[i * c:(i + 1) * c] for i in range(parts)]
    while len(pieces) > 1:
        pieces = [op(pieces[i], pieces[i + 1]) for i in range(0, len(pieces), 2)]
    return reduce_fn(pieces[0], axis=0, keepdims=True)


def _attn_kernel(qt_ref, qn_ref, k_ref, vt_ref, o_ref, sa_scr, sb_scr, ma_scr, mb_scr, acc_scr, *, tk):
    qi = pl.program_id(2)
    tq = qt_ref.shape[-1]
    acc_scr[...] = jnp.zeros_like(acc_scr)

    def key_tile(kt):
        return k_ref[pl.ds(pl.multiple_of(kt * tk, tk), tk), :]

    strips = [(c, c + ATTN_STRIP) for c in range(0, tq, ATTN_STRIP)]

    def scores(kt, s_scr, m_scr):
        k = key_tile(kt)
        for h in range(2):
            for a, b in strips:
                s = jnp.dot(k[:, h * LANE:(h + 1) * LANE], qt_ref[0, h * LANE:(h + 1) * LANE, a:b],
                            preferred_element_type=F32)
                s_scr[h, :, a:b] = s
                m_scr[h, :, a:b] = _col_reduce(s, jnp.maximum, jnp.max)

    def update(kt, h, s, m_tile, m_prev, lo):
        m_new = jnp.maximum(m_prev, m_tile)
        alpha = jnp.exp2(m_prev - m_new)
        hi = lo + s.shape[-1]
        pv = None
        for r in range(0, tk, ATTN_KEY_CHUNK):
            p = jnp.exp2((s[r:r + ATTN_KEY_CHUNK] - m_new).astype(BF16))
            vt = vt_ref[0, kt, h * VT_ROWS:(h + 1) * VT_ROWS, r:r + ATTN_KEY_CHUNK]
            d = jnp.dot(vt, p, preferred_element_type=F32)
            pv = d if pv is None else pv + d
        acc_scr[h, :, lo:hi] = alpha * acc_scr[h, :, lo:hi] + pv
        return m_new

    def diag_update(kt, h, s_scr, m_prev, a, b, lo):
        s = s_scr[h, :, a:b]
        if a < lo + tk:
            key = lax.broadcasted_iota(jnp.int32, s.shape, 0)
            qry = lax.broadcasted_iota(jnp.int32, s.shape, 1) + (a - lo)
            s = jnp.where(key <= qry, s, NEG)
        return update(kt, h, s, _col_reduce(s, jnp.maximum, jnp.max), m_prev, a)

    def diagonal_tiles(carry, lookahead):
        k_last = key_tile(2 * qi + 1)
        if lookahead:
            k0 = key_tile(0)
        out = []
        for h in range(2):
            ms = []
            for a, b in strips:
                if a >= tk:
                    sb_scr[h, :, a:b] = jnp.dot(k_last[:, h * LANE:(h + 1) * LANE],
                                                qt_ref[0, h * LANE:(h + 1) * LANE, a:b],
                                                preferred_element_type=F32)
                ms.append(diag_update(2 * qi, h, sa_scr, carry[h][:, a:b], a, b, 0))
                if lookahead:
                    s = jnp.dot(k0[:, h * LANE:(h + 1) * LANE], qn_ref[0, h * LANE:(h + 1) * LANE, a:b],
                                preferred_element_type=F32)
                    sa_scr[h, :, a:b] = s
                    ma_scr[h, :, a:b] = _col_reduce(s, jnp.maximum, jnp.max)
            out.append(ms)
        for h in range(2):
            for i, (a, b) in enumerate(strips):
                if a >= tk:
                    diag_update(2 * qi + 1, h, sb_scr, out[h][i], a, b, tk)
        hd = LANE // 2
        ot = jnp.concatenate([acc_scr[h, :hd] / acc_scr[h, hd:hd + 1] for h in range(2)], axis=0)
        o_ref[...] = ot.T.astype(o_ref.dtype)

    def fused_step(kt, cur_s, cur_m, nxt_s, nxt_m, carry):
        k = key_tile(kt + 1)
        ms = [[], []]
        for a, b in strips:
            for h in range(2):
                s = jnp.dot(k[:, h * LANE:(h + 1) * LANE], qt_ref[0, h * LANE:(h + 1) * LANE, a:b],
                            preferred_element_type=F32)
                nxt_s[h, :, a:b] = s
                nxt_m[h, :, a:b] = _col_reduce(s, jnp.maximum, jnp.max)
                ms[h].append(update(kt, h, cur_s[h, :, a:b], cur_m[h, :, a:b], carry[h][:, a:b], a))
        return tuple(jnp.concatenate(m, axis=-1) for m in ms)

    def body(j, carry):
        kt = 2 * j
        carry = fused_step(kt, sa_scr, ma_scr, sb_scr, mb_scr, carry)
        return fused_step(kt + 1, sb_scr, mb_scr, sa_scr, ma_scr, carry)

    init = tuple(jnp.full((1, tq), NEG, F32) for _ in range(2))

    @pl.when(qi == 0)
    def _():
        scores(0, sa_scr, ma_scr)

    carry = lax.fori_loop(0, qi, body, init)
    last = pl.num_programs(2) - 1

    @pl.when(qi < last)
    def _():
        diagonal_tiles(carry, True)

    @pl.when(qi == last)
    def _():
        diagonal_tiles(carry, False)


def _attention(qt, kx, vt, B, S, tk, name):
    T = B * S
    tq = 2 * tk
    nq = S // tq
    kern = functools.partial(_attn_kernel, tk=tk)
    return pl.pallas_call(
        kern,
        grid=(B, PAIRS, nq),
        in_specs=[
            pl.BlockSpec((1, QK_W, tq), lambda b, p, i: (b, p, i)),
            pl.BlockSpec((1, QK_W, tq), lambda b, p, i: (b, p, jnp.minimum(i + 1, nq - 1))),
            pl.BlockSpec((S, QK_W), lambda b, p, i: (b, p)),
            pl.BlockSpec((1, S // tk, 2 * VT_ROWS, tk), lambda b, p, i: (b, 0, p, 0)),
        ],
        out_specs=pl.BlockSpec((tq, LANE), lambda b, p, i: (b * nq + i, p)),
        out_shape=jax.ShapeDtypeStruct((T, PAIRS * LANE), BF16),
        scratch_shapes=[
            pltpu.VMEM((2, tk, tq), F32),
            pltpu.VMEM((2, tk, tq), F32),
            pltpu.VMEM((2, 1, tq), F32),
            pltpu.VMEM((2, 1, tq), F32),
            pltpu.VMEM((2, VT_ROWS, tq), F32),
        ],
        compiler_params=_cparams(("parallel", "parallel", "arbitrary")),
        name=name,
    )(qt, qt, kx, vt)


def _mla_prep_kernel(cq_ref, ckv_ref, kr_ref, qnw_ref, kvnw_ref, wq_ref, wk_ref, wv_ref,
                     cq_tab, sq_tab, ck_tab, sk_tab, qt_ref, k_ref, vt_ref):
    def norm(x_ref, w_ref):
        x = x_ref[...].astype(F32)
        ms = jnp.mean(x * x, axis=-1, keepdims=True)
        return (x * lax.rsqrt(ms + RMS_EPS) * w_ref[...]).astype(BF16)

    c_q = norm(cq_ref, qnw_ref)
    c_kv = norm(ckv_ref, kvnw_ref)
    tile8 = lambda a: jnp.concatenate([a] * MLA_HEADS, axis=-1)
    q_raw = jnp.dot(c_q, wq_ref[...], preferred_element_type=F32)
    for c in range(MLA_HEADS):
        qs = q_raw[:, c * LANE:(c + 1) * LANE]
        q = qs * cq_tab[...] + _rot_half_lanes(qs, MLA_ROPE_DIM) * sq_tab[...]
        _store_t(qt_ref, (0, slice(c * LANE, (c + 1) * LANE)), q)
    kr = kr_ref[...].astype(F32)
    k_rope = kr * ck_tab[...] + _rot_half_lanes(kr, MLA_ROPE_DIM) * sk_tab[...]
    k = jnp.dot(c_kv, wk_ref[...], preferred_element_type=F32) + tile8(k_rope)
    k_ref[...] = k.astype(BF16)
    v = jnp.dot(c_kv, wv_ref[...], preferred_element_type=F32)
    for p in range(PAIRS):
        _store_vt_pair(vt_ref, p, v[:, p * LANE:(p + 1) * LANE])


def _mla_weights(w_uq, w_uk, w_uv):
    q3 = w_uq.reshape(MLA_Q_RANK, MLA_HEADS, MLA_QK_DIM)
    nope, ropep = q3[..., :MLA_NOPE_DIM], q3[..., MLA_NOPE_DIM:]
    z32 = jnp.zeros((MLA_Q_RANK, MLA_HEADS, LANE - MLA_QK_DIM), F32)
    wq = jnp.concatenate([nope, ropep, z32], axis=-1).reshape(MLA_Q_RANK, MLA_HEADS * LANE)
    k3 = w_uk.reshape(MLA_KV_RANK, MLA_HEADS, MLA_NOPE_DIM)
    wk = jnp.concatenate([k3, jnp.zeros_like(k3)], axis=-1).reshape(MLA_KV_RANK, MLA_HEADS * LANE)
    return wq.astype(BF16), wk.astype(BF16), w_uv.astype(BF16)


def _rope_inv_freq(d):
    return (np.float32(ROPE_THETA) ** (-np.arange(0, d, 2, dtype=np.float32) / np.float32(d))).astype(np.float32)


def _mla_tables(S):
    inv = _rope_inv_freq(MLA_ROPE_DIM)
    inv128 = np.concatenate([np.zeros(MLA_NOPE_DIM, np.float32), inv, inv, np.zeros(LANE - MLA_QK_DIM, np.float32)])
    keep = (np.arange(LANE) < MLA_QK_DIM).astype(np.float32)
    ang = jnp.arange(S, dtype=F32)[:, None] * jnp.asarray(inv128)[None, :]
    ctab = jnp.cos(ang) * jnp.asarray(keep)[None, :]
    stab = jnp.sin(ang)
    scale = MLA_QK_DIM ** -0.5 * LOG2E
    return ctab * scale, stab * scale, ctab, stab


def _mla_prep(proj, qnw, kvnw, wq, wk, wv, tabs, B, S, tm):
    ns = S // tm
    HW = MLA_HEADS * LANE
    const = lambda shape: pl.BlockSpec(shape, lambda b, s: (0, 0))
    tab = pl.BlockSpec((tm, LANE), lambda b, s: (s, 0))
    out_specs, out_shape = _attn_operand_specs(B, S, tm)
    return pl.pallas_call(
        _mla_prep_kernel,
        grid=(B, ns),
        in_specs=[
            pl.BlockSpec((tm, MLA_Q_RANK), lambda b, s: (b * ns + s, U_CQ // 3)),
            pl.BlockSpec((tm, MLA_KV_RANK), lambda b, s: (b * ns + s, U_CKV // 2)),
            pl.BlockSpec((tm, LANE), lambda b, s: (b * ns + s, U_KR)),
            const((1, MLA_Q_RANK)), const((1, MLA_KV_RANK)),
            const((MLA_Q_RANK, HW)), const((MLA_KV_RANK, HW)),
            const((MLA_KV_RANK, MLA_V_W)),
            tab, tab, tab, tab,
        ],
        out_specs=out_specs,
        out_shape=out_shape,
        compiler_params=_cparams(("parallel", "parallel")),
        name="mla_prep",
    )(proj, proj, proj, qnw, kvnw, wq, wk, wv, *tabs)


def _ret_kernel(rq_ref, rk_ref, rv_ref, rg_ref, cos_ref, sin_ref, perm_ref,
                din_ref, xi_ref, zeta_ref, gch_ref, gnw_ref, o_ref, r_scr):
    @pl.when(pl.program_id(0) == 0)
    def _():
        r_scr[...] = jnp.zeros_like(r_scr)

    tile4 = lambda a: jnp.concatenate([a] * (RET_QK_W // LANE), axis=-1)
    cos4 = tile4(cos_ref[...])
    sin4 = tile4(sin_ref[...])
    for bi in range(rq_ref.shape[0]):
        def rope(x_ref):
            x = x_ref[bi]
            rot = jnp.dot(x, perm_ref[...], preferred_element_type=F32)
            return x.astype(F32) * cos4 + rot * sin4

        q = rope(rq_ref)
        k = rope(rk_ref) * (RET_QK_DIM ** -0.5)
        for p in range(RET_HEADS // 2):
            q2 = q[:, p * LANE:(p + 1) * LANE]
            k2 = k[:, p * LANE:(p + 1) * LANE]
            k2b = k2.astype(BF16)
            kz2 = (k2 * zeta_ref[p]).astype(BF16)
            r2 = r_scr[bi, p]
            r2b = r2.astype(BF16)
            new_r = gch_ref[p] * r2
            for j in range(2):
                h = 2 * p + j
                half = ((j * RET_QK_DIM, (j + 1) * RET_QK_DIM),)
                qm = jnp.where(_lane_mask(q2.shape, half), q2, 0.0).astype(BF16)
                inner = lax.dot_general(qm, k2b, (((1,), (1,)), ((), ())), preferred_element_type=F32)
                inner = inner * din_ref[h]
                vh = rv_ref[bi, :, h * LANE:(h + 1) * LANE]
                o = (jnp.dot(inner.astype(BF16), vh, preferred_element_type=F32)
                     + jnp.dot(qm, r2b, preferred_element_type=F32) * xi_ref[h])
                upd = lax.dot_general(kz2, vh, (((0,), (0,)), ((), ())), preferred_element_type=F32)
                rows = lax.broadcasted_iota(jnp.int32, upd.shape, 0)
                new_r = new_r + jnp.where((rows >= half[0][0]) & (rows < half[0][1]), upd, 0.0)
                mu = jnp.mean(o, axis=-1, keepdims=True)
                d = o - mu
                var = jnp.mean(d * d, axis=-1, keepdims=True)
                on = d * lax.rsqrt(var + RMS_EPS) * gnw_ref[:, h * LANE:(h + 1) * LANE]
                g = rg_ref[bi, :, h * LANE:(h + 1) * LANE].astype(F32)
                o_ref[bi, :, h * LANE:(h + 1) * LANE] = (g * _sigmoid(g) * on).astype(o_ref.dtype)
            r_scr[bi, p] = new_r


def _ret_tables(S, C):
    inv128 = np.tile(_rope_inv_freq(RET_QK_DIM), 2 * LANE // RET_QK_DIM)
    ang = jnp.arange(S, dtype=F32)[:, None] * jnp.asarray(inv128)[None, :]
    cos = jnp.cos(ang)
    sin = jnp.sin(ang)
    gammas = 1.0 - 2.0 ** (-5.0 - jnp.arange(RET_HEADS, dtype=F32))
    log_g = jnp.log(gammas)
    j = jnp.arange(C, dtype=F32)
    diff = j[:, None] - j[None, :]
    din = jnp.where(diff[None] >= 0, jnp.exp(jnp.maximum(diff, 0.0)[None] * log_g[:, None, None]), 0.0)
    xi = jnp.exp((j[None, :] + 1.0) * log_g[:, None])
    zeta = jnp.exp((C - 1.0 - j[None, :]) * log_g[:, None])
    gch = jnp.exp(C * log_g)
    xi_t = jnp.broadcast_to(xi[:, :, None], (RET_HEADS, C, LANE))
    zeta_p = jnp.repeat(zeta.reshape(RET_HEADS // 2, 2, C).transpose(0, 2, 1), RET_QK_DIM, axis=-1)
    gch_p = jnp.broadcast_to(jnp.repeat(gch.reshape(RET_HEADS // 2, 2), RET_QK_DIM, axis=-1)[:, :, None],
                             (RET_HEADS // 2, LANE, LANE))
    return cos, sin, din, xi_t, zeta_p, gch_p


def _retention(proj, gnw, tabs, B, S, C):
    nc = S // C
    cos, sin, din, xi, zeta, gch = tabs
    proj3 = proj.reshape(B, S, N_PROJ)
    perm = np.zeros((RET_QK_W, RET_QK_W), np.float32)
    half = RET_QK_DIM // 2
    for h in range(RET_HEADS):
        for j in range(half):
            perm[h * RET_QK_DIM + half + j, h * RET_QK_DIM + j] = -1.0
            perm[h * RET_QK_DIM + j, h * RET_QK_DIM + half + j] = 1.0
    perm = jnp.asarray(perm, BF16)
    blk = lambda w, unit: pl.BlockSpec((B, C, w), lambda c: (0, c, unit * LANE // w))
    full = lambda a: pl.BlockSpec(a.shape, lambda c: (0,) * a.ndim)
    out = pl.pallas_call(
        _ret_kernel,
        grid=(nc,),
        in_specs=[
            blk(RET_QK_W, U_RQ), blk(RET_QK_W, U_RK),
            blk(RET_V_W, U_RV), blk(RET_V_W, U_RG),
            pl.BlockSpec((C, LANE), lambda c: (c, 0)),
            pl.BlockSpec((C, LANE), lambda c: (c, 0)),
            full(perm), full(din), full(xi), full(zeta), full(gch), full(gnw),
        ],
        out_specs=pl.BlockSpec((B, C, RET_V_W), lambda c: (0, c, 0)),
        out_shape=jax.ShapeDtypeStruct((B, S, RET_V_W), BF16),
        scratch_shapes=[pltpu.VMEM((B, RET_HEADS // 2, LANE, LANE), F32)],
        compiler_params=_cparams(("arbitrary",)),
        name="retention",
    )(proj3, proj3, proj3, proj3, cos, sin, perm, din, xi, zeta, gch, gnw)
    return out.reshape(B * S, RET_V_W)


def _merge_kernel(of_ref, or_ref, om_ref, g0_ref, g1_ref, g2_ref, x_ref,
                  wf_ref, wr_ref, wm_ref, wo_ref, nw_ref, xo_ref, h_ref):
    merged = (_sigmoid(g0_ref[...].astype(F32)) * jnp.dot(of_ref[...], wf_ref[...], preferred_element_type=F32)
              + _sigmoid(g1_ref[...].astype(F32)) * jnp.dot(or_ref[...], wr_ref[...], preferred_element_type=F32)
              + _sigmoid(g2_ref[...].astype(F32)) * jnp.dot(om_ref[...], wm_ref[...], preferred_element_type=F32))
    x = x_ref[...] + jnp.dot(merged.astype(BF16), wo_ref[...], preferred_element_type=F32)
    xo_ref[...] = x
    ms = jnp.mean(x * x, axis=-1, keepdims=True)
    h_ref[...] = (x * lax.rsqrt(ms + RMS_EPS) * nw_ref[...]).astype(h_ref.dtype)


def _merge(o_fox, o_ret, o_mla, proj, x2, wf, wr, wm, wo, nw, tm):
    T = x2.shape[0]
    row = lambda w, cb=0: pl.BlockSpec((tm, w), lambda i: (i, cb))
    const = lambda a: pl.BlockSpec(a.shape, lambda i: (0, 0))
    g_unit = U_GL // 8
    return pl.pallas_call(
        _merge_kernel,
        grid=(T // tm,),
        in_specs=[
            row(FOX_W), row(RET_V_W), row(MLA_V_W),
            row(D_MODEL, g_unit), row(D_MODEL, g_unit + 1), row(D_MODEL, g_unit + 2),
            row(D_MODEL),
            const(wf), const(wr), const(wm), const(wo), const(nw),
        ],
        out_specs=[row(D_MODEL), row(D_MODEL)],
        out_shape=[jax.ShapeDtypeStruct((T, D_MODEL), F32), jax.ShapeDtypeStruct((T, D_MODEL), BF16)],
        compiler_params=_cparams(("parallel",)),
        name="merge",
    )(o_fox, o_ret, o_mla, proj, proj, proj, x2, wf, wr, wm, wo, nw)


def _swiglu_tile(h, wg, wu, wd):
    a = jnp.dot(h, wg, preferred_element_type=F32)
    b = jnp.dot(h, wu, preferred_element_type=F32)
    return jnp.dot((a * _sigmoid(a) * b).astype(BF16), wd, preferred_element_type=F32)


def _dense_ffn_kernel(h_ref, wg_ref, wu_ref, wd_ref, x_ref, o_ref, acc_scr):
    f = pl.program_id(1)

    @pl.when(f == 0)
    def _():
        acc_scr[...] = jnp.zeros_like(acc_scr)

    acc_scr[...] += _swiglu_tile(h_ref[...], wg_ref[...], wu_ref[...], wd_ref[...])

    @pl.when(f == pl.num_programs(1) - 1)
    def _():
        o_ref[...] = x_ref[...] + acc_scr[...]


def _dense_ffn(h, x2, wg, wu, wd, tm, tf):
    T = x2.shape[0]
    return pl.pallas_call(
        _dense_ffn_kernel,
        grid=(T // tm, D_FF // tf),
        in_specs=[
            pl.BlockSpec((tm, D_MODEL), lambda i, f: (i, 0)),
            pl.BlockSpec((D_MODEL, tf), lambda i, f: (0, f)),
            pl.BlockSpec((D_MODEL, tf), lambda i, f: (0, f)),
            pl.BlockSpec((tf, D_MODEL), lambda i, f: (f, 0)),
            pl.BlockSpec((tm, D_MODEL), lambda i, f: (i, 0)),
        ],
        out_specs=pl.BlockSpec((tm, D_MODEL), lambda i, f: (i, 0)),
        out_shape=jax.ShapeDtypeStruct((T, D_MODEL), F32),
        scratch_shapes=[pltpu.VMEM((tm, D_MODEL), F32)],
        compiler_params=_cparams(("parallel", "arbitrary")),
        name="dense_ffn",
    )(h, wg, wu, wd, x2)


M_E0, M_E1, M_P0, M_P1, M_W0, M_W1 = 0, 1, 2, 3, 4, 5
ROUTE_ROWS = 8


def _split2(x):
    hi = x.astype(BF16)
    lo = (x - hi.astype(F32)).astype(BF16)
    return hi, lo


HALF_D = D_MODEL // 2


def _pack_rows(h):
    bits = pltpu.bitcast(h.astype(BF16).astype(F32), jnp.uint32)
    return (bits[:, :HALF_D] & jnp.uint32(0xFFFF0000)) | (bits[:, HALF_D:] >> 16)


def _unpack_rows(u):
    left = pltpu.bitcast(u & jnp.uint32(0xFFFF0000), F32)
    right = pltpu.bitcast(u << 16, F32)
    return jnp.concatenate([left, right], axis=-1).astype(BF16)


def _router_kernel(x_ref, nw_ref, whi_ref, wlo_ref, tri_ref, h_ref, meta_ref, route_ref, cnt_ref, carry_scr):
    @pl.when(pl.program_id(0) == 0)
    def _():
        carry_scr[...] = jnp.zeros_like(carry_scr)

    x = x_ref[...]
    ms = jnp.mean(x * x, axis=-1, keepdims=True)
    h = x * lax.rsqrt(ms + RMS_EPS) * nw_ref[...]
    h_ref[...] = _pack_rows(h)
    hhi, hlo = _split2(h)
    logits = (jnp.dot(hhi, whi_ref[...], preferred_element_type=F32)
              + jnp.dot(hhi, wlo_ref[...], preferred_element_type=F32)
              + jnp.dot(hlo, whi_ref[...], preferred_element_type=F32))
    lane = lax.broadcasted_iota(jnp.int32, logits.shape, 1)
    lanef = lane.astype(F32)
    logits = jnp.where(lane < N_EXPERTS, logits, NEG)
    m0 = jnp.max(logits, axis=-1, keepdims=True)
    e0 = jnp.min(jnp.where(logits == m0, lanef, float(LANE)), axis=-1, keepdims=True)
    oh0 = lanef == e0
    rest = jnp.where(oh0, NEG, logits)
    m1 = jnp.max(rest, axis=-1, keepdims=True)
    e1 = jnp.min(jnp.where(rest == m1, lanef, float(LANE)), axis=-1, keepdims=True)
    oh1 = lanef == e1
    z = jnp.exp(m1 - m0)
    w0 = 1.0 / (1.0 + z)
    w1 = z / (1.0 + z)
    both = jnp.where(oh0 | oh1, 1.0, 0.0)
    before = carry_scr[0:1, :] + jnp.dot(tri_ref[...], both.astype(BF16), preferred_element_type=F32)
    p0 = jnp.sum(jnp.where(oh0, before, 0.0), axis=-1, keepdims=True)
    p1 = jnp.sum(jnp.where(oh1, before, 0.0), axis=-1, keepdims=True)
    total = carry_scr[0:1, :] + jnp.sum(both, axis=0, keepdims=True)
    carry_scr[0:1, :] = total
    cnt_ref[...] = jnp.broadcast_to(total, cnt_ref.shape)
    meta = jnp.zeros(logits.shape, F32)
    for idx, val in ((M_E0, e0), (M_E1, e1), (M_P0, p0), (M_P1, p1), (M_W0, w0), (M_W1, w1)):
        meta = jnp.where(lane == idx, val, meta)
    meta_ref[...] = meta
    route_ref[0] = meta.T[:ROUTE_ROWS, :]


def _router(x2, nw, w_router, tr):
    T = x2.shape[0]
    wr = jnp.pad(w_router.astype(F32), ((0, 0), (0, LANE - N_EXPERTS)))
    whi = wr.astype(BF16)
    wlo = (wr - whi.astype(F32)).astype(BF16)
    tri = jnp.asarray(np.tril(np.ones((tr, tr), np.float32), -1), BF16)
    const = lambda a: pl.BlockSpec(a.shape, lambda i: (0, 0))
    return pl.pallas_call(
        _router_kernel,
        grid=(T // tr,),
        in_specs=[pl.BlockSpec((tr, D_MODEL), lambda i: (i, 0)), const(nw), const(whi), const(wlo), const(tri)],
        out_specs=[
            pl.BlockSpec((tr, HALF_D), lambda i: (i, 0)),
            pl.BlockSpec((tr, LANE), lambda i: (i, 0)),
            pl.BlockSpec((1, ROUTE_ROWS, tr), lambda i: (i, 0, 0)),
            pl.BlockSpec((8, LANE), lambda i: (0, 0)),
        ],
        out_shape=[
            jax.ShapeDtypeStruct((T, HALF_D), jnp.uint32),
            jax.ShapeDtypeStruct((T, LANE), F32),
            jax.ShapeDtypeStruct((T // tr, ROUTE_ROWS, tr), F32),
            jax.ShapeDtypeStruct((8, LANE), F32),
        ],
        scratch_shapes=[pltpu.VMEM((8, LANE), F32)],
        compiler_params=_cparams(("arbitrary",)),
        name="router",
    )(x2, nw, whi, wlo, tri)


def _scatter_kernel(dest_ref, h_ref, xs_in, xs_hbm, sem, *, tt):
    del xs_in

    def row_copy(t, d_row):
        return pltpu.make_async_copy(h_ref.at[pl.ds(t, 1)], xs_hbm.at[pl.ds(d_row, 1)], sem)

    for t in range(tt):
        for j in range(TOP_K):
            row_copy(t, dest_ref[0, 0, j * tt + t]).start(priority=j)

    for _ in range(TOP_K):
        pltpu.make_async_copy(h_ref, xs_hbm.at[pl.ds(0, tt)], sem).wait()


def _scatter_rows(dest, h, n_rows, tt):
    T = h.shape[0]
    width = h.shape[1]
    xs0 = jnp.zeros((n_rows, width), h.dtype)
    assert dest.shape == (T // tt, TOP_K, tt)
    dest3 = dest.reshape(T // tt, 1, TOP_K * tt)
    return pl.pallas_call(
        functools.partial(_scatter_kernel, tt=tt),
        grid=(T // tt,),
        in_specs=[
            pl.BlockSpec((1, 1, 2 * tt), lambda i: (i, 0, 0), memory_space=pltpu.SMEM),
            pl.BlockSpec((tt, width), lambda i: (i, 0)),
            pl.BlockSpec(memory_space=pl.ANY),
        ],
        out_specs=pl.BlockSpec(memory_space=pl.ANY),
        out_shape=jax.ShapeDtypeStruct((n_rows, width), h.dtype),
        scratch_shapes=[pltpu.SemaphoreType.DMA(())],
        input_output_aliases={2: 0},
        compiler_params=_cparams(("arbitrary",)),
        name="moe_scatter",
    )(dest3, h, xs0)


def _moe_ffn_kernel(te_ref, na_ref, xs_ref, wg_ref, wu_ref, wd_ref, ys_ref, acc_scr):
    i = pl.program_id(0)
    f = pl.program_id(1)

    @pl.when(i < na_ref[0])
    def _():
        @pl.when(f == 0)
        def _():
            acc_scr[...] = jnp.zeros_like(acc_scr)

        acc_scr[...] += _swiglu_tile(_unpack_rows(xs_ref[...]), wg_ref[0].astype(BF16),
                                     wu_ref[0].astype(BF16), wd_ref[0].astype(BF16))

        @pl.when(f == pl.num_programs(1) - 1)
        def _():
            ys_ref[...] = acc_scr[...]

    @pl.when((i >= na_ref[0]) & (f == 0))
    def _():
        ys_ref[...] = jnp.zeros_like(ys_ref)


def _moe_ffn(tile_expert, n_active, xs, wg, wu, wd, tm, tf):
    n_rows = xs.shape[0]
    n_tiles = n_rows // tm
    nf = D_FF // tf

    def row_map(i, f, te, na):
        return (jnp.minimum(i, na[0] - 1), 0)

    def fcol(i, f, na):
        return jnp.where(i < na[0], f, nf - 1)

    grid_spec = pltpu.PrefetchScalarGridSpec(
        num_scalar_prefetch=2,
        grid=(n_tiles, nf),
        in_specs=[
            pl.BlockSpec((tm, HALF_D), row_map),
            pl.BlockSpec((1, D_MODEL, tf), lambda i, f, te, na: (te[i], 0, fcol(i, f, na))),
            pl.BlockSpec((1, D_MODEL, tf), lambda i, f, te, na: (te[i], 0, fcol(i, f, na))),
            pl.BlockSpec((1, tf, D_MODEL), lambda i, f, te, na: (te[i], fcol(i, f, na), 0)),
        ],
        out_specs=pl.BlockSpec((tm, D_MODEL), lambda i, f, te, na: (i, 0)),
        scratch_shapes=[pltpu.VMEM((tm, D_MODEL), F32)],
    )
    return pl.pallas_call(
        _moe_ffn_kernel,
        grid_spec=grid_spec,
        out_shape=jax.ShapeDtypeStruct((n_rows, D_MODEL), F32),
        compiler_params=_cparams(("arbitrary", "arbitrary")),
        name="moe_ffn",
    )(tile_expert, n_active, xs, wg, wu, wd)


def _combine_kernel(dest_ref, dnext_ref, x_ref, meta_ref, fw_ref, ys_hbm, o_ref, g_scr, sems, *, tc):
    i = pl.program_id(0)

    def gather(d_ref, half, s):
        for t in range(tc):
            for j in range(TOP_K):
                row = d_ref[0, 0, (half * TOP_K + j) * tc + t]
                pltpu.make_async_copy(ys_hbm.at[pl.ds(row, 1)], g_scr.at[s, j, pl.ds(t, 1)],
                                      sems.at[s]).start(priority=j)

    def wait(s):
        for j in range(TOP_K):
            pltpu.make_async_copy(ys_hbm.at[pl.ds(0, tc)], g_scr.at[s, j], sems.at[s]).wait()

    def combine(s):
        rows = pl.ds(s * tc, tc)
        meta = meta_ref[rows, :]
        w0 = meta[:, M_W0:M_W0 + 1]
        w1 = meta[:, M_W1:M_W1 + 1]
        x = x_ref[rows, :] + w0 * g_scr[s, 0] + w1 * g_scr[s, 1]
        ms = jnp.mean(x * x, axis=-1, keepdims=True)
        o_ref[rows, :] = x * lax.rsqrt(ms + RMS_EPS) * fw_ref[...]

    @pl.when(i == 0)
    def _():
        gather(dest_ref, 0, 0)

    gather(dest_ref, 1, 1)
    wait(0)
    combine(0)
    gather(dnext_ref, 0, 0)
    wait(1)
    combine(1)

    @pl.when(i == pl.num_programs(0) - 1)
    def _():
        wait(0)


def _combine(dest, x2, meta, fw, ys, tc):
    T = x2.shape[0]
    n = T // (2 * tc)
    assert dest.shape == (2 * n, TOP_K, tc)
    dest3 = dest.reshape(n, 1, 2 * TOP_K * tc)
    dspec = lambda imap: pl.BlockSpec((1, 1, 2 * TOP_K * tc), imap, memory_space=pltpu.SMEM)
    return pl.pallas_call(
        functools.partial(_combine_kernel, tc=tc),
        grid=(n,),
        in_specs=[
            dspec(lambda i: (i, 0, 0)),
            dspec(lambda i: (jnp.minimum(i + 1, n - 1), 0, 0)),
            pl.BlockSpec((2 * tc, D_MODEL), lambda i: (i, 0)),
            pl.BlockSpec((2 * tc, LANE), lambda i: (i, 0)),
            pl.BlockSpec((1, D_MODEL), lambda i: (0, 0)),
            pl.BlockSpec(memory_space=pl.ANY),
        ],
        out_specs=pl.BlockSpec((2 * tc, D_MODEL), lambda i: (i, 0)),
        out_shape=jax.ShapeDtypeStruct((T, D_MODEL), F32),
        scratch_shapes=[pltpu.VMEM((2, TOP_K, tc, D_MODEL), F32), pltpu.SemaphoreType.DMA((2,))],
        compiler_params=_cparams(("arbitrary",)),
        name="moe_combine",
    )(dest3, dest3, x2, meta, fw, ys)


def _moe_plan(route, counts, tm, n_tiles):
    cnt = counts[0, :N_EXPERTS].astype(jnp.int32)
    padded = ((cnt + tm - 1) // tm) * tm
    ends = jnp.cumsum(padded)
    offs = ends - padded
    e = route[:, M_E0:M_E1 + 1, :].astype(jnp.int32)
    pos = route[:, M_P0:M_P1 + 1, :].astype(jnp.int32)
    dest = offs[e] + pos
    starts = jnp.arange(n_tiles, dtype=jnp.int32) * tm
    n_active = (ends[-1] // tm).astype(jnp.int32)
    te = jnp.sum(starts[:, None] >= ends[None, :], axis=-1).astype(jnp.int32)
    last_e = jnp.sum(jnp.maximum(ends[-1] - tm, 0) >= ends).astype(jnp.int32)
    te = jnp.where(starts < ends[-1], te, last_e)
    return dest, te, n_active.reshape(1)


def _moe_layer(x_mid, nw, w_router, wg, wu, wd, final_w, tiles):
    T = x_mid.shape[0]
    tm = tiles["moe_tm"]
    n_tiles = (TOP_K * T) // tm + N_EXPERTS
    tr = tiles["moe_row_t"]
    h, meta, route, counts = _router(x_mid, nw, w_router, tr)
    dest, te, n_active = _moe_plan(route, counts, tm, n_tiles)
    xs = _scatter_rows(dest, h, n_tiles * tm, tr)
    ys = _moe_ffn(te, n_active, xs, wg, wu, wd, tm, tiles["moe_tf"])
    return _combine(dest, x_mid, meta, final_w, ys, tr)


def _tiles(S, T):
    return dict(
        inproj_tm=min(1024, T), inproj_tn=2816,
        attn_t=min(512, S // 2), ret_c=min(256, S),
        merge_tm=min(512, T), ffn_tm=min(1024, T), ffn_tf=512,
        moe_row_t=min(256, T // 2),
        moe_tm=min(1024, T), moe_tf=512,
    )


def kernel(x, norm_mix_w, w_in, fox_f_bias, ret_gn_w, mla_q_norm_w, mla_kv_norm_w, mla_w_uq, mla_w_uk, mla_w_uv, w_br_fox, w_br_ret, w_br_mla, w_out, norm_ffn_w, dense_w_gate, dense_w_up, dense_w_down, moe_w_router, moe_w_gate, moe_w_up, moe_w_down, final_norm_w):
    B, S, D = x.shape
    assert D == D_MODEL and w_in.shape[0] == DEPTH == 2
    T = B * S
    tl = _tiles(S, T)
    row = lambda a: a.reshape(1, -1).astype(F32)
    ret_tabs = _ret_tables(S, tl["ret_c"])
    mla_tabs = _mla_tables(S)
    x2 = x.reshape(T, D)
    w_all, w_ff = _build_inproj_weights(w_in)
    for l in range(DEPTH):
        proj, ff32 = _inproj(x2, row(norm_mix_w[l]), w_all[l], w_ff[l], tl["inproj_tm"], tl["inproj_tn"])
        fqt, fkx, fvt = _fox_prep(proj, ff32, fox_f_bias[l], B, S, tl["attn_t"])
        o_fox = _attention(fqt, fkx, fvt, B, S, tl["attn_t"], "fox_attn")
        o_ret = _retention(proj, row(ret_gn_w[l]), ret_tabs, B, S, tl["ret_c"])
        wq, wk, wv = _mla_weights(mla_w_uq[l], mla_w_uk[l], mla_w_uv[l])
        mqt, mkx, mvt = _mla_prep(proj, row(mla_q_norm_w[l]), row(mla_kv_norm_w[l]), wq, wk, wv,
                                  mla_tabs, B, S, tl["attn_t"])
        o_mla = _attention(mqt, mkx, mvt, B, S, tl["attn_t"], "mla_attn")
        x_mid, h2 = _merge(o_fox, o_ret, o_mla, proj, x2,
                           w_br_fox[l].astype(BF16), w_br_ret[l].astype(BF16), w_br_mla[l].astype(BF16),
                           w_out[l].astype(BF16), row(norm_ffn_w[l]), tl["merge_tm"])
        if l % 2 == 0:
            i = l // 2
            x2 = _dense_ffn(h2, x_mid, dense_w_gate[i].astype(BF16), dense_w_up[i].astype(BF16),
                            dense_w_down[i].astype(BF16), tl["ffn_tm"], tl["ffn_tf"])
        else:
            i = l // 2
            x2 = _moe_layer(x_mid, row(norm_ffn_w[l]), moe_w_router[i], moe_w_gate[i],
                            moe_w_up[i], moe_w_down[i], row(final_norm_w), tl)
    return x2.reshape(B, S, D)
```

```python
import functools

import numpy as np
import jax
import jax.numpy as jnp
from jax import lax
from jax.experimental import pallas as pl
from jax.experimental.pallas import tpu as pltpu

F32 = jnp.float32
BF16 = jnp.bfloat16

D_MODEL = 1024
DEPTH = 2
FOX_HEADS = 8
FOX_HEAD_DIM = 64
FOX_W = FOX_HEADS * FOX_HEAD_DIM
RET_HEADS = 8
RET_QK_DIM = 64
RET_V_DIM = 128
RET_QK_W = RET_HEADS * RET_QK_DIM
RET_V_W = RET_HEADS * RET_V_DIM
MLA_HEADS = 8
MLA_Q_RANK = 384
MLA_KV_RANK = 256
MLA_NOPE_DIM = 64
MLA_ROPE_DIM = 32
MLA_V_DIM = 64
MLA_QK_DIM = MLA_NOPE_DIM + MLA_ROPE_DIM
MLA_V_W = MLA_HEADS * MLA_V_DIM
N_BRANCHES = 3
ROPE_THETA = 10000.0
RMS_EPS = 1e-6
D_FF = 3584
N_EXPERTS = 8
TOP_K = 2
IN_SPLITS = (FOX_W, FOX_W, FOX_W, FOX_HEADS,
             RET_QK_W, RET_QK_W, RET_V_W, RET_V_W,
             MLA_Q_RANK, MLA_KV_RANK, MLA_ROPE_DIM,
             N_BRANCHES * D_MODEL)

LANE = 128
NEG = -1e30
VMEM_LIMIT = 56 * 1024 * 1024

U_FQ, U_FK, U_FV = 0, 4, 8
U_RQ, U_RK = 12, 16
U_KR, U_CQ = 20, 21
U_RV, U_RG, U_GL, U_CKV = 24, 32, 40, 64
N_PROJ = 66 * LANE
FOX_EXT_STRIDE = 8
PAIRS = 4
QK_W = 2 * LANE
FFN_COL_CHUNK = 256
ATTN_KEY_CHUNK = 256
ATTN_STRIP = 256
VT_ROWS = 80
LOG2E = 1.4426950408889634


def _cparams(sem, vmem=VMEM_LIMIT):
    return pltpu.CompilerParams(dimension_semantics=sem, vmem_limit_bytes=vmem)


def _sigmoid(x):
    return 1.0 / (1.0 + jnp.exp(-x))


def _lane_mask(shape, ranges):
    lane = lax.broadcasted_iota(jnp.int32, shape, len(shape) - 1)
    m = None
    for a, b in ranges:
        r = (lane >= a) & (lane < b)
        m = r if m is None else (m | r)
    return m


def _inproj_kernel(x_ref, nw_ref, w_ref, wff_ref, out_ref, ff_ref, h_scr):
    @pl.when(pl.program_id(1) == 0)
    def _():
        x = x_ref[...]
        ms = jnp.mean(x * x, axis=-1, keepdims=True)
        h = (x * lax.rsqrt(ms + RMS_EPS) * nw_ref[...]).astype(BF16)
        h_scr[...] = h
        ff_ref[...] = jnp.dot(h, wff_ref[...], preferred_element_type=F32)

    out_ref[...] = jnp.dot(h_scr[...], w_ref[...], preferred_element_type=F32).astype(out_ref.dtype)


def _inproj(x2, nw, w_all, w_ff, tm, tn):
    T = x2.shape[0]
    return pl.pallas_call(
        _inproj_kernel,
        grid=(T // tm, N_PROJ // tn),
        in_specs=[
            pl.BlockSpec((tm, D_MODEL), lambda i, j: (i, 0)),
            pl.BlockSpec((1, D_MODEL), lambda i, j: (0, 0)),
            pl.BlockSpec((D_MODEL, tn), lambda i, j: (0, j)),
            pl.BlockSpec((D_MODEL, LANE), lambda i, j: (0, 0)),
        ],
        out_specs=[
            pl.BlockSpec((tm, tn), lambda i, j: (i, j)),
            pl.BlockSpec((tm, LANE), lambda i, j: (i, 0)),
        ],
        out_shape=[
            jax.ShapeDtypeStruct((T, N_PROJ), BF16),
            jax.ShapeDtypeStruct((T, LANE), F32),
        ],
        scratch_shapes=[pltpu.VMEM((tm, D_MODEL), BF16)],
        compiler_params=_cparams(("parallel", "arbitrary")),
        name="inproj",
    )(x2, nw, w_all, w_ff)


def _rot_half_lanes(x, d):
    lane = lax.broadcasted_iota(jnp.int32, x.shape, 1)
    from_right = pltpu.roll(x, LANE - d // 2, 1)
    from_left = pltpu.roll(x, d // 2, 1)
    return jnp.where(lane % d < d // 2, -from_right, from_left)


def _wprep_kernel(w_ref, *out_refs):
    offs = [int(i) for i in np.cumsum((0,) + IN_SPLITS)]
    o_fq, o_fk, o_fv, o_ff, o_rq, o_rk, o_rv, o_rg, o_cq, o_ckv, o_kr, o_gl, _ = offs
    depth = w_ref.shape[0]
    lane = lax.broadcasted_iota(jnp.int32, (w_ref.shape[1], LANE), 1)
    rope_lanes = (lane >= MLA_NOPE_DIM) & (lane < MLA_QK_DIM)
    for l in range(depth):
        out_ref, ff_ref = out_refs[l], out_refs[depth + l]

        def put(unit, piece, out_ref=out_ref):
            out_ref[:, unit * LANE:unit * LANE + piece.shape[-1]] = piece.astype(out_ref.dtype)

        put(U_FQ, w_ref[l, :, o_fq:o_ff])
        ff_ref[...] = jnp.where(lane < FOX_HEADS, w_ref[l, :, o_ff:o_ff + LANE], 0.0).astype(ff_ref.dtype)
        put(U_RQ, w_ref[l, :, o_rq:o_rv])
        put(U_CKV, w_ref[l, :, o_ckv:o_kr])
        put(U_KR, jnp.where(rope_lanes, w_ref[l, :, o_kr - MLA_NOPE_DIM:o_kr - MLA_NOPE_DIM + LANE], 0.0))
        put(U_RV, w_ref[l, :, o_rv:o_rg])
        put(U_RG, w_ref[l, :, o_rg:o_cq])
        put(U_GL, w_ref[l, :, o_gl:o_gl + N_BRANCHES * D_MODEL])
        put(U_CQ, w_ref[l, :, o_cq:o_ckv])


def _build_inproj_weights(w_in, rows=128):
    depth, _, d_in = w_in.shape
    outs = pl.pallas_call(
        _wprep_kernel,
        grid=(D_MODEL // rows,),
        in_specs=[pl.BlockSpec((depth, rows, d_in), lambda r: (0, r, 0))],
        out_specs=([pl.BlockSpec((rows, N_PROJ), lambda r: (r, 0))] * depth
                   + [pl.BlockSpec((rows, LANE), lambda r: (r, 0))] * depth),
        out_shape=([jax.ShapeDtypeStruct((D_MODEL, N_PROJ), BF16)] * depth
                   + [jax.ShapeDtypeStruct((D_MODEL, LANE), BF16)] * depth),
        compiler_params=_cparams(("parallel",)),
        name="inproj_weight_prep",
    )(w_in)
    return outs[:depth], outs[depth:]


def _split3(x):
    hi = x.astype(BF16)
    r1 = x - hi.astype(F32)
    mid = r1.astype(BF16)
    lo = (r1 - mid.astype(F32)).astype(BF16)
    return hi, mid, lo


def _store_t(dst_ref, idx, x):
    dst_ref[idx] = x.T.astype(dst_ref.dtype)


def _store_vt_pair(vt_ref, p, v2):
    hd = LANE // 2
    vt = v2.T.astype(vt_ref.dtype)
    pad_rows = lax.broadcasted_iota(jnp.int32, (VT_ROWS - hd, vt.shape[1]), 0)
    ones_then_zeros = jnp.where(pad_rows == 0, 1.0, 0.0).astype(vt_ref.dtype)
    for j in range(2):
        base = (2 * p + j) * VT_ROWS
        vt_ref[0, 0, base:base + hd, :] = vt[j * hd:(j + 1) * hd]
        vt_ref[0, 0, base + hd:base + VT_ROWS, :] = ones_then_zeros


def _fox_prep_kernel(fq_ref, fk_ref, fv_ref, ff_ref, bias_ref, tri_ref, pq_ref, pk_ref, cq_ref, ck_ref,
                     qt_ref, kx_ref, vt_ref, carry_scr):
    @pl.when(pl.program_id(1) == 0)
    def _():
        carry_scr[...] = jnp.zeros_like(carry_scr)

    ts = ff_ref.shape[0]
    z = ff_ref[...] + bias_ref[...]
    logf = jnp.minimum(z, 0.0) - jnp.log(1.0 + jnp.exp(-jnp.abs(z)))
    tri = tri_ref[...]
    cum = carry_scr[0:1, :]
    for part in _split3(logf):
        cum = cum + jnp.dot(tri, part, preferred_element_type=F32)
    carry_scr[0:1, :] = cum[ts - 1:ts, :]

    cum3 = jnp.concatenate(_split3(cum * LOG2E), axis=-1)
    eq = cq_ref[...] + jnp.dot(cum3, pq_ref[...], preferred_element_type=F32)
    ek = ck_ref[...] + jnp.dot(cum3, pk_ref[...], preferred_element_type=F32)
    scale = FOX_HEAD_DIM ** -0.5 * LOG2E
    lane = lax.broadcasted_iota(jnp.int32, (ts, LANE), 1)
    for p in range(FOX_HEADS // 2):
        src = slice(p * LANE, (p + 1) * LANE)
        q2 = fq_ref[:, src].astype(F32) * scale
        k2 = fk_ref[:, src].astype(F32)
        for j in range(2):
            h = 2 * p + j
            dst = slice(h * LANE, (h + 1) * LANE)
            qh = q2 if j == 0 else pltpu.roll(q2, LANE // 2, 1)
            kh = k2 if j == 0 else pltpu.roll(k2, LANE // 2, 1)
            own = (lane >= FOX_HEAD_DIM + FOX_EXT_STRIDE * h) & (lane < FOX_HEAD_DIM + FOX_EXT_STRIDE * (h + 1))
            _store_t(qt_ref, (0, dst), jnp.where(lane < FOX_HEAD_DIM, qh, jnp.where(own, eq, 0.0)))
            kx_ref[:, dst] = jnp.where(lane < FOX_HEAD_DIM, kh, ek).astype(BF16)
        _store_vt_pair(vt_ref, p, fv_ref[:, src].astype(F32))


def _fox_prep_consts(ts):
    tri = np.tril(np.ones((ts, ts), np.float32))
    pq = np.zeros((3 * LANE, LANE), np.float32)
    pk = np.zeros((3 * LANE, LANE), np.float32)
    cq = np.zeros((1, LANE), np.float32)
    ck = np.zeros((1, LANE), np.float32)
    for h in range(FOX_HEADS):
        base = FOX_HEAD_DIM + FOX_EXT_STRIDE * h
        for j in range(3):
            pq[j * LANE + h, base + j] = 1.0
            ck[0, base + j] = 1.0
            cq[0, base + 3 + j] = 1.0
            pk[j * LANE + h, base + 3 + j] = -1.0
    return (jnp.asarray(tri, BF16), jnp.asarray(pq, BF16), jnp.asarray(pk, BF16),
            jnp.asarray(cq), jnp.asarray(ck))


def _attn_operand_specs(B, S, t):
    T = B * S
    ns = S // t
    specs = [
        pl.BlockSpec((1, PAIRS * QK_W, t), lambda b, s: (b, 0, s)),
        pl.BlockSpec((t, PAIRS * QK_W), lambda b, s: (b * ns + s, 0)),
        pl.BlockSpec((1, 1, 2 * PAIRS * VT_ROWS, t), lambda b, s: (b, s, 0, 0)),
    ]
    shapes = [
        jax.ShapeDtypeStruct((B, PAIRS * QK_W, S), BF16),
        jax.ShapeDtypeStruct((T, PAIRS * QK_W), BF16),
        jax.ShapeDtypeStruct((B, ns, 2 * PAIRS * VT_ROWS, t), BF16),
    ]
    return specs, shapes


def _fox_prep(proj, ff32, bias, B, S, ts):
    ns = S // ts
    tri, pq, pk, cq, ck = _fox_prep_consts(ts)
    bias128 = jnp.pad(bias.astype(F32), (0, LANE - FOX_HEADS)).reshape(1, LANE)
    const = lambda shape: pl.BlockSpec(shape, lambda b, s: (0, 0))
    out_specs, out_shape = _attn_operand_specs(B, S, ts)
    return pl.pallas_call(
        _fox_prep_kernel,
        grid=(B, ns),
        in_specs=[
            pl.BlockSpec((ts, FOX_W), lambda b, s: (b * ns + s, U_FQ // 4)),
            pl.BlockSpec((ts, FOX_W), lambda b, s: (b * ns + s, U_FK // 4)),
            pl.BlockSpec((ts, FOX_W), lambda b, s: (b * ns + s, U_FV // 4)),
            pl.BlockSpec((ts, LANE), lambda b, s: (b * ns + s, 0)),
            const((1, LANE)), const((ts, ts)), const(pq.shape), const(pk.shape), const(cq.shape), const(ck.shape),
        ],
        out_specs=out_specs,
        out_shape=out_shape,
        scratch_shapes=[pltpu.VMEM((8, LANE), F32)],
        compiler_params=_cparams(("parallel", "arbitrary")),
        name="fox_prep",
    )(proj, proj, proj, ff32, bias128, tri, pq, pk, cq, ck)


def _col_reduce(x, op, reduce_fn, parts=2):
    c = x.shape[0] // parts
    pieces = [x[i * c:(i + 1) * c] for i in range(parts)]
    while len(pieces) > 1:
        pieces = [op(pieces[i], pieces[i + 1]) for i in range(0, len(pieces), 2)]
    return reduce_fn(pieces[0], axis=0, keepdims=True)


def _attn_kernel(qt_ref, qn_ref, k_ref, vt_ref, o_ref, sa_scr, sb_scr, ma_scr, mb_scr, acc_scr, *, tk):
    qi = pl.program_id(2)
    tq = qt_ref.shape[-1]
    acc_scr[...] = jnp.zeros_like(acc_scr)

    def key_tile(kt):
        return k_ref[pl.ds(pl.multiple_of(kt * tk, tk), tk), :]

    strips = [(c, c + ATTN_STRIP) for c in range(0, tq, ATTN_STRIP)]

    def scores(kt, s_scr, m_scr):
        k = key_tile(kt)
        for h in range(2):
            for a, b in strips:
                s = jnp.dot(k[:, h * LANE:(h + 1) * LANE], qt_ref[0, h * LANE:(h + 1) * LANE, a:b],
                            preferred_element_type=F32)
                s_scr[h, :, a:b] = s
                m_scr[h, :, a:b] = _col_reduce(s, jnp.maximum, jnp.max)

    def update(kt, h, s, m_tile, m_prev, lo):
        m_new = jnp.maximum(m_prev, m_tile)
        alpha = jnp.exp2(m_prev - m_new)
        hi = lo + s.shape[-1]
        pv = None
        for r in range(0, tk, ATTN_KEY_CHUNK):
            p = jnp.exp2((s[r:r + ATTN_KEY_CHUNK] - m_new).astype(BF16))
            vt = vt_ref[0, kt, h * VT_ROWS:(h + 1) * VT_ROWS, r:r + ATTN_KEY_CHUNK]
            d = jnp.dot(vt, p, preferred_element_type=F32)
            pv = d if pv is None else pv + d
        acc_scr[h, :, lo:hi] = alpha * acc_scr[h, :, lo:hi] + pv
        return m_new

    def diag_update(kt, h, s_scr, m_prev, a, b, lo):
        s = s_scr[h, :, a:b]
        if a < lo + tk:
            key = lax.broadcasted_iota(jnp.int32, s.shape, 0)
            qry = lax.broadcasted_iota(jnp.int32, s.shape, 1) + (a - lo)
            s = jnp.where(key <= qry, s, NEG)
        return update(kt, h, s, _col_reduce(s, jnp.maximum, jnp.max), m_prev, a)

    def diagonal_tiles(carry, lookahead):
        k_last = key_tile(2 * qi + 1)
        if lookahead:
            k0 = key_tile(0)
        out = []
        for h in range(2):
            ms = []
            for a, b in strips:
                if a >= tk:
                    sb_scr[h, :, a:b] = jnp.dot(k_last[:, h * LANE:(h + 1) * LANE],
                                                qt_ref[0, h * LANE:(h + 1) * LANE, a:b],
                                                preferred_element_type=F32)
                ms.append(diag_update(2 * qi, h, sa_scr, carry[h][:, a:b], a, b, 0))
                if lookahead:
                    s = jnp.dot(k0[:, h * LANE:(h + 1) * LANE], qn_ref[0, h * LANE:(h + 1) * LANE, a:b],
                                preferred_element_type=F32)
                    sa_scr[h, :, a:b] = s
                    ma_scr[h, :, a:b] = _col_reduce(s, jnp.maximum, jnp.max)
            out.append(ms)
        for h in range(2):
            for i, (a, b) in enumerate(strips):
                if a >= tk:
                    diag_update(2 * qi + 1, h, sb_scr, out[h][i], a, b, tk)
        hd = LANE // 2
        ot = jnp.concatenate([acc_scr[h, :hd] / acc_scr[h, hd:hd + 1] for h in range(2)], axis=0)
        o_ref[...] = ot.T.astype(o_ref.dtype)

    def fused_step(kt, cur_s, cur_m, nxt_s, nxt_m, carry):
        k = key_tile(kt + 1)
        ms = [[], []]
        for a, b in strips:
            for h in range(2):
                s = jnp.dot(k[:, h * LANE:(h + 1) * LANE], qt_ref[0, h * LANE:(h + 1) * LANE, a:b],
                            preferred_element_type=F32)
                nxt_s[h, :, a:b] = s
                nxt_m[h, :, a:b] = _col_reduce(s, jnp.maximum, jnp.max)
                ms[h].append(update(kt, h, cur_s[h, :, a:b], cur_m[h, :, a:b], carry[h][:, a:b], a))
        return tuple(jnp.concatenate(m, axis=-1) for m in ms)

    def body(j, carry):
        kt = 2 * j
        carry = fused_step(kt, sa_scr, ma_scr, sb_scr, mb_scr, carry)
        return fused_step(kt + 1, sb_scr, mb_scr, sa_scr, ma_scr, carry)

    init = tuple(jnp.full((1, tq), NEG, F32) for _ in range(2))

    @pl.when(qi == 0)
    def _():
        scores(0, sa_scr, ma_scr)

    carry = lax.fori_loop(0, qi, body, init)
    last = pl.num_programs(2) - 1

    @pl.when(qi < last)
    def _():
        diagonal_tiles(carry, True)

    @pl.when(qi == last)
    def _():
        diagonal_tiles(carry, False)


def _attention(qt, kx, vt, B, S, tk, name):
    T = B * S
    tq = 2 * tk
    nq = S // tq
    kern = functools.partial(_attn_kernel, tk=tk)
    return pl.pallas_call(
        kern,
        grid=(B, PAIRS, nq),
        in_specs=[
            pl.BlockSpec((1, QK_W, tq), lambda b, p, i: (b, p, i)),
            pl.BlockSpec((1, QK_W, tq), lambda b, p, i: (b, p, jnp.minimum(i + 1, nq - 1))),
            pl.BlockSpec((S, QK_W), lambda b, p, i: (b, p)),
            pl.BlockSpec((1, S // tk, 2 * VT_ROWS, tk), lambda b, p, i: (b, 0, p, 0)),
        ],
        out_specs=pl.BlockSpec((tq, LANE), lambda b, p, i: (b * nq + i, p)),
        out_shape=jax.ShapeDtypeStruct((T, PAIRS * LANE), BF16),
        scratch_shapes=[
            pltpu.VMEM((2, tk, tq), F32),
            pltpu.VMEM((2, tk, tq), F32),
            pltpu.VMEM((2, 1, tq), F32),
            pltpu.VMEM((2, 1, tq), F32),
            pltpu.VMEM((2, VT_ROWS, tq), F32),
        ],
        compiler_params=_cparams(("parallel", "parallel", "arbitrary")),
        name=name,
    )(qt, qt, kx, vt)


def _mla_prep_kernel(cq_ref, ckv_ref, kr_ref, qnw_ref, kvnw_ref, wq_ref, wk_ref, wv_ref,
                     cq_tab, sq_tab, ck_tab, sk_tab, qt_ref, k_ref, vt_ref):
    def norm(x_ref, w_ref):
        x = x_ref[...].astype(F32)
        ms = jnp.mean(x * x, axis=-1, keepdims=True)
        return (x * lax.rsqrt(ms + RMS_EPS) * w_ref[...]).astype(BF16)

    c_q = norm(cq_ref, qnw_ref)
    c_kv = norm(ckv_ref, kvnw_ref)
    tile8 = lambda a: jnp.concatenate([a] * MLA_HEADS, axis=-1)
    q_raw = jnp.dot(c_q, wq_ref[...], preferred_element_type=F32)
    for c in range(MLA_HEADS):
        qs = q_raw[:, c * LANE:(c + 1) * LANE]
        q = qs * cq_tab[...] + _rot_half_lanes(qs, MLA_ROPE_DIM) * sq_tab[...]
        _store_t(qt_ref, (0, slice(c * LANE, (c + 1) * LANE)), q)
    kr = kr_ref[...].astype(F32)
    k_rope = kr * ck_tab[...] + _rot_half_lanes(kr, MLA_ROPE_DIM) * sk_tab[...]
    k = jnp.dot(c_kv, wk_ref[...], preferred_element_type=F32) + tile8(k_rope)
    k_ref[...] = k.astype(BF16)
    v = jnp.dot(c_kv, wv_ref[...], preferred_element_type=F32)
    for p in range(PAIRS):
        _store_vt_pair(vt_ref, p, v[:, p * LANE:(p + 1) * LANE])


def _mla_weights(w_uq, w_uk, w_uv):
    q3 = w_uq.reshape(MLA_Q_RANK, MLA_HEADS, MLA_QK_DIM)
    nope, ropep = q3[..., :MLA_NOPE_DIM], q3[..., MLA_NOPE_DIM:]
    z32 = jnp.zeros((MLA_Q_RANK, MLA_HEADS, LANE - MLA_QK_DIM), F32)
    wq = jnp.concatenate([nope, ropep, z32], axis=-1).reshape(MLA_Q_RANK, MLA_HEADS * LANE)
    k3 = w_uk.reshape(MLA_KV_RANK, MLA_HEADS, MLA_NOPE_DIM)
    wk = jnp.concatenate([k3, jnp.zeros_like(k3)], axis=-1).reshape(MLA_KV_RANK, MLA_HEADS * LANE)
    return wq.astype(BF16), wk.astype(BF16), w_uv.astype(BF16)


def _rope_inv_freq(d):
    return (np.float32(ROPE_THETA) ** (-np.arange(0, d, 2, dtype=np.float32) / np.float32(d))).astype(np.float32)


def _mla_tables(S):
    inv = _rope_inv_freq(MLA_ROPE_DIM)
    inv128 = np.concatenate([np.zeros(MLA_NOPE_DIM, np.float32), inv, inv, np.zeros(LANE - MLA_QK_DIM, np.float32)])
    keep = (np.arange(LANE) < MLA_QK_DIM).astype(np.float32)
    ang = jnp.arange(S, dtype=F32)[:, None] * jnp.asarray(inv128)[None, :]
    ctab = jnp.cos(ang) * jnp.asarray(keep)[None, :]
    stab = jnp.sin(ang)
    scale = MLA_QK_DIM ** -0.5 * LOG2E
    return ctab * scale, stab * scale, ctab, stab


def _mla_prep(proj, qnw, kvnw, wq, wk, wv, tabs, B, S, tm):
    ns = S // tm
    HW = MLA_HEADS * LANE
    const = lambda shape: pl.BlockSpec(shape, lambda b, s: (0, 0))
    tab = pl.BlockSpec((tm, LANE), lambda b, s: (s, 0))
    out_specs, out_shape = _attn_operand_specs(B, S, tm)
    return pl.pallas_call(
        _mla_prep_kernel,
        grid=(B, ns),
        in_specs=[
            pl.BlockSpec((tm, MLA_Q_RANK), lambda b, s: (b * ns + s, U_CQ // 3)),
            pl.BlockSpec((tm, MLA_KV_RANK), lambda b, s: (b * ns + s, U_CKV // 2)),
            pl.BlockSpec((tm, LANE), lambda b, s: (b * ns + s, U_KR)),
            const((1, MLA_Q_RANK)), const((1, MLA_KV_RANK)),
            const((MLA_Q_RANK, HW)), const((MLA_KV_RANK, HW)),
            const((MLA_KV_RANK, MLA_V_W)),
            tab, tab, tab, tab,
        ],
        out_specs=out_specs,
        out_shape=out_shape,
        compiler_params=_cparams(("parallel", "parallel")),
        name="mla_prep",
    )(proj, proj, proj, qnw, kvnw, wq, wk, wv, *tabs)


def _ret_kernel(rq_ref, rk_ref, rv_ref, rg_ref, cos_ref, sin_ref, perm_ref,
                din_ref, xi_ref, zeta_ref, gch_ref, gnw_ref, o_ref, r_scr):
    @pl.when(pl.program_id(0) == 0)
    def _():
        r_scr[...] = jnp.zeros_like(r_scr)

    tile4 = lambda a: jnp.concatenate([a] * (RET_QK_W // LANE), axis=-1)
    cos4 = tile4(cos_ref[...])
    sin4 = tile4(sin_ref[...])
    for bi in range(rq_ref.shape[0]):
        def rope(x_ref):
            x = x_ref[bi]
            rot = jnp.dot(x, perm_ref[...], preferred_element_type=F32)
            return x.astype(F32) * cos4 + rot * sin4

        q = rope(rq_ref)
        k = rope(rk_ref) * (RET_QK_DIM ** -0.5)
        for p in range(RET_HEADS // 2):
            q2 = q[:, p * LANE:(p + 1) * LANE]
            k2 = k[:, p * LANE:(p + 1) * LANE]
            k2b = k2.astype(BF16)
            kz2 = (k2 * zeta_ref[p]).astype(BF16)
            r2 = r_scr[bi, p]
            r2b = r2.astype(BF16)
            new_r = gch_ref[p] * r2
            for j in range(2):
                h = 2 * p + j
                half = ((j * RET_QK_DIM, (j + 1) * RET_QK_DIM),)
                qm = jnp.where(_lane_mask(q2.shape, half), q2, 0.0).astype(BF16)
                inner = lax.dot_general(qm, k2b, (((1,), (1,)), ((), ())), preferred_element_type=F32)
                inner = inner * din_ref[h]
                vh = rv_ref[bi, :, h * LANE:(h + 1) * LANE]
                o = (jnp.dot(inner.astype(BF16), vh, preferred_element_type=F32)
                     + jnp.dot(qm, r2b, preferred_element_type=F32) * xi_ref[h])
                upd = lax.dot_general(kz2, vh, (((0,), (0,)), ((), ())), preferred_element_type=F32)
                rows = lax.broadcasted_iota(jnp.int32, upd.shape, 0)
                new_r = new_r + jnp.where((rows >= half[0][0]) & (rows < half[0][1]), upd, 0.0)
                mu = jnp.mean(o, axis=-1, keepdims=True)
                d = o - mu
                var = jnp.mean(d * d, axis=-1, keepdims=True)
                on = d * lax.rsqrt(var + RMS_EPS) * gnw_ref[:, h * LANE:(h + 1) * LANE]
                g = rg_ref[bi, :, h * LANE:(h + 1) * LANE].astype(F32)
                o_ref[bi, :, h * LANE:(h + 1) * LANE] = (g * _sigmoid(g) * on).astype(o_ref.dtype)
            r_scr[bi, p] = new_r


def _ret_tables(S, C):
    inv128 = np.tile(_rope_inv_freq(RET_QK_DIM), 2 * LANE // RET_QK_DIM)
    ang = jnp.arange(S, dtype=F32)[:, None] * jnp.asarray(inv128)[None, :]
    cos = jnp.cos(ang)
    sin = jnp.sin(ang)
    gammas = 1.0 - 2.0 ** (-5.0 - jnp.arange(RET_HEADS, dtype=F32))
    log_g = jnp.log(gammas)
    j = jnp.arange(C, dtype=F32)
    diff = j[:, None] - j[None, :]
    din = jnp.where(diff[None] >= 0, jnp.exp(jnp.maximum(diff, 0.0)[None] * log_g[:, None, None]), 0.0)
    xi = jnp.exp((j[None, :] + 1.0) * log_g[:, None])
    zeta = jnp.exp((C - 1.0 - j[None, :]) * log_g[:, None])
    gch = jnp.exp(C * log_g)
    xi_t = jnp.broadcast_to(xi[:, :, None], (RET_HEADS, C, LANE))
    zeta_p = jnp.repeat(zeta.reshape(RET_HEADS // 2, 2, C).transpose(0, 2, 1), RET_QK_DIM, axis=-1)
    gch_p = jnp.broadcast_to(jnp.repeat(gch.reshape(RET_HEADS // 2, 2), RET_QK_DIM, axis=-1)[:, :, None],
                             (RET_HEADS // 2, LANE, LANE))
    return cos, sin, din, xi_t, zeta_p, gch_p


def _retention(proj, gnw, tabs, B, S, C):
    nc = S // C
    cos, sin, din, xi, zeta, gch = tabs
    proj3 = proj.reshape(B, S, N_PROJ)
    perm = np.zeros((RET_QK_W, RET_QK_W), np.float32)
    half = RET_QK_DIM // 2
    for h in range(RET_HEADS):
        for j in range(half):
            perm[h * RET_QK_DIM + half + j, h * RET_QK_DIM + j] = -1.0
            perm[h * RET_QK_DIM + j, h * RET_QK_DIM + half + j] = 1.0
    perm = jnp.asarray(perm, BF16)
    blk = lambda w, unit: pl.BlockSpec((B, C, w), lambda c: (0, c, unit * LANE // w))
    full = lambda a: pl.BlockSpec(a.shape, lambda c: (0,) * a.ndim)
    out = pl.pallas_call(
        _ret_kernel,
        grid=(nc,),
        in_specs=[
            blk(RET_QK_W, U_RQ), blk(RET_QK_W, U_RK),
            blk(RET_V_W, U_RV), blk(RET_V_W, U_RG),
            pl.BlockSpec((C, LANE), lambda c: (c, 0)),
            pl.BlockSpec((C, LANE), lambda c: (c, 0)),
            full(perm), full(din), full(xi), full(zeta), full(gch), full(gnw),
        ],
        out_specs=pl.BlockSpec((B, C, RET_V_W), lambda c: (0, c, 0)),
        out_shape=jax.ShapeDtypeStruct((B, S, RET_V_W), BF16),
        scratch_shapes=[pltpu.VMEM((B, RET_HEADS // 2, LANE, LANE), F32)],
        compiler_params=_cparams(("arbitrary",)),
        name="retention",
    )(proj3, proj3, proj3, proj3, cos, sin, perm, din, xi, zeta, gch, gnw)
    return out.reshape(B * S, RET_V_W)


def _merge_kernel(of_ref, or_ref, om_ref, g0_ref, g1_ref, g2_ref, x_ref,
                  wf_ref, wr_ref, wm_ref, wo_ref, nw_ref, xo_ref, h_ref):
    merged = (_sigmoid(g0_ref[...].astype(F32)) * jnp.dot(of_ref[...], wf_ref[...], preferred_element_type=F32)
              + _sigmoid(g1_ref[...].astype(F32)) * jnp.dot(or_ref[...], wr_ref[...], preferred_element_type=F32)
              + _sigmoid(g2_ref[...].astype(F32)) * jnp.dot(om_ref[...], wm_ref[...], preferred_element_type=F32))
    x = x_ref[...] + jnp.dot(merged.astype(BF16), wo_ref[...], preferred_element_type=F32)
    xo_ref[...] = x
    ms = jnp.mean(x * x, axis=-1, keepdims=True)
    h_ref[...] = (x * lax.rsqrt(ms + RMS_EPS) * nw_ref[...]).astype(h_ref.dtype)


def _merge(o_fox, o_ret, o_mla, proj, x2, wf, wr, wm, wo, nw, tm):
    T = x2.shape[0]
    row = lambda w, cb=0: pl.BlockSpec((tm, w), lambda i: (i, cb))
    const = lambda a: pl.BlockSpec(a.shape, lambda i: (0, 0))
    g_unit = U_GL // 8
    return pl.pallas_call(
        _merge_kernel,
        grid=(T // tm,),
        in_specs=[
            row(FOX_W), row(RET_V_W), row(MLA_V_W),
            row(D_MODEL, g_unit), row(D_MODEL, g_unit + 1), row(D_MODEL, g_unit + 2),
            row(D_MODEL),
            const(wf), const(wr), const(wm), const(wo), const(nw),
        ],
        out_specs=[row(D_MODEL), row(D_MODEL)],
        out_shape=[jax.ShapeDtypeStruct((T, D_MODEL), F32), jax.ShapeDtypeStruct((T, D_MODEL), BF16)],
        compiler_params=_cparams(("parallel",)),
        name="merge",
    )(o_fox, o_ret, o_mla, proj, proj, proj, x2, wf, wr, wm, wo, nw)


def _swiglu_tile(h, wg, wu, wd):
    out = None
    for c in range(0, wg.shape[1], FFN_COL_CHUNK):
        a = jnp.dot(h, wg[:, c:c + FFN_COL_CHUNK], preferred_element_type=F32)
        b = jnp.dot(h, wu[:, c:c + FFN_COL_CHUNK], preferred_element_type=F32)
        d = jnp.dot((a * _sigmoid(a) * b).astype(BF16), wd[c:c + FFN_COL_CHUNK], preferred_element_type=F32)
        out = d if out is None else out + d
    return out


def _dense_ffn_kernel(h_ref, wg_ref, wu_ref, wd_ref, x_ref, o_ref, acc_scr):
    f = pl.program_id(1)

    @pl.when(f == 0)
    def _():
        acc_scr[...] = jnp.zeros_like(acc_scr)

    acc_scr[...] += _swiglu_tile(h_ref[...], wg_ref[...], wu_ref[...], wd_ref[...])

    @pl.when(f == pl.num_programs(1) - 1)
    def _():
        o_ref[...] = x_ref[...] + acc_scr[...]


def _dense_ffn(h, x2, wg, wu, wd, tm, tf):
    T = x2.shape[0]
    return pl.pallas_call(
        _dense_ffn_kernel,
        grid=(T // tm, D_FF // tf),
        in_specs=[
            pl.BlockSpec((tm, D_MODEL), lambda i, f: (i, 0)),
            pl.BlockSpec((D_MODEL, tf), lambda i, f: (0, f)),
            pl.BlockSpec((D_MODEL, tf), lambda i, f: (0, f)),
            pl.BlockSpec((tf, D_MODEL), lambda i, f: (f, 0)),
            pl.BlockSpec((tm, D_MODEL), lambda i, f: (i, 0)),
        ],
        out_specs=pl.BlockSpec((tm, D_MODEL), lambda i, f: (i, 0)),
        out_shape=jax.ShapeDtypeStruct((T, D_MODEL), F32),
        scratch_shapes=[pltpu.VMEM((tm, D_MODEL), F32)],
        compiler_params=_cparams(("parallel", "arbitrary")),
        name="dense_ffn",
    )(h, wg, wu, wd, x2)


M_E0, M_E1, M_P0, M_P1, M_W0, M_W1 = 0, 1, 2, 3, 4, 5
ROUTE_ROWS = 8


def _split2(x):
    hi = x.astype(BF16)
    lo = (x - hi.astype(F32)).astype(BF16)
    return hi, lo


HALF_D = D_MODEL // 2


def _pack_rows(h):
    bits = pltpu.bitcast(h.astype(BF16).astype(F32), jnp.uint32)
    return (bits[:, :HALF_D] & jnp.uint32(0xFFFF0000)) | (bits[:, HALF_D:] >> 16)


def _unpack_rows(u):
    left = pltpu.bitcast(u & jnp.uint32(0xFFFF0000), F32)
    right = pltpu.bitcast(u << 16, F32)
    return jnp.concatenate([left, right], axis=-1).astype(BF16)


def _router_kernel(x_ref, nw_ref, whi_ref, wlo_ref, tri_ref, h_ref, meta_ref, route_ref, cnt_ref, carry_scr):
    @pl.when(pl.program_id(0) == 0)
    def _():
        carry_scr[...] = jnp.zeros_like(carry_scr)

    x = x_ref[...]
    ms = jnp.mean(x * x, axis=-1, keepdims=True)
    h = x * lax.rsqrt(ms + RMS_EPS) * nw_ref[...]
    h_ref[...] = _pack_rows(h)
    hhi, hlo = _split2(h)
    logits = (jnp.dot(hhi, whi_ref[...], preferred_element_type=F32)
              + jnp.dot(hhi, wlo_ref[...], preferred_element_type=F32)
              + jnp.dot(hlo, whi_ref[...], preferred_element_type=F32))
    lane = lax.broadcasted_iota(jnp.int32, logits.shape, 1)
    lanef = lane.astype(F32)
    logits = jnp.where(lane < N_EXPERTS, logits, NEG)
    m0 = jnp.max(logits, axis=-1, keepdims=True)
    e0 = jnp.min(jnp.where(logits == m0, lanef, float(LANE)), axis=-1, keepdims=True)
    oh0 = lanef == e0
    rest = jnp.where(oh0, NEG, logits)
    m1 = jnp.max(rest, axis=-1, keepdims=True)
    e1 = jnp.min(jnp.where(rest == m1, lanef, float(LANE)), axis=-1, keepdims=True)
    oh1 = lanef == e1
    z = jnp.exp(m1 - m0)
    w0 = 1.0 / (1.0 + z)
    w1 = z / (1.0 + z)
    both = jnp.where(oh0 | oh1, 1.0, 0.0)
    before = carry_scr[0:1, :] + jnp.dot(tri_ref[...], both.astype(BF16), preferred_element_type=F32)
    p0 = jnp.sum(jnp.where(oh0, before, 0.0), axis=-1, keepdims=True)
    p1 = jnp.sum(jnp.where(oh1, before, 0.0), axis=-1, keepdims=True)
    total = carry_scr[0:1, :] + jnp.sum(both, axis=0, keepdims=True)
    carry_scr[0:1, :] = total
    cnt_ref[...] = jnp.broadcast_to(total, cnt_ref.shape)
    meta = jnp.zeros(logits.shape, F32)
    for idx, val in ((M_E0, e0), (M_E1, e1), (M_P0, p0), (M_P1, p1), (M_W0, w0), (M_W1, w1)):
        meta = jnp.where(lane == idx, val, meta)
    meta_ref[...] = meta
    route_ref[0] = meta.T[:ROUTE_ROWS, :]


def _router(x2, nw, w_router, tr):
    T = x2.shape[0]
    wr = jnp.pad(w_router.astype(F32), ((0, 0), (0, LANE - N_EXPERTS)))
    whi = wr.astype(BF16)
    wlo = (wr - whi.astype(F32)).astype(BF16)
    tri = jnp.asarray(np.tril(np.ones((tr, tr), np.float32), -1), BF16)
    const = lambda a: pl.BlockSpec(a.shape, lambda i: (0, 0))
    return pl.pallas_call(
        _router_kernel,
        grid=(T // tr,),
        in_specs=[pl.BlockSpec((tr, D_MODEL), lambda i: (i, 0)), const(nw), const(whi), const(wlo), const(tri)],
        out_specs=[
            pl.BlockSpec((tr, HALF_D), lambda i: (i, 0)),
            pl.BlockSpec((tr, LANE), lambda i: (i, 0)),
            pl.BlockSpec((1, ROUTE_ROWS, tr), lambda i: (i, 0, 0)),
            pl.BlockSpec((8, LANE), lambda i: (0, 0)),
        ],
        out_shape=[
            jax.ShapeDtypeStruct((T, HALF_D), jnp.uint32),
            jax.ShapeDtypeStruct((T, LANE), F32),
            jax.ShapeDtypeStruct((T // tr, ROUTE_ROWS, tr), F32),
            jax.ShapeDtypeStruct((8, LANE), F32),
        ],
        scratch_shapes=[pltpu.VMEM((8, LANE), F32)],
        compiler_params=_cparams(("arbitrary",)),
        name="router",
    )(x2, nw, whi, wlo, tri)


def _scatter_kernel(dest_ref, h_ref, xs_in, xs_hbm, sem, *, tt):
    del xs_in

    def row_copy(t, d_row):
        return pltpu.make_async_copy(h_ref.at[pl.ds(t, 1)], xs_hbm.at[pl.ds(d_row, 1)], sem)

    for t in range(tt):
        for j in range(TOP_K):
            row_copy(t, dest_ref[0, 0, j * tt + t]).start(priority=j)

    for _ in range(TOP_K):
        pltpu.make_async_copy(h_ref, xs_hbm.at[pl.ds(0, tt)], sem).wait()


def _scatter_rows(dest, h, n_rows, tt):
    T = h.shape[0]
    width = h.shape[1]
    xs0 = jnp.zeros((n_rows, width), h.dtype)
    assert dest.shape == (T // tt, TOP_K, tt)
    dest3 = dest.reshape(T // tt, 1, TOP_K * tt)
    return pl.pallas_call(
        functools.partial(_scatter_kernel, tt=tt),
        grid=(T // tt,),
        in_specs=[
            pl.BlockSpec((1, 1, 2 * tt), lambda i: (i, 0, 0), memory_space=pltpu.SMEM),
            pl.BlockSpec((tt, width), lambda i: (i, 0)),
            pl.BlockSpec(memory_space=pl.ANY),
        ],
        out_specs=pl.BlockSpec(memory_space=pl.ANY),
        out_shape=jax.ShapeDtypeStruct((n_rows, width), h.dtype),
        scratch_shapes=[pltpu.SemaphoreType.DMA(())],
        input_output_aliases={2: 0},
        compiler_params=_cparams(("arbitrary",)),
        name="moe_scatter",
    )(dest3, h, xs0)


def _moe_ffn_kernel(te_ref, na_ref, xs_ref, wg_ref, wu_ref, wd_ref, ys_ref, acc_scr):
    i = pl.program_id(0)
    f = pl.program_id(1)

    @pl.when(i < na_ref[0])
    def _():
        @pl.when(f == 0)
        def _():
            acc_scr[...] = jnp.zeros_like(acc_scr)

        acc_scr[...] += _swiglu_tile(_unpack_rows(xs_ref[...]), wg_ref[0].astype(BF16),
                                     wu_ref[0].astype(BF16), wd_ref[0].astype(BF16))

        @pl.when(f == pl.num_programs(1) - 1)
        def _():
            ys_ref[...] = acc_scr[...]

    @pl.when((i >= na_ref[0]) & (f == 0))
    def _():
        ys_ref[...] = jnp.zeros_like(ys_ref)


def _moe_ffn(tile_expert, n_active, xs, wg, wu, wd, tm, tf):
    n_rows = xs.shape[0]
    n_tiles = n_rows // tm
    nf = D_FF // tf

    def row_map(i, f, te, na):
        return (jnp.minimum(i, na[0] - 1), 0)

    def fcol(i, f, na):
        return jnp.where(i < na[0], f, nf - 1)

    grid_spec = pltpu.PrefetchScalarGridSpec(
        num_scalar_prefetch=2,
        grid=(n_tiles, nf),
        in_specs=[
            pl.BlockSpec((tm, HALF_D), row_map),
            pl.BlockSpec((1, D_MODEL, tf), lambda i, f, te, na: (te[i], 0, fcol(i, f, na))),
            pl.BlockSpec((1, D_MODEL, tf), lambda i, f, te, na: (te[i], 0, fcol(i, f, na))),
            pl.BlockSpec((1, tf, D_MODEL), lambda i, f, te, na: (te[i], fcol(i, f, na), 0)),
        ],
        out_specs=pl.BlockSpec((tm, D_MODEL), lambda i, f, te, na: (i, 0)),
        scratch_shapes=[pltpu.VMEM((tm, D_MODEL), F32)],
    )
    return pl.pallas_call(
        _moe_ffn_kernel,
        grid_spec=grid_spec,
        out_shape=jax.ShapeDtypeStruct((n_rows, D_MODEL), F32),
        compiler_params=_cparams(("arbitrary", "arbitrary")),
        name="moe_ffn",
    )(tile_expert, n_active, xs, wg, wu, wd)


def _combine_kernel(dest_ref, dnext_ref, x_ref, meta_ref, fw_ref, ys_hbm, o_ref, g_scr, sems, *, tc):
    i = pl.program_id(0)

    def gather(d_ref, half, s):
        for t in range(tc):
            for j in range(TOP_K):
                row = d_ref[0, 0, (half * TOP_K + j) * tc + t]
                pltpu.make_async_copy(ys_hbm.at[pl.ds(row, 1)], g_scr.at[s, j, pl.ds(t, 1)],
                                      sems.at[s]).start(priority=j)

    def wait(s):
        for j in range(TOP_K):
            pltpu.make_async_copy(ys_hbm.at[pl.ds(0, tc)], g_scr.at[s, j], sems.at[s]).wait()

    def combine(s):
        rows = pl.ds(s * tc, tc)
        meta = meta_ref[rows, :]
        w0 = meta[:, M_W0:M_W0 + 1]
        w1 = meta[:, M_W1:M_W1 + 1]
        x = x_ref[rows, :] + w0 * g_scr[s, 0] + w1 * g_scr[s, 1]
        ms = jnp.mean(x * x, axis=-1, keepdims=True)
        o_ref[rows, :] = x * lax.rsqrt(ms + RMS_EPS) * fw_ref[...]

    @pl.when(i == 0)
    def _():
        gather(dest_ref, 0, 0)

    gather(dest_ref, 1, 1)
    wait(0)
    combine(0)
    gather(dnext_ref, 0, 0)
    wait(1)
    combine(1)

    @pl.when(i == pl.num_programs(0) - 1)
    def _():
        wait(0)


def _combine(dest, x2, meta, fw, ys, tc):
    T = x2.shape[0]
    n = T // (2 * tc)
    assert dest.shape == (2 * n, TOP_K, tc)
    dest3 = dest.reshape(n, 1, 2 * TOP_K * tc)
    dspec = lambda imap: pl.BlockSpec((1, 1, 2 * TOP_K * tc), imap, memory_space=pltpu.SMEM)
    return pl.pallas_call(
        functools.partial(_combine_kernel, tc=tc),
        grid=(n,),
        in_specs=[
            dspec(lambda i: (i, 0, 0)),
            dspec(lambda i: (jnp.minimum(i + 1, n - 1), 0, 0)),
            pl.BlockSpec((2 * tc, D_MODEL), lambda i: (i, 0)),
            pl.BlockSpec((2 * tc, LANE), lambda i: (i, 0)),
            pl.BlockSpec((1, D_MODEL), lambda i: (0, 0)),
            pl.BlockSpec(memory_space=pl.ANY),
        ],
        out_specs=pl.BlockSpec((2 * tc, D_MODEL), lambda i: (i, 0)),
        out_shape=jax.ShapeDtypeStruct((T, D_MODEL), F32),
        scratch_shapes=[pltpu.VMEM((2, TOP_K, tc, D_MODEL), F32), pltpu.SemaphoreType.DMA((2,))],
        compiler_params=_cparams(("arbitrary",)),
        name="moe_combine",
    )(dest3, dest3, x2, meta, fw, ys)


def _moe_plan(route, counts, tm, n_tiles):
    cnt = counts[0, :N_EXPERTS].astype(jnp.int32)
    padded = ((cnt + tm - 1) // tm) * tm
    ends = jnp.cumsum(padded)
    offs = ends - padded
    e = route[:, M_E0:M_E1 + 1, :].astype(jnp.int32)
    pos = route[:, M_P0:M_P1 + 1, :].astype(jnp.int32)
    dest = offs[e] + pos
    starts = jnp.arange(n_tiles, dtype=jnp.int32) * tm
    n_active = (ends[-1] // tm).astype(jnp.int32)
    te = jnp.sum(starts[:, None] >= ends[None, :], axis=-1).astype(jnp.int32)
    last_e = jnp.sum(jnp.maximum(ends[-1] - tm, 0) >= ends).astype(jnp.int32)
    te = jnp.where(starts < ends[-1], te, last_e)
    return dest, te, n_active.reshape(1)


def _moe_layer(x_mid, nw, w_router, wg, wu, wd, final_w, tiles):
    T = x_mid.shape[0]
    tm = tiles["moe_tm"]
    n_tiles = (TOP_K * T) // tm + N_EXPERTS
    tr = tiles["moe_row_t"]
    h, meta, route, counts = _router(x_mid, nw, w_router, tr)
    dest, te, n_active = _moe_plan(route, counts, tm, n_tiles)
    xs = _scatter_rows(dest, h, n_tiles * tm, tr)
    ys = _moe_ffn(te, n_active, xs, wg, wu, wd, tm, tiles["moe_tf"])
    return _combine(dest, x_mid, meta, final_w, ys, tr)


def _tiles(S, T):
    return dict(
        inproj_tm=min(1024, T), inproj_tn=2816,
        attn_t=min(512, S // 2), ret_c=min(256, S),
        merge_tm=min(512, T), ffn_tm=min(1024, T), ffn_tf=512,
        moe_row_t=min(256, T // 2),
        moe_tm=min(1024, T), moe_tf=512,
    )


def kernel(x, norm_mix_w, w_in, fox_f_bias, ret_gn_w, mla_q_norm_w, mla_kv_norm_w, mla_w_uq, mla_w_uk, mla_w_uv, w_br_fox, w_br_ret, w_br_mla, w_out, norm_ffn_w, dense_w_gate, dense_w_up, dense_w_down, moe_w_router, moe_w_gate, moe_w_up, moe_w_down, final_norm_w):
    B, S, D = x.shape
    assert D == D_MODEL and w_in.shape[0] == DEPTH == 2
    T = B * S
    tl = _tiles(S, T)
    row = lambda a: a.reshape(1, -1).astype(F32)
    ret_tabs = _ret_tables(S, tl["ret_c"])
    mla_tabs = _mla_tables(S)
    x2 = x.reshape(T, D)
    w_all, w_ff = _build_inproj_weights(w_in)
    for l in range(DEPTH):
        proj, ff32 = _inproj(x2, row(norm_mix_w[l]), w_all[l], w_ff[l], tl["inproj_tm"], tl["inproj_tn"])
        fqt, fkx, fvt = _fox_prep(proj, ff32, fox_f_bias[l], B, S, tl["attn_t"])
        o_fox = _attention(fqt, fkx, fvt, B, S, tl["attn_t"], "fox_attn")
        o_ret = _retention(proj, row(ret_gn_w[l]), ret_tabs, B, S, tl["ret_c"])
        wq, wk, wv = _mla_weights(mla_w_uq[l], mla_w_uk[l], mla_w_uv[l])
        mqt, mkx, mvt = _mla_prep(proj, row(mla_q_norm_w[l]), row(mla_kv_norm_w[l]), wq, wk, wv,
                                  mla_tabs, B, S, tl["attn_t"])
        o_mla = _attention(mqt, mkx, mvt, B, S, tl["attn_t"], "mla_attn")
        x_mid, h2 = _merge(o_fox, o_ret, o_mla, proj, x2,
                           w_br_fox[l].astype(BF16), w_br_ret[l].astype(BF16), w_br_mla[l].astype(BF16),
                           w_out[l].astype(BF16), row(norm_ffn_w[l]), tl["merge_tm"])
        if l % 2 == 0:
            i = l // 2
            x2 = _dense_ffn(h2, x_mid, dense_w_gate[i].astype(BF16), dense_w_up[i].astype(BF16),
                            dense_w_down[i].astype(BF16), tl["ffn_tm"], tl["ffn_tf"])
        else:
            i = l // 2
            x2 = _moe_layer(x_mid, row(norm_ffn_w[l]), moe_w_router[i], moe_w_gate[i],
                            moe_w_up[i], moe_w_down[i], row(final_norm_w), tl)
    return x2.reshape(B, S, D)
```
